```python
import jax
import jax.numpy as jnp
from jax import lax
import numpy as np

D_MODEL = 1024
BATCH = 4
SEQ = 4096
DEPTH = 2

CTX_LEN = 256
GRID_W = 64
N_MOD = 6
EPS = 1e-6
N_EVEN = (DEPTH + 1) // 2
N_ODD = DEPTH // 2
LRU_WIDTH = D_MODEL
LRU_HEADS = 8
LRU_HEAD_DIM = LRU_WIDTH // LRU_HEADS
LRU_C = 8.0
CONV_W = 4
GLA_HEADS = 4
GLA_DK = D_MODEL // 2
GLA_DV = D_MODEL
GLA_RANK = 16
GLA_TAU = 16.0
HGRN_HEADS = 8
HGRN_DIM = D_MODEL
CHUNK = 64
N_EXPERTS = 64
TOP_K = 8
N_GROUPS = 8
TOPK_GROUPS = 4
EXPERT_FF = D_MODEL // 4
SHARED_FF = D_MODEL // 4
ROUTED_SCALE = 2.5
MOE_BLOCK = 128
AB_SIZES = (LRU_WIDTH, LRU_WIDTH, GLA_DK, GLA_DK, GLA_DV, GLA_DV, 2 * GLA_RANK)
C_SIZES = (HGRN_DIM,) * 5

kernel_name = 'hybrid_rglru_gla_hgrn2_moe_prefix_dit'


def split_cols(u, sizes):
    out, off = [], 0
    for s in sizes:
        out.append(u[..., off:off + s])
        off += s
    return out


def rmsnorm(t, gain):
    tf = t.astype(jnp.float32)
    tf = tf * lax.rsqrt(jnp.mean(tf * tf, axis=-1, keepdims=True) + EPS)
    return tf.astype(t.dtype) * gain


def head_rmsnorm(t):
    return t * lax.rsqrt(jnp.mean(t * t, axis=-1, keepdims=True) + EPS)


def to_heads(t, n_heads):
    b, s, _ = t.shape
    return t.reshape(b, s, n_heads, -1).transpose(0, 2, 1, 3)


def from_heads(t):
    b, h, s, d = t.shape
    return t.transpose(0, 2, 1, 3).reshape(b, s, h * d)


def grid_transpose(t, rows, cols):
    b, n, d = t.shape
    return t.reshape(b, rows, cols, d).transpose(0, 2, 1, 3).reshape(b, n, d)


def prefix_order(t_ctx, t_lat, reverse, axis):
    if reverse:
        t_ctx, t_lat = jnp.flip(t_ctx, axis), jnp.flip(t_lat, axis)
    return jnp.concatenate([t_ctx, t_lat], axis=axis)


def split_order(t, n_ctx, reverse, axis):
    t_ctx, t_lat = jnp.split(t, [n_ctx], axis=axis)
    if reverse:
        t_ctx, t_lat = jnp.flip(t_ctx, axis), jnp.flip(t_lat, axis)
    return t_ctx, t_lat


def centred_dwconv(t, w, b):
    s = t.shape[1]
    left = CONV_W // 2
    tp = jnp.pad(t, ((0, 0), (left, CONV_W - 1 - left), (0, 0)))
    return b + sum(tp[:, j:j + s] * w[j] for j in range(CONV_W))


def rglru_gates(xc, wa, ba, wi, bi, lam):
    b, s, _ = xc.shape
    xf = xc.astype(jnp.float32)
    xh = xf.reshape(b, s, LRU_HEADS, LRU_HEAD_DIM)
    r = jax.nn.sigmoid(jnp.einsum('bshd,hde->bshe', xh, wa.astype(jnp.float32)).reshape(b, s, LRU_WIDTH) + ba.astype(jnp.float32))
    i = jax.nn.sigmoid(jnp.einsum('bshd,hde->bshe', xh, wi.astype(jnp.float32)).reshape(b, s, LRU_WIDTH) + bi.astype(jnp.float32))
    log_a = -LRU_C * r * jax.nn.softplus(-lam.astype(jnp.float32))
    return jnp.exp(log_a), jnp.sqrt(-jnp.expm1(2.0 * log_a)) * (i * xf)


def linear_scan(a, u):
    def combine(l, r):
        return l[0] * r[0], r[0] * l[1] + r[1]
    return lax.associative_scan(combine, (a, u), axis=1)[1]


def chunked_gated_linear(q, k, v, log_a):
    bsz, nh, s, dk = q.shape
    dv = v.shape[-1]
    n = s // CHUNK
    qc = q.astype(jnp.float32).reshape(bsz, nh, n, CHUNK, dk)
    kc = k.astype(jnp.float32).reshape(bsz, nh, n, CHUNK, dk)
    vc = v.astype(jnp.float32).reshape(bsz, nh, n, CHUNK, dv)
    b = jnp.cumsum(log_a.astype(jnp.float32).reshape(bsz, nh, n, CHUNK, dk), axis=3)
    b_ref = b[:, :, :, CHUNK // 2 - 1:CHUNK // 2]
    b_last = b[:, :, :, -1:]
    scores = jnp.einsum('bhncd,bhnsd->bhncs', qc * jnp.exp(b - b_ref), kc * jnp.exp(b_ref - b))
    scores = jnp.where(jnp.tril(jnp.ones((CHUNK, CHUNK), dtype=bool)), scores, 0.0)
    o_intra = jnp.einsum('bhncs,bhnsv->bhncv', scores, vc)
    q_dec = qc * jnp.exp(b)
    k_dec = kc * jnp.exp(b_last - b)
    chunk_decay = jnp.exp(b[:, :, :, -1])

    def step(state, inp):
        qd, kd, vv, dec = inp
        o = jnp.einsum('bhcd,bhdv->bhcv', qd, state)
        state = dec[..., None] * state + jnp.einsum('bhcd,bhcv->bhdv', kd, vv)
        return state, o

    xs = tuple(jnp.moveaxis(t, 2, 0) for t in (q_dec, k_dec, vc, chunk_decay))
    _, o_inter = lax.scan(step, jnp.zeros((bsz, nh, dk, dv), jnp.float32), xs)
    return (o_intra + jnp.moveaxis(o_inter, 0, 2)).reshape(bsz, nh, s, dv)


def directional_chunked(q, k, v, g, n_ctx, reverse):
    ordered = [prefix_order(t_c, t_l, reverse, 2) for t_c, t_l in (q, k, v, g)]
    return split_order(chunked_gated_linear(*ordered), n_ctx, reverse, 2)


def gla_log_alpha(alr, wa2, ba):
    z = alr.astype(jnp.float32) @ wa2.astype(jnp.float32) + ba.astype(jnp.float32)
    return to_heads(jax.nn.log_sigmoid(z) / GLA_TAU, GLA_HEADS)


def mixer_ab(h_ctx, h_lat, w_in, conv_w, conv_b, lru_wa, lru_ba, lru_wi, lru_bi, lru_lambda,
             gla_wa2, gla_ba, w_out):
    n_ctx = h_ctx.shape[1]
    dt = h_lat.dtype
    cols = [split_cols(h @ w_in, AB_SIZES) for h in (h_ctx, h_lat)]
    conv = [centred_dwconv(cl[1], conv_w, conv_b) for cl in cols]
    rec = [0.0, 0.0]
    for d in range(2):
        rev = d == 1
        gates = [rglru_gates(xc, lru_wa[d], lru_ba[d], lru_wi[d], lru_bi[d], lru_lambda[d]) for xc in conv]
        a = prefix_order(gates[0][0], gates[1][0], rev, 1)
        u = prefix_order(gates[0][1], gates[1][1], rev, 1)
        hs = split_order(linear_scan(a, u), n_ctx, rev, 1)
        rec = [rec[j] + hs[j] for j in range(2)]
    y_a = [jax.nn.gelu(cl[0]) * r.astype(dt) for cl, r in zip(cols, rec)]
    q = tuple(to_heads(cl[2] * (GLA_DK // GLA_HEADS) ** -0.5, GLA_HEADS) for cl in cols)
    k = tuple(to_heads(cl[3], GLA_HEADS) for cl in cols)
    v = tuple(to_heads(cl[4], GLA_HEADS) for cl in cols)
    att = [0.0, 0.0]
    for d in range(2):
        g = tuple(gla_log_alpha(cl[6][..., d * GLA_RANK:(d + 1) * GLA_RANK], gla_wa2[d], gla_ba[d]) for cl in cols)
        o = directional_chunked(q, k, v, g, n_ctx, d == 1)
        att = [att[j] + o[j] for j in range(2)]
    y_b = [from_heads(head_rmsnorm(o)).astype(dt) * jax.nn.silu(cl[5]) for cl, o in zip(cols, att)]
    out_ctx = jnp.concatenate([y_a[0], y_b[0]], axis=-1) @ w_out
    out_lat = jnp.concatenate([y_a[1], y_b[1]], axis=-1) @ w_out
    return out_ctx, out_lat


def mixer_c(h_ctx, h_lat, layer, w_in, lb_logits, w_out):
    n_ctx = h_ctx.shape[1]
    dt = h_lat.dtype
    rows = h_lat.shape[1] // GRID_W
    h_lat = grid_transpose(h_lat, rows, GRID_W)
    lb_cum = jnp.cumsum(jax.nn.softmax(lb_logits.astype(jnp.float32), axis=0), axis=0)
    lb = lb_cum[layer] - lb_cum[0]
    log_lb, log_ub = jnp.log(lb), jnp.log1p(-lb)

    def forget(z):
        zf = z.astype(jnp.float32)
        log_f = jnp.logaddexp(log_lb, log_ub + jax.nn.log_sigmoid(zf))
        one_minus_f = jnp.exp(log_ub) * jax.nn.sigmoid(-zf)
        return to_heads(log_f, HGRN_HEADS), to_heads(one_minus_f, HGRN_HEADS)

    cols = [split_cols(h @ w_in, C_SIZES) for h in (h_ctx, h_lat)]
    q = tuple(to_heads(cl[0], HGRN_HEADS) for cl in cols)
    v = tuple(to_heads(cl[3], HGRN_HEADS) for cl in cols)
    acc = [0.0, 0.0]
    for d in range(2):
        fg = [forget(cl[1 + d]) for cl in cols]
        o = directional_chunked(q, (fg[0][1], fg[1][1]), v, (fg[0][0], fg[1][0]), n_ctx, d == 1)
        acc = [acc[j] + o[j] for j in range(2)]
    y = [from_heads(head_rmsnorm(o)).astype(dt) * jax.nn.silu(cl[4]) for cl, o in zip(cols, acc)]
    out_ctx = y[0] @ w_out
    out_lat = grid_transpose(y[1] @ w_out, GRID_W, rows)
    return out_ctx, out_lat


def swiglu(t, w_gate, w_up, w_down):
    return (jax.nn.silu(t @ w_gate) * (t @ w_up)) @ w_down


def moe_ffn(h, router, bias, w_gate, w_up, w_down, ws_gate, ws_up, ws_down):
    n_tok = h.shape[0]
    affinity = jax.nn.sigmoid(jnp.dot(h.astype(jnp.float32), router.astype(jnp.float32)))
    biased = affinity + bias.astype(jnp.float32)
    per_group = N_EXPERTS // N_GROUPS
    group_score = lax.top_k(biased.reshape(n_tok, N_GROUPS, per_group), 2)[0].sum(-1)
    _, top_groups = lax.top_k(group_score, TOPK_GROUPS)
    group_keep = jax.nn.one_hot(top_groups, N_GROUPS).sum(1) > 0
    expert_keep = jnp.repeat(group_keep, per_group, axis=1)
    _, top_e = lax.top_k(jnp.where(expert_keep, biased, -jnp.inf), TOP_K)
    gate = jnp.take_along_axis(affinity, top_e, axis=1)
    gate = ROUTED_SCALE * gate / jnp.sum(gate, axis=-1, keepdims=True)
    n_rows = n_tok * TOP_K
    n_blocks = -(-n_rows // MOE_BLOCK) + N_EXPERTS
    flat_e = top_e.reshape(-1)
    order = jnp.argsort(flat_e)
    sorted_e = flat_e[order]
    counts = jnp.bincount(flat_e, length=N_EXPERTS)
    padded = (counts + MOE_BLOCK - 1) // MOE_BLOCK * MOE_BLOCK
    ends = jnp.cumsum(padded)
    dest = (ends - padded)[sorted_e] + jnp.arange(n_rows) - (jnp.cumsum(counts) - counts)[sorted_e]
    row_tok = jnp.zeros((n_blocks * MOE_BLOCK,), jnp.int32).at[dest].set((order // TOP_K).astype(jnp.int32))
    row_w = jnp.zeros((n_blocks * MOE_BLOCK,), h.dtype).at[dest].set(gate.reshape(-1)[order].astype(h.dtype))
    block_e = jnp.minimum(jnp.searchsorted(ends, jnp.arange(n_blocks) * MOE_BLOCK, side='right'), N_EXPERTS - 1)

    def block_step(acc, blk):
        tok, wt, e = blk
        y = swiglu(h[tok], w_gate[e], w_up[e], w_down[e])
        return acc.at[tok].add(y * wt[:, None]), None

    shared = swiglu(h, ws_gate, ws_up, ws_down)
    out, _ = lax.scan(block_step, shared, (row_tok.reshape(n_blocks, MOE_BLOCK), row_w.reshape(n_blocks, MOE_BLOCK), block_e))
    return out


def setup_inputs(seed: int = 0) -> dict:
    key = jax.random.key(seed)
    keys = iter(jax.random.split(key, 40))

    def nrm(shape, scale):
        return scale * jax.random.normal(next(keys), shape, jnp.float32)

    ab_cols = sum(AB_SIZES)
    c_cols = sum(C_SIZES)
    a_pow = jax.random.uniform(next(keys), (N_EVEN, 2, LRU_WIDTH), jnp.float32, 0.9, 0.999)
    a_base = a_pow ** (1.0 / LRU_C)
    lru_lambda = jnp.log(a_base) - jnp.log1p(-a_base)
    return {
        'x': nrm((BATCH, SEQ, D_MODEL), 1.0),
        'c': nrm((BATCH, D_MODEL), 1.0),
        'ctx': nrm((BATCH, CTX_LEN, D_MODEL), 1.0),
        'c_ctx': nrm((D_MODEL,), 1.0),
        'mod_w': nrm((DEPTH, D_MODEL, N_MOD * D_MODEL), 0.5 * D_MODEL ** -0.5),
        'mod_b': nrm((DEPTH, N_MOD * D_MODEL), 0.02),
        'pre_gain': 1.0 + nrm((DEPTH, 2, D_MODEL), 0.1),
        'post_gain': 1.0 + nrm((DEPTH, 2, D_MODEL), 0.1),
        'ab_w_in': nrm((N_EVEN, D_MODEL, ab_cols), D_MODEL ** -0.5),
        'ab_conv_w': nrm((N_EVEN, CONV_W, LRU_WIDTH), CONV_W ** -0.5),
        'ab_conv_b': nrm((N_EVEN, LRU_WIDTH), 0.02),
        'ab_lru_wa': nrm((N_EVEN, 2, LRU_HEADS, LRU_HEAD_DIM, LRU_HEAD_DIM), LRU_HEAD_DIM ** -0.5),
        'ab_lru_ba': nrm((N_EVEN, 2, LRU_WIDTH), 0.02),
        'ab_lru_wi': nrm((N_EVEN, 2, LRU_HEADS, LRU_HEAD_DIM, LRU_HEAD_DIM), LRU_HEAD_DIM ** -0.5),
        'ab_lru_bi': nrm((N_EVEN, 2, LRU_WIDTH), 0.02),
        'ab_lru_lambda': lru_lambda,
        'ab_gla_wa2': nrm((N_EVEN, 2, GLA_RANK, GLA_DK), GLA_RANK ** -0.5),
        'ab_gla_ba': nrm((N_EVEN, 2, GLA_DK), 0.02),
        'ab_w_out': nrm((N_EVEN, LRU_WIDTH + GLA_DV, D_MODEL), (LRU_WIDTH + GLA_DV) ** -0.5),
        'c_w_in': nrm((N_ODD, D_MODEL, c_cols), D_MODEL ** -0.5),
        'hgrn_lb_logits': nrm((DEPTH, HGRN_DIM), 0.5),
        'c_w_out': nrm((N_ODD, HGRN_DIM, D_MODEL), HGRN_DIM ** -0.5),
        'moe_router': nrm((DEPTH, D_MODEL, N_EXPERTS), D_MODEL ** -0.5),
        'moe_bias': nrm((DEPTH, N_EXPERTS), 0.01),
        'moe_w_gate': nrm((DEPTH, N_EXPERTS, D_MODEL, EXPERT_FF), D_MODEL ** -0.5),
        'moe_w_up': nrm((DEPTH, N_EXPERTS, D_MODEL, EXPERT_FF), D_MODEL ** -0.5),
        'moe_w_down': nrm((DEPTH, N_EXPERTS, EXPERT_FF, D_MODEL), EXPERT_FF ** -0.5),
        'moe_ws_gate': nrm((DEPTH, D_MODEL, SHARED_FF), D_MODEL ** -0.5),
        'moe_ws_up': nrm((DEPTH, D_MODEL, SHARED_FF), D_MODEL ** -0.5),
        'moe_ws_down': nrm((DEPTH, SHARED_FF, D_MODEL), SHARED_FF ** -0.5),
    }


def reference(x, c, ctx, c_ctx, mod_w, mod_b, pre_gain, post_gain,
              ab_w_in, ab_conv_w, ab_conv_b, ab_lru_wa, ab_lru_ba, ab_lru_wi, ab_lru_bi, ab_lru_lambda,
              ab_gla_wa2, ab_gla_ba, ab_w_out, c_w_in, hgrn_lb_logits, c_w_out,
              moe_router, moe_bias, moe_w_gate, moe_w_up, moe_w_down, moe_ws_gate, moe_ws_up, moe_ws_down):
    silu_c = jax.nn.silu(c)
    silu_cc = jax.nn.silu(c_ctx)
    n_tok_ctx = ctx.shape[0] * ctx.shape[1]
    for layer in range(DEPTH):
        last = layer == DEPTH - 1
        p = layer // 2
        m_l = jnp.split((silu_c @ mod_w[layer] + mod_b[layer])[:, None, :], N_MOD, axis=-1)
        m_c = jnp.split(silu_cc @ mod_w[layer] + mod_b[layer], N_MOD, axis=-1)
        h_c = rmsnorm(ctx, pre_gain[layer, 0]) * (1.0 + m_c[1]) + m_c[0]
        h_l = rmsnorm(x, pre_gain[layer, 0]) * (1.0 + m_l[1]) + m_l[0]
        if layer % 2 == 0:
            o_c, o_l = mixer_ab(h_c, h_l, ab_w_in[p], ab_conv_w[p], ab_conv_b[p], ab_lru_wa[p], ab_lru_ba[p],
                                ab_lru_wi[p], ab_lru_bi[p], ab_lru_lambda[p], ab_gla_wa2[p], ab_gla_ba[p], ab_w_out[p])
        else:
            o_c, o_l = mixer_c(h_c, h_l, layer, c_w_in[p], hgrn_lb_logits, c_w_out[p])
        x = x + m_l[2] * rmsnorm(o_l, post_gain[layer, 0])
        moe_p = (moe_router[layer], moe_bias[layer], moe_w_gate[layer], moe_w_up[layer], moe_w_down[layer],
                 moe_ws_gate[layer], moe_ws_up[layer], moe_ws_down[layer])
        h_l = rmsnorm(x, pre_gain[layer, 1]) * (1.0 + m_l[4]) + m_l[3]
        if last:
            f_l = moe_ffn(h_l.reshape(-1, D_MODEL), *moe_p).reshape(x.shape)
        else:
            ctx = ctx + m_c[2] * rmsnorm(o_c, post_gain[layer, 0])
            h_c = rmsnorm(ctx, pre_gain[layer, 1]) * (1.0 + m_c[4]) + m_c[3]
            f = moe_ffn(jnp.concatenate([h_c.reshape(-1, D_MODEL), h_l.reshape(-1, D_MODEL)], axis=0), *moe_p)
            ctx = ctx + m_c[5] * rmsnorm(f[:n_tok_ctx].reshape(ctx.shape), post_gain[layer, 1])
            f_l = f[n_tok_ctx:].reshape(x.shape)
        x = x + m_l[5] * rmsnorm(f_l, post_gain[layer, 1])
    return x
```

```python
import functools

import jax
import jax.numpy as jnp
from jax import lax
from jax.experimental import pallas as pl
from jax.experimental.pallas import tpu as pltpu

F32 = jnp.float32
BF16 = jnp.bfloat16

EPS = 1e-6
N_MOD = 6
ROW_TILE = 256
CHUNK = 64
GRID_W = 64
LRU_HEADS = 8
LRU_C = 8.0
CONV_W = 4
GLA_HEADS = 4
GLA_RANK = 16
GLA_TAU = 16.0
HGRN_HEADS = 8
N_EXPERTS = 64
TOP_K = 8
N_GROUPS = 8
TOPK_GROUPS = 4
ROUTED_SCALE = 2.5
LANE = 128
VMEM_LIMIT = 48 * 1024 * 1024

_NT = (((1,), (1,)), ((), ()))
_TN = (((0,), (0,)), ((), ()))


def _params(*sem):
    return pltpu.CompilerParams(dimension_semantics=sem, vmem_limit_bytes=VMEM_LIMIT)


def _rms(x, gain):
    ms = jnp.mean(x * x, axis=-1, keepdims=True)
    return x * lax.rsqrt(ms + EPS) * gain


def _sigmoid(x):
    return jax.nn.sigmoid(x)


def _silu(x):
    return x * _sigmoid(x)


def _log_sigmoid(x):
    return jnp.minimum(x, 0.0) - jnp.log1p(jnp.exp(-jnp.abs(x)))


def _softplus(x):
    return jnp.maximum(x, 0.0) + jnp.log1p(jnp.exp(-jnp.abs(x)))


def _modulate(xn, tab_ref, shift_row, scale_row):
    parts = []
    for s in range(tab_ref.shape[0]):
        t = tab_ref[s]
        rows = xn[s * ROW_TILE:(s + 1) * ROW_TILE]
        parts.append(rows * (1.0 + t[scale_row:scale_row + 1]) + t[shift_row:shift_row + 1])
    return parts[0] if len(parts) == 1 else jnp.concatenate(parts, axis=0)


def _gated_residual(x, branch, tab_ref, gate_row):
    parts = []
    for s in range(tab_ref.shape[0]):
        sl = slice(s * ROW_TILE, (s + 1) * ROW_TILE)
        parts.append(x[sl] + tab_ref[s][gate_row:gate_row + 1] * branch[sl])
    return parts[0] if len(parts) == 1 else jnp.concatenate(parts, axis=0)


def _mod_kernel(c_ref, w_ref, b_ref, o_ref):
    s = _silu(c_ref[...])
    o_ref[0] = jnp.dot(s, w_ref[0], precision=lax.Precision.HIGHEST,
                       preferred_element_type=F32) + b_ref[0]


def _modulation(cvec, mod_w, mod_b):
    depth, d, n = mod_w.shape
    tn = 1024
    return pl.pallas_call(
        _mod_kernel,
        out_shape=jax.ShapeDtypeStruct((depth, 8, n), F32),
        grid=(depth, n // tn),
        in_specs=[pl.BlockSpec((8, d), lambda l, j: (0, 0)),
                  pl.BlockSpec((1, d, tn), lambda l, j: (l, 0, j)),
                  pl.BlockSpec((1, 1, tn), lambda l, j: (l, 0, j))],
        out_specs=pl.BlockSpec((1, 8, tn), lambda l, j: (l, 0, j)),
        compiler_params=_params("parallel", "parallel"),
        name="adaln_modulation",
    )(cvec, mod_w, mod_b.reshape(depth, 1, n))


def _in_proj_kernel(x_ref, tab_ref, gain_ref, w_ref, o_ref):
    h = _modulate(_rms(x_ref[...], gain_ref[...]), tab_ref, 0, 1)
    o_ref[...] = jnp.dot(h.astype(BF16), w_ref[...], preferred_element_type=F32)


def _in_proj(x, tab, gain, w, tm, tn):
    ntok, d = x.shape
    n = w.shape[1]
    nsub = tm // ROW_TILE
    return pl.pallas_call(
        _in_proj_kernel,
        out_shape=jax.ShapeDtypeStruct((ntok, n), F32),
        grid=(ntok // tm, n // tn),
        in_specs=[pl.BlockSpec((tm, d), lambda i, j: (i, 0)),
                  pl.BlockSpec((nsub, 8, d), lambda i, j: (i, 0, 0)),
                  pl.BlockSpec((1, d), lambda i, j: (0, 0)),
                  pl.BlockSpec((d, tn), lambda i, j: (0, j))],
        out_specs=pl.BlockSpec((tm, tn), lambda i, j: (i, j)),
        compiler_params=_params("parallel", "parallel"),
        name="norm_mod_in_proj",
    )(x, tab, gain, w)


def _lru_kernel(*refs, seg_len, aliased):
    if aliased:
        (g_ref, x_ref, h0_ref, cw_ref, cb_ref, wg_ref, bg_ref, lam_ref, _,
         y_ref, ht_ref, xc_scr, hf_scr) = refs
    else:
        (g_ref, x_ref, h0_ref, cw_ref, cb_ref, wg_ref, bg_ref, lam_ref,
         y_ref, ht_ref, xc_scr, hf_scr) = refs
    T = ROW_TILE
    nt = seg_len // T
    row = lax.broadcasted_iota(jnp.int32, (T, LANE), 0)
    cw = cw_ref[...]
    cb = cb_ref[...]
    sp = _softplus(-lam_ref[...])
    h0 = h0_ref[0]

    def conv_tile(t):
        t0 = pl.multiple_of(t * T, T)
        cur = x_ref[pl.ds(t0, T), :]
        p0 = pl.multiple_of(jnp.maximum(t0 - 8, 0), 8)
        n0 = pl.multiple_of(jnp.minimum(t0 + T, seg_len - 8), 8)
        pv = jnp.where(t > 0, 1.0, 0.0)
        nv = jnp.where(t < nt - 1, 1.0, 0.0)
        prev8 = x_ref[pl.ds(p0, 8), :] * pv
        next8 = x_ref[pl.ds(n0, 8), :] * nv
        xm1 = jnp.where(row == 0, prev8[7:8], pltpu.roll(cur, 1, 0))
        xm2 = jnp.where(row == 0, prev8[6:7], jnp.where(row == 1, prev8[7:8], pltpu.roll(cur, 2, 0)))
        xp1 = jnp.where(row == T - 1, next8[0:1], pltpu.roll(cur, T - 1, 0))
        return cb + xm2 * cw[0:1] + xm1 * cw[1:2] + cur * cw[2:3] + xp1 * cw[3:4]

    def gates(xc, d):
        z = jnp.dot(xc.astype(BF16), wg_ref[d, 0], preferred_element_type=F32) + bg_ref[d, 0]
        r = _sigmoid(z[:, :LANE])
        i = _sigmoid(z[:, LANE:])
        log_a = (-LRU_C) * r * sp[d:d + 1]
        a = jnp.exp(log_a)
        return a, jnp.sqrt(1.0 - a * a) * (i * xc)

    def scan_tile(a, u, carry, reverse):
        s = 1
        while s < T:
            if reverse:
                keep = row < T - s
                a_s = jnp.where(keep, pltpu.roll(a, T - s, 0), 1.0)
                u_s = jnp.where(keep, pltpu.roll(u, T - s, 0), 0.0)
            else:
                keep = row >= s
                a_s = jnp.where(keep, pltpu.roll(a, s, 0), 1.0)
                u_s = jnp.where(keep, pltpu.roll(u, s, 0), 0.0)
            u = u + a * u_s
            a = a * a_s
            s *= 2
        return u + a * carry

    def fwd_body(t, carry):
        t0 = pl.multiple_of(t * T, T)
        xc = conv_tile(t)
        xc_scr[pl.ds(t0, T), :] = xc
        a, u = gates(xc, 0)
        h = scan_tile(a, u, carry, False)
        hf_scr[pl.ds(t0, T), :] = h
        return h[T - 1:T]

    def bwd_body(k, carry):
        t = nt - 1 - k
        t0 = pl.multiple_of(t * T, T)
        xc = xc_scr[pl.ds(t0, T), :]
        a, u = gates(xc, 1)
        h = scan_tile(a, u, carry, True)
        rec = hf_scr[pl.ds(t0, T), :] + h
        y_ref[pl.ds(t0, T), :] = (jax.nn.gelu(g_ref[pl.ds(t0, T), :]) * rec).astype(y_ref.dtype)
        return h[0:1]

    hf = lax.fori_loop(0, nt, fwd_body, h0[0:1])
    hb = lax.fori_loop(0, nt, bwd_body, h0[1:2])
    ht_ref[0] = jnp.concatenate([hf, hb], axis=0)


def _lru_segment(u, h0, lru_w, y_prev, *, batch, seg_len, row_blk0, g_col0, x_col0):
    ntok = u.shape[0]
    width = LRU_HEADS * LANE
    cw, cb, wg, bg, lam = lru_w
    aliased = y_prev is not None
    rb = lambda b, h: (row_blk0 + b, 0)
    in_specs = [pl.BlockSpec((seg_len, LANE), lambda b, h: (row_blk0 + b, g_col0 + h)),
                pl.BlockSpec((seg_len, LANE), lambda b, h: (row_blk0 + b, x_col0 + h)),
                pl.BlockSpec((1, 2, LANE), lambda b, h: (b, 0, h)),
                pl.BlockSpec((8, LANE), lambda b, h: (0, h)),
                pl.BlockSpec((1, LANE), lambda b, h: (0, h)),
                pl.BlockSpec((2, 1, LANE, 2 * LANE), lambda b, h: (0, h, 0, 0)),
                pl.BlockSpec((2, 1, 1, 2 * LANE), lambda b, h: (0, h, 0, 0)),
                pl.BlockSpec((2, LANE), lambda b, h: (0, h))]
    args = [u, u, h0, cw, cb, wg, bg, lam]
    aliases = {}
    if aliased:
        in_specs.append(pl.BlockSpec(memory_space=pl.ANY))
        args.append(y_prev)
        aliases = {len(args) - 1: 0}
    del rb
    return pl.pallas_call(
        functools.partial(_lru_kernel, seg_len=seg_len, aliased=aliased),
        out_shape=(jax.ShapeDtypeStruct((ntok, width), BF16),
                   jax.ShapeDtypeStruct((batch, 2, width), F32)),
        grid=(batch, LRU_HEADS),
        in_specs=in_specs,
        out_specs=(pl.BlockSpec((seg_len, LANE), lambda b, h: (row_blk0 + b, h)),
                   pl.BlockSpec((1, 2, LANE), lambda b, h: (b, 0, h))),
        scratch_shapes=[pltpu.VMEM((seg_len, LANE), F32), pltpu.VMEM((seg_len, LANE), F32)],
        input_output_aliases=aliases,
        compiler_params=_params("parallel", "parallel"),
        name="rglru_segment",
    )(*args)


def _chunk_cumsum(g, reverse):
    rows = g.shape[0]
    pos = lax.broadcasted_iota(jnp.int32, g.shape, 0) & (CHUNK - 1)
    b = g
    s = 1
    while s < CHUNK:
        if reverse:
            b = b + jnp.where(pos < CHUNK - s, pltpu.roll(b, rows - s, 0), 0.0)
        else:
            b = b + jnp.where(pos >= s, pltpu.roll(b, s, 0), 0.0)
        s *= 2
    return b


def _chunk_step(q, k, v, b, st, reverse, tri):
    iref = CHUNK // 2 if reverse else CHUNK // 2 - 1
    ilast = 0 if reverse else CHUNK - 1
    b_ref = b[iref:iref + 1]
    b_last = b[ilast:ilast + 1]
    qe = (q * jnp.exp(b - b_ref)).astype(BF16)
    ke = (k * jnp.exp(b_ref - b)).astype(BF16)
    sc = lax.dot_general(qe, ke, _NT, preferred_element_type=F32)
    sc = jnp.where(tri, sc, 0.0).astype(BF16)
    vb = v.astype(BF16)
    o = jnp.dot(sc, vb, preferred_element_type=F32)
    qd = (q * jnp.exp(b)).astype(BF16)
    o = o + lax.dot_general(qd, st.astype(BF16), _NT, preferred_element_type=F32)
    kd = (k * jnp.exp(b_last - b)).astype(BF16)
    st = st * jnp.exp(b_last) + lax.dot_general(vb, kd, _TN, preferred_element_type=F32)
    return o, st


def _chunk_scan(q, k, v, b, st_scr, o_ref, *, heads, dk, dv, reverse):
    rows = q.shape[0]
    nchunk = rows // CHUNK
    r = lax.broadcasted_iota(jnp.int32, (CHUNK, CHUNK), 0)
    c = lax.broadcasted_iota(jnp.int32, (CHUNK, CHUNK), 1)
    tri = (c >= r) if reverse else (c <= r)
    order = range(nchunk - 1, -1, -1) if reverse else range(nchunk)
    for h in range(heads):
        st = st_scr[h]
        for ci in order:
            rs = slice(ci * CHUNK, (ci + 1) * CHUNK)
            ks = slice(h * dk, (h + 1) * dk)
            vs = slice(h * dv, (h + 1) * dv)
            o, st = _chunk_step(q[rs, ks], k[rs, ks], v[rs, vs], b[rs, ks], st, reverse, tri)
            o_ref[rs, vs] = o
        st_scr[h] = st


def _gla_kernel(*refs, reverse, aliased):
    if aliased:
        q_ref, k_ref, v_ref, a_ref, wa_ref, ba_ref, s0_ref, _, o_ref, st_ref, st_scr = refs
    else:
        q_ref, k_ref, v_ref, a_ref, wa_ref, ba_ref, s0_ref, o_ref, st_ref, st_scr = refs
    t = pl.program_id(1)

    @pl.when(t == 0)
    def _():
        st_scr[...] = s0_ref[0]

    dk = q_ref.shape[1] // GLA_HEADS
    dv = v_ref.shape[1] // GLA_HEADS
    z = jnp.dot(a_ref[...].astype(BF16), wa_ref[...], preferred_element_type=F32) + ba_ref[...]
    g = _log_sigmoid(z) * (1.0 / GLA_TAU)
    b = _chunk_cumsum(g, reverse)
    q = q_ref[...] * (dk ** -0.5)
    _chunk_scan(q, k_ref[...], v_ref[...], b, st_scr, o_ref,
                heads=GLA_HEADS, dk=dk, dv=dv, reverse=reverse)

    @pl.when(t == pl.num_programs(1) - 1)
    def _():
        st_ref[0] = st_scr[...]


def _gla_segment(u, s0, wa, ba, o_prev, *, batch, seg_len, row_blk0, reverse, cols):
    ntok = u.shape[0]
    q0, k0, v0, a0 = cols
    dkt = wa.shape[1]
    dvt = s0.shape[1] * s0.shape[2]
    nblk = seg_len // ROW_TILE
    aliased = o_prev is not None

    def blk(b, t):
        return row_blk0 + b * nblk + (nblk - 1 - t if reverse else t)

    in_specs = [pl.BlockSpec((ROW_TILE, dkt), lambda b, t: (blk(b, t), q0 * LANE // dkt)),
                pl.BlockSpec((ROW_TILE, dkt), lambda b, t: (blk(b, t), k0 * LANE // dkt)),
                pl.BlockSpec((ROW_TILE, dvt), lambda b, t: (blk(b, t), v0 * LANE // dvt)),
                pl.BlockSpec((ROW_TILE, LANE), lambda b, t: (blk(b, t), a0)),
                pl.BlockSpec((LANE, dkt), lambda b, t: (0, 0)),
                pl.BlockSpec((1, dkt), lambda b, t: (0, 0)),
                pl.BlockSpec((1,) + s0.shape[1:], lambda b, t: (b, 0, 0, 0))]
    args = [u, u, u, u, wa, ba, s0]
    aliases = {}
    if aliased:
        in_specs.append(pl.BlockSpec(memory_space=pl.ANY))
        args.append(o_prev)
        aliases = {len(args) - 1: 0}
    return pl.pallas_call(
        functools.partial(_gla_kernel, reverse=reverse, aliased=aliased),
        out_shape=(jax.ShapeDtypeStruct((ntok, dvt), F32),
                   jax.ShapeDtypeStruct(s0.shape, F32)),
        grid=(batch, nblk),
        in_specs=in_specs,
        out_specs=(pl.BlockSpec((ROW_TILE, dvt), lambda b, t: (blk(b, t), 0)),
                   pl.BlockSpec((1,) + s0.shape[1:], lambda b, t: (b, 0, 0, 0))),
        scratch_shapes=[pltpu.VMEM(s0.shape[1:], F32)],
        input_output_aliases=aliases,
        compiler_params=_params("parallel", "arbitrary"),
        name="gla_segment",
    )(*args)


def _hgrn_kernel(*refs, reverse, aliased):
    if aliased:
        q_ref, z_ref, v_ref, lb_ref, ub_ref, s0_ref, _, o_ref, st_ref, st_scr = refs
    else:
        q_ref, z_ref, v_ref, lb_ref, ub_ref, s0_ref, o_ref, st_ref, st_scr = refs
    t = pl.program_id(1)

    @pl.when(t == 0)
    def _():
        st_scr[...] = s0_ref[0]

    d = q_ref.shape[1] // HGRN_HEADS
    z = z_ref[...]
    log_lb = lb_ref[...]
    log_ub = ub_ref[...]
    y = log_ub + _log_sigmoid(z)
    log_f = jnp.maximum(log_lb, y) + jnp.log1p(jnp.exp(-jnp.abs(log_lb - y)))
    k = jnp.exp(log_ub) * _sigmoid(-z)
    b = _chunk_cumsum(log_f, reverse)
    _chunk_scan(q_ref[...], k, v_ref[...], b, st_scr, o_ref,
                heads=HGRN_HEADS, dk=d, dv=d, reverse=reverse)

    @pl.when(t == pl.num_programs(1) - 1)
    def _():
        st_ref[0] = st_scr[...]


def _hgrn_segment(u, s0, log_lb, log_ub, o_prev, *, batch, rows, nblk, row_blk0, reverse, col_fn,
                  out_col_fn, out_shape):
    width = log_lb.shape[1]
    aliased = o_prev is not None

    def tt(t):
        return nblk - 1 - t if reverse else t

    zsec = 2 if reverse else 1
    in_specs = [pl.BlockSpec((rows, width), lambda b, t: (row_blk0 + b, col_fn(tt(t), 0))),
                pl.BlockSpec((rows, width), lambda b, t: (row_blk0 + b, col_fn(tt(t), zsec))),
                pl.BlockSpec((rows, width), lambda b, t: (row_blk0 + b, col_fn(tt(t), 3))),
                pl.BlockSpec((1, width), lambda b, t: (0, 0)),
                pl.BlockSpec((1, width), lambda b, t: (0, 0)),
                pl.BlockSpec((1,) + s0.shape[1:], lambda b, t: (b, 0, 0, 0))]
    args = [u, u, u, log_lb, log_ub, s0]
    aliases = {}
    if aliased:
        in_specs.append(pl.BlockSpec(memory_space=pl.ANY))
        args.append(o_prev)
        aliases = {len(args) - 1: 0}
    return pl.pallas_call(
        functools.partial(_hgrn_kernel, reverse=reverse, aliased=aliased),
        out_shape=(jax.ShapeDtypeStruct(out_shape, F32),
                   jax.ShapeDtypeStruct(s0.shape, F32)),
        grid=(batch, nblk),
        in_specs=in_specs,
        out_specs=(pl.BlockSpec((rows, width), lambda b, t: (row_blk0 + b, out_col_fn(tt(t)))),
                   pl.BlockSpec((1,) + s0.shape[1:], lambda b, t: (b, 0, 0, 0))),
        scratch_shapes=[pltpu.VMEM(s0.shape[1:], F32)],
        input_output_aliases=aliases,
        compiler_params=_params("parallel", "arbitrary"),
        name="hgrn_segment",
    )(*args)


def _head_norm_gate(o, gate, heads):
    dh = o.shape[1] // heads
    parts = []
    for h in range(heads):
        oh = o[:, h * dh:(h + 1) * dh]
        ms = jnp.mean(oh * oh, axis=-1, keepdims=True)
        parts.append((oh * lax.rsqrt(ms + EPS) * _silu(gate[:, h * dh:(h + 1) * dh])).astype(BF16))
    return jnp.concatenate(parts, axis=1)


def _out_proj_tail(o, x_ref, tab_ref, pg_ref, ng_ref, rt_ref, xo_ref, h_ref, lg_ref):
    xn = _gated_residual(x_ref[...], _rms(o, pg_ref[...]), tab_ref, 2)
    xo_ref[...] = xn
    h2 = _modulate(_rms(xn, ng_ref[...]), tab_ref, 3, 4)
    h_ref[...] = h2.astype(BF16)
    lg_ref[...] = lax.dot_general(rt_ref[...], h2, _NT, precision=lax.Precision.HIGHEST,
                                  preferred_element_type=F32)


def _out_proj_ab_kernel(ya_ref, of_ref, ob_ref, gate_ref, w_ref, x_ref, tab_ref, pg_ref, ng_ref,
                        rt_ref, xo_ref, h_ref, lg_ref):
    yb = _head_norm_gate(of_ref[...] + ob_ref[...], gate_ref[...], GLA_HEADS)
    da = ya_ref.shape[1]
    o = jnp.dot(ya_ref[...], w_ref[:da], preferred_element_type=F32)
    o = o + jnp.dot(yb, w_ref[da:], preferred_element_type=F32)
    _out_proj_tail(o, x_ref, tab_ref, pg_ref, ng_ref, rt_ref, xo_ref, h_ref, lg_ref)


def _out_proj_c_kernel(of_ref, ob_ref, gate_ref, w_ref, x_ref, tab_ref, pg_ref, ng_ref,
                       rt_ref, xo_ref, h_ref, lg_ref):
    y = _head_norm_gate(of_ref[...] + ob_ref[...], gate_ref[...], HGRN_HEADS)
    o = jnp.dot(y, w_ref[...], preferred_element_type=F32)
    _out_proj_tail(o, x_ref, tab_ref, pg_ref, ng_ref, rt_ref, xo_ref, h_ref, lg_ref)


def _out_proj(kernel, lead, lead_specs, w, x, tab, post_gain, next_gain, router_t, tm):
    ntok, d = x.shape
    nsub = tm // ROW_TILE
    ne = router_t.shape[0]
    in_specs = list(lead_specs) + [
        pl.BlockSpec(w.shape, lambda i: (0, 0)),
        pl.BlockSpec((tm, d), lambda i: (i, 0)),
        pl.BlockSpec((nsub, 8, d), lambda i: (i, 0, 0)),
        pl.BlockSpec((1, d), lambda i: (0, 0)),
        pl.BlockSpec((1, d), lambda i: (0, 0)),
        pl.BlockSpec((ne, d), lambda i: (0, 0))]
    return pl.pallas_call(
        kernel,
        out_shape=(jax.ShapeDtypeStruct((ntok, d), F32),
                   jax.ShapeDtypeStruct((ntok, d), BF16),
                   jax.ShapeDtypeStruct((ne, ntok), F32)),
        grid=(ntok // tm,),
        in_specs=in_specs,
        out_specs=(pl.BlockSpec((tm, d), lambda i: (i, 0)),
                   pl.BlockSpec((tm, d), lambda i: (i, 0)),
                   pl.BlockSpec((ne, tm), lambda i: (0, i))),
        compiler_params=_params("parallel"),
        name="out_proj_norm_residual",
    )(*lead, w, x, tab, post_gain, next_gain, router_t)


def _route_kernel(lg_ref, bias_ref, cw_ref):
    tm = lg_ref.shape[1]
    per_group = N_EXPERTS // N_GROUPS
    shape3 = (N_GROUPS, per_group, tm)
    aff = _sigmoid(lg_ref[...]).reshape(shape3)
    biased = aff + bias_ref[...].reshape(shape3)
    neg = -jnp.inf
    sub = lax.broadcasted_iota(jnp.int32, shape3, 1)
    grp = lax.broadcasted_iota(jnp.int32, shape3, 0)
    m1 = jnp.max(biased, axis=1, keepdims=True)
    i1 = jnp.min(jnp.where(biased == m1, sub, per_group), axis=1, keepdims=True)
    m2 = jnp.max(jnp.where(sub == i1, neg, biased), axis=1, keepdims=True)
    score = m1 + m2
    gidx = lax.broadcasted_iota(jnp.int32, score.shape, 0)
    keep = jnp.zeros(score.shape, F32)
    for _ in range(TOPK_GROUPS):
        m = jnp.max(score, axis=0, keepdims=True)
        im = jnp.min(jnp.where(score == m, gidx, N_GROUPS), axis=0, keepdims=True)
        sel = gidx == im
        keep = jnp.where(sel, 1.0, keep)
        score = jnp.where(sel, neg, score)
    work = jnp.where(jnp.broadcast_to(keep, shape3) > 0.0, biased, neg)
    eidx = grp * per_group + sub
    chosen = jnp.zeros(shape3, F32)
    for _ in range(TOP_K):
        m = jnp.max(jnp.max(work, axis=0, keepdims=True), axis=1, keepdims=True)
        cand = jnp.where(work == m, eidx, N_EXPERTS)
        im = jnp.min(jnp.min(cand, axis=0, keepdims=True), axis=1, keepdims=True)
        sel = eidx == im
        chosen = jnp.where(sel, 1.0, chosen)
        work = jnp.where(sel, neg, work)
    gate = aff * chosen
    den = jnp.sum(jnp.sum(gate, axis=0, keepdims=True), axis=1, keepdims=True)
    cw = (ROUTED_SCALE * gate / den).reshape(N_EXPERTS, tm)
    cw = jnp.concatenate([cw, jnp.zeros((LANE - N_EXPERTS, tm), F32)], axis=0)
    cw_ref[...] = cw.T


def _route(logits_t, bias, tm):
    ne, ntok = logits_t.shape
    bias_b = jnp.broadcast_to(bias.astype(F32)[:, None], (ne, tm))
    return pl.pallas_call(
        _route_kernel,
        out_shape=jax.ShapeDtypeStruct((ntok, LANE), F32),
        grid=(ntok // tm,),
        in_specs=[pl.BlockSpec((ne, tm), lambda i: (0, i)),
                  pl.BlockSpec((ne, tm), lambda i: (0, 0))],
        out_specs=pl.BlockSpec((tm, LANE), lambda i: (i, 0)),
        compiler_params=_params("parallel"),
        name="moe_route",
    )(logits_t, bias_b)


def _moe_kernel(h_ref, cw_ref, wg_ref, wu_ref, wd_ref, sg_ref, su_ref, sd_ref, x_ref, tab_ref,
                pg_ref, o_ref, acc_ref, *, epb):
    j = pl.program_id(1)
    h = h_ref[...]

    @pl.when(j == 0)
    def _():
        g = jnp.dot(h, sg_ref[...], preferred_element_type=F32)
        u = jnp.dot(h, su_ref[...], preferred_element_type=F32)
        acc_ref[...] = jnp.dot((_silu(g) * u).astype(BF16), sd_ref[...], preferred_element_type=F32)

    cw = cw_ref[...]
    lane = lax.broadcasted_iota(jnp.int32, cw.shape, 1)
    for e in range(epb):
        we = jnp.sum(jnp.where(lane == j * epb + e, cw, 0.0), axis=1, keepdims=True)
        g = jnp.dot(h, wg_ref[e], preferred_element_type=F32)
        u = jnp.dot(h, wu_ref[e], preferred_element_type=F32)
        a = (_silu(g) * u * we).astype(BF16)
        acc_ref[...] += jnp.dot(a, wd_ref[e], preferred_element_type=F32)

    @pl.when(j == pl.num_programs(1) - 1)
    def _():
        o_ref[...] = _gated_residual(x_ref[...], _rms(acc_ref[...], pg_ref[...]), tab_ref, 5)


def _moe(h, cw, x, tab, post_gain, wg, wu, wd, sg, su, sd, ntok_out, tm, epb):
    d = h.shape[1]
    ne, _, ff = wg.shape
    nsub = tm // ROW_TILE
    return pl.pallas_call(
        functools.partial(_moe_kernel, epb=epb),
        out_shape=jax.ShapeDtypeStruct((ntok_out, d), F32),
        grid=(ntok_out // tm, ne // epb),
        in_specs=[pl.BlockSpec((tm, d), lambda i, j: (i, 0)),
                  pl.BlockSpec((tm, LANE), lambda i, j: (i, 0)),
                  pl.BlockSpec((epb, d, ff), lambda i, j: (j, 0, 0)),
                  pl.BlockSpec((epb, d, ff), lambda i, j: (j, 0, 0)),
                  pl.BlockSpec((epb, ff, d), lambda i, j: (j, 0, 0)),
                  pl.BlockSpec(sg.shape, lambda i, j: (0, 0)),
                  pl.BlockSpec(su.shape, lambda i, j: (0, 0)),
                  pl.BlockSpec(sd.shape, lambda i, j: (0, 0)),
                  pl.BlockSpec((tm, d), lambda i, j: (i, 0)),
                  pl.BlockSpec((nsub, 8, d), lambda i, j: (i, 0, 0)),
                  pl.BlockSpec((1, d), lambda i, j: (0, 0))],
        out_specs=pl.BlockSpec((tm, d), lambda i, j: (i, 0)),
        scratch_shapes=[pltpu.VMEM((tm, d), F32)],
        compiler_params=_params("parallel", "arbitrary"),
        name="moe_experts",
    )(h, cw, wg, wu, wd, sg, su, sd, x, tab, post_gain)


def kernel(x, c, ctx, c_ctx, mod_w, mod_b, pre_gain, post_gain, ab_w_in, ab_conv_w, ab_conv_b, ab_lru_wa, ab_lru_ba, ab_lru_wi, ab_lru_bi, ab_lru_lambda, ab_gla_wa2, ab_gla_ba, ab_w_out, c_w_in, hgrn_lb_logits, c_w_out, moe_router, moe_bias, moe_w_gate, moe_w_up, moe_w_down, moe_ws_gate, moe_ws_up, moe_ws_down):
    batch, seq, d = x.shape
    n_ctx = ctx.shape[1]
    depth = mod_w.shape[0]
    assert n_ctx == ROW_TILE and seq % (ROW_TILE * 2) == 0 and d % LANE == 0
    assert depth == 2 and batch + 1 <= 8
    n_lat = batch * seq
    ntok = n_lat + batch * n_ctx
    lat_tiles = n_lat // ROW_TILE

    xs = jnp.concatenate([x.reshape(n_lat, d), ctx.reshape(batch * n_ctx, d)], axis=0)

    cvec = jnp.concatenate([c, c_ctx[None, :], jnp.zeros((8 - batch - 1, d), F32)], axis=0)
    mods = _modulation(cvec, mod_w, mod_b).reshape(depth, 8, N_MOD, d)
    tile_row = jnp.concatenate([jnp.repeat(jnp.arange(batch), seq // ROW_TILE),
                                jnp.full((batch,), batch)]).astype(jnp.int32)
    tabs = jnp.pad(mods[:, tile_row], ((0, 0), (0, 0), (0, 8 - N_MOD), (0, 0)))

    def moe_weights(layer):
        return (moe_w_gate[layer].astype(BF16), moe_w_up[layer].astype(BF16),
                moe_w_down[layer].astype(BF16), moe_ws_gate[layer].astype(BF16),
                moe_ws_up[layer].astype(BF16), moe_ws_down[layer].astype(BF16))

    tm_proj, tm_out, tm_route, tm_moe, epb = 512, 256, 512, 512, 4

    ab_cols = ab_w_in.shape[2]
    gla_dk = ab_gla_wa2.shape[3]
    lru_w = d
    gla_dv = (ab_cols - 2 * lru_w - 2 * gla_dk - 2 * GLA_RANK) // 2
    tn0 = 1792
    n0 = -(-(ab_cols) // tn0) * tn0
    w_in0 = jnp.pad(ab_w_in[0], ((0, 0), (0, n0 - ab_cols))).astype(BF16)
    u0 = _in_proj(xs, tabs[0], pre_gain[0, 0][None, :], w_in0, tm_proj, tn0)

    cw = jnp.pad(ab_conv_w[0], ((0, 8 - CONV_W), (0, 0)))
    cb = ab_conv_b[0][None, :]
    wg = jnp.concatenate([ab_lru_wa[0], ab_lru_wi[0]], axis=-1).astype(BF16)
    bg = jnp.concatenate([ab_lru_ba[0].reshape(2, LRU_HEADS, 1, LANE),
                          ab_lru_bi[0].reshape(2, LRU_HEADS, 1, LANE)], axis=-1)
    lru_params = (cw, cb, wg, bg, ab_lru_lambda[0])
    h_zero = jnp.zeros((batch, 2, lru_w), F32)
    ya, h_ctx = _lru_segment(u0, h_zero, lru_params, None, batch=batch, seg_len=n_ctx,
                             row_blk0=n_lat // n_ctx, g_col0=0, x_col0=lru_w // LANE)
    ya, _ = _lru_segment(u0, h_ctx, lru_params, ya, batch=batch, seg_len=seq,
                         row_blk0=0, g_col0=0, x_col0=lru_w // LANE)

    q0 = 2 * lru_w // LANE
    k0 = q0 + gla_dk // LANE
    v0 = k0 + gla_dk // LANE
    gate0 = v0 + gla_dv // LANE
    a0 = gate0 + gla_dv // LANE
    s_zero = jnp.zeros((batch, GLA_HEADS, gla_dv // GLA_HEADS, gla_dk // GLA_HEADS), F32)
    o_dir = []
    for dr in range(2):
        wa = jnp.zeros((LANE, gla_dk), F32).at[dr * GLA_RANK:(dr + 1) * GLA_RANK].set(ab_gla_wa2[0, dr])
        ba = ab_gla_ba[0, dr][None, :]
        o_c, s_c = _gla_segment(u0, s_zero, wa.astype(BF16), ba, None, batch=batch, seg_len=n_ctx,
                                row_blk0=lat_tiles, reverse=dr == 1, cols=(q0, k0, v0, a0))
        o_l, _ = _gla_segment(u0, s_c, wa.astype(BF16), ba, o_c, batch=batch, seg_len=seq,
                              row_blk0=0, reverse=dr == 1, cols=(q0, k0, v0, a0))
        o_dir.append(o_l)

    lead_specs = [pl.BlockSpec((tm_out, lru_w), lambda i: (i, 0)),
                  pl.BlockSpec((tm_out, gla_dv), lambda i: (i, 0)),
                  pl.BlockSpec((tm_out, gla_dv), lambda i: (i, 0)),
                  pl.BlockSpec((tm_out, gla_dv), lambda i: (i, gate0 * LANE // gla_dv))]
    xs, h2, lg = _out_proj(_out_proj_ab_kernel, (ya, o_dir[0], o_dir[1], u0), lead_specs,
                           ab_w_out[0].astype(BF16), xs, tabs[0], post_gain[0, 0][None, :],
                           pre_gain[0, 1][None, :], moe_router[0].T, tm_out)
    cwt = _route(lg, moe_bias[0], tm_route)
    xs = _moe(h2, cwt, xs, tabs[0], post_gain[0, 1][None, :], *moe_weights(0), ntok, tm_moe, epb)

    hd = c_w_out.shape[1]
    tn1 = 1280
    u1 = _in_proj(xs, tabs[1], pre_gain[1, 0][None, :], c_w_in[0].astype(BF16), tm_proj, tn1)
    c_cols = u1.shape[1]
    nsec = c_cols // hd
    lb_sm = jax.nn.softmax(hgrn_lb_logits.astype(F32), axis=0)
    lb_cum = jnp.cumsum(lb_sm, axis=0)
    lb = lb_cum[1] - lb_cum[0]
    log_lb = jnp.log(lb)[None, :]
    log_ub = jnp.log1p(-lb)[None, :]
    rows = seq // GRID_W
    u1_grid = u1.reshape(ntok // GRID_W, GRID_W * c_cols)
    dh = hd // HGRN_HEADS
    s_zero = jnp.zeros((batch, HGRN_HEADS, dh, dh), F32)
    o_dir = []
    for dr in range(2):
        o_c, s_c = _hgrn_segment(u1, s_zero, log_lb, log_ub, None, batch=batch, rows=n_ctx, nblk=1,
                                 row_blk0=n_lat // n_ctx, reverse=dr == 1,
                                 col_fn=lambda t, sec: sec, out_col_fn=lambda t: 0,
                                 out_shape=(ntok, hd))
        o_l, _ = _hgrn_segment(u1_grid, s_c, log_lb, log_ub,
                               o_c.reshape(ntok // GRID_W, GRID_W * hd), batch=batch, rows=rows,
                               nblk=GRID_W, row_blk0=0, reverse=dr == 1,
                               col_fn=lambda t, sec: t * nsec + sec, out_col_fn=lambda t: t,
                               out_shape=(ntok // GRID_W, GRID_W * hd))
        o_dir.append(o_l.reshape(ntok, hd))

    lead_specs = [pl.BlockSpec((tm_out, hd), lambda i: (i, 0)),
                  pl.BlockSpec((tm_out, hd), lambda i: (i, 0)),
                  pl.BlockSpec((tm_out, hd), lambda i: (i, nsec - 1))]
    xs, h2, lg = _out_proj(_out_proj_c_kernel, (o_dir[0], o_dir[1], u1), lead_specs,
                           c_w_out[0].astype(BF16), xs, tabs[1], post_gain[1, 0][None, :],
                           pre_gain[1, 1][None, :], moe_router[1].T, tm_out)
    cwt = _route(lg, moe_bias[1], tm_route)
    out = _moe(h2, cwt, xs, tabs[1], post_gain[1, 1][None, :], *moe_weights(1), n_lat, tm_moe, epb)
    return out.reshape(batch, seq, d)
```

```python
import functools

import jax
import jax.numpy as jnp
from jax import lax
from jax.experimental import pallas as pl
from jax.experimental.pallas import tpu as pltpu

F32 = jnp.float32
BF16 = jnp.bfloat16

EPS = 1e-6
N_MOD = 6
ROW_TILE = 256
CHUNK = 64
GRID_W = 64
LRU_HEADS = 8
LRU_C = 8.0
CONV_W = 4
GLA_HEADS = 4
GLA_RANK = 16
GLA_TAU = 16.0
HGRN_HEADS = 8
N_EXPERTS = 64
TOP_K = 8
N_GROUPS = 8
TOPK_GROUPS = 4
ROUTED_SCALE = 2.5
LANE = 128
SUBLANE = 8
BF16_ROWS = 16
VMEM_LIMIT = 48 * 1024 * 1024
VMEM_LIMIT_MOE = 56 * 1024 * 1024

_NT = (((1,), (1,)), ((), ()))
_TN = (((0,), (0,)), ((), ()))


def _params(*sem, vmem=VMEM_LIMIT):
    return pltpu.CompilerParams(dimension_semantics=sem, vmem_limit_bytes=vmem)


def _rms(x, gain):
    ms = jnp.mean(x * x, axis=-1, keepdims=True)
    return x * lax.rsqrt(ms + EPS) * gain


def _sigmoid(x):
    return jax.nn.sigmoid(x)


def _silu(x):
    return x * _sigmoid(x)


def _log_sigmoid(x):
    return jnp.minimum(x, 0.0) - jnp.log1p(jnp.exp(-jnp.abs(x)))


def _softplus(x):
    return jnp.maximum(x, 0.0) + jnp.log1p(jnp.exp(-jnp.abs(x)))


def _modulate(xn, tab_ref, shift_row, scale_row):
    parts = []
    for s in range(tab_ref.shape[0]):
        t = tab_ref[s]
        rows = xn[s * ROW_TILE:(s + 1) * ROW_TILE]
        parts.append(rows * (1.0 + t[scale_row:scale_row + 1]) + t[shift_row:shift_row + 1])
    return parts[0] if len(parts) == 1 else jnp.concatenate(parts, axis=0)


def _gated_residual(x, branch, tab_ref, gate_row):
    parts = []
    for s in range(tab_ref.shape[0]):
        sl = slice(s * ROW_TILE, (s + 1) * ROW_TILE)
        parts.append(x[sl] + tab_ref[s][gate_row:gate_row + 1] * branch[sl])
    return parts[0] if len(parts) == 1 else jnp.concatenate(parts, axis=0)


def _mod_kernel(c_ref, w_ref, b_ref, o_ref):
    s = _silu(c_ref[...])
    o_ref[0] = jnp.dot(s, w_ref[0], precision=lax.Precision.HIGHEST,
                       preferred_element_type=F32) + b_ref[0]


def _modulation(cvec, mod_w, mod_b):
    depth, d, n = mod_w.shape
    tn = 1024
    return pl.pallas_call(
        _mod_kernel,
        out_shape=jax.ShapeDtypeStruct((depth, 8, n), F32),
        grid=(depth, n // tn),
        in_specs=[pl.BlockSpec((8, d), lambda l, j: (0, 0)),
                  pl.BlockSpec((1, d, tn), lambda l, j: (l, 0, j)),
                  pl.BlockSpec((1, 1, tn), lambda l, j: (l, 0, j))],
        out_specs=pl.BlockSpec((1, 8, tn), lambda l, j: (l, 0, j)),
        compiler_params=_params("parallel", "parallel"),
        name="adaln_modulation",
    )(cvec, mod_w, mod_b.reshape(depth, 1, n))


def _in_proj_kernel(x_ref, tab_ref, gain_ref, w_ref, o_ref):
    h = _modulate(_rms(x_ref[...], gain_ref[...]), tab_ref, 0, 1)
    o_ref[...] = jnp.dot(h.astype(BF16), w_ref[...], preferred_element_type=F32).astype(o_ref.dtype)


def _in_proj(x, tab, gain, w, tm, tn, tile0, ntiles):
    d = x.shape[1]
    n = w.shape[1]
    nsub = tm // ROW_TILE
    return pl.pallas_call(
        _in_proj_kernel,
        out_shape=jax.ShapeDtypeStruct((ntiles * tm, n), BF16),
        grid=(ntiles, n // tn),
        in_specs=[pl.BlockSpec((tm, d), lambda i, j: (tile0 + i, 0)),
                  pl.BlockSpec((nsub, 8, d), lambda i, j: (tile0 + i, 0, 0)),
                  pl.BlockSpec((1, d), lambda i, j: (0, 0)),
                  pl.BlockSpec((d, tn), lambda i, j: (0, j))],
        out_specs=pl.BlockSpec((tm, tn), lambda i, j: (i, j)),
        compiler_params=_params("parallel", "parallel"),
        name="norm_mod_in_proj",
    )(x, tab, gain, w)


def _in_proj_grid_kernel(x_ref, tab_ref, gain_ref, p_ref, w_ref, o_ref, hp_scr):
    @pl.when(pl.program_id(1) == 0)
    def _():
        h = _modulate(_rms(x_ref[...], gain_ref[...]), tab_ref, 0, 1).astype(BF16)
        hp_scr[...] = jnp.dot(p_ref[...], h, preferred_element_type=F32).astype(BF16)

    o = jnp.dot(hp_scr[...], w_ref[...], preferred_element_type=F32).astype(o_ref.dtype)
    o_ref[...] = o.reshape(o_ref.shape)


def _in_proj_grid(x, tab, gain, perm, w, tn, ntiles, band, bands_per_image):
    d = x.shape[1]
    n = w.shape[1]
    tm = band * GRID_W
    nsub = tm // ROW_TILE
    images = ntiles // bands_per_image
    return pl.pallas_call(
        _in_proj_grid_kernel,
        out_shape=jax.ShapeDtypeStruct((images * GRID_W, bands_per_image, band, n), F32),
        grid=(ntiles, n // tn),
        in_specs=[pl.BlockSpec((tm, d), lambda i, j: (i, 0)),
                  pl.BlockSpec((nsub, 8, d), lambda i, j: (i, 0, 0)),
                  pl.BlockSpec((1, d), lambda i, j: (0, 0)),
                  pl.BlockSpec((tm, tm), lambda i, j: (0, 0)),
                  pl.BlockSpec((d, tn), lambda i, j: (0, j))],
        out_specs=pl.BlockSpec((GRID_W, None, band, tn),
                               lambda i, j: (i // bands_per_image, i % bands_per_image, 0, j)),
        scratch_shapes=[pltpu.VMEM((tm, d), BF16)],
        compiler_params=_params("parallel", "arbitrary"),
        name="norm_mod_in_proj_grid",
    )(x, tab, gain, perm, w)


def _lru_kernel(*refs, seg_len, aliased):
    if aliased:
        (g_ref, x_ref, h0_ref, cw_ref, cb_ref, wg_ref, bg_ref, lam_ref, _,
         y_ref, ht_ref, xc_scr, hf_scr) = refs
    else:
        (g_ref, x_ref, h0_ref, cw_ref, cb_ref, wg_ref, bg_ref, lam_ref,
         y_ref, ht_ref, xc_scr, hf_scr) = refs
    T = ROW_TILE
    H = BF16_ROWS
    nt = seg_len // T
    row = lax.broadcasted_iota(jnp.int32, (T, LANE), 0)
    cw = cw_ref[...]
    cb = cb_ref[...]
    sp = _softplus(-lam_ref[...])
    h0 = h0_ref[0]

    def conv_tile(t):
        t0 = pl.multiple_of(t * T, T)
        cur = x_ref[pl.ds(t0, T), :].astype(F32)
        p0 = pl.multiple_of(jnp.maximum(t0 - H, 0), H)
        n0 = pl.multiple_of(jnp.minimum(t0 + T, seg_len - H), H)
        pv = jnp.where(t > 0, 1.0, 0.0)
        nv = jnp.where(t < nt - 1, 1.0, 0.0)
        prev = x_ref[pl.ds(p0, H), :].astype(F32) * pv
        nxt = x_ref[pl.ds(n0, H), :].astype(F32) * nv
        xm1 = jnp.where(row == 0, prev[H - 1:H], pltpu.roll(cur, 1, 0))
        xm2 = jnp.where(row == 0, prev[H - 2:H - 1],
                        jnp.where(row == 1, prev[H - 1:H], pltpu.roll(cur, 2, 0)))
        xp1 = jnp.where(row == T - 1, nxt[0:1], pltpu.roll(cur, T - 1, 0))
        return cb + xm2 * cw[0:1] + xm1 * cw[1:2] + cur * cw[2:3] + xp1 * cw[3:4]

    def gates(xc, d):
        z = jnp.dot(xc.astype(BF16), wg_ref[d, 0], preferred_element_type=F32) + bg_ref[d, 0]
        r = _sigmoid(z[:, :LANE])
        i = _sigmoid(z[:, LANE:])
        log_a = (-LRU_C) * r * sp[d:d + 1]
        a = jnp.exp(log_a)
        return a, jnp.sqrt(1.0 - a * a) * (i * xc)

    def scan_tile(a, u, carry, reverse):
        s = 1
        while s < SUBLANE:
            if reverse:
                keep = row < T - s
                a_s = jnp.where(keep, pltpu.roll(a, T - s, 0), 1.0)
                u_s = jnp.where(keep, pltpu.roll(u, T - s, 0), 0.0)
            else:
                keep = row >= s
                a_s = jnp.where(keep, pltpu.roll(a, s, 0), 1.0)
                u_s = jnp.where(keep, pltpu.roll(u, s, 0), 0.0)
            u = u + a * u_s
            a = a * a_s
            s *= 2
        while s < T:
            if reverse:
                u = jnp.concatenate([u[:T - s] + a[:T - s] * u[s:], u[T - s:]], axis=0)
                a = jnp.concatenate([a[:T - s] * a[s:], a[T - s:]], axis=0)
            else:
                u = jnp.concatenate([u[:s], u[s:] + a[s:] * u[:T - s]], axis=0)
                a = jnp.concatenate([a[:s], a[s:] * a[:T - s]], axis=0)
            s *= 2
        return u + a * carry

    def fwd_body(t, carry):
        t0 = pl.multiple_of(t * T, T)
        xc = conv_tile(t)
        xc_scr[pl.ds(t0, T), :] = xc
        a, u = gates(xc, 0)
        h = scan_tile(a, u, carry, False)
        hf_scr[pl.ds(t0, T), :] = h
        return h[T - 1:T]

    def bwd_body(k, carry):
        t = nt - 1 - k
        t0 = pl.multiple_of(t * T, T)
        xc = xc_scr[pl.ds(t0, T), :]
        a, u = gates(xc, 1)
        h = scan_tile(a, u, carry, True)
        rec = hf_scr[pl.ds(t0, T), :] + h
        g = g_ref[pl.ds(t0, T), :].astype(F32)
        y_ref[pl.ds(t0, T), :] = (jax.nn.gelu(g) * rec).astype(y_ref.dtype)
        return h[0:1]

    hf = lax.fori_loop(0, nt, fwd_body, h0[0:1])
    hb = lax.fori_loop(0, nt, bwd_body, h0[1:2])
    ht_ref[0] = jnp.concatenate([hf, hb], axis=0)


def _lru_segment(u, h0, lru_w, y_prev, *, batch, seg_len, row_blk0, g_col0, x_col0):
    ntok = u.shape[0]
    width = LRU_HEADS * LANE
    cw, cb, wg, bg, lam = lru_w
    aliased = y_prev is not None
    in_specs = [pl.BlockSpec((seg_len, LANE), lambda b, h: (row_blk0 + b, g_col0 + h)),
                pl.BlockSpec((seg_len, LANE), lambda b, h: (row_blk0 + b, x_col0 + h)),
                pl.BlockSpec((1, 2, LANE), lambda b, h: (b, 0, h)),
                pl.BlockSpec((8, LANE), lambda b, h: (0, h)),
                pl.BlockSpec((1, LANE), lambda b, h: (0, h)),
                pl.BlockSpec((2, 1, LANE, 2 * LANE), lambda b, h: (0, h, 0, 0)),
                pl.BlockSpec((2, 1, 1, 2 * LANE), lambda b, h: (0, h, 0, 0)),
                pl.BlockSpec((2, LANE), lambda b, h: (0, h))]
    args = [u, u, h0, cw, cb, wg, bg, lam]
    aliases = {}
    if aliased:
        in_specs.append(pl.BlockSpec(memory_space=pl.ANY))
        args.append(y_prev)
        aliases = {len(args) - 1: 0}
    return pl.pallas_call(
        functools.partial(_lru_kernel, seg_len=seg_len, aliased=aliased),
        out_shape=(jax.ShapeDtypeStruct((ntok, width), BF16),
                   jax.ShapeDtypeStruct((batch, 2, width), F32)),
        grid=(batch, LRU_HEADS),
        in_specs=in_specs,
        out_specs=(pl.BlockSpec((seg_len, LANE), lambda b, h: (row_blk0 + b, h)),
                   pl.BlockSpec((1, 2, LANE), lambda b, h: (b, 0, h))),
        scratch_shapes=[pltpu.VMEM((seg_len, LANE), F32), pltpu.VMEM((seg_len, LANE), F32)],
        input_output_aliases=aliases,
        compiler_params=_params("parallel", "parallel"),
        name="rglru_segment",
    )(*args)


def _chunk_cumsum(g, reverse):
    rows = g.shape[0]
    pos = lax.broadcasted_iota(jnp.int32, g.shape, 0) & (CHUNK - 1)
    b = g
    s = 1
    while s < CHUNK:
        if reverse:
            b = b + jnp.where(pos < CHUNK - s, pltpu.roll(b, rows - s, 0), 0.0)
        else:
            b = b + jnp.where(pos >= s, pltpu.roll(b, s, 0), 0.0)
        s *= 2
    return b


def _chunk_step(q, k, v, b, st, reverse, tri):
    iref = CHUNK // 2 if reverse else CHUNK // 2 - 1
    ilast = 0 if reverse else CHUNK - 1
    b_ref = b[iref:iref + 1]
    b_last = b[ilast:ilast + 1]
    qe = (q * jnp.exp(b - b_ref)).astype(BF16)
    ke = (k * jnp.exp(b_ref - b)).astype(BF16)
    sc = lax.dot_general(qe, ke, _NT, preferred_element_type=F32)
    sc = jnp.where(tri, sc, 0.0).astype(BF16)
    vb = v.astype(BF16)
    o = jnp.dot(sc, vb, preferred_element_type=F32)
    qd = (q * jnp.exp(b)).astype(BF16)
    o = o + lax.dot_general(qd, st.astype(BF16), _NT, preferred_element_type=F32)
    kd = (k * jnp.exp(b_last - b)).astype(BF16)
    st = st * jnp.exp(b_last) + lax.dot_general(vb, kd, _TN, preferred_element_type=F32)
    return o, st


def _chunk_scan(q, k, v, b, st_scr, o_ref, *, heads, dk, dv, reverse):
    rows = q.shape[0]
    nchunk = rows // CHUNK
    r = lax.broadcasted_iota(jnp.int32, (CHUNK, CHUNK), 0)
    c = lax.broadcasted_iota(jnp.int32, (CHUNK, CHUNK), 1)
    tri = (c >= r) if reverse else (c <= r)
    order = range(nchunk - 1, -1, -1) if reverse else range(nchunk)
    for h in range(heads):
        st = st_scr[h]
        for ci in order:
            rs = slice(ci * CHUNK, (ci + 1) * CHUNK)
            ks = slice(h * dk, (h + 1) * dk)
            vs = slice(h * dv, (h + 1) * dv)
            o, st = _chunk_step(q[rs, ks], k[rs, ks], v[rs, vs], b[rs, ks], st, reverse, tri)
            if o_ref is not None:
                o_ref[rs, vs] = o.astype(o_ref.dtype)
        st_scr[h] = st


def _gla_kernel(*refs, reverse, aliased):
    if aliased:
        q_ref, k_ref, v_ref, a_ref, wa_ref, ba_ref, s0_ref, _, o_ref, st_ref, st_scr = refs
    else:
        q_ref, k_ref, v_ref, a_ref, wa_ref, ba_ref, s0_ref, o_ref, st_ref, st_scr = refs
    t = pl.program_id(1)

    @pl.when(t == 0)
    def _():
        st_scr[...] = s0_ref[0]

    dk = q_ref.shape[1] // GLA_HEADS
    dv = v_ref.shape[1] // GLA_HEADS
    z = jnp.dot(a_ref[...], wa_ref[...], preferred_element_type=F32) + ba_ref[...]
    g = _log_sigmoid(z) * (1.0 / GLA_TAU)
    b = _chunk_cumsum(g, reverse)
    q = q_ref[...].astype(F32) * (dk ** -0.5)
    _chunk_scan(q, k_ref[...].astype(F32), v_ref[...], b, st_scr, o_ref,
                heads=GLA_HEADS, dk=dk, dv=dv, reverse=reverse)

    @pl.when(t == pl.num_programs(1) - 1)
    def _():
        st_ref[0] = st_scr[...]


def _gla_segment(u, s0, wa, ba, o_prev, *, batch, seg_len, row_blk0, reverse, cols):
    ntok = u.shape[0]
    q0, k0, v0, a0 = cols
    dkt = wa.shape[1]
    dvt = s0.shape[1] * s0.shape[2]
    nblk = seg_len // ROW_TILE
    aliased = o_prev is not None

    def blk(b, t):
        return row_blk0 + b * nblk + (nblk - 1 - t if reverse else t)

    in_specs = [pl.BlockSpec((ROW_TILE, dkt), lambda b, t: (blk(b, t), q0 * LANE // dkt)),
                pl.BlockSpec((ROW_TILE, dkt), lambda b, t: (blk(b, t), k0 * LANE // dkt)),
                pl.BlockSpec((ROW_TILE, dvt), lambda b, t: (blk(b, t), v0 * LANE // dvt)),
                pl.BlockSpec((ROW_TILE, LANE), lambda b, t: (blk(b, t), a0)),
                pl.BlockSpec((LANE, dkt), lambda b, t: (0, 0)),
                pl.BlockSpec((1, dkt), lambda b, t: (0, 0)),
                pl.BlockSpec((1,) + s0.shape[1:], lambda b, t: (b, 0, 0, 0))]
    args = [u, u, u, u, wa, ba, s0]
    aliases = {}
    if aliased:
        in_specs.append(pl.BlockSpec(memory_space=pl.ANY))
        args.append(o_prev)
        aliases = {len(args) - 1: 0}
    return pl.pallas_call(
        functools.partial(_gla_kernel, reverse=reverse, aliased=aliased),
        out_shape=(jax.ShapeDtypeStruct((ntok, dvt), BF16),
                   jax.ShapeDtypeStruct(s0.shape, F32)),
        grid=(batch, nblk),
        in_specs=in_specs,
        out_specs=(pl.BlockSpec((ROW_TILE, dvt), lambda b, t: (blk(b, t), 0)),
                   pl.BlockSpec((1,) + s0.shape[1:], lambda b, t: (b, 0, 0, 0))),
        scratch_shapes=[pltpu.VMEM(s0.shape[1:], F32)],
        input_output_aliases=aliases,
        compiler_params=_params("parallel", "arbitrary"),
        name="gla_segment",
    )(*args)


def _hgrn_kernel(*refs, reverse, emit_o):
    if emit_o:
        q_ref, z_ref, v_ref, lb_ref, ub_ref, s0_ref, o_ref, st_ref, st_scr = refs
    else:
        q_ref, z_ref, v_ref, lb_ref, ub_ref, s0_ref, st_ref, st_scr = refs
        o_ref = None
    t = pl.program_id(1)

    @pl.when(t == 0)
    def _():
        st_scr[...] = s0_ref[0]

    d = q_ref.shape[1] // HGRN_HEADS
    z = z_ref[...].astype(F32)
    log_lb = lb_ref[...]
    log_ub = ub_ref[...]
    y = log_ub + _log_sigmoid(z)
    log_f = jnp.maximum(log_lb, y) + jnp.log1p(jnp.exp(-jnp.abs(log_lb - y)))
    k = jnp.exp(log_ub) * _sigmoid(-z)
    b = _chunk_cumsum(log_f, reverse)
    _chunk_scan(q_ref[...].astype(F32), k, v_ref[...], b, st_scr, o_ref,
                heads=HGRN_HEADS, dk=d, dv=d, reverse=reverse)

    @pl.when(t == pl.num_programs(1) - 1)
    def _():
        st_ref[0] = st_scr[...]


def _hgrn_segment(u, s0, log_lb, log_ub, *, batch, seg_len, reverse, emit_o):
    ntok = u.shape[0]
    width = log_lb.shape[1]
    nblk = seg_len // ROW_TILE
    zsec = 2 if reverse else 1

    def blk(b, t):
        return b * nblk + (nblk - 1 - t if reverse else t)

    st_spec = pl.BlockSpec((1,) + s0.shape[1:], lambda b, t: (b, 0, 0, 0))
    in_specs = [pl.BlockSpec((ROW_TILE, width), lambda b, t: (blk(b, t), 0)),
                pl.BlockSpec((ROW_TILE, width), lambda b, t: (blk(b, t), zsec)),
                pl.BlockSpec((ROW_TILE, width), lambda b, t: (blk(b, t), 3)),
                pl.BlockSpec((1, width), lambda b, t: (0, 0)),
                pl.BlockSpec((1, width), lambda b, t: (0, 0)),
                st_spec]
    st_shape = jax.ShapeDtypeStruct(s0.shape, F32)
    if emit_o:
        out_shape = (jax.ShapeDtypeStruct((ntok, width), F32), st_shape)
        out_specs = (pl.BlockSpec((ROW_TILE, width), lambda b, t: (blk(b, t), 0)), st_spec)
    else:
        out_shape = (st_shape,)
        out_specs = (st_spec,)
    return pl.pallas_call(
        functools.partial(_hgrn_kernel, reverse=reverse, emit_o=emit_o),
        out_shape=out_shape,
        grid=(batch, nblk),
        in_specs=in_specs,
        out_specs=out_specs,
        scratch_shapes=[pltpu.VMEM(s0.shape[1:], F32)],
        compiler_params=_params("parallel", "arbitrary"),
        name="hgrn_segment",
    )(u, u, u, log_lb, log_ub, s0)


def _head_norm_gate(o, gate, heads):
    dh = o.shape[1] // heads
    parts = []
    for h in range(heads):
        oh = o[:, h * dh:(h + 1) * dh]
        ms = jnp.mean(oh * oh, axis=-1, keepdims=True)
        parts.append((oh * lax.rsqrt(ms + EPS) * _silu(gate[:, h * dh:(h + 1) * dh])).astype(BF16))
    return jnp.concatenate(parts, axis=1)


def _out_proj_tail(o, x_ref, tab_ref, pg_ref, ng_ref, rt_ref, xo_ref, h_ref, lg_ref):
    xn = _gated_residual(x_ref[...], _rms(o, pg_ref[...]), tab_ref, 2)
    xo_ref[...] = xn
    h2 = _modulate(_rms(xn, ng_ref[...]), tab_ref, 3, 4)
    h_ref[...] = h2.astype(BF16)
    lg_ref[...] = lax.dot_general(rt_ref[...], h2, _NT, precision=lax.Precision.HIGHEST,
                                  preferred_element_type=F32)


def _out_proj_ab_kernel(ya_ref, of_ref, ob_ref, gate_ref, w_ref, x_ref, tab_ref, pg_ref, ng_ref,
                        rt_ref, xo_ref, h_ref, lg_ref):
    o_att = of_ref[...].astype(F32) + ob_ref[...].astype(F32)
    yb = _head_norm_gate(o_att, gate_ref[...].astype(F32), GLA_HEADS)
    da = ya_ref.shape[1]
    o = jnp.dot(ya_ref[...], w_ref[:da], preferred_element_type=F32)
    o = o + jnp.dot(yb, w_ref[da:], preferred_element_type=F32)
    _out_proj_tail(o, x_ref, tab_ref, pg_ref, ng_ref, rt_ref, xo_ref, h_ref, lg_ref)


def _out_proj_c_kernel(of_ref, ob_ref, gate_ref, p_ref, w_ref, x_ref, tab_ref, pg_ref, ng_ref,
                       rt_ref, xo_ref, h_ref, lg_ref):
    tm = x_ref.shape[0]
    width = of_ref.shape[-1]
    o_att = of_ref[...].astype(F32) + ob_ref[...].astype(F32)
    y = _head_norm_gate(o_att.reshape(tm, width), gate_ref[...].astype(F32).reshape(tm, width),
                        HGRN_HEADS)
    y = jnp.dot(p_ref[...], y, preferred_element_type=F32).astype(BF16)
    o = jnp.dot(y, w_ref[...], preferred_element_type=F32)
    _out_proj_tail(o, x_ref, tab_ref, pg_ref, ng_ref, rt_ref, xo_ref, h_ref, lg_ref)


def _out_proj(kernel, lead, lead_specs, w, x, tab, post_gain, next_gain, router_t, tm, ntiles):
    d = x.shape[1]
    ntok = ntiles * tm
    nsub = tm // ROW_TILE
    ne = router_t.shape[0]
    in_specs = list(lead_specs) + [
        pl.BlockSpec(w.shape, lambda i: (0, 0)),
        pl.BlockSpec((tm, d), lambda i: (i, 0)),
        pl.BlockSpec((nsub, 8, d), lambda i: (i, 0, 0)),
        pl.BlockSpec((1, d), lambda i: (0, 0)),
        pl.BlockSpec((1, d), lambda i: (0, 0)),
        pl.BlockSpec((ne, d), lambda i: (0, 0))]
    return pl.pallas_call(
        kernel,
        out_shape=(jax.ShapeDtypeStruct((ntok, d), F32),
                   jax.ShapeDtypeStruct((ntok, d), BF16),
                   jax.ShapeDtypeStruct((ne, ntok), F32)),
        grid=(ntiles,),
        in_specs=in_specs,
        out_specs=(pl.BlockSpec((tm, d), lambda i: (i, 0)),
                   pl.BlockSpec((tm, d), lambda i: (i, 0)),
                   pl.BlockSpec((ne, tm), lambda i: (0, i))),
        compiler_params=_params("parallel"),
        name="out_proj_norm_residual",
    )(*lead, w, x, tab, post_gain, next_gain, router_t)


def _route_kernel(lg_ref, bias_ref, cw_ref):
    tm = lg_ref.shape[1]
    per_group = N_EXPERTS // N_GROUPS
    shape3 = (N_GROUPS, per_group, tm)
    aff = _sigmoid(lg_ref[...]).reshape(shape3)
    biased = aff + bias_ref[...].reshape(shape3)
    neg = -jnp.inf
    sub = lax.broadcasted_iota(jnp.int32, shape3, 1)
    grp = lax.broadcasted_iota(jnp.int32, shape3, 0)
    m1 = jnp.max(biased, axis=1, keepdims=True)
    i1 = jnp.min(jnp.where(biased == m1, sub, per_group), axis=1, keepdims=True)
    m2 = jnp.max(jnp.where(sub == i1, neg, biased), axis=1, keepdims=True)
    score = m1 + m2
    gidx = lax.broadcasted_iota(jnp.int32, score.shape, 0)
    keep = jnp.zeros(score.shape, F32)
    for _ in range(TOPK_GROUPS):
        m = jnp.max(score, axis=0, keepdims=True)
        im = jnp.min(jnp.where(score == m, gidx, N_GROUPS), axis=0, keepdims=True)
        sel = gidx == im
        keep = jnp.where(sel, 1.0, keep)
        score = jnp.where(sel, neg, score)
    work = jnp.where(jnp.broadcast_to(keep, shape3) > 0.0, biased, neg)
    eidx = grp * per_group + sub
    chosen = jnp.zeros(shape3, F32)
    for _ in range(TOP_K):
        m = jnp.max(jnp.max(work, axis=0, keepdims=True), axis=1, keepdims=True)
        cand = jnp.where(work == m, eidx, N_EXPERTS)
        im = jnp.min(jnp.min(cand, axis=0, keepdims=True), axis=1, keepdims=True)
        sel = eidx == im
        chosen = jnp.where(sel, 1.0, chosen)
        work = jnp.where(sel, neg, work)
    gate = aff * chosen
    den = jnp.sum(jnp.sum(gate, axis=0, keepdims=True), axis=1, keepdims=True)
    cw = (ROUTED_SCALE * gate / den).reshape(N_EXPERTS, tm)
    cw = jnp.concatenate([cw, jnp.zeros((LANE - N_EXPERTS, tm), F32)], axis=0)
    cw_ref[...] = cw.T


def _route(logits_t, bias, tm):
    ne, ntok = logits_t.shape
    bias_b = jnp.broadcast_to(bias.astype(F32)[:, None], (ne, tm))
    return pl.pallas_call(
        _route_kernel,
        out_shape=jax.ShapeDtypeStruct((ntok, LANE), F32),
        grid=(ntok // tm,),
        in_specs=[pl.BlockSpec((ne, tm), lambda i: (0, i)),
                  pl.BlockSpec((ne, tm), lambda i: (0, 0))],
        out_specs=pl.BlockSpec((tm, LANE), lambda i: (i, 0)),
        compiler_params=_params("parallel"),
        name="moe_route",
    )(logits_t, bias_b)


def _moe_kernel(h_ref, cw_ref, wg_ref, wu_ref, wd_ref, sg_ref, su_ref, sd_ref, x_ref, tab_ref,
                pg_ref, o_ref, acc_ref, *, epb):
    j = pl.program_id(1)
    h = h_ref[...]

    @pl.when(j == 0)
    def _():
        g = jnp.dot(h, sg_ref[...], preferred_element_type=F32)
        u = jnp.dot(h, su_ref[...], preferred_element_type=F32)
        acc_ref[...] = jnp.dot((_silu(g) * u).astype(BF16), sd_ref[...], preferred_element_type=F32)

    cw = cw_ref[...]
    lane = lax.broadcasted_iota(jnp.int32, cw.shape, 1)
    for e in range(epb):
        we = jnp.sum(jnp.where(lane == j * epb + e, cw, 0.0), axis=1, keepdims=True)
        g = jnp.dot(h, wg_ref[e], preferred_element_type=F32)
        u = jnp.dot(h, wu_ref[e], preferred_element_type=F32)
        a = (_silu(g) * u * we).astype(BF16)
        acc_ref[...] += jnp.dot(a, wd_ref[e], preferred_element_type=F32)

    @pl.when(j == pl.num_programs(1) - 1)
    def _():
        o_ref[...] = _gated_residual(x_ref[...], _rms(acc_ref[...], pg_ref[...]), tab_ref, 5)


def _moe(h, cw, x, tab, post_gain, wg, wu, wd, sg, su, sd, ntok_out, tm, epb):
    d = h.shape[1]
    ne, _, ff = wg.shape
    nsub = tm // ROW_TILE
    once = pl.Buffered(1)
    return pl.pallas_call(
        functools.partial(_moe_kernel, epb=epb),
        out_shape=jax.ShapeDtypeStruct((ntok_out, d), F32),
        grid=(ntok_out // tm, ne // epb),
        in_specs=[pl.BlockSpec((tm, d), lambda i, j: (i, 0)),
                  pl.BlockSpec((tm, LANE), lambda i, j: (i, 0)),
                  pl.BlockSpec((epb, d, ff), lambda i, j: (j, 0, 0)),
                  pl.BlockSpec((epb, d, ff), lambda i, j: (j, 0, 0)),
                  pl.BlockSpec((epb, ff, d), lambda i, j: (j, 0, 0)),
                  pl.BlockSpec(sg.shape, lambda i, j: (0, 0), pipeline_mode=once),
                  pl.BlockSpec(su.shape, lambda i, j: (0, 0), pipeline_mode=once),
                  pl.BlockSpec(sd.shape, lambda i, j: (0, 0), pipeline_mode=once),
                  pl.BlockSpec((tm, d), lambda i, j: (i, 0), pipeline_mode=once),
                  pl.BlockSpec((nsub, 8, d), lambda i, j: (i, 0, 0)),
                  pl.BlockSpec((1, d), lambda i, j: (0, 0))],
        out_specs=pl.BlockSpec((tm, d), lambda i, j: (i, 0)),
        scratch_shapes=[pltpu.VMEM((tm, d), F32)],
        compiler_params=_params("parallel", "arbitrary", vmem=VMEM_LIMIT_MOE),
        name="moe_experts",
    )(h, cw, wg, wu, wd, sg, su, sd, x, tab, post_gain)


def _grid_permutation(band):
    tm = band * GRID_W
    dst = jnp.arange(tm)
    src = (dst % band) * GRID_W + dst // band
    return (src[:, None] == jnp.arange(tm)[None, :]).astype(BF16)


def kernel(x, c, ctx, c_ctx, mod_w, mod_b, pre_gain, post_gain, ab_w_in, ab_conv_w, ab_conv_b, ab_lru_wa, ab_lru_ba, ab_lru_wi, ab_lru_bi, ab_lru_lambda, ab_gla_wa2, ab_gla_ba, ab_w_out, c_w_in, hgrn_lb_logits, c_w_out, moe_router, moe_bias, moe_w_gate, moe_w_up, moe_w_down, moe_ws_gate, moe_ws_up, moe_ws_down):
    batch, seq, d = x.shape
    n_ctx = ctx.shape[1]
    depth = mod_w.shape[0]
    band = SUBLANE
    tm_grid = band * GRID_W
    rows_img = seq // GRID_W
    assert n_ctx == ROW_TILE and seq % (2 * ROW_TILE) == 0 and d % LANE == 0
    assert depth == 2 and batch + 1 <= 8 and rows_img % band == 0 and tm_grid % ROW_TILE == 0
    n_lat = batch * seq
    n_ctx_tok = batch * n_ctx
    ntok = n_lat + n_ctx_tok
    lat_tiles = n_lat // ROW_TILE

    xs = jnp.concatenate([x.reshape(n_lat, d), ctx.reshape(n_ctx_tok, d)], axis=0)

    cvec = jnp.concatenate([c, c_ctx[None, :], jnp.zeros((8 - batch - 1, d), F32)], axis=0)
    mods = _modulation(cvec, mod_w, mod_b).reshape(depth, 8, N_MOD, d)
    tile_row = jnp.concatenate([jnp.repeat(jnp.arange(batch), seq // ROW_TILE),
                                jnp.full((batch,), batch)]).astype(jnp.int32)
    tabs = jnp.pad(mods[:, tile_row], ((0, 0), (0, 0), (0, 8 - N_MOD), (0, 0)))

    def moe_weights(layer):
        return (moe_w_gate[layer].astype(BF16), moe_w_up[layer].astype(BF16),
                moe_w_down[layer].astype(BF16), moe_ws_gate[layer].astype(BF16),
                moe_ws_up[layer].astype(BF16), moe_ws_down[layer].astype(BF16))

    tm_proj, tm_out, tm_route, tm_moe, epb = 512, 256, 512, 1024, 4

    ab_cols = ab_w_in.shape[2]
    gla_dk = ab_gla_wa2.shape[3]
    lru_w = d
    gla_dv = (ab_cols - 2 * lru_w - 2 * gla_dk - 2 * GLA_RANK) // 2
    tn0 = 1792
    n0 = -(-(ab_cols) // tn0) * tn0
    w_in0 = jnp.pad(ab_w_in[0], ((0, 0), (0, n0 - ab_cols))).astype(BF16)
    u0 = _in_proj(xs, tabs[0], pre_gain[0, 0][None, :], w_in0, tm_proj, tn0, 0, ntok // tm_proj)

    cw = jnp.pad(ab_conv_w[0], ((0, 8 - CONV_W), (0, 0)))
    cb = ab_conv_b[0][None, :]
    wg = jnp.concatenate([ab_lru_wa[0], ab_lru_wi[0]], axis=-1).astype(BF16)
    bg = jnp.concatenate([ab_lru_ba[0].reshape(2, LRU_HEADS, 1, LANE),
                          ab_lru_bi[0].reshape(2, LRU_HEADS, 1, LANE)], axis=-1)
    lru_params = (cw, cb, wg, bg, ab_lru_lambda[0])
    h_zero = jnp.zeros((batch, 2, lru_w), F32)
    ya, h_ctx = _lru_segment(u0, h_zero, lru_params, None, batch=batch, seg_len=n_ctx,
                             row_blk0=n_lat // n_ctx, g_col0=0, x_col0=lru_w // LANE)
    ya, _ = _lru_segment(u0, h_ctx, lru_params, ya, batch=batch, seg_len=seq,
                         row_blk0=0, g_col0=0, x_col0=lru_w // LANE)

    q0 = 2 * lru_w // LANE
    k0 = q0 + gla_dk // LANE
    v0 = k0 + gla_dk // LANE
    gate0 = v0 + gla_dv // LANE
    a0 = gate0 + gla_dv // LANE
    s_zero = jnp.zeros((batch, GLA_HEADS, gla_dv // GLA_HEADS, gla_dk // GLA_HEADS), F32)
    o_dir = []
    for dr in range(2):
        wa = jnp.zeros((LANE, gla_dk), F32).at[dr * GLA_RANK:(dr + 1) * GLA_RANK].set(ab_gla_wa2[0, dr])
        ba = ab_gla_ba[0, dr][None, :]
        o_c, s_c = _gla_segment(u0, s_zero, wa.astype(BF16), ba, None, batch=batch, seg_len=n_ctx,
                                row_blk0=lat_tiles, reverse=dr == 1, cols=(q0, k0, v0, a0))
        o_l, _ = _gla_segment(u0, s_c, wa.astype(BF16), ba, o_c, batch=batch, seg_len=seq,
                              row_blk0=0, reverse=dr == 1, cols=(q0, k0, v0, a0))
        o_dir.append(o_l)

    lead_specs = [pl.BlockSpec((tm_out, lru_w), lambda i: (i, 0)),
                  pl.BlockSpec((tm_out, gla_dv), lambda i: (i, 0)),
                  pl.BlockSpec((tm_out, gla_dv), lambda i: (i, 0)),
                  pl.BlockSpec((tm_out, gla_dv), lambda i: (i, gate0 * LANE // gla_dv))]
    xs, h2, lg = _out_proj(_out_proj_ab_kernel, (ya, o_dir[0], o_dir[1], u0), lead_specs,
                           ab_w_out[0].astype(BF16), xs, tabs[0], post_gain[0, 0][None, :],
                           pre_gain[0, 1][None, :], moe_router[0].T, tm_out, ntok // tm_out)
    cwt = _route(lg, moe_bias[0], tm_route)
    xs = _moe(h2, cwt, xs, tabs[0], post_gain[0, 1][None, :], *moe_weights(0), ntok, tm_moe, epb)

    hd = c_w_out.shape[1]
    tn1 = 1280
    w_in1 = c_w_in[0].astype(BF16)
    c_cols = w_in1.shape[1]
    nsec = c_cols // hd
    perm = _grid_permutation(band)
    gain1 = pre_gain[1, 0][None, :]
    u1l = _in_proj_grid(xs, tabs[1], gain1, perm, w_in1, tn1, n_lat // tm_grid, band,
                        rows_img // band)
    u1c = _in_proj(xs, tabs[1], gain1, w_in1, tm_proj, tn1, n_lat // tm_proj, n_ctx_tok // tm_proj)
    lb_sm = jax.nn.softmax(hgrn_lb_logits.astype(F32), axis=0)
    lb_cum = jnp.cumsum(lb_sm, axis=0)
    lb = lb_cum[1] - lb_cum[0]
    log_lb = jnp.log(lb)[None, :]
    log_ub = jnp.log1p(-lb)[None, :]
    dh = hd // HGRN_HEADS
    s_zero = jnp.zeros((batch, HGRN_HEADS, dh, dh), F32)
    u1l_flat = u1l.reshape(n_lat, c_cols)
    o_dir = []
    for dr in range(2):
        (s_c,) = _hgrn_segment(u1c, s_zero, log_lb, log_ub, batch=batch, seg_len=n_ctx,
                               reverse=dr == 1, emit_o=False)
        o_l, _ = _hgrn_segment(u1l_flat, s_c, log_lb, log_ub, batch=batch, seg_len=seq,
                               reverse=dr == 1, emit_o=True)
        o_dir.append(o_l.reshape(n_lat // (rows_img), rows_img // band, band, hd))

    bpi = rows_img // band
    grid_blk = (GRID_W, None, band, hd)
    lead_specs = [pl.BlockSpec(grid_blk, lambda i: (i // bpi, i % bpi, 0, 0)),
                  pl.BlockSpec(grid_blk, lambda i: (i // bpi, i % bpi, 0, 0)),
                  pl.BlockSpec(grid_blk, lambda i: (i // bpi, i % bpi, 0, nsec - 1)),
                  pl.BlockSpec((tm_grid, tm_grid), lambda i: (0, 0))]
    xs, h2, lg = _out_proj(_out_proj_c_kernel, (o_dir[0], o_dir[1], u1l, perm.T), lead_specs,
                           c_w_out[0].astype(BF16), xs, tabs[1], post_gain[1, 0][None, :],
                           pre_gain[1, 1][None, :], moe_router[1].T, tm_grid, n_lat // tm_grid)
    cwt = _route(lg, moe_bias[1], tm_route)
    out = _moe(h2, cwt, xs, tabs[1], post_gain[1, 1][None, :], *moe_weights(1), n_lat, tm_moe, epb)
    return out.reshape(batch, seq, d)
```

```python
import functools

import jax
import jax.numpy as jnp
from jax import lax
from jax.experimental import pallas as pl
from jax.experimental.pallas import tpu as pltpu

F32 = jnp.float32
BF16 = jnp.bfloat16

EPS = 1e-6
N_MOD = 6
ROW_TILE = 256
CHUNK = 64
GRID_W = 64
LRU_HEADS = 8
LRU_C = 8.0
CONV_W = 4
GLA_HEADS = 4
GLA_RANK = 16
GLA_TAU = 16.0
HGRN_HEADS = 8
N_EXPERTS = 64
TOP_K = 8
N_GROUPS = 8
TOPK_GROUPS = 4
ROUTED_SCALE = 2.5
LANE = 128
SUBLANE = 8
BF16_ROWS = 16
VMEM_LIMIT = 48 * 1024 * 1024
VMEM_LIMIT_MOE = 56 * 1024 * 1024

_NT = (((1,), (1,)), ((), ()))
_TN = (((0,), (0,)), ((), ()))


def _params(*sem, vmem=VMEM_LIMIT):
    return pltpu.CompilerParams(dimension_semantics=sem, vmem_limit_bytes=vmem)


def _rms(x, gain):
    ms = jnp.mean(x * x, axis=-1, keepdims=True)
    return x * lax.rsqrt(ms + EPS) * gain


def _sigmoid(x):
    return jax.nn.sigmoid(x)


def _silu(x):
    return x * _sigmoid(x)


def _log_sigmoid(x):
    return jnp.minimum(x, 0.0) - jnp.log1p(jnp.exp(-jnp.abs(x)))


def _softplus(x):
    return jnp.maximum(x, 0.0) + jnp.log1p(jnp.exp(-jnp.abs(x)))


def _modulate(xn, tab_ref, shift_row, scale_row):
    parts = []
    for s in range(tab_ref.shape[0]):
        t = tab_ref[s]
        rows = xn[s * ROW_TILE:(s + 1) * ROW_TILE]
        parts.append(rows * (1.0 + t[scale_row:scale_row + 1]) + t[shift_row:shift_row + 1])
    return parts[0] if len(parts) == 1 else jnp.concatenate(parts, axis=0)


def _gated_residual(x, branch, tab_ref, gate_row):
    parts = []
    for s in range(tab_ref.shape[0]):
        sl = slice(s * ROW_TILE, (s + 1) * ROW_TILE)
        parts.append(x[sl] + tab_ref[s][gate_row:gate_row + 1] * branch[sl])
    return parts[0] if len(parts) == 1 else jnp.concatenate(parts, axis=0)


def _mod_kernel(c_ref, w_ref, b_ref, o_ref):
    s = _silu(c_ref[...])
    o_ref[0] = jnp.dot(s, w_ref[0], precision=lax.Precision.HIGHEST,
                       preferred_element_type=F32) + b_ref[0]


def _modulation(cvec, mod_w, mod_b):
    depth, d, n = mod_w.shape
    tn = 1024
    return pl.pallas_call(
        _mod_kernel,
        out_shape=jax.ShapeDtypeStruct((depth, 8, n), F32),
        grid=(depth, n // tn),
        in_specs=[pl.BlockSpec((8, d), lambda l, j: (0, 0)),
                  pl.BlockSpec((1, d, tn), lambda l, j: (l, 0, j)),
                  pl.BlockSpec((1, 1, tn), lambda l, j: (l, 0, j))],
        out_specs=pl.BlockSpec((1, 8, tn), lambda l, j: (l, 0, j)),
        compiler_params=_params("parallel", "parallel"),
        name="adaln_modulation",
    )(cvec, mod_w, mod_b.reshape(depth, 1, n))


def _in_proj_kernel(x_ref, tab_ref, gain_ref, w_ref, o_ref, *, tn):
    h = _modulate(_rms(x_ref[...], gain_ref[...]), tab_ref, 0, 1).astype(BF16)
    for j in range(w_ref.shape[1] // tn):
        cs = slice(j * tn, (j + 1) * tn)
        o_ref[:, cs] = jnp.dot(h, w_ref[:, cs], preferred_element_type=F32).astype(o_ref.dtype)


def _in_proj(x, tab, gain, w, tm, tn, tile0, ntiles):
    d = x.shape[1]
    n = w.shape[1]
    nsub = tm // ROW_TILE
    return pl.pallas_call(
        functools.partial(_in_proj_kernel, tn=tn),
        out_shape=jax.ShapeDtypeStruct((ntiles * tm, n), BF16),
        grid=(ntiles,),
        in_specs=[pl.BlockSpec((tm, d), lambda i: (tile0 + i, 0)),
                  pl.BlockSpec((nsub, 8, d), lambda i: (tile0 + i, 0, 0)),
                  pl.BlockSpec((1, d), lambda i: (0, 0)),
                  pl.BlockSpec((d, n), lambda i: (0, 0), pipeline_mode=pl.Buffered(1))],
        out_specs=pl.BlockSpec((tm, n), lambda i: (i, 0)),
        compiler_params=_params("parallel"),
        name="norm_mod_in_proj",
    )(x, tab, gain, w)


def _in_proj_grid_kernel(x_ref, tab_ref, gain_ref, p_ref, w_ref, o_ref, *, tn):
    h = _modulate(_rms(x_ref[...], gain_ref[...]), tab_ref, 0, 1).astype(BF16)
    hp = jnp.dot(p_ref[...], h, preferred_element_type=F32).astype(BF16)
    for j in range(w_ref.shape[1] // tn):
        cs = slice(j * tn, (j + 1) * tn)
        o = jnp.dot(hp, w_ref[:, cs], preferred_element_type=F32).astype(o_ref.dtype)
        o_ref[:, :, cs] = o.reshape(o_ref.shape[:2] + (tn,))


def _in_proj_grid(x, tab, gain, perm, w, tn, ntiles, band, bands_per_image):
    d = x.shape[1]
    n = w.shape[1]
    tm = band * GRID_W
    nsub = tm // ROW_TILE
    images = ntiles // bands_per_image
    once = pl.Buffered(1)
    return pl.pallas_call(
        functools.partial(_in_proj_grid_kernel, tn=tn),
        out_shape=jax.ShapeDtypeStruct((images * GRID_W, bands_per_image, band, n), F32),
        grid=(ntiles,),
        in_specs=[pl.BlockSpec((tm, d), lambda i: (i, 0)),
                  pl.BlockSpec((nsub, 8, d), lambda i: (i, 0, 0)),
                  pl.BlockSpec((1, d), lambda i: (0, 0)),
                  pl.BlockSpec((tm, tm), lambda i: (0, 0), pipeline_mode=once),
                  pl.BlockSpec((d, n), lambda i: (0, 0), pipeline_mode=once)],
        out_specs=pl.BlockSpec((GRID_W, None, band, n),
                               lambda i: (i // bands_per_image, i % bands_per_image, 0, 0)),
        compiler_params=_params("parallel"),
        name="norm_mod_in_proj_grid",
    )(x, tab, gain, perm, w)


def _lru_kernel(*refs, seg_len, aliased):
    if aliased:
        (g_ref, x_ref, h0_ref, cw_ref, cb_ref, wg_ref, bg_ref, lam_ref, _,
         y_ref, ht_ref, xc_scr, hf_scr) = refs
    else:
        (g_ref, x_ref, h0_ref, cw_ref, cb_ref, wg_ref, bg_ref, lam_ref,
         y_ref, ht_ref, xc_scr, hf_scr) = refs
    T = ROW_TILE
    H = BF16_ROWS
    nt = seg_len // T
    row = lax.broadcasted_iota(jnp.int32, (T, LANE), 0)
    cw = cw_ref[...]
    cb = cb_ref[...]
    sp = _softplus(-lam_ref[...])
    h0 = h0_ref[0]

    def conv_tile(t):
        t0 = pl.multiple_of(t * T, T)
        cur = x_ref[pl.ds(t0, T), :].astype(F32)
        p0 = pl.multiple_of(jnp.maximum(t0 - H, 0), H)
        n0 = pl.multiple_of(jnp.minimum(t0 + T, seg_len - H), H)
        pv = jnp.where(t > 0, 1.0, 0.0)
        nv = jnp.where(t < nt - 1, 1.0, 0.0)
        prev = x_ref[pl.ds(p0, H), :].astype(F32) * pv
        nxt = x_ref[pl.ds(n0, H), :].astype(F32) * nv
        xm1 = jnp.where(row == 0, prev[H - 1:H], pltpu.roll(cur, 1, 0))
        xm2 = jnp.where(row == 0, prev[H - 2:H - 1],
                        jnp.where(row == 1, prev[H - 1:H], pltpu.roll(cur, 2, 0)))
        xp1 = jnp.where(row == T - 1, nxt[0:1], pltpu.roll(cur, T - 1, 0))
        return cb + xm2 * cw[0:1] + xm1 * cw[1:2] + cur * cw[2:3] + xp1 * cw[3:4]

    def gates(xc, d):
        z = jnp.dot(xc.astype(BF16), wg_ref[d, 0], preferred_element_type=F32) + bg_ref[d, 0]
        r = _sigmoid(z[:, :LANE])
        i = _sigmoid(z[:, LANE:])
        log_a = (-LRU_C) * r * sp[d:d + 1]
        a = jnp.exp(log_a)
        return a, jnp.sqrt(1.0 - a * a) * (i * xc)

    def scan_tile(a, u, carry, reverse):
        s = 1
        while s < SUBLANE:
            if reverse:
                keep = row < T - s
                a_s = jnp.where(keep, pltpu.roll(a, T - s, 0), 1.0)
                u_s = jnp.where(keep, pltpu.roll(u, T - s, 0), 0.0)
            else:
                keep = row >= s
                a_s = jnp.where(keep, pltpu.roll(a, s, 0), 1.0)
                u_s = jnp.where(keep, pltpu.roll(u, s, 0), 0.0)
            u = u + a * u_s
            a = a * a_s
            s *= 2
        while s < T:
            if reverse:
                u = jnp.concatenate([u[:T - s] + a[:T - s] * u[s:], u[T - s:]], axis=0)
                a = jnp.concatenate([a[:T - s] * a[s:], a[T - s:]], axis=0)
            else:
                u = jnp.concatenate([u[:s], u[s:] + a[s:] * u[:T - s]], axis=0)
                a = jnp.concatenate([a[:s], a[s:] * a[:T - s]], axis=0)
            s *= 2
        return u + a * carry

    def fwd_body(t, carry):
        t0 = pl.multiple_of(t * T, T)
        xc = conv_tile(t)
        xc_scr[pl.ds(t0, T), :] = xc
        a, u = gates(xc, 0)
        h = scan_tile(a, u, carry, False)
        hf_scr[pl.ds(t0, T), :] = h
        return h[T - 1:T]

    def bwd_body(k, carry):
        t = nt - 1 - k
        t0 = pl.multiple_of(t * T, T)
        xc = xc_scr[pl.ds(t0, T), :]
        a, u = gates(xc, 1)
        h = scan_tile(a, u, carry, True)
        rec = hf_scr[pl.ds(t0, T), :] + h
        g = g_ref[pl.ds(t0, T), :].astype(F32)
        y_ref[pl.ds(t0, T), :] = (jax.nn.gelu(g) * rec).astype(y_ref.dtype)
        return h[0:1]

    hf = lax.fori_loop(0, nt, fwd_body, h0[0:1])
    hb = lax.fori_loop(0, nt, bwd_body, h0[1:2])
    ht_ref[0] = jnp.concatenate([hf, hb], axis=0)


def _lru_segment(u, h0, lru_w, y_prev, *, batch, seg_len, row_blk0, g_col0, x_col0):
    ntok = u.shape[0]
    width = LRU_HEADS * LANE
    cw, cb, wg, bg, lam = lru_w
    aliased = y_prev is not None
    in_specs = [pl.BlockSpec((seg_len, LANE), lambda b, h: (row_blk0 + b, g_col0 + h)),
                pl.BlockSpec((seg_len, LANE), lambda b, h: (row_blk0 + b, x_col0 + h)),
                pl.BlockSpec((1, 2, LANE), lambda b, h: (b, 0, h)),
                pl.BlockSpec((8, LANE), lambda b, h: (0, h)),
                pl.BlockSpec((1, LANE), lambda b, h: (0, h)),
                pl.BlockSpec((2, 1, LANE, 2 * LANE), lambda b, h: (0, h, 0, 0)),
                pl.BlockSpec((2, 1, 1, 2 * LANE), lambda b, h: (0, h, 0, 0)),
                pl.BlockSpec((2, LANE), lambda b, h: (0, h))]
    args = [u, u, h0, cw, cb, wg, bg, lam]
    aliases = {}
    if aliased:
        in_specs.append(pl.BlockSpec(memory_space=pl.ANY))
        args.append(y_prev)
        aliases = {len(args) - 1: 0}
    return pl.pallas_call(
        functools.partial(_lru_kernel, seg_len=seg_len, aliased=aliased),
        out_shape=(jax.ShapeDtypeStruct((ntok, width), BF16),
                   jax.ShapeDtypeStruct((batch, 2, width), F32)),
        grid=(batch, LRU_HEADS),
        in_specs=in_specs,
        out_specs=(pl.BlockSpec((seg_len, LANE), lambda b, h: (row_blk0 + b, h)),
                   pl.BlockSpec((1, 2, LANE), lambda b, h: (b, 0, h))),
        scratch_shapes=[pltpu.VMEM((seg_len, LANE), F32), pltpu.VMEM((seg_len, LANE), F32)],
        input_output_aliases=aliases,
        compiler_params=_params("parallel", "parallel"),
        name="rglru_segment",
    )(*args)


def _chunk_cumsum(g, reverse):
    rows = g.shape[0]
    pos = lax.broadcasted_iota(jnp.int32, g.shape, 0) & (CHUNK - 1)
    b = g
    s = 1
    while s < CHUNK:
        if reverse:
            b = b + jnp.where(pos < CHUNK - s, pltpu.roll(b, rows - s, 0), 0.0)
        else:
            b = b + jnp.where(pos >= s, pltpu.roll(b, s, 0), 0.0)
        s *= 2
    return b


def _chunk_step(q, k, v, b, st, reverse, tri):
    iref = CHUNK // 2 if reverse else CHUNK // 2 - 1
    ilast = 0 if reverse else CHUNK - 1
    b_ref = b[iref:iref + 1]
    b_last = b[ilast:ilast + 1]
    qe = (q * jnp.exp(b - b_ref)).astype(BF16)
    ke = (k * jnp.exp(b_ref - b)).astype(BF16)
    sc = lax.dot_general(qe, ke, _NT, preferred_element_type=F32)
    sc = jnp.where(tri, sc, 0.0).astype(BF16)
    vb = v.astype(BF16)
    o = jnp.dot(sc, vb, preferred_element_type=F32)
    qd = (q * jnp.exp(b)).astype(BF16)
    o = o + lax.dot_general(qd, st.astype(BF16), _NT, preferred_element_type=F32)
    kd = (k * jnp.exp(b_last - b)).astype(BF16)
    st = st * jnp.exp(b_last) + lax.dot_general(vb, kd, _TN, preferred_element_type=F32)
    return o, st


def _chunk_scan(q, k, v, b, st_scr, o_ref, *, heads, dk, dv, reverse):
    rows = q.shape[0]
    nchunk = rows // CHUNK
    r = lax.broadcasted_iota(jnp.int32, (CHUNK, CHUNK), 0)
    c = lax.broadcasted_iota(jnp.int32, (CHUNK, CHUNK), 1)
    tri = (c >= r) if reverse else (c <= r)
    order = range(nchunk - 1, -1, -1) if reverse else range(nchunk)
    for h in range(heads):
        st = st_scr[h]
        for ci in order:
            rs = slice(ci * CHUNK, (ci + 1) * CHUNK)
            ks = slice(h * dk, (h + 1) * dk)
            vs = slice(h * dv, (h + 1) * dv)
            o, st = _chunk_step(q[rs, ks], k[rs, ks], v[rs, vs], b[rs, ks], st, reverse, tri)
            if o_ref is not None:
                o_ref[rs, vs] = o.astype(o_ref.dtype)
        st_scr[h] = st


def _gla_kernel(*refs, aliased):
    ins, outs = (refs[:13], refs[13:]) if aliased else (refs[:11], refs[11:])
    wa_ref, ba_ref, s0_ref = ins[8:11]
    of_ref, ob_ref, st_ref, st_scr = outs
    t = pl.program_id(1)

    @pl.when(t == 0)
    def _():
        st_scr[...] = s0_ref[:, 0]

    for d, o_ref in ((0, of_ref), (1, ob_ref)):
        q_ref, k_ref, v_ref, a_ref = ins[4 * d:4 * d + 4]
        dk = q_ref.shape[1] // GLA_HEADS
        dv = v_ref.shape[1] // GLA_HEADS
        z = jnp.dot(a_ref[...], wa_ref[d], preferred_element_type=F32) + ba_ref[d]
        g = _log_sigmoid(z) * (1.0 / GLA_TAU)
        b = _chunk_cumsum(g, d == 1)
        q = q_ref[...].astype(F32) * (dk ** -0.5)
        _chunk_scan(q, k_ref[...].astype(F32), v_ref[...], b, st_scr.at[d], o_ref,
                    heads=GLA_HEADS, dk=dk, dv=dv, reverse=d == 1)

    @pl.when(t == pl.num_programs(1) - 1)
    def _():
        st_ref[:, 0] = st_scr[...]


def _gla_segment(u, s0, wa, ba, o_prev, *, batch, seg_len, row_blk0, cols):
    ntok = u.shape[0]
    q0, k0, v0, a0 = cols
    dkt = wa.shape[2]
    dvt = s0.shape[2] * s0.shape[3]
    nblk = seg_len // ROW_TILE
    aliased = o_prev is not None

    def blk(d):
        return lambda b, t: row_blk0 + b * nblk + (nblk - 1 - t if d else t)

    in_specs = []
    for d in range(2):
        rb = blk(d)
        in_specs += [pl.BlockSpec((ROW_TILE, dkt), lambda b, t, rb=rb: (rb(b, t), q0 * LANE // dkt)),
                     pl.BlockSpec((ROW_TILE, dkt), lambda b, t, rb=rb: (rb(b, t), k0 * LANE // dkt)),
                     pl.BlockSpec((ROW_TILE, dvt), lambda b, t, rb=rb: (rb(b, t), v0 * LANE // dvt)),
                     pl.BlockSpec((ROW_TILE, LANE), lambda b, t, rb=rb: (rb(b, t), a0))]
    st_spec = pl.BlockSpec((2, 1) + s0.shape[2:], lambda b, t: (0, b, 0, 0, 0))
    in_specs += [pl.BlockSpec(wa.shape, lambda b, t: (0, 0, 0)),
                 pl.BlockSpec(ba.shape, lambda b, t: (0, 0, 0)),
                 st_spec]
    args = [u] * 8 + [wa, ba, s0]
    aliases = {}
    if aliased:
        in_specs += [pl.BlockSpec(memory_space=pl.ANY)] * 2
        args += list(o_prev)
        aliases = {len(args) - 2: 0, len(args) - 1: 1}
    o_shape = jax.ShapeDtypeStruct((ntok, dvt), BF16)
    return pl.pallas_call(
        functools.partial(_gla_kernel, aliased=aliased),
        out_shape=(o_shape, o_shape, jax.ShapeDtypeStruct(s0.shape, F32)),
        grid=(batch, nblk),
        in_specs=in_specs,
        out_specs=(pl.BlockSpec((ROW_TILE, dvt), lambda b, t: (blk(0)(b, t), 0)),
                   pl.BlockSpec((ROW_TILE, dvt), lambda b, t: (blk(1)(b, t), 0)),
                   st_spec),
        scratch_shapes=[pltpu.VMEM((2,) + s0.shape[2:], F32)],
        input_output_aliases=aliases,
        compiler_params=_params("parallel", "arbitrary"),
        name="gla_segment",
    )(*args)


def _hgrn_kernel(*refs, emit_o):
    ins = refs[:9]
    lb_ref, ub_ref, s0_ref = ins[6:9]
    if emit_o:
        of_ref, ob_ref, st_ref, st_scr = refs[9:]
    else:
        st_ref, st_scr = refs[9:]
        of_ref = ob_ref = None
    t = pl.program_id(1)

    @pl.when(t == 0)
    def _():
        st_scr[...] = s0_ref[:, 0]

    log_lb = lb_ref[...]
    log_ub = ub_ref[...]
    for d, o_ref in ((0, of_ref), (1, ob_ref)):
        q_ref, z_ref, v_ref = ins[3 * d:3 * d + 3]
        dh = q_ref.shape[1] // HGRN_HEADS
        z = z_ref[...].astype(F32)
        y = log_ub + _log_sigmoid(z)
        log_f = jnp.maximum(log_lb, y) + jnp.log1p(jnp.exp(-jnp.abs(log_lb - y)))
        k = jnp.exp(log_ub) * _sigmoid(-z)
        b = _chunk_cumsum(log_f, d == 1)
        _chunk_scan(q_ref[...].astype(F32), k, v_ref[...], b, st_scr.at[d], o_ref,
                    heads=HGRN_HEADS, dk=dh, dv=dh, reverse=d == 1)

    @pl.when(t == pl.num_programs(1) - 1)
    def _():
        st_ref[:, 0] = st_scr[...]


def _hgrn_segment(u, s0, log_lb, log_ub, *, batch, seg_len, emit_o):
    ntok = u.shape[0]
    width = log_lb.shape[1]
    nblk = seg_len // ROW_TILE

    def blk(d):
        return lambda b, t: b * nblk + (nblk - 1 - t if d else t)

    in_specs = []
    for d in range(2):
        rb = blk(d)
        in_specs += [pl.BlockSpec((ROW_TILE, width), lambda b, t, rb=rb: (rb(b, t), 0)),
                     pl.BlockSpec((ROW_TILE, width), lambda b, t, rb=rb, d=d: (rb(b, t), 1 + d)),
                     pl.BlockSpec((ROW_TILE, width), lambda b, t, rb=rb: (rb(b, t), 3))]
    st_spec = pl.BlockSpec((2, 1) + s0.shape[2:], lambda b, t: (0, b, 0, 0, 0))
    in_specs += [pl.BlockSpec((1, width), lambda b, t: (0, 0)),
                 pl.BlockSpec((1, width), lambda b, t: (0, 0)),
                 st_spec]
    st_shape = jax.ShapeDtypeStruct(s0.shape, F32)
    if emit_o:
        o_shape = jax.ShapeDtypeStruct((ntok, width), F32)
        out_shape = (o_shape, o_shape, st_shape)
        out_specs = (pl.BlockSpec((ROW_TILE, width), lambda b, t: (blk(0)(b, t), 0)),
                     pl.BlockSpec((ROW_TILE, width), lambda b, t: (blk(1)(b, t), 0)),
                     st_spec)
    else:
        out_shape = (st_shape,)
        out_specs = (st_spec,)
    return pl.pallas_call(
        functools.partial(_hgrn_kernel, emit_o=emit_o),
        out_shape=out_shape,
        grid=(batch, nblk),
        in_specs=in_specs,
        out_specs=out_specs,
        scratch_shapes=[pltpu.VMEM((2,) + s0.shape[2:], F32)],
        compiler_params=_params("parallel", "arbitrary"),
        name="hgrn_segment",
    )(*([u] * 6), log_lb, log_ub, s0)


def _head_norm_gate(o, gate, heads):
    dh = o.shape[1] // heads
    parts = []
    for h in range(heads):
        oh = o[:, h * dh:(h + 1) * dh]
        ms = jnp.mean(oh * oh, axis=-1, keepdims=True)
        parts.append((oh * lax.rsqrt(ms + EPS) * _silu(gate[:, h * dh:(h + 1) * dh])).astype(BF16))
    return jnp.concatenate(parts, axis=1)


def _out_proj_tail(o, x_ref, tab_ref, pg_ref, ng_ref, rt_ref, xo_ref, h_ref, lg_ref):
    xn = _gated_residual(x_ref[...], _rms(o, pg_ref[...]), tab_ref, 2)
    xo_ref[...] = xn
    h2 = _modulate(_rms(xn, ng_ref[...]), tab_ref, 3, 4)
    h_ref[...] = h2.astype(BF16)
    lg_ref[...] = lax.dot_general(rt_ref[...], h2, _NT, precision=lax.Precision.HIGHEST,
                                  preferred_element_type=F32)


def _out_proj_ab_kernel(ya_ref, of_ref, ob_ref, gate_ref, w_ref, x_ref, tab_ref, pg_ref, ng_ref,
                        rt_ref, xo_ref, h_ref, lg_ref):
    o_att = of_ref[...].astype(F32) + ob_ref[...].astype(F32)
    yb = _head_norm_gate(o_att, gate_ref[...].astype(F32), GLA_HEADS)
    da = ya_ref.shape[1]
    o = jnp.dot(ya_ref[...], w_ref[:da], preferred_element_type=F32)
    o = o + jnp.dot(yb, w_ref[da:], preferred_element_type=F32)
    _out_proj_tail(o, x_ref, tab_ref, pg_ref, ng_ref, rt_ref, xo_ref, h_ref, lg_ref)


def _out_proj_c_kernel(of_ref, ob_ref, gate_ref, p_ref, w_ref, x_ref, tab_ref, pg_ref, ng_ref,
                       rt_ref, xo_ref, h_ref, lg_ref):
    tm = x_ref.shape[0]
    width = of_ref.shape[-1]
    o_att = of_ref[...].astype(F32) + ob_ref[...].astype(F32)
    y = _head_norm_gate(o_att.reshape(tm, width), gate_ref[...].astype(F32).reshape(tm, width),
                        HGRN_HEADS)
    y = jnp.dot(p_ref[...], y, preferred_element_type=F32).astype(BF16)
    o = jnp.dot(y, w_ref[...], preferred_element_type=F32)
    _out_proj_tail(o, x_ref, tab_ref, pg_ref, ng_ref, rt_ref, xo_ref, h_ref, lg_ref)


def _out_proj(kernel, lead, lead_specs, w, x, tab, post_gain, next_gain, router_t, tm, ntiles):
    d = x.shape[1]
    ntok = ntiles * tm
    nsub = tm // ROW_TILE
    ne = router_t.shape[0]
    in_specs = list(lead_specs) + [
        pl.BlockSpec(w.shape, lambda i: (0, 0), pipeline_mode=pl.Buffered(1)),
        pl.BlockSpec((tm, d), lambda i: (i, 0)),
        pl.BlockSpec((nsub, 8, d), lambda i: (i, 0, 0)),
        pl.BlockSpec((1, d), lambda i: (0, 0)),
        pl.BlockSpec((1, d), lambda i: (0, 0)),
        pl.BlockSpec((ne, d), lambda i: (0, 0))]
    return pl.pallas_call(
        kernel,
        out_shape=(jax.ShapeDtypeStruct((ntok, d), F32),
                   jax.ShapeDtypeStruct((ntok, d), BF16),
                   jax.ShapeDtypeStruct((ne, ntok), F32)),
        grid=(ntiles,),
        in_specs=in_specs,
        out_specs=(pl.BlockSpec((tm, d), lambda i: (i, 0)),
                   pl.BlockSpec((tm, d), lambda i: (i, 0)),
                   pl.BlockSpec((ne, tm), lambda i: (0, i))),
        compiler_params=_params("parallel"),
        name="out_proj_norm_residual",
    )(*lead, w, x, tab, post_gain, next_gain, router_t)


def _route_kernel(lg_ref, bias_ref, cw_ref):
    tm = lg_ref.shape[1]
    per_group = N_EXPERTS // N_GROUPS
    shape3 = (N_GROUPS, per_group, tm)
    aff = _sigmoid(lg_ref[...]).reshape(shape3)
    biased = aff + bias_ref[...].reshape(shape3)
    neg = -jnp.inf
    sub = lax.broadcasted_iota(jnp.int32, shape3, 1)
    grp = lax.broadcasted_iota(jnp.int32, shape3, 0)
    m1 = jnp.max(biased, axis=1, keepdims=True)
    i1 = jnp.min(jnp.where(biased == m1, sub, per_group), axis=1, keepdims=True)
    m2 = jnp.max(jnp.where(sub == i1, neg, biased), axis=1, keepdims=True)
    score = m1 + m2
    gidx = lax.broadcasted_iota(jnp.int32, score.shape, 0)
    keep = jnp.zeros(score.shape, F32)
    for _ in range(TOPK_GROUPS):
        m = jnp.max(score, axis=0, keepdims=True)
        im = jnp.min(jnp.where(score == m, gidx, N_GROUPS), axis=0, keepdims=True)
        sel = gidx == im
        keep = jnp.where(sel, 1.0, keep)
        score = jnp.where(sel, neg, score)
    work = jnp.where(jnp.broadcast_to(keep, shape3) > 0.0, biased, neg)
    eidx = grp * per_group + sub
    chosen = jnp.zeros(shape3, F32)
    for _ in range(TOP_K):
        m = jnp.max(jnp.max(work, axis=0, keepdims=True), axis=1, keepdims=True)
        cand = jnp.where(work == m, eidx, N_EXPERTS)
        im = jnp.min(jnp.min(cand, axis=0, keepdims=True), axis=1, keepdims=True)
        sel = eidx == im
        chosen = jnp.where(sel, 1.0, chosen)
        work = jnp.where(sel, neg, work)
    gate = aff * chosen
    den = jnp.sum(jnp.sum(gate, axis=0, keepdims=True), axis=1, keepdims=True)
    cw = (ROUTED_SCALE * gate / den).reshape(N_EXPERTS, tm)
    cw = jnp.concatenate([cw, jnp.zeros((LANE - N_EXPERTS, tm), F32)], axis=0)
    cw_ref[...] = cw.T


def _route(logits_t, bias, tm):
    ne, ntok = logits_t.shape
    bias_b = jnp.broadcast_to(bias.astype(F32)[:, None], (ne, tm))
    return pl.pallas_call(
        _route_kernel,
        out_shape=jax.ShapeDtypeStruct((ntok, LANE), F32),
        grid=(ntok // tm,),
        in_specs=[pl.BlockSpec((ne, tm), lambda i: (0, i)),
                  pl.BlockSpec((ne, tm), lambda i: (0, 0))],
        out_specs=pl.BlockSpec((tm, LANE), lambda i: (i, 0)),
        compiler_params=_params("parallel"),
        name="moe_route",
    )(logits_t, bias_b)


def _moe_kernel(h_ref, cw_ref, wg_ref, wu_ref, wd_ref, sg_ref, su_ref, sd_ref, x_ref, tab_ref,
                pg_ref, o_ref, acc_ref, *, epb):
    j = pl.program_id(1)
    h = h_ref[...]

    @pl.when(j == 0)
    def _():
        g = jnp.dot(h, sg_ref[...], preferred_element_type=F32)
        u = jnp.dot(h, su_ref[...], preferred_element_type=F32)
        acc_ref[...] = jnp.dot((_silu(g) * u).astype(BF16), sd_ref[...], preferred_element_type=F32)

    cw = cw_ref[...]
    lane = lax.broadcasted_iota(jnp.int32, cw.shape, 1)
    for e in range(epb):
        we = jnp.sum(jnp.where(lane == j * epb + e, cw, 0.0), axis=1, keepdims=True)
        g = jnp.dot(h, wg_ref[e].astype(BF16), preferred_element_type=F32)
        u = jnp.dot(h, wu_ref[e].astype(BF16), preferred_element_type=F32)
        a = (_silu(g) * u * we).astype(BF16)
        acc_ref[...] += jnp.dot(a, wd_ref[e].astype(BF16), preferred_element_type=F32)

    @pl.when(j == pl.num_programs(1) - 1)
    def _():
        o_ref[...] = _gated_residual(x_ref[...], _rms(acc_ref[...], pg_ref[...]), tab_ref, 5)


def _moe(h, cw, x, tab, post_gain, wg, wu, wd, sg, su, sd, ntok_out, tm, epb):
    d = h.shape[1]
    ne, _, ff = wg.shape
    nsub = tm // ROW_TILE
    once = pl.Buffered(1)
    return pl.pallas_call(
        functools.partial(_moe_kernel, epb=epb),
        out_shape=jax.ShapeDtypeStruct((ntok_out, d), F32),
        grid=(ntok_out // tm, ne // epb),
        in_specs=[pl.BlockSpec((tm, d), lambda i, j: (i, 0)),
                  pl.BlockSpec((tm, LANE), lambda i, j: (i, 0)),
                  pl.BlockSpec((epb, d, ff), lambda i, j: (j, 0, 0)),
                  pl.BlockSpec((epb, d, ff), lambda i, j: (j, 0, 0)),
                  pl.BlockSpec((epb, ff, d), lambda i, j: (j, 0, 0)),
                  pl.BlockSpec(sg.shape, lambda i, j: (0, 0), pipeline_mode=once),
                  pl.BlockSpec(su.shape, lambda i, j: (0, 0), pipeline_mode=once),
                  pl.BlockSpec(sd.shape, lambda i, j: (0, 0), pipeline_mode=once),
                  pl.BlockSpec((tm, d), lambda i, j: (i, 0), pipeline_mode=once),
                  pl.BlockSpec((nsub, 8, d), lambda i, j: (i, 0, 0)),
                  pl.BlockSpec((1, d), lambda i, j: (0, 0))],
        out_specs=pl.BlockSpec((tm, d), lambda i, j: (i, 0)),
        scratch_shapes=[pltpu.VMEM((tm, d), F32)],
        compiler_params=_params("parallel", "arbitrary", vmem=VMEM_LIMIT_MOE),
        name="moe_experts",
    )(h, cw, wg, wu, wd, sg, su, sd, x, tab, post_gain)


def _grid_permutation(band):
    tm = band * GRID_W
    dst = jnp.arange(tm)
    src = (dst % band) * GRID_W + dst // band
    return (src[:, None] == jnp.arange(tm)[None, :]).astype(BF16)


def kernel(x, c, ctx, c_ctx, mod_w, mod_b, pre_gain, post_gain, ab_w_in, ab_conv_w, ab_conv_b, ab_lru_wa, ab_lru_ba, ab_lru_wi, ab_lru_bi, ab_lru_lambda, ab_gla_wa2, ab_gla_ba, ab_w_out, c_w_in, hgrn_lb_logits, c_w_out, moe_router, moe_bias, moe_w_gate, moe_w_up, moe_w_down, moe_ws_gate, moe_ws_up, moe_ws_down):
    batch, seq, d = x.shape
    n_ctx = ctx.shape[1]
    depth = mod_w.shape[0]
    band = SUBLANE
    tm_grid = band * GRID_W
    rows_img = seq // GRID_W
    assert n_ctx == ROW_TILE and seq % (2 * ROW_TILE) == 0 and d % LANE == 0
    assert depth == 2 and batch + 1 <= 8 and rows_img % band == 0 and tm_grid % ROW_TILE == 0
    n_lat = batch * seq
    n_ctx_tok = batch * n_ctx
    ntok = n_lat + n_ctx_tok
    lat_tiles = n_lat // ROW_TILE

    xs = jnp.concatenate([x.reshape(n_lat, d), ctx.reshape(n_ctx_tok, d)], axis=0)

    cvec = jnp.concatenate([c, c_ctx[None, :], jnp.zeros((8 - batch - 1, d), F32)], axis=0)
    mods = _modulation(cvec, mod_w, mod_b).reshape(depth, 8, N_MOD, d)
    tile_row = jnp.concatenate([jnp.repeat(jnp.arange(batch), seq // ROW_TILE),
                                jnp.full((batch,), batch)]).astype(jnp.int32)
    tabs = jnp.pad(mods[:, tile_row], ((0, 0), (0, 0), (0, 8 - N_MOD), (0, 0)))

    def moe_weights(layer):
        return (moe_w_gate[layer], moe_w_up[layer], moe_w_down[layer],
                moe_ws_gate[layer].astype(BF16), moe_ws_up[layer].astype(BF16),
                moe_ws_down[layer].astype(BF16))

    tm_proj, tm_out, tm_route, tm_moe, epb = 512, 512, 512, 1024, 2

    ab_cols = ab_w_in.shape[2]
    gla_dk = ab_gla_wa2.shape[3]
    lru_w = d
    gla_dv = (ab_cols - 2 * lru_w - 2 * gla_dk - 2 * GLA_RANK) // 2
    tn0 = 1792
    n0 = -(-(ab_cols) // tn0) * tn0
    w_in0 = jnp.pad(ab_w_in[0], ((0, 0), (0, n0 - ab_cols))).astype(BF16)
    u0 = _in_proj(xs, tabs[0], pre_gain[0, 0][None, :], w_in0, tm_proj, tn0, 0, ntok // tm_proj)

    cw = jnp.pad(ab_conv_w[0], ((0, 8 - CONV_W), (0, 0)))
    cb = ab_conv_b[0][None, :]
    wg = jnp.concatenate([ab_lru_wa[0], ab_lru_wi[0]], axis=-1).astype(BF16)
    bg = jnp.concatenate([ab_lru_ba[0].reshape(2, LRU_HEADS, 1, LANE),
                          ab_lru_bi[0].reshape(2, LRU_HEADS, 1, LANE)], axis=-1)
    lru_params = (cw, cb, wg, bg, ab_lru_lambda[0])
    h_zero = jnp.zeros((batch, 2, lru_w), F32)
    ya, h_ctx = _lru_segment(u0, h_zero, lru_params, None, batch=batch, seg_len=n_ctx,
                             row_blk0=n_lat // n_ctx, g_col0=0, x_col0=lru_w // LANE)
    ya, _ = _lru_segment(u0, h_ctx, lru_params, ya, batch=batch, seg_len=seq,
                         row_blk0=0, g_col0=0, x_col0=lru_w // LANE)

    q0 = 2 * lru_w // LANE
    k0 = q0 + gla_dk // LANE
    v0 = k0 + gla_dk // LANE
    gate0 = v0 + gla_dv // LANE
    a0 = gate0 + gla_dv // LANE
    s_zero = jnp.zeros((2, batch, GLA_HEADS, gla_dv // GLA_HEADS, gla_dk // GLA_HEADS), F32)
    wa = jnp.stack([jnp.zeros((LANE, gla_dk), F32).at[dr * GLA_RANK:(dr + 1) * GLA_RANK]
                    .set(ab_gla_wa2[0, dr]) for dr in range(2)]).astype(BF16)
    ba = ab_gla_ba[0][:, None, :]
    o_cf, o_cb, s_c = _gla_segment(u0, s_zero, wa, ba, None, batch=batch, seg_len=n_ctx,
                                   row_blk0=lat_tiles, cols=(q0, k0, v0, a0))
    o_f, o_b, _ = _gla_segment(u0, s_c, wa, ba, (o_cf, o_cb), batch=batch, seg_len=seq,
                               row_blk0=0, cols=(q0, k0, v0, a0))
    o_dir = [o_f, o_b]

    lead_specs = [pl.BlockSpec((tm_out, lru_w), lambda i: (i, 0)),
                  pl.BlockSpec((tm_out, gla_dv), lambda i: (i, 0)),
                  pl.BlockSpec((tm_out, gla_dv), lambda i: (i, 0)),
                  pl.BlockSpec((tm_out, gla_dv), lambda i: (i, gate0 * LANE // gla_dv))]
    xs, h2, lg = _out_proj(_out_proj_ab_kernel, (ya, o_dir[0], o_dir[1], u0), lead_specs,
                           ab_w_out[0].astype(BF16), xs, tabs[0], post_gain[0, 0][None, :],
                           pre_gain[0, 1][None, :], moe_router[0].T, tm_out, ntok // tm_out)
    cwt = _route(lg, moe_bias[0], tm_route)
    xs = _moe(h2, cwt, xs, tabs[0], post_gain[0, 1][None, :], *moe_weights(0), ntok, tm_moe, epb)

    hd = c_w_out.shape[1]
    tn1 = 1280
    w_in1 = c_w_in[0].astype(BF16)
    c_cols = w_in1.shape[1]
    nsec = c_cols // hd
    perm = _grid_permutation(band)
    gain1 = pre_gain[1, 0][None, :]
    u1l = _in_proj_grid(xs, tabs[1], gain1, perm, w_in1, tn1, n_lat // tm_grid, band,
                        rows_img // band)
    u1c = _in_proj(xs, tabs[1], gain1, w_in1, tm_proj, tn1, n_lat // tm_proj, n_ctx_tok // tm_proj)
    lb_sm = jax.nn.softmax(hgrn_lb_logits.astype(F32), axis=0)
    lb_cum = jnp.cumsum(lb_sm, axis=0)
    lb = lb_cum[1] - lb_cum[0]
    log_lb = jnp.log(lb)[None, :]
    log_ub = jnp.log1p(-lb)[None, :]
    dh = hd // HGRN_HEADS
    s_zero = jnp.zeros((2, batch, HGRN_HEADS, dh, dh), F32)
    u1l_flat = u1l.reshape(n_lat, c_cols)
    (s_c,) = _hgrn_segment(u1c, s_zero, log_lb, log_ub, batch=batch, seg_len=n_ctx, emit_o=False)
    o_f, o_b, _ = _hgrn_segment(u1l_flat, s_c, log_lb, log_ub, batch=batch, seg_len=seq, emit_o=True)
    o_dir = [o.reshape(n_lat // rows_img, rows_img // band, band, hd) for o in (o_f, o_b)]

    bpi = rows_img // band
    grid_blk = (GRID_W, None, band, hd)
    lead_specs = [pl.BlockSpec(grid_blk, lambda i: (i // bpi, i % bpi, 0, 0)),
                  pl.BlockSpec(grid_blk, lambda i: (i // bpi, i % bpi, 0, 0)),
                  pl.BlockSpec(grid_blk, lambda i: (i // bpi, i % bpi, 0, nsec - 1)),
                  pl.BlockSpec((tm_grid, tm_grid), lambda i: (0, 0))]
    xs, h2, lg = _out_proj(_out_proj_c_kernel, (o_dir[0], o_dir[1], u1l, perm.T), lead_specs,
                           c_w_out[0].astype(BF16), xs, tabs[1], post_gain[1, 0][None, :],
                           pre_gain[1, 1][None, :], moe_router[1].T, tm_grid, n_lat // tm_grid)
    cwt = _route(lg, moe_bias[1], tm_route)
    out = _moe(h2, cwt, xs, tabs[1], post_gain[1, 1][None, :], *moe_weights(1), n_lat, tm_moe, epb)
    return out.reshape(batch, seq, d)
```

```python
import functools

import jax
import jax.numpy as jnp
from jax import lax
from jax.experimental import pallas as pl
from jax.experimental.pallas import tpu as pltpu

F32 = jnp.float32
BF16 = jnp.bfloat16

EPS = 1e-6
N_MOD = 6
ROW_TILE = 256
CHUNK = 64
CHUNK_SHIFT = 6
GRID_W = 64
LRU_HEADS = 8
LRU_C = 8.0
CONV_W = 4
GLA_HEADS = 4
GLA_RANK = 16
GLA_TAU = 16.0
HGRN_HEADS = 8
N_EXPERTS = 64
TOP_K = 8
N_GROUPS = 8
TOPK_GROUPS = 4
ROUTED_SCALE = 2.5
LANE = 128
SUBLANE = 8
BF16_ROWS = 16
VMEM_LIMIT = 48 * 1024 * 1024
VMEM_LIMIT_MOE = 56 * 1024 * 1024

_NT = (((1,), (1,)), ((), ()))
_TN = (((0,), (0,)), ((), ()))


def _params(*sem, vmem=VMEM_LIMIT):
    return pltpu.CompilerParams(dimension_semantics=sem, vmem_limit_bytes=vmem)


def _rms(x, gain):
    ms = jnp.mean(x * x, axis=-1, keepdims=True)
    return x * lax.rsqrt(ms + EPS) * gain


def _sigmoid(x):
    return jax.nn.sigmoid(x)


def _silu(x):
    return x * _sigmoid(x)


def _log_sigmoid(x):
    return jnp.minimum(x, 0.0) - jnp.log1p(jnp.exp(-jnp.abs(x)))


def _softplus(x):
    return jnp.maximum(x, 0.0) + jnp.log1p(jnp.exp(-jnp.abs(x)))


def _modulate(xn, tab_ref, shift_row, scale_row):
    parts = []
    for s in range(tab_ref.shape[0]):
        t = tab_ref[s]
        rows = xn[s * ROW_TILE:(s + 1) * ROW_TILE]
        parts.append(rows * (1.0 + t[scale_row:scale_row + 1]) + t[shift_row:shift_row + 1])
    return parts[0] if len(parts) == 1 else jnp.concatenate(parts, axis=0)


def _gated_residual(x, branch, tab_ref, gate_row):
    parts = []
    for s in range(tab_ref.shape[0]):
        sl = slice(s * ROW_TILE, (s + 1) * ROW_TILE)
        parts.append(x[sl] + tab_ref[s][gate_row:gate_row + 1] * branch[sl])
    return parts[0] if len(parts) == 1 else jnp.concatenate(parts, axis=0)


def _mod_kernel(c_ref, w_ref, b_ref, o_ref):
    s = _silu(c_ref[...])
    o_ref[0] = jnp.dot(s, w_ref[0], precision=lax.Precision.HIGHEST,
                       preferred_element_type=F32) + b_ref[0]


def _modulation(cvec, mod_w, mod_b):
    depth, d, n = mod_w.shape
    tn = 1024
    return pl.pallas_call(
        _mod_kernel,
        out_shape=jax.ShapeDtypeStruct((depth, 8, n), F32),
        grid=(depth, n // tn),
        in_specs=[pl.BlockSpec((8, d), lambda l, j: (0, 0)),
                  pl.BlockSpec((1, d, tn), lambda l, j: (l, 0, j)),
                  pl.BlockSpec((1, 1, tn), lambda l, j: (l, 0, j))],
        out_specs=pl.BlockSpec((1, 8, tn), lambda l, j: (l, 0, j)),
        compiler_params=_params("parallel", "parallel"),
        name="adaln_modulation",
    )(cvec, mod_w, mod_b.reshape(depth, 1, n))


def _in_proj_kernel(x_ref, tab_ref, gain_ref, w_ref, o_ref, *, tn):
    h = _modulate(_rms(x_ref[...], gain_ref[...]), tab_ref, 0, 1).astype(BF16)
    for j in range(w_ref.shape[1] // tn):
        cs = slice(j * tn, (j + 1) * tn)
        o_ref[:, cs] = jnp.dot(h, w_ref[:, cs], preferred_element_type=F32).astype(o_ref.dtype)


def _in_proj(x, tab, gain, w, tm, tn, tile0, ntiles):
    d = x.shape[1]
    n = w.shape[1]
    nsub = tm // ROW_TILE
    return pl.pallas_call(
        functools.partial(_in_proj_kernel, tn=tn),
        out_shape=jax.ShapeDtypeStruct((ntiles * tm, n), BF16),
        grid=(ntiles,),
        in_specs=[pl.BlockSpec((tm, d), lambda i: (tile0 + i, 0)),
                  pl.BlockSpec((nsub, 8, d), lambda i: (tile0 + i, 0, 0)),
                  pl.BlockSpec((1, d), lambda i: (0, 0)),
                  pl.BlockSpec((d, n), lambda i: (0, 0), pipeline_mode=pl.Buffered(1))],
        out_specs=pl.BlockSpec((tm, n), lambda i: (i, 0)),
        compiler_params=_params("parallel"),
        name="norm_mod_in_proj",
    )(x, tab, gain, w)


def _in_proj_grid_kernel(x_ref, tab_ref, gain_ref, p_ref, w_ref, o_ref, *, tn):
    h = _modulate(_rms(x_ref[...], gain_ref[...]), tab_ref, 0, 1).astype(BF16)
    hp = jnp.dot(p_ref[...], h, preferred_element_type=F32).astype(BF16)
    for j in range(w_ref.shape[1] // tn):
        cs = slice(j * tn, (j + 1) * tn)
        o = jnp.dot(hp, w_ref[:, cs], preferred_element_type=F32).astype(o_ref.dtype)
        o_ref[:, :, cs] = o.reshape(o_ref.shape[:2] + (tn,))


def _in_proj_grid(x, tab, gain, perm, w, tn, ntiles, band, bands_per_image):
    d = x.shape[1]
    n = w.shape[1]
    tm = band * GRID_W
    nsub = tm // ROW_TILE
    images = ntiles // bands_per_image
    once = pl.Buffered(1)
    return pl.pallas_call(
        functools.partial(_in_proj_grid_kernel, tn=tn),
        out_shape=jax.ShapeDtypeStruct((images * GRID_W, bands_per_image, band, n), F32),
        grid=(ntiles,),
        in_specs=[pl.BlockSpec((tm, d), lambda i: (i, 0)),
                  pl.BlockSpec((nsub, 8, d), lambda i: (i, 0, 0)),
                  pl.BlockSpec((1, d), lambda i: (0, 0)),
                  pl.BlockSpec((tm, tm), lambda i: (0, 0), pipeline_mode=once),
                  pl.BlockSpec((d, n), lambda i: (0, 0), pipeline_mode=once)],
        out_specs=pl.BlockSpec((GRID_W, None, band, n),
                               lambda i: (i // bands_per_image, i % bands_per_image, 0, 0)),
        compiler_params=_params("parallel"),
        name="norm_mod_in_proj_grid",
    )(x, tab, gain, perm, w)


def _lru_kernel(gc_ref, xc_ref, gl_ref, xl_ref, p_ref, cw_ref, cb_ref, wg_ref, bg_ref,
                lam_ref, yc_ref, yl_ref, xs_scr, hf_scr):
    T = ROW_TILE
    V = T // SUBLANE
    H = BF16_ROWS
    sub = lax.broadcasted_iota(jnp.int32, (SUBLANE, LANE), 0)
    cw = cw_ref[...]
    cb = cb_ref[...]
    sp = _softplus(-lam_ref[...])

    def vreg(a, v):
        return a[v * SUBLANE:(v + 1) * SUBLANE]

    def shift_down(y, first):
        return jnp.where(sub == 0, first, pltpu.roll(y, 1, 0))

    def shift_up(y, last):
        return jnp.where(sub == SUBLANE - 1, last, pltpu.roll(y, SUBLANE - 1, 0))

    def conv_tile(x_ref, t, nt, seg_len):
        t0 = pl.multiple_of(t * T, T)
        xp = jnp.dot(p_ref[...], x_ref[pl.ds(t0, T), :], preferred_element_type=F32)
        p0 = pl.multiple_of(jnp.maximum(t0 - H, 0), H)
        n0 = pl.multiple_of(jnp.minimum(t0 + T, seg_len - H), H)
        prev = x_ref[pl.ds(p0, H), :].astype(F32) * jnp.where(t > 0, 1.0, 0.0)
        nxt = x_ref[pl.ds(n0, H), :].astype(F32) * jnp.where(t < nt - 1, 1.0, 0.0)
        m1, m2, p1 = prev[H - 1:H], prev[H - 2:H - 1], nxt[0:1]
        last_m1 = shift_down(vreg(xp, V - 1), m1)
        xm1 = jnp.concatenate([last_m1, xp[:T - SUBLANE]], axis=0)
        xm2 = jnp.concatenate([shift_down(vreg(xp, V - 2), m2), last_m1, xp[:T - 2 * SUBLANE]], axis=0)
        xp1 = jnp.concatenate([xp[SUBLANE:], shift_up(vreg(xp, 0), p1)], axis=0)
        return cb + xm2 * cw[0:1] + xm1 * cw[1:2] + xp * cw[2:3] + xp1 * cw[3:4]

    def gates(xc, d):
        z = jnp.dot(xc.astype(BF16), wg_ref[d, 0], preferred_element_type=F32) + bg_ref[d, 0]
        r = _sigmoid(z[:, :LANE])
        i = _sigmoid(z[:, LANE:])
        log_a = (-LRU_C) * r * sp[d:d + 1]
        a = jnp.exp(log_a)
        return a, jnp.sqrt(1.0 - a * a) * (i * xc)

    def scan_tile(a, u, carry, reverse):
        hs, cum = [None] * V, [None] * V
        hp = ap = None
        for v in (range(V - 1, -1, -1) if reverse else range(V)):
            av, uv = vreg(a, v), vreg(u, v)
            hp, ap = (uv, av) if hp is None else (av * hp + uv, av * ap)
            hs[v], cum[v] = hp, ap
        s = 1
        while s < SUBLANE:
            keep = (sub < SUBLANE - s) if reverse else (sub >= s)
            sh = SUBLANE - s if reverse else s
            h_sh = jnp.where(keep, pltpu.roll(hp, sh, 0), 0.0)
            a_sh = jnp.where(keep, pltpu.roll(ap, sh, 0), 1.0)
            hp = hp + ap * h_sh
            ap = ap * a_sh
            s *= 2
        state = hp + ap * carry
        if reverse:
            start, out = shift_up(state, carry), state[0:1]
        else:
            start, out = shift_down(state, carry), state[SUBLANE - 1:SUBLANE]
        h = jnp.concatenate([hs[v] + cum[v] * start for v in range(V)], axis=0)
        return h, out

    def run_segment(g_ref, x_ref, y_ref, seg_len, carry_f, carry_b):
        nt = seg_len // T
        unroll = 2 if nt % 2 == 0 else 1

        def fwd_body(i, carry):
            for j in range(unroll):
                t = i * unroll + j
                t0 = pl.multiple_of(t * T, T)
                xc = conv_tile(x_ref, t, nt, seg_len)
                xs_scr[pl.ds(t0, T), :] = xc
                a, u = gates(xc, 0)
                h, carry = scan_tile(a, u, carry, False)
                hf_scr[pl.ds(t0, T), :] = h
            return carry

        def bwd_body(i, carry):
            for j in range(unroll):
                t = nt - 1 - (i * unroll + j)
                t0 = pl.multiple_of(t * T, T)
                a, u = gates(xs_scr[pl.ds(t0, T), :], 1)
                h, carry = scan_tile(a, u, carry, True)
                g = jnp.dot(p_ref[...], g_ref[pl.ds(t0, T), :], preferred_element_type=F32)
                rec = hf_scr[pl.ds(t0, T), :] + h
                y_ref[pl.ds(t0, T), :] = (jax.nn.gelu(g) * rec).astype(y_ref.dtype)
            return carry

        steps = nt // unroll
        return (lax.fori_loop(0, steps, fwd_body, carry_f),
                lax.fori_loop(0, steps, bwd_body, carry_b))

    zero = jnp.zeros((1, LANE), F32)
    hf, hb = run_segment(gc_ref, xc_ref, yc_ref, xc_ref.shape[0], zero, zero)
    run_segment(gl_ref, xl_ref, yl_ref, xl_ref.shape[0], hf, hb)


def _time_permutation():
    v = ROW_TILE // SUBLANE
    dst = jnp.arange(ROW_TILE)
    src = (dst % SUBLANE) * v + dst // SUBLANE
    return src[:, None] == jnp.arange(ROW_TILE)[None, :]


def _lru(u, lru_w, *, batch, seq, n_ctx, g_col0, x_col0):
    width = LRU_HEADS * LANE
    cw, cb, wg, bg, lam = lru_w
    ctx_blk0 = batch * seq // n_ctx
    perm = _time_permutation()
    once = pl.Buffered(1)
    return pl.pallas_call(
        _lru_kernel,
        out_shape=(jax.ShapeDtypeStruct((batch * n_ctx, width), BF16),
                   jax.ShapeDtypeStruct((batch * seq, width), BF16)),
        grid=(batch, LRU_HEADS),
        in_specs=[pl.BlockSpec((n_ctx, LANE), lambda b, h: (ctx_blk0 + b, g_col0 + h)),
                  pl.BlockSpec((n_ctx, LANE), lambda b, h: (ctx_blk0 + b, x_col0 + h)),
                  pl.BlockSpec((seq, LANE), lambda b, h: (b, g_col0 + h)),
                  pl.BlockSpec((seq, LANE), lambda b, h: (b, x_col0 + h)),
                  pl.BlockSpec((ROW_TILE, ROW_TILE), lambda b, h: (0, 0), pipeline_mode=once),
                  pl.BlockSpec((8, LANE), lambda b, h: (0, h)),
                  pl.BlockSpec((1, LANE), lambda b, h: (0, h)),
                  pl.BlockSpec((2, 1, LANE, 2 * LANE), lambda b, h: (0, h, 0, 0)),
                  pl.BlockSpec((2, 1, 1, 2 * LANE), lambda b, h: (0, h, 0, 0)),
                  pl.BlockSpec((2, LANE), lambda b, h: (0, h))],
        out_specs=(pl.BlockSpec((n_ctx, LANE), lambda b, h: (b, h)),
                   pl.BlockSpec((seq, LANE), lambda b, h: (b, h))),
        scratch_shapes=[pltpu.VMEM((seq, LANE), F32), pltpu.VMEM((seq, LANE), F32)],
        compiler_params=_params("parallel", "parallel"),
        name="rglru",
    )(u, u, u, u, perm.astype(BF16), cw, cb, wg, bg, lam)


def _chunk_mask(rows, reverse):
    r = lax.broadcasted_iota(jnp.int32, (rows, rows), 0)
    c = lax.broadcasted_iota(jnp.int32, (rows, rows), 1)
    same = (r >> CHUNK_SHIFT) == (c >> CHUNK_SHIFT)
    tri = (c >= r) if reverse else (c <= r)
    return jnp.where(same, jnp.where(tri, 1.0, 0.0), 0.0)


def _split3(x):
    hi = x.astype(BF16)
    r1 = x - hi.astype(F32)
    mid = r1.astype(BF16)
    lo = (r1 - mid.astype(F32)).astype(BF16)
    return hi, mid, lo


def _block_scan(head_inputs, rows, st_scr, o_ref, *, heads, dv, reverse, batched):
    nchunk = rows // CHUNK
    mask = _chunk_mask(rows, reverse)
    mask_b = mask.astype(BF16)
    keep = mask > 0.5
    keep_c = keep[:CHUNK, :CHUNK]
    iref = CHUNK // 2 if reverse else CHUNK // 2 - 1
    ilast = 0 if reverse else CHUNK - 1
    order = range(nchunk - 1, -1, -1) if reverse else range(nchunk)

    def per_chunk(rows_1, width):
        return jnp.concatenate([jnp.broadcast_to(r, (CHUNK, width)) for r in rows_1], axis=0)

    prepared = []
    for h in range(heads):
        q, k, v, g = head_inputs(h)
        width = g.shape[1]
        b = sum(jnp.dot(mask_b, part, preferred_element_type=F32) for part in _split3(g))
        b_ref = [b[c * CHUNK + iref:c * CHUNK + iref + 1] for c in range(nchunk)]
        b_last = [b[c * CHUNK + ilast:c * CHUNK + ilast + 1] for c in range(nchunk)]
        e1 = jnp.exp(b - per_chunk(b_ref, width))
        r1 = 1.0 / e1
        qe = q * e1
        ke = k * r1
        qd = (qe * per_chunk([jnp.exp(r) for r in b_ref], width)).astype(BF16)
        kd = (ke * per_chunk([jnp.exp(l - r) for l, r in zip(b_last, b_ref)], width)).astype(BF16)
        decays = [jnp.exp(l) for l in b_last]
        prepared.append((qe.astype(BF16), ke.astype(BF16), qd, kd, v.astype(BF16), decays))

    for h in range(heads):
        qe, ke, qd, kd, vb, decays = prepared[h]
        if batched:
            sc = lax.dot_general(qe, ke, _NT, preferred_element_type=F32)
            sc = jnp.where(keep, sc, 0.0).astype(BF16)
            o = jnp.dot(sc, vb, preferred_element_type=F32)
        st = st_scr[h]
        parts = [None] * nchunk
        for ci in order:
            rs = slice(ci * CHUNK, (ci + 1) * CHUNK)
            if batched:
                o_c = o[rs]
            else:
                sc = lax.dot_general(qe[rs], ke[rs], _NT, preferred_element_type=F32)
                sc = jnp.where(keep_c, sc, 0.0).astype(BF16)
                o_c = jnp.dot(sc, vb[rs], preferred_element_type=F32)
            parts[ci] = o_c + lax.dot_general(qd[rs], st.astype(BF16), _NT,
                                              preferred_element_type=F32)
            st = st * decays[ci] + lax.dot_general(vb[rs], kd[rs], _TN, preferred_element_type=F32)
        if o_ref is not None:
            o_ref[:, h * dv:(h + 1) * dv] = jnp.concatenate(parts, axis=0).astype(o_ref.dtype)
        st_scr[h] = st


def _gla_kernel(*refs):
    ins = refs[:10]
    wa_ref, ba_ref = ins[8:10]
    of_ref, ob_ref, st_scr = refs[10:]

    @pl.when(pl.program_id(1) == 0)
    def _():
        st_scr[...] = jnp.zeros(st_scr.shape, F32)

    for d, o_ref in ((0, of_ref), (1, ob_ref)):
        q_ref, k_ref, v_ref, a_ref = ins[4 * d:4 * d + 4]
        dk = q_ref.shape[1] // GLA_HEADS
        dv = v_ref.shape[1] // GLA_HEADS

        def head_inputs(h, d=d, q_ref=q_ref, k_ref=k_ref, v_ref=v_ref, a_ref=a_ref, dk=dk, dv=dv):
            ks = slice(h * dk, (h + 1) * dk)
            z = jnp.dot(a_ref[...], wa_ref[d, :, ks], preferred_element_type=F32) + ba_ref[d, :, ks]
            g = _log_sigmoid(z) * (1.0 / GLA_TAU)
            q = q_ref[:, ks].astype(F32) * (dk ** -0.5)
            return q, k_ref[:, ks].astype(F32), v_ref[:, h * dv:(h + 1) * dv], g

        _block_scan(head_inputs, q_ref.shape[0], st_scr.at[d], o_ref,
                    heads=GLA_HEADS, dv=dv, reverse=d == 1, batched=True)


def _gla(u, wa, ba, *, batch, seq, n_ctx, cols, dv_total):
    ntok = u.shape[0]
    q0, k0, v0, a0 = cols
    dkt = wa.shape[2]
    dvt = dv_total
    nblk = seq // ROW_TILE
    ctx_blk0 = batch * nblk
    assert n_ctx == ROW_TILE

    def blk(d):
        def f(b, t):
            lat = b * nblk + (nblk - t if d else t - 1)
            return jnp.where(t == 0, ctx_blk0 + b, lat)
        return f

    in_specs = []
    for d in range(2):
        rb = blk(d)
        in_specs += [pl.BlockSpec((ROW_TILE, dkt), lambda b, t, rb=rb: (rb(b, t), q0 * LANE // dkt)),
                     pl.BlockSpec((ROW_TILE, dkt), lambda b, t, rb=rb: (rb(b, t), k0 * LANE // dkt)),
                     pl.BlockSpec((ROW_TILE, dvt), lambda b, t, rb=rb: (rb(b, t), v0 * LANE // dvt)),
                     pl.BlockSpec((ROW_TILE, LANE), lambda b, t, rb=rb: (rb(b, t), a0))]
    in_specs += [pl.BlockSpec(wa.shape, lambda b, t: (0, 0, 0)),
                 pl.BlockSpec(ba.shape, lambda b, t: (0, 0, 0))]
    o_shape = jax.ShapeDtypeStruct((ntok, dvt), BF16)
    return pl.pallas_call(
        _gla_kernel,
        out_shape=(o_shape, o_shape),
        grid=(batch, nblk + 1),
        in_specs=in_specs,
        out_specs=(pl.BlockSpec((ROW_TILE, dvt), lambda b, t: (blk(0)(b, t), 0)),
                   pl.BlockSpec((ROW_TILE, dvt), lambda b, t: (blk(1)(b, t), 0))),
        scratch_shapes=[pltpu.VMEM((2, GLA_HEADS, dvt // GLA_HEADS, dkt // GLA_HEADS), F32)],
        compiler_params=_params("parallel", "arbitrary"),
        name="gla",
    )(*([u] * 8), wa, ba)


def _hgrn_kernel(*refs, emit_o):
    ins = refs[:9]
    lb_ref, ub_ref, s0_ref = ins[6:9]
    if emit_o:
        of_ref, ob_ref, st_ref, st_scr = refs[9:]
    else:
        st_ref, st_scr = refs[9:]
        of_ref = ob_ref = None
    t = pl.program_id(1)

    @pl.when(t == 0)
    def _():
        st_scr[...] = s0_ref[:, 0]

    lb = lb_ref[...]
    ub = ub_ref[...]
    for d, o_ref in ((0, of_ref), (1, ob_ref)):
        q_ref, z_ref, v_ref = ins[3 * d:3 * d + 3]
        dh = q_ref.shape[1] // HGRN_HEADS

        def head_inputs(h, q_ref=q_ref, z_ref=z_ref, v_ref=v_ref, dh=dh):
            hs = slice(h * dh, (h + 1) * dh)
            z = z_ref[:, hs].astype(F32)
            e = jnp.exp(-jnp.abs(z))
            r = 1.0 / (1.0 + e)
            er = e * r
            pos = z >= 0.0
            log_f = jnp.log(lb[:, hs] + ub[:, hs] * jnp.where(pos, r, er))
            k = ub[:, hs] * jnp.where(pos, er, r)
            return q_ref[:, hs].astype(F32), k, v_ref[:, hs], log_f

        _block_scan(head_inputs, q_ref.shape[0], st_scr.at[d], o_ref,
                    heads=HGRN_HEADS, dv=dh, reverse=d == 1, batched=True)

    @pl.when(t == pl.num_programs(1) - 1)
    def _():
        st_ref[:, 0] = st_scr[...]


def _hgrn_segment(u, s0, lb, ub, *, batch, seg_len, emit_o):
    ntok = u.shape[0]
    width = lb.shape[1]
    nblk = seg_len // ROW_TILE

    def blk(d):
        return lambda b, t: b * nblk + (nblk - 1 - t if d else t)

    in_specs = []
    for d in range(2):
        rb = blk(d)
        in_specs += [pl.BlockSpec((ROW_TILE, width), lambda b, t, rb=rb: (rb(b, t), 0)),
                     pl.BlockSpec((ROW_TILE, width), lambda b, t, rb=rb, d=d: (rb(b, t), 1 + d)),
                     pl.BlockSpec((ROW_TILE, width), lambda b, t, rb=rb: (rb(b, t), 3))]
    st_spec = pl.BlockSpec((2, 1) + s0.shape[2:], lambda b, t: (0, b, 0, 0, 0))
    in_specs += [pl.BlockSpec((1, width), lambda b, t: (0, 0)),
                 pl.BlockSpec((1, width), lambda b, t: (0, 0)),
                 st_spec]
    st_shape = jax.ShapeDtypeStruct(s0.shape, F32)
    if emit_o:
        o_shape = jax.ShapeDtypeStruct((ntok, width), F32)
        out_shape = (o_shape, o_shape, st_shape)
        out_specs = (pl.BlockSpec((ROW_TILE, width), lambda b, t: (blk(0)(b, t), 0)),
                     pl.BlockSpec((ROW_TILE, width), lambda b, t: (blk(1)(b, t), 0)),
                     st_spec)
    else:
        out_shape = (st_shape,)
        out_specs = (st_spec,)
    return pl.pallas_call(
        functools.partial(_hgrn_kernel, emit_o=emit_o),
        out_shape=out_shape,
        grid=(batch, nblk),
        in_specs=in_specs,
        out_specs=out_specs,
        scratch_shapes=[pltpu.VMEM((2,) + s0.shape[2:], F32)],
        compiler_params=_params("parallel", "arbitrary"),
        name="hgrn_segment",
    )(*([u] * 6), lb, ub, s0)


def _head_norm_gate(o, gate, heads):
    dh = o.shape[1] // heads
    parts = []
    for h in range(heads):
        oh = o[:, h * dh:(h + 1) * dh]
        ms = jnp.mean(oh * oh, axis=-1, keepdims=True)
        parts.append((oh * lax.rsqrt(ms + EPS) * _silu(gate[:, h * dh:(h + 1) * dh])).astype(BF16))
    return jnp.concatenate(parts, axis=1)


def _out_proj_tail(o, x_ref, tab_ref, pg_ref, ng_ref, rt_ref, xo_ref, h_ref, lg_ref):
    xn = _gated_residual(x_ref[...], _rms(o, pg_ref[...]), tab_ref, 2)
    xo_ref[...] = xn
    h2 = _modulate(_rms(xn, ng_ref[...]), tab_ref, 3, 4)
    h_ref[...] = h2.astype(BF16)
    lg_ref[...] = lax.dot_general(rt_ref[...], h2, _NT, precision=lax.Precision.HIGHEST,
                                  preferred_element_type=F32)


def _out_proj_ab_kernel(yl_ref, yc_ref, pt_ref, of_ref, ob_ref, gate_ref, w_ref, x_ref, tab_ref,
                        pg_ref, ng_ref, rt_ref, xo_ref, h_ref, lg_ref, *, lat_tiles):
    o_att = of_ref[...].astype(F32) + ob_ref[...].astype(F32)
    yb = _head_norm_gate(o_att, gate_ref[...].astype(F32), GLA_HEADS)
    ya = jnp.where(pl.program_id(0) < lat_tiles, yl_ref[...], yc_ref[...])
    ya = jnp.concatenate(
        [jnp.dot(pt_ref[...], ya[s:s + ROW_TILE], preferred_element_type=F32).astype(BF16)
         for s in range(0, ya.shape[0], ROW_TILE)], axis=0)
    da = ya.shape[1]
    o = jnp.dot(ya, w_ref[:da], preferred_element_type=F32)
    o = o + jnp.dot(yb, w_ref[da:], preferred_element_type=F32)
    _out_proj_tail(o, x_ref, tab_ref, pg_ref, ng_ref, rt_ref, xo_ref, h_ref, lg_ref)


def _out_proj_c_kernel(of_ref, ob_ref, gate_ref, p_ref, w_ref, x_ref, tab_ref, pg_ref, ng_ref,
                       rt_ref, xo_ref, h_ref, lg_ref):
    tm = x_ref.shape[0]
    width = of_ref.shape[-1]
    o_att = of_ref[...].astype(F32) + ob_ref[...].astype(F32)
    y = _head_norm_gate(o_att.reshape(tm, width), gate_ref[...].astype(F32).reshape(tm, width),
                        HGRN_HEADS)
    y = jnp.dot(p_ref[...], y, preferred_element_type=F32).astype(BF16)
    o = jnp.dot(y, w_ref[...], preferred_element_type=F32)
    _out_proj_tail(o, x_ref, tab_ref, pg_ref, ng_ref, rt_ref, xo_ref, h_ref, lg_ref)


def _out_proj(kernel, lead, lead_specs, w, x, tab, post_gain, next_gain, router_t, tm, ntiles):
    d = x.shape[1]
    ntok = ntiles * tm
    nsub = tm // ROW_TILE
    ne = router_t.shape[0]
    in_specs = list(lead_specs) + [
        pl.BlockSpec(w.shape, lambda i: (0, 0), pipeline_mode=pl.Buffered(1)),
        pl.BlockSpec((tm, d), lambda i: (i, 0)),
        pl.BlockSpec((nsub, 8, d), lambda i: (i, 0, 0)),
        pl.BlockSpec((1, d), lambda i: (0, 0)),
        pl.BlockSpec((1, d), lambda i: (0, 0)),
        pl.BlockSpec((ne, d), lambda i: (0, 0))]
    return pl.pallas_call(
        kernel,
        out_shape=(jax.ShapeDtypeStruct((ntok, d), F32),
                   jax.ShapeDtypeStruct((ntok, d), BF16),
                   jax.ShapeDtypeStruct((ne, ntok), F32)),
        grid=(ntiles,),
        in_specs=in_specs,
        out_specs=(pl.BlockSpec((tm, d), lambda i: (i, 0)),
                   pl.BlockSpec((tm, d), lambda i: (i, 0)),
                   pl.BlockSpec((ne, tm), lambda i: (0, i))),
        compiler_params=_params("parallel"),
        name="out_proj_norm_residual",
    )(*lead, w, x, tab, post_gain, next_gain, router_t)


def _route_kernel(lg_ref, bias_ref, cw_ref):
    tm = lg_ref.shape[1]
    per_group = N_EXPERTS // N_GROUPS
    shape3 = (N_GROUPS, per_group, tm)
    aff = _sigmoid(lg_ref[...]).reshape(shape3)
    biased = aff + bias_ref[...].reshape(shape3)
    neg = -jnp.inf
    sub = lax.broadcasted_iota(jnp.int32, shape3, 1)
    grp = lax.broadcasted_iota(jnp.int32, shape3, 0)
    m1 = jnp.max(biased, axis=1, keepdims=True)
    i1 = jnp.min(jnp.where(biased == m1, sub, per_group), axis=1, keepdims=True)
    m2 = jnp.max(jnp.where(sub == i1, neg, biased), axis=1, keepdims=True)
    score = m1 + m2
    gidx = lax.broadcasted_iota(jnp.int32, score.shape, 0)
    keep = jnp.zeros(score.shape, F32)
    for _ in range(TOPK_GROUPS):
        m = jnp.max(score, axis=0, keepdims=True)
        im = jnp.min(jnp.where(score == m, gidx, N_GROUPS), axis=0, keepdims=True)
        sel = gidx == im
        keep = jnp.where(sel, 1.0, keep)
        score = jnp.where(sel, neg, score)
    work = jnp.where(jnp.broadcast_to(keep, shape3) > 0.0, biased, neg)
    eidx = grp * per_group + sub
    chosen = jnp.zeros(shape3, F32)
    for _ in range(TOP_K):
        m = jnp.max(jnp.max(work, axis=0, keepdims=True), axis=1, keepdims=True)
        cand = jnp.where(work == m, eidx, N_EXPERTS)
        im = jnp.min(jnp.min(cand, axis=0, keepdims=True), axis=1, keepdims=True)
        sel = eidx == im
        chosen = jnp.where(sel, 1.0, chosen)
        work = jnp.where(sel, neg, work)
    gate = aff * chosen
    den = jnp.sum(jnp.sum(gate, axis=0, keepdims=True), axis=1, keepdims=True)
    cw = (ROUTED_SCALE * gate / den).reshape(N_EXPERTS, tm)
    cw = jnp.concatenate([cw, jnp.zeros((LANE - N_EXPERTS, tm), F32)], axis=0)
    cw_ref[...] = cw.T


def _route(logits_t, bias, tm):
    ne, ntok = logits_t.shape
    bias_b = jnp.broadcast_to(bias.astype(F32)[:, None], (ne, tm))
    return pl.pallas_call(
        _route_kernel,
        out_shape=jax.ShapeDtypeStruct((ntok, LANE), F32),
        grid=(ntok // tm,),
        in_specs=[pl.BlockSpec((ne, tm), lambda i: (0, i)),
                  pl.BlockSpec((ne, tm), lambda i: (0, 0))],
        out_specs=pl.BlockSpec((tm, LANE), lambda i: (i, 0)),
        compiler_params=_params("parallel"),
        name="moe_route",
    )(logits_t, bias_b)


def _moe_kernel(h_ref, cw_ref, wg_ref, wu_ref, wd_ref, sg_ref, su_ref, sd_ref, x_ref, tab_ref,
                pg_ref, o_ref, acc_ref, *, epb):
    j = pl.program_id(1)
    h = h_ref[...]

    @pl.when(j == 0)
    def _():
        g = jnp.dot(h, sg_ref[...], preferred_element_type=F32)
        u = jnp.dot(h, su_ref[...], preferred_element_type=F32)
        acc_ref[...] = jnp.dot((_silu(g) * u).astype(BF16), sd_ref[...], preferred_element_type=F32)

    cw = cw_ref[...]
    lane = lax.broadcasted_iota(jnp.int32, cw.shape, 1)
    for e in range(epb):
        we = jnp.sum(jnp.where(lane == j * epb + e, cw, 0.0), axis=1, keepdims=True)
        g = jnp.dot(h, wg_ref[e].astype(BF16), preferred_element_type=F32)
        u = jnp.dot(h, wu_ref[e].astype(BF16), preferred_element_type=F32)
        a = (_silu(g) * u * we).astype(BF16)
        acc_ref[...] += jnp.dot(a, wd_ref[e].astype(BF16), preferred_element_type=F32)

    @pl.when(j == pl.num_programs(1) - 1)
    def _():
        o_ref[...] = _gated_residual(x_ref[...], _rms(acc_ref[...], pg_ref[...]), tab_ref, 5)


def _moe(h, cw, x, tab, post_gain, layer, wg, wu, wd, sg, su, sd, ntok_out, tm, epb):
    d = h.shape[1]
    _, ne, _, ff = wg.shape
    nsub = tm // ROW_TILE
    once = pl.Buffered(1)
    return pl.pallas_call(
        functools.partial(_moe_kernel, epb=epb),
        out_shape=jax.ShapeDtypeStruct((ntok_out, d), F32),
        grid=(ntok_out // tm, ne // epb),
        in_specs=[pl.BlockSpec((tm, d), lambda i, j: (i, 0)),
                  pl.BlockSpec((tm, LANE), lambda i, j: (i, 0)),
                  pl.BlockSpec((None, epb, d, ff), lambda i, j: (layer, j, 0, 0)),
                  pl.BlockSpec((None, epb, d, ff), lambda i, j: (layer, j, 0, 0)),
                  pl.BlockSpec((None, epb, ff, d), lambda i, j: (layer, j, 0, 0)),
                  pl.BlockSpec(sg.shape, lambda i, j: (0, 0), pipeline_mode=once),
                  pl.BlockSpec(su.shape, lambda i, j: (0, 0), pipeline_mode=once),
                  pl.BlockSpec(sd.shape, lambda i, j: (0, 0), pipeline_mode=once),
                  pl.BlockSpec((tm, d), lambda i, j: (i, 0), pipeline_mode=once),
                  pl.BlockSpec((nsub, 8, d), lambda i, j: (i, 0, 0)),
                  pl.BlockSpec((1, d), lambda i, j: (0, 0))],
        out_specs=pl.BlockSpec((tm, d), lambda i, j: (i, 0)),
        scratch_shapes=[pltpu.VMEM((tm, d), F32)],
        compiler_params=_params("parallel", "arbitrary", vmem=VMEM_LIMIT_MOE),
        name="moe_experts",
    )(h, cw, wg, wu, wd, sg, su, sd, x, tab, post_gain)


def _grid_permutation(band):
    tm = band * GRID_W
    dst = jnp.arange(tm)
    src = (dst % band) * GRID_W + dst // band
    return (src[:, None] == jnp.arange(tm)[None, :]).astype(BF16)


def kernel(x, c, ctx, c_ctx, mod_w, mod_b, pre_gain, post_gain, ab_w_in, ab_conv_w, ab_conv_b, ab_lru_wa, ab_lru_ba, ab_lru_wi, ab_lru_bi, ab_lru_lambda, ab_gla_wa2, ab_gla_ba, ab_w_out, c_w_in, hgrn_lb_logits, c_w_out, moe_router, moe_bias, moe_w_gate, moe_w_up, moe_w_down, moe_ws_gate, moe_ws_up, moe_ws_down):
    batch, seq, d = x.shape
    n_ctx = ctx.shape[1]
    depth = mod_w.shape[0]
    band = SUBLANE
    tm_grid = band * GRID_W
    rows_img = seq // GRID_W
    assert n_ctx == ROW_TILE and seq % (2 * ROW_TILE) == 0 and d % LANE == 0
    assert depth == 2 and batch + 1 <= 8 and rows_img % band == 0 and tm_grid % ROW_TILE == 0
    n_lat = batch * seq
    n_ctx_tok = batch * n_ctx
    ntok = n_lat + n_ctx_tok

    xs = jnp.concatenate([x.reshape(n_lat, d), ctx.reshape(n_ctx_tok, d)], axis=0)

    cvec = jnp.concatenate([c, c_ctx[None, :], jnp.zeros((8 - batch - 1, d), F32)], axis=0)
    mods = _modulation(cvec, mod_w, mod_b).reshape(depth, 8, N_MOD, d)
    tile_row = jnp.concatenate([jnp.repeat(jnp.arange(batch), seq // ROW_TILE),
                                jnp.full((batch,), batch)]).astype(jnp.int32)
    tabs = jnp.pad(mods[:, tile_row], ((0, 0), (0, 0), (0, 8 - N_MOD), (0, 0)))

    def moe(layer, h2, cwt, xs_in, ntok_out):
        return _moe(h2, cwt, xs_in, tabs[layer], post_gain[layer, 1][None, :], layer,
                    moe_w_gate, moe_w_up, moe_w_down, moe_ws_gate[layer].astype(BF16),
                    moe_ws_up[layer].astype(BF16), moe_ws_down[layer].astype(BF16),
                    ntok_out, tm_moe, epb)

    tm_proj, tm_out, tm_route, tm_moe, epb = 512, 512, 512, 1024, 2

    ab_cols = ab_w_in.shape[2]
    gla_dk = ab_gla_wa2.shape[3]
    lru_w = d
    gla_dv = (ab_cols - 2 * lru_w - 2 * gla_dk - 2 * GLA_RANK) // 2
    tn0 = 1792
    n0 = -(-(ab_cols) // tn0) * tn0
    w_in0 = jnp.pad(ab_w_in[0], ((0, 0), (0, n0 - ab_cols))).astype(BF16)
    u0 = _in_proj(xs, tabs[0], pre_gain[0, 0][None, :], w_in0, tm_proj, tn0, 0, ntok // tm_proj)

    cw = jnp.pad(ab_conv_w[0], ((0, 8 - CONV_W), (0, 0)))
    cb = ab_conv_b[0][None, :]
    wg = jnp.concatenate([ab_lru_wa[0], ab_lru_wi[0]], axis=-1).astype(BF16)
    bg = jnp.concatenate([ab_lru_ba[0].reshape(2, LRU_HEADS, 1, LANE),
                          ab_lru_bi[0].reshape(2, LRU_HEADS, 1, LANE)], axis=-1)
    ya_c, ya_l = _lru(u0, (cw, cb, wg, bg, ab_lru_lambda[0]), batch=batch, seq=seq, n_ctx=n_ctx,
                      g_col0=0, x_col0=lru_w // LANE)

    q0 = 2 * lru_w // LANE
    k0 = q0 + gla_dk // LANE
    v0 = k0 + gla_dk // LANE
    gate0 = v0 + gla_dv // LANE
    a0 = gate0 + gla_dv // LANE
    wa = jnp.stack([jnp.zeros((LANE, gla_dk), F32).at[dr * GLA_RANK:(dr + 1) * GLA_RANK]
                    .set(ab_gla_wa2[0, dr]) for dr in range(2)]).astype(BF16)
    ba = ab_gla_ba[0][:, None, :]
    o_f, o_b = _gla(u0, wa, ba, batch=batch, seq=seq, n_ctx=n_ctx, cols=(q0, k0, v0, a0),
                    dv_total=gla_dv)

    lat_tiles = n_lat // tm_out
    ctx_tiles = n_ctx_tok // tm_out
    lead_specs = [pl.BlockSpec((tm_out, lru_w), lambda i: (jnp.minimum(i, lat_tiles - 1), 0)),
                  pl.BlockSpec((tm_out, lru_w),
                               lambda i: (jnp.clip(i - lat_tiles, 0, ctx_tiles - 1), 0)),
                  pl.BlockSpec((ROW_TILE, ROW_TILE), lambda i: (0, 0)),
                  pl.BlockSpec((tm_out, gla_dv), lambda i: (i, 0)),
                  pl.BlockSpec((tm_out, gla_dv), lambda i: (i, 0)),
                  pl.BlockSpec((tm_out, gla_dv), lambda i: (i, gate0 * LANE // gla_dv))]
    xs, h2, lg = _out_proj(functools.partial(_out_proj_ab_kernel, lat_tiles=lat_tiles),
                           (ya_l, ya_c, _time_permutation().T.astype(BF16), o_f, o_b, u0), lead_specs,
                           ab_w_out[0].astype(BF16), xs, tabs[0], post_gain[0, 0][None, :],
                           pre_gain[0, 1][None, :], moe_router[0].T, tm_out, ntok // tm_out)
    xs = moe(0, h2, _route(lg, moe_bias[0], tm_route), xs, ntok)

    hd = c_w_out.shape[1]
    tn1 = 1280
    w_in1 = c_w_in[0].astype(BF16)
    c_cols = w_in1.shape[1]
    nsec = c_cols // hd
    perm = _grid_permutation(band)
    gain1 = pre_gain[1, 0][None, :]
    u1l = _in_proj_grid(xs, tabs[1], gain1, perm, w_in1, tn1, n_lat // tm_grid, band,
                        rows_img // band)
    u1c = _in_proj(xs, tabs[1], gain1, w_in1, tm_proj, tn1, n_lat // tm_proj, n_ctx_tok // tm_proj)
    lb_sm = jax.nn.softmax(hgrn_lb_logits.astype(F32), axis=0)
    lb_cum = jnp.cumsum(lb_sm, axis=0)
    lb = (lb_cum[1] - lb_cum[0])[None, :]
    ub = 1.0 - lb
    dh = hd // HGRN_HEADS
    s_zero = jnp.zeros((2, batch, HGRN_HEADS, dh, dh), F32)
    u1l_flat = u1l.reshape(n_lat, c_cols)
    (s_c,) = _hgrn_segment(u1c, s_zero, lb, ub, batch=batch, seg_len=n_ctx, emit_o=False)
    o_f, o_b, _ = _hgrn_segment(u1l_flat, s_c, lb, ub, batch=batch, seg_len=seq, emit_o=True)
    o_dir = [o.reshape(n_lat // rows_img, rows_img // band, band, hd) for o in (o_f, o_b)]

    bpi = rows_img // band
    grid_blk = (GRID_W, None, band, hd)
    lead_specs = [pl.BlockSpec(grid_blk, lambda i: (i // bpi, i % bpi, 0, 0)),
                  pl.BlockSpec(grid_blk, lambda i: (i // bpi, i % bpi, 0, 0)),
                  pl.BlockSpec(grid_blk, lambda i: (i // bpi, i % bpi, 0, nsec - 1)),
                  pl.BlockSpec((tm_grid, tm_grid), lambda i: (0, 0))]
    xs, h2, lg = _out_proj(_out_proj_c_kernel, (o_dir[0], o_dir[1], u1l, perm.T), lead_specs,
                           c_w_out[0].astype(BF16), xs, tabs[1], post_gain[1, 0][None, :],
                           pre_gain[1, 1][None, :], moe_router[1].T, tm_grid, n_lat // tm_grid)
    out = moe(1, h2, _route(lg, moe_bias[1], tm_route), xs, n_lat)
    return out.reshape(batch, seq, d)
```

```python
import functools

import jax
import jax.numpy as jnp
from jax import lax
from jax.experimental import pallas as pl
from jax.experimental.pallas import tpu as pltpu

F32 = jnp.float32
BF16 = jnp.bfloat16

EPS = 1e-6
N_MOD = 6
ROW_TILE = 256
CHUNK = 64
CHUNK_SHIFT = 6
GRID_W = 64
LRU_HEADS = 8
LRU_C = 8.0
CONV_W = 4
GLA_HEADS = 4
GLA_RANK = 16
GLA_TAU = 16.0
HGRN_HEADS = 8
N_EXPERTS = 64
TOP_K = 8
N_GROUPS = 8
TOPK_GROUPS = 4
ROUTED_SCALE = 2.5
LANE = 128
SUBLANE = 8
BF16_ROWS = 16
VMEM_LIMIT = 48 * 1024 * 1024
VMEM_LIMIT_MOE = 58 * 1024 * 1024

_NT = (((1,), (1,)), ((), ()))
_TN = (((0,), (0,)), ((), ()))


def _params(*sem, vmem=VMEM_LIMIT):
    return pltpu.CompilerParams(dimension_semantics=sem, vmem_limit_bytes=vmem)


def _rms(x, gain):
    ms = jnp.mean(x * x, axis=-1, keepdims=True)
    return x * lax.rsqrt(ms + EPS) * gain


def _sigmoid(x):
    return jax.nn.sigmoid(x)


def _silu(x):
    return x * _sigmoid(x)


def _log_sigmoid(x):
    return jnp.minimum(x, 0.0) - jnp.log1p(jnp.exp(-jnp.abs(x)))


def _softplus(x):
    return jnp.maximum(x, 0.0) + jnp.log1p(jnp.exp(-jnp.abs(x)))


def _modulate(xn, tab_ref, shift_row, scale_row):
    parts = []
    for s in range(tab_ref.shape[0]):
        t = tab_ref[s]
        rows = xn[s * ROW_TILE:(s + 1) * ROW_TILE]
        parts.append(rows * (1.0 + t[scale_row:scale_row + 1]) + t[shift_row:shift_row + 1])
    return parts[0] if len(parts) == 1 else jnp.concatenate(parts, axis=0)


def _gated_residual(x, branch, tab_ref, gate_row):
    parts = []
    for s in range(tab_ref.shape[0]):
        sl = slice(s * ROW_TILE, (s + 1) * ROW_TILE)
        parts.append(x[sl] + tab_ref[s][gate_row:gate_row + 1] * branch[sl])
    return parts[0] if len(parts) == 1 else jnp.concatenate(parts, axis=0)


def _mod_kernel(c_ref, w_ref, b_ref, o_ref):
    s = _silu(c_ref[...])
    o_ref[0] = jnp.dot(s, w_ref[0], precision=lax.Precision.HIGHEST,
                       preferred_element_type=F32) + b_ref[0]


def _modulation(cvec, mod_w, mod_b):
    depth, d, n = mod_w.shape
    tn = 1024
    return pl.pallas_call(
        _mod_kernel,
        out_shape=jax.ShapeDtypeStruct((depth, 8, n), F32),
        grid=(depth, n // tn),
        in_specs=[pl.BlockSpec((8, d), lambda l, j: (0, 0)),
                  pl.BlockSpec((1, d, tn), lambda l, j: (l, 0, j)),
                  pl.BlockSpec((1, 1, tn), lambda l, j: (l, 0, j))],
        out_specs=pl.BlockSpec((1, 8, tn), lambda l, j: (l, 0, j)),
        compiler_params=_params("parallel", "parallel"),
        name="adaln_modulation",
    )(cvec, mod_w, mod_b.reshape(depth, 1, n))


def _in_proj_kernel(x_ref, tab_ref, gain_ref, w_ref, o_ref, *, tn):
    h = _modulate(_rms(x_ref[...], gain_ref[...]), tab_ref, 0, 1).astype(BF16)
    for j in range(w_ref.shape[1] // tn):
        cs = slice(j * tn, (j + 1) * tn)
        o_ref[:, cs] = jnp.dot(h, w_ref[:, cs], preferred_element_type=F32).astype(o_ref.dtype)


def _in_proj(x, tab, gain, w, tm, tn, tab0):
    d = x.shape[1]
    n = w.shape[1]
    nsub = tm // ROW_TILE
    ntiles = x.shape[0] // tm
    return pl.pallas_call(
        functools.partial(_in_proj_kernel, tn=tn),
        out_shape=jax.ShapeDtypeStruct((ntiles * tm, n), BF16),
        grid=(ntiles,),
        in_specs=[pl.BlockSpec((tm, d), lambda i: (i, 0)),
                  pl.BlockSpec((nsub, 8, d), lambda i: (tab0 + i, 0, 0)),
                  pl.BlockSpec((1, d), lambda i: (0, 0)),
                  pl.BlockSpec((d, n), lambda i: (0, 0), pipeline_mode=pl.Buffered(1))],
        out_specs=pl.BlockSpec((tm, n), lambda i: (i, 0)),
        compiler_params=_params("parallel"),
        name="norm_mod_in_proj",
    )(x, tab, gain, w)


def _in_proj_grid_kernel(x_ref, tab_ref, gain_ref, p_ref, w_ref, o_ref, *, tn):
    h = _modulate(_rms(x_ref[...], gain_ref[...]), tab_ref, 0, 1).astype(BF16)
    hp = jnp.dot(p_ref[...], h, preferred_element_type=F32).astype(BF16)
    for j in range(w_ref.shape[1] // tn):
        cs = slice(j * tn, (j + 1) * tn)
        o = jnp.dot(hp, w_ref[:, cs], preferred_element_type=F32).astype(o_ref.dtype)
        o_ref[:, :, cs] = o.reshape(o_ref.shape[:2] + (tn,))


def _in_proj_grid(x, tab, gain, perm, w, tn, ntiles, band, bands_per_image):
    d = x.shape[1]
    n = w.shape[1]
    tm = band * GRID_W
    nsub = tm // ROW_TILE
    images = ntiles // bands_per_image
    once = pl.Buffered(1)
    return pl.pallas_call(
        functools.partial(_in_proj_grid_kernel, tn=tn),
        out_shape=jax.ShapeDtypeStruct((images * GRID_W, bands_per_image, band, n), F32),
        grid=(ntiles,),
        in_specs=[pl.BlockSpec((tm, d), lambda i: (i, 0)),
                  pl.BlockSpec((nsub, 8, d), lambda i: (i, 0, 0)),
                  pl.BlockSpec((1, d), lambda i: (0, 0)),
                  pl.BlockSpec((tm, tm), lambda i: (0, 0), pipeline_mode=once),
                  pl.BlockSpec((d, n), lambda i: (0, 0), pipeline_mode=once)],
        out_specs=pl.BlockSpec((GRID_W, None, band, n),
                               lambda i: (i // bands_per_image, i % bands_per_image, 0, 0)),
        compiler_params=_params("parallel"),
        name="norm_mod_in_proj_grid",
    )(x, tab, gain, perm, w)


def _lru_kernel(gc_ref, xc_ref, gl_ref, xl_ref, p_ref, cw_ref, cb_ref, wg_ref, bg_ref,
                lam_ref, yc_ref, yl_ref, xs_scr, hf_scr):
    T = ROW_TILE
    V = T // SUBLANE
    H = BF16_ROWS
    sub = lax.broadcasted_iota(jnp.int32, (SUBLANE, LANE), 0)
    cw = cw_ref[...]
    cb = cb_ref[...]
    sp = _softplus(-lam_ref[...])

    def vreg(a, v):
        return a[v * SUBLANE:(v + 1) * SUBLANE]

    def shift_down(y, first):
        return jnp.where(sub == 0, first, pltpu.roll(y, 1, 0))

    def shift_up(y, last):
        return jnp.where(sub == SUBLANE - 1, last, pltpu.roll(y, SUBLANE - 1, 0))

    def conv_tile(x_ref, t, nt, seg_len):
        t0 = pl.multiple_of(t * T, T)
        xp = jnp.dot(p_ref[...], x_ref[pl.ds(t0, T), :], preferred_element_type=F32)
        p0 = pl.multiple_of(jnp.maximum(t0 - H, 0), H)
        n0 = pl.multiple_of(jnp.minimum(t0 + T, seg_len - H), H)
        prev = x_ref[pl.ds(p0, H), :].astype(F32) * jnp.where(t > 0, 1.0, 0.0)
        nxt = x_ref[pl.ds(n0, H), :].astype(F32) * jnp.where(t < nt - 1, 1.0, 0.0)
        m1, m2, p1 = prev[H - 1:H], prev[H - 2:H - 1], nxt[0:1]
        last_m1 = shift_down(vreg(xp, V - 1), m1)
        xm1 = jnp.concatenate([last_m1, xp[:T - SUBLANE]], axis=0)
        xm2 = jnp.concatenate([shift_down(vreg(xp, V - 2), m2), last_m1, xp[:T - 2 * SUBLANE]], axis=0)
        xp1 = jnp.concatenate([xp[SUBLANE:], shift_up(vreg(xp, 0), p1)], axis=0)
        return cb + xm2 * cw[0:1] + xm1 * cw[1:2] + xp * cw[2:3] + xp1 * cw[3:4]

    def gates(xc, d):
        z = jnp.dot(xc.astype(BF16), wg_ref[d, 0], preferred_element_type=F32) + bg_ref[d, 0]
        r = _sigmoid(z[:, :LANE])
        i = _sigmoid(z[:, LANE:])
        log_a = (-LRU_C) * r * sp[d:d + 1]
        a = jnp.exp(log_a)
        return a, jnp.sqrt(1.0 - a * a) * (i * xc)

    def scan_tile(a, u, carry, reverse):
        hs, cum = [None] * V, [None] * V
        hp = ap = None
        for v in (range(V - 1, -1, -1) if reverse else range(V)):
            av, uv = vreg(a, v), vreg(u, v)
            hp, ap = (uv, av) if hp is None else (av * hp + uv, av * ap)
            hs[v], cum[v] = hp, ap
        s = 1
        while s < SUBLANE:
            keep = (sub < SUBLANE - s) if reverse else (sub >= s)
            sh = SUBLANE - s if reverse else s
            h_sh = jnp.where(keep, pltpu.roll(hp, sh, 0), 0.0)
            a_sh = jnp.where(keep, pltpu.roll(ap, sh, 0), 1.0)
            hp = hp + ap * h_sh
            ap = ap * a_sh
            s *= 2
        state = hp + ap * carry
        if reverse:
            start, out = shift_up(state, carry), state[0:1]
        else:
            start, out = shift_down(state, carry), state[SUBLANE - 1:SUBLANE]
        h = jnp.concatenate([hs[v] + cum[v] * start for v in range(V)], axis=0)
        return h, out

    def run_segment(g_ref, x_ref, y_ref, seg_len, carry_f, carry_b):
        nt = seg_len // T
        unroll = 2 if nt % 2 == 0 else 1

        def fwd_body(i, carry):
            for j in range(unroll):
                t = i * unroll + j
                t0 = pl.multiple_of(t * T, T)
                xc = conv_tile(x_ref, t, nt, seg_len)
                xs_scr[pl.ds(t0, T), :] = xc
                a, u = gates(xc, 0)
                h, carry = scan_tile(a, u, carry, False)
                hf_scr[pl.ds(t0, T), :] = h
            return carry

        def bwd_body(i, carry):
            for j in range(unroll):
                t = nt - 1 - (i * unroll + j)
                t0 = pl.multiple_of(t * T, T)
                a, u = gates(xs_scr[pl.ds(t0, T), :], 1)
                h, carry = scan_tile(a, u, carry, True)
                g = jnp.dot(p_ref[...], g_ref[pl.ds(t0, T), :], preferred_element_type=F32)
                rec = hf_scr[pl.ds(t0, T), :] + h
                y_ref[pl.ds(t0, T), :] = (jax.nn.gelu(g) * rec).astype(y_ref.dtype)
            return carry

        steps = nt // unroll
        return (lax.fori_loop(0, steps, fwd_body, carry_f),
                lax.fori_loop(0, steps, bwd_body, carry_b))

    zero = jnp.zeros((1, LANE), F32)
    hf, hb = run_segment(gc_ref, xc_ref, yc_ref, xc_ref.shape[0], zero, zero)
    run_segment(gl_ref, xl_ref, yl_ref, xl_ref.shape[0], hf, hb)


def _time_permutation():
    v = ROW_TILE // SUBLANE
    dst = jnp.arange(ROW_TILE)
    src = (dst % SUBLANE) * v + dst // SUBLANE
    return src[:, None] == jnp.arange(ROW_TILE)[None, :]


def _lru(u, lru_w, *, batch, seq, n_ctx, g_col0, x_col0):
    width = LRU_HEADS * LANE
    cw, cb, wg, bg, lam = lru_w
    ctx_blk0 = batch * seq // n_ctx
    perm = _time_permutation()
    once = pl.Buffered(1)
    return pl.pallas_call(
        _lru_kernel,
        out_shape=(jax.ShapeDtypeStruct((batch * n_ctx, width), BF16),
                   jax.ShapeDtypeStruct((batch * seq, width), BF16)),
        grid=(batch, LRU_HEADS),
        in_specs=[pl.BlockSpec((n_ctx, LANE), lambda b, h: (ctx_blk0 + b, g_col0 + h)),
                  pl.BlockSpec((n_ctx, LANE), lambda b, h: (ctx_blk0 + b, x_col0 + h)),
                  pl.BlockSpec((seq, LANE), lambda b, h: (b, g_col0 + h)),
                  pl.BlockSpec((seq, LANE), lambda b, h: (b, x_col0 + h)),
                  pl.BlockSpec((ROW_TILE, ROW_TILE), lambda b, h: (0, 0), pipeline_mode=once),
                  pl.BlockSpec((8, LANE), lambda b, h: (0, h)),
                  pl.BlockSpec((1, LANE), lambda b, h: (0, h)),
                  pl.BlockSpec((2, 1, LANE, 2 * LANE), lambda b, h: (0, h, 0, 0)),
                  pl.BlockSpec((2, 1, 1, 2 * LANE), lambda b, h: (0, h, 0, 0)),
                  pl.BlockSpec((2, LANE), lambda b, h: (0, h))],
        out_specs=(pl.BlockSpec((n_ctx, LANE), lambda b, h: (b, h)),
                   pl.BlockSpec((seq, LANE), lambda b, h: (b, h))),
        scratch_shapes=[pltpu.VMEM((seq, LANE), F32), pltpu.VMEM((seq, LANE), F32)],
        compiler_params=_params("parallel", "parallel"),
        name="rglru",
    )(u, u, u, u, perm.astype(BF16), cw, cb, wg, bg, lam)


def _chunk_mask(rows, reverse):
    r = lax.broadcasted_iota(jnp.int32, (rows, rows), 0)
    c = lax.broadcasted_iota(jnp.int32, (rows, rows), 1)
    same = (r >> CHUNK_SHIFT) == (c >> CHUNK_SHIFT)
    tri = (c >= r) if reverse else (c <= r)
    return jnp.where(same, jnp.where(tri, 1.0, 0.0), 0.0)


def _split3(x):
    hi = x.astype(BF16)
    r1 = x - hi.astype(F32)
    mid = r1.astype(BF16)
    lo = (r1 - mid.astype(F32)).astype(BF16)
    return hi, mid, lo


def _block_scan(head_inputs, rows, st_scr, o_ref, *, heads, dv, reverse, batched):
    nchunk = rows // CHUNK
    mask = _chunk_mask(rows, reverse)
    mask_b = mask.astype(BF16)
    keep = mask > 0.5
    keep_c = keep[:CHUNK, :CHUNK]
    iref = CHUNK // 2 if reverse else CHUNK // 2 - 1
    ilast = 0 if reverse else CHUNK - 1
    order = range(nchunk - 1, -1, -1) if reverse else range(nchunk)

    def per_chunk(rows_1, width):
        return jnp.concatenate([jnp.broadcast_to(r, (CHUNK, width)) for r in rows_1], axis=0)

    prepared = []
    for h in range(heads):
        q, k, v, g = head_inputs(h)
        width = g.shape[1]
        b = sum(jnp.dot(mask_b, part, preferred_element_type=F32) for part in _split3(g))
        b_ref = [b[c * CHUNK + iref:c * CHUNK + iref + 1] for c in range(nchunk)]
        b_last = [b[c * CHUNK + ilast:c * CHUNK + ilast + 1] for c in range(nchunk)]
        e1 = jnp.exp(b - per_chunk(b_ref, width))
        r1 = 1.0 / e1
        qe = q * e1
        ke = k * r1
        qd = (qe * per_chunk([jnp.exp(r) for r in b_ref], width)).astype(BF16)
        kd = (ke * per_chunk([jnp.exp(l - r) for l, r in zip(b_last, b_ref)], width)).astype(BF16)
        decays = [jnp.exp(l) for l in b_last]
        prepared.append((qe.astype(BF16), ke.astype(BF16), qd, kd, v.astype(BF16), decays))

    for h in range(heads):
        qe, ke, qd, kd, vb, decays = prepared[h]
        if batched:
            sc = lax.dot_general(qe, ke, _NT, preferred_element_type=F32)
            sc = jnp.where(keep, sc, 0.0).astype(BF16)
            o = jnp.dot(sc, vb, preferred_element_type=F32)
        st = st_scr[h]
        parts = [None] * nchunk
        for ci in order:
            rs = slice(ci * CHUNK, (ci + 1) * CHUNK)
            if batched:
                o_c = o[rs]
            else:
                sc = lax.dot_general(qe[rs], ke[rs], _NT, preferred_element_type=F32)
                sc = jnp.where(keep_c, sc, 0.0).astype(BF16)
                o_c = jnp.dot(sc, vb[rs], preferred_element_type=F32)
            parts[ci] = o_c + lax.dot_general(qd[rs], st.astype(BF16), _NT,
                                              preferred_element_type=F32)
            st = st * decays[ci] + lax.dot_general(vb[rs], kd[rs], _TN, preferred_element_type=F32)
        if o_ref is not None:
            o_ref[:, h * dv:(h + 1) * dv] = jnp.concatenate(parts, axis=0).astype(o_ref.dtype)
        st_scr[h] = st


def _gla_kernel(*refs):
    ins = refs[:10]
    wa_ref, ba_ref = ins[8:10]
    of_ref, ob_ref, st_scr = refs[10:]

    @pl.when(pl.program_id(1) == 0)
    def _():
        st_scr[...] = jnp.zeros(st_scr.shape, F32)

    for d, o_ref in ((0, of_ref), (1, ob_ref)):
        q_ref, k_ref, v_ref, a_ref = ins[4 * d:4 * d + 4]
        dk = q_ref.shape[1] // GLA_HEADS
        dv = v_ref.shape[1] // GLA_HEADS

        def head_inputs(h, d=d, q_ref=q_ref, k_ref=k_ref, v_ref=v_ref, a_ref=a_ref, dk=dk, dv=dv):
            ks = slice(h * dk, (h + 1) * dk)
            z = jnp.dot(a_ref[...], wa_ref[d, :, ks], preferred_element_type=F32) + ba_ref[d, :, ks]
            g = _log_sigmoid(z) * (1.0 / GLA_TAU)
            q = q_ref[:, ks].astype(F32) * (dk ** -0.5)
            return q, k_ref[:, ks].astype(F32), v_ref[:, h * dv:(h + 1) * dv], g

        _block_scan(head_inputs, q_ref.shape[0], st_scr.at[d], o_ref,
                    heads=GLA_HEADS, dv=dv, reverse=d == 1, batched=True)


def _gla(u, wa, ba, *, batch, seq, n_ctx, cols, dv_total):
    ntok = u.shape[0]
    q0, k0, v0, a0 = cols
    dkt = wa.shape[2]
    dvt = dv_total
    nblk = seq // ROW_TILE
    ctx_blk0 = batch * nblk
    assert n_ctx == ROW_TILE

    def blk(d):
        def f(b, t):
            lat = b * nblk + (nblk - t if d else t - 1)
            return jnp.where(t == 0, ctx_blk0 + b, lat)
        return f

    in_specs = []
    for d in range(2):
        rb = blk(d)
        in_specs += [pl.BlockSpec((ROW_TILE, dkt), lambda b, t, rb=rb: (rb(b, t), q0 * LANE // dkt)),
                     pl.BlockSpec((ROW_TILE, dkt), lambda b, t, rb=rb: (rb(b, t), k0 * LANE // dkt)),
                     pl.BlockSpec((ROW_TILE, dvt), lambda b, t, rb=rb: (rb(b, t), v0 * LANE // dvt)),
                     pl.BlockSpec((ROW_TILE, LANE), lambda b, t, rb=rb: (rb(b, t), a0))]
    in_specs += [pl.BlockSpec(wa.shape, lambda b, t: (0, 0, 0)),
                 pl.BlockSpec(ba.shape, lambda b, t: (0, 0, 0))]
    o_shape = jax.ShapeDtypeStruct((ntok, dvt), BF16)
    return pl.pallas_call(
        _gla_kernel,
        out_shape=(o_shape, o_shape),
        grid=(batch, nblk + 1),
        in_specs=in_specs,
        out_specs=(pl.BlockSpec((ROW_TILE, dvt), lambda b, t: (blk(0)(b, t), 0)),
                   pl.BlockSpec((ROW_TILE, dvt), lambda b, t: (blk(1)(b, t), 0))),
        scratch_shapes=[pltpu.VMEM((2, GLA_HEADS, dvt // GLA_HEADS, dkt // GLA_HEADS), F32)],
        compiler_params=_params("parallel", "arbitrary"),
        name="gla",
    )(*([u] * 8), wa, ba)


def _hgrn_kernel(*refs, emit_o):
    ins = refs[:9]
    lb_ref, ub_ref, s0_ref = ins[6:9]
    if emit_o:
        of_ref, ob_ref, st_ref, st_scr = refs[9:]
    else:
        st_ref, st_scr = refs[9:]
        of_ref = ob_ref = None
    t = pl.program_id(1)

    @pl.when(t == 0)
    def _():
        st_scr[...] = s0_ref[:, 0]

    lb = lb_ref[...]
    ub = ub_ref[...]
    for d, o_ref in ((0, of_ref), (1, ob_ref)):
        q_ref, z_ref, v_ref = ins[3 * d:3 * d + 3]
        dh = q_ref.shape[1] // HGRN_HEADS

        def head_inputs(h, q_ref=q_ref, z_ref=z_ref, v_ref=v_ref, dh=dh):
            hs = slice(h * dh, (h + 1) * dh)
            z = z_ref[:, hs].astype(F32)
            e = jnp.exp(-jnp.abs(z))
            r = 1.0 / (1.0 + e)
            er = e * r
            pos = z >= 0.0
            log_f = jnp.log(lb[:, hs] + ub[:, hs] * jnp.where(pos, r, er))
            k = ub[:, hs] * jnp.where(pos, er, r)
            return q_ref[:, hs].astype(F32), k, v_ref[:, hs], log_f

        _block_scan(head_inputs, q_ref.shape[0], st_scr.at[d], o_ref,
                    heads=HGRN_HEADS, dv=dh, reverse=d == 1, batched=True)

    @pl.when(t == pl.num_programs(1) - 1)
    def _():
        st_ref[:, 0] = st_scr[...]


def _hgrn_segment(u, s0, lb, ub, *, batch, seg_len, emit_o):
    ntok = u.shape[0]
    width = lb.shape[1]
    nblk = seg_len // ROW_TILE

    def blk(d):
        return lambda b, t: b * nblk + (nblk - 1 - t if d else t)

    in_specs = []
    for d in range(2):
        rb = blk(d)
        in_specs += [pl.BlockSpec((ROW_TILE, width), lambda b, t, rb=rb: (rb(b, t), 0)),
                     pl.BlockSpec((ROW_TILE, width), lambda b, t, rb=rb, d=d: (rb(b, t), 1 + d)),
                     pl.BlockSpec((ROW_TILE, width), lambda b, t, rb=rb: (rb(b, t), 3))]
    st_spec = pl.BlockSpec((2, 1) + s0.shape[2:], lambda b, t: (0, b, 0, 0, 0))
    in_specs += [pl.BlockSpec((1, width), lambda b, t: (0, 0)),
                 pl.BlockSpec((1, width), lambda b, t: (0, 0)),
                 st_spec]
    st_shape = jax.ShapeDtypeStruct(s0.shape, F32)
    if emit_o:
        o_shape = jax.ShapeDtypeStruct((ntok, width), F32)
        out_shape = (o_shape, o_shape, st_shape)
        out_specs = (pl.BlockSpec((ROW_TILE, width), lambda b, t: (blk(0)(b, t), 0)),
                     pl.BlockSpec((ROW_TILE, width), lambda b, t: (blk(1)(b, t), 0)),
                     st_spec)
    else:
        out_shape = (st_shape,)
        out_specs = (st_spec,)
    return pl.pallas_call(
        functools.partial(_hgrn_kernel, emit_o=emit_o),
        out_shape=out_shape,
        grid=(batch, nblk),
        in_specs=in_specs,
        out_specs=out_specs,
        scratch_shapes=[pltpu.VMEM((2,) + s0.shape[2:], F32)],
        compiler_params=_params("parallel", "arbitrary"),
        name="hgrn_segment",
    )(*([u] * 6), lb, ub, s0)


def _head_norm_gate(o, gate, heads):
    dh = o.shape[1] // heads
    parts = []
    for h in range(heads):
        oh = o[:, h * dh:(h + 1) * dh]
        ms = jnp.mean(oh * oh, axis=-1, keepdims=True)
        parts.append((oh * lax.rsqrt(ms + EPS) * _silu(gate[:, h * dh:(h + 1) * dh])).astype(BF16))
    return jnp.concatenate(parts, axis=1)


def _out_proj_tail(o, x_ref, tab_ref, pg_ref, ng_ref, rt_ref, xo_ref, h_ref, lg_ref):
    xn = _gated_residual(x_ref[...], _rms(o, pg_ref[...]), tab_ref, 2)
    xo_ref[...] = xn
    h2 = _modulate(_rms(xn, ng_ref[...]), tab_ref, 3, 4)
    h_ref[...] = h2.astype(BF16)
    lg_ref[...] = lax.dot_general(rt_ref[...], h2, _NT, precision=lax.Precision.HIGHEST,
                                  preferred_element_type=F32)


def _out_proj_ab_kernel(yl_ref, yc_ref, pt_ref, of_ref, ob_ref, gate_ref, w_ref, x_ref, tab_ref,
                        pg_ref, ng_ref, rt_ref, xo_ref, h_ref, lg_ref, *, lat_tiles):
    o_att = of_ref[...].astype(F32) + ob_ref[...].astype(F32)
    yb = _head_norm_gate(o_att, gate_ref[...].astype(F32), GLA_HEADS)
    ya = jnp.where(pl.program_id(0) < lat_tiles, yl_ref[...], yc_ref[...])
    ya = jnp.concatenate(
        [jnp.dot(pt_ref[...], ya[s:s + ROW_TILE], preferred_element_type=F32).astype(BF16)
         for s in range(0, ya.shape[0], ROW_TILE)], axis=0)
    da = ya.shape[1]
    o = jnp.dot(ya, w_ref[:da], preferred_element_type=F32)
    o = o + jnp.dot(yb, w_ref[da:], preferred_element_type=F32)
    _out_proj_tail(o, x_ref, tab_ref, pg_ref, ng_ref, rt_ref, xo_ref, h_ref, lg_ref)


def _out_proj_c_kernel(of_ref, ob_ref, gate_ref, p_ref, w_ref, x_ref, tab_ref, pg_ref, ng_ref,
                       rt_ref, xo_ref, h_ref, lg_ref):
    tm = x_ref.shape[0]
    width = of_ref.shape[-1]
    o_att = of_ref[...].astype(F32) + ob_ref[...].astype(F32)
    y = _head_norm_gate(o_att.reshape(tm, width), gate_ref[...].astype(F32).reshape(tm, width),
                        HGRN_HEADS)
    y = jnp.dot(p_ref[...], y, preferred_element_type=F32).astype(BF16)
    o = jnp.dot(y, w_ref[...], preferred_element_type=F32)
    _out_proj_tail(o, x_ref, tab_ref, pg_ref, ng_ref, rt_ref, xo_ref, h_ref, lg_ref)


def _out_proj(kernel, lead, lead_specs, w, x, tab, post_gain, next_gain, router_t, tm, ntiles):
    d = x.shape[1]
    ntok = ntiles * tm
    nsub = tm // ROW_TILE
    ne = router_t.shape[0]
    in_specs = list(lead_specs) + [
        pl.BlockSpec(w.shape, lambda i: (0, 0), pipeline_mode=pl.Buffered(1)),
        pl.BlockSpec((tm, d), lambda i: (i, 0)),
        pl.BlockSpec((nsub, 8, d), lambda i: (i, 0, 0)),
        pl.BlockSpec((1, d), lambda i: (0, 0)),
        pl.BlockSpec((1, d), lambda i: (0, 0)),
        pl.BlockSpec((ne, d), lambda i: (0, 0))]
    return pl.pallas_call(
        kernel,
        out_shape=(jax.ShapeDtypeStruct((ntok, d), F32),
                   jax.ShapeDtypeStruct((ntok, d), BF16),
                   jax.ShapeDtypeStruct((ne, ntok), F32)),
        grid=(ntiles,),
        in_specs=in_specs,
        out_specs=(pl.BlockSpec((tm, d), lambda i: (i, 0)),
                   pl.BlockSpec((tm, d), lambda i: (i, 0)),
                   pl.BlockSpec((ne, tm), lambda i: (0, i))),
        compiler_params=_params("parallel"),
        name="out_proj_norm_residual",
    )(*lead, w, x, tab, post_gain, next_gain, router_t)


def _route_kernel(lg_ref, bias_ref, cw_ref):
    tm = lg_ref.shape[1]
    per_group = N_EXPERTS // N_GROUPS
    shape3 = (N_GROUPS, per_group, tm)
    aff = _sigmoid(lg_ref[...]).reshape(shape3)
    biased = aff + bias_ref[...].reshape(shape3)
    neg = -jnp.inf
    sub = lax.broadcasted_iota(jnp.int32, shape3, 1)
    grp = lax.broadcasted_iota(jnp.int32, shape3, 0)
    m1 = jnp.max(biased, axis=1, keepdims=True)
    i1 = jnp.min(jnp.where(biased == m1, sub, per_group), axis=1, keepdims=True)
    m2 = jnp.max(jnp.where(sub == i1, neg, biased), axis=1, keepdims=True)
    score = m1 + m2
    gidx = lax.broadcasted_iota(jnp.int32, score.shape, 0)
    keep = jnp.zeros(score.shape, F32)
    for _ in range(TOPK_GROUPS):
        m = jnp.max(score, axis=0, keepdims=True)
        im = jnp.min(jnp.where(score == m, gidx, N_GROUPS), axis=0, keepdims=True)
        sel = gidx == im
        keep = jnp.where(sel, 1.0, keep)
        score = jnp.where(sel, neg, score)
    work = jnp.where(jnp.broadcast_to(keep, shape3) > 0.0, biased, neg)
    eidx = grp * per_group + sub
    chosen = jnp.zeros(shape3, F32)
    for _ in range(TOP_K):
        m = jnp.max(jnp.max(work, axis=0, keepdims=True), axis=1, keepdims=True)
        cand = jnp.where(work == m, eidx, N_EXPERTS)
        im = jnp.min(jnp.min(cand, axis=0, keepdims=True), axis=1, keepdims=True)
        sel = eidx == im
        chosen = jnp.where(sel, 1.0, chosen)
        work = jnp.where(sel, neg, work)
    gate = aff * chosen
    den = jnp.sum(jnp.sum(gate, axis=0, keepdims=True), axis=1, keepdims=True)
    cw = (ROUTED_SCALE * gate / den).reshape(N_EXPERTS, tm)
    cw = jnp.concatenate([cw, jnp.zeros((LANE - N_EXPERTS, tm), F32)], axis=0)
    cw_ref[...] = cw.T


def _route(logits_t, bias, tm):
    ne, ntok = logits_t.shape
    bias_b = jnp.broadcast_to(bias.astype(F32)[:, None], (ne, tm))
    return pl.pallas_call(
        _route_kernel,
        out_shape=jax.ShapeDtypeStruct((ntok, LANE), F32),
        grid=(ntok // tm,),
        in_specs=[pl.BlockSpec((ne, tm), lambda i: (0, i)),
                  pl.BlockSpec((ne, tm), lambda i: (0, 0))],
        out_specs=pl.BlockSpec((tm, LANE), lambda i: (i, 0)),
        compiler_params=_params("parallel"),
        name="moe_route",
    )(logits_t, bias_b)


def _moe_kernel(h_ref, cw_ref, wg_ref, wu_ref, wd_ref, sg_ref, su_ref, sd_ref, x_ref, tab_ref,
                pg_ref, o_ref, *, epb):
    j = pl.program_id(1)
    h = h_ref[...]

    @pl.when(j == 0)
    def _():
        g = jnp.dot(h, sg_ref[...], preferred_element_type=F32)
        u = jnp.dot(h, su_ref[...], preferred_element_type=F32)
        o_ref[...] = jnp.dot((_silu(g) * u).astype(BF16), sd_ref[...], preferred_element_type=F32)

    cw = cw_ref[...]
    lane = lax.broadcasted_iota(jnp.int32, cw.shape, 1)
    for e in range(epb):
        we = jnp.sum(jnp.where(lane == j * epb + e, cw, 0.0), axis=1, keepdims=True)
        g = jnp.dot(h, wg_ref[e].astype(BF16), preferred_element_type=F32)
        u = jnp.dot(h, wu_ref[e].astype(BF16), preferred_element_type=F32)
        a = (_silu(g) * u * we).astype(BF16)
        o_ref[...] += jnp.dot(a, wd_ref[e].astype(BF16), preferred_element_type=F32)

    @pl.when(j == pl.num_programs(1) - 1)
    def _():
        o_ref[...] = _gated_residual(x_ref[...], _rms(o_ref[...], pg_ref[...]), tab_ref, 5)


def _moe(h, cw, x, tab, post_gain, layer, wg, wu, wd, sg, su, sd, tile0, ntiles, tm, epb):
    d = h.shape[1]
    _, ne, _, ff = wg.shape
    nsub = tm // ROW_TILE
    once = pl.Buffered(1)
    return pl.pallas_call(
        functools.partial(_moe_kernel, epb=epb),
        out_shape=jax.ShapeDtypeStruct((ntiles * tm, d), F32),
        grid=(ntiles, ne // epb),
        in_specs=[pl.BlockSpec((tm, d), lambda i, j: (tile0 + i, 0), pipeline_mode=once),
                  pl.BlockSpec((tm, LANE), lambda i, j: (tile0 + i, 0), pipeline_mode=once),
                  pl.BlockSpec((None, epb, d, ff), lambda i, j: (layer, j, 0, 0)),
                  pl.BlockSpec((None, epb, d, ff), lambda i, j: (layer, j, 0, 0)),
                  pl.BlockSpec((None, epb, ff, d), lambda i, j: (layer, j, 0, 0)),
                  pl.BlockSpec(sg.shape, lambda i, j: (0, 0), pipeline_mode=once),
                  pl.BlockSpec(su.shape, lambda i, j: (0, 0), pipeline_mode=once),
                  pl.BlockSpec(sd.shape, lambda i, j: (0, 0), pipeline_mode=once),
                  pl.BlockSpec((tm, d), lambda i, j: (tile0 + i, 0), pipeline_mode=once),
                  pl.BlockSpec((nsub, 8, d), lambda i, j: (tile0 + i, 0, 0)),
                  pl.BlockSpec((1, d), lambda i, j: (0, 0))],
        out_specs=pl.BlockSpec((tm, d), lambda i, j: (i, 0)),
        compiler_params=_params("parallel", "arbitrary", vmem=VMEM_LIMIT_MOE),
        name="moe_experts",
    )(h, cw, wg, wu, wd, sg, su, sd, x, tab, post_gain)


def _grid_permutation(band):
    tm = band * GRID_W
    dst = jnp.arange(tm)
    src = (dst % band) * GRID_W + dst // band
    return (src[:, None] == jnp.arange(tm)[None, :]).astype(BF16)


def kernel(x, c, ctx, c_ctx, mod_w, mod_b, pre_gain, post_gain, ab_w_in, ab_conv_w, ab_conv_b, ab_lru_wa, ab_lru_ba, ab_lru_wi, ab_lru_bi, ab_lru_lambda, ab_gla_wa2, ab_gla_ba, ab_w_out, c_w_in, hgrn_lb_logits, c_w_out, moe_router, moe_bias, moe_w_gate, moe_w_up, moe_w_down, moe_ws_gate, moe_ws_up, moe_ws_down):
    batch, seq, d = x.shape
    n_ctx = ctx.shape[1]
    depth = mod_w.shape[0]
    band = SUBLANE
    tm_grid = band * GRID_W
    rows_img = seq // GRID_W
    assert n_ctx == ROW_TILE and seq % (2 * ROW_TILE) == 0 and d % LANE == 0
    assert depth == 2 and batch + 1 <= 8 and rows_img % band == 0 and tm_grid % ROW_TILE == 0
    n_lat = batch * seq
    n_ctx_tok = batch * n_ctx
    ntok = n_lat + n_ctx_tok

    xs = jnp.concatenate([x.reshape(n_lat, d), ctx.reshape(n_ctx_tok, d)], axis=0)

    cvec = jnp.concatenate([c, c_ctx[None, :], jnp.zeros((8 - batch - 1, d), F32)], axis=0)
    mods = _modulation(cvec, mod_w, mod_b).reshape(depth, 8, N_MOD, d)
    tile_row = jnp.concatenate([jnp.repeat(jnp.arange(batch), seq // ROW_TILE),
                                jnp.full((batch,), batch)]).astype(jnp.int32)
    tabs = jnp.pad(mods[:, tile_row], ((0, 0), (0, 0), (0, 8 - N_MOD), (0, 0)))

    def moe(layer, h2, cwt, xs_in, tile0, ntiles, tm):
        return _moe(h2, cwt, xs_in, tabs[layer], post_gain[layer, 1][None, :], layer,
                    moe_w_gate, moe_w_up, moe_w_down, moe_ws_gate[layer].astype(BF16),
                    moe_ws_up[layer].astype(BF16), moe_ws_down[layer].astype(BF16),
                    tile0, ntiles, tm, epb)

    tm_proj, tm_out, tm_route, tm_moe, epb = 512, 512, 512, 2048, 2
    assert n_lat % tm_moe == 0 and n_lat % n_ctx_tok == 0

    ab_cols = ab_w_in.shape[2]
    gla_dk = ab_gla_wa2.shape[3]
    lru_w = d
    gla_dv = (ab_cols - 2 * lru_w - 2 * gla_dk - 2 * GLA_RANK) // 2
    tn0 = 1792
    n0 = -(-(ab_cols) // tn0) * tn0
    w_in0 = jnp.pad(ab_w_in[0], ((0, 0), (0, n0 - ab_cols))).astype(BF16)
    u0 = _in_proj(xs, tabs[0], pre_gain[0, 0][None, :], w_in0, tm_proj, tn0, 0)

    cw = jnp.pad(ab_conv_w[0], ((0, 8 - CONV_W), (0, 0)))
    cb = ab_conv_b[0][None, :]
    wg = jnp.concatenate([ab_lru_wa[0], ab_lru_wi[0]], axis=-1).astype(BF16)
    bg = jnp.concatenate([ab_lru_ba[0].reshape(2, LRU_HEADS, 1, LANE),
                          ab_lru_bi[0].reshape(2, LRU_HEADS, 1, LANE)], axis=-1)
    ya_c, ya_l = _lru(u0, (cw, cb, wg, bg, ab_lru_lambda[0]), batch=batch, seq=seq, n_ctx=n_ctx,
                      g_col0=0, x_col0=lru_w // LANE)

    q0 = 2 * lru_w // LANE
    k0 = q0 + gla_dk // LANE
    v0 = k0 + gla_dk // LANE
    gate0 = v0 + gla_dv // LANE
    a0 = gate0 + gla_dv // LANE
    wa = jnp.stack([jnp.zeros((LANE, gla_dk), F32).at[dr * GLA_RANK:(dr + 1) * GLA_RANK]
                    .set(ab_gla_wa2[0, dr]) for dr in range(2)]).astype(BF16)
    ba = ab_gla_ba[0][:, None, :]
    o_f, o_b = _gla(u0, wa, ba, batch=batch, seq=seq, n_ctx=n_ctx, cols=(q0, k0, v0, a0),
                    dv_total=gla_dv)

    lat_tiles = n_lat // tm_out
    ctx_tiles = n_ctx_tok // tm_out
    lead_specs = [pl.BlockSpec((tm_out, lru_w), lambda i: (jnp.minimum(i, lat_tiles - 1), 0)),
                  pl.BlockSpec((tm_out, lru_w),
                               lambda i: (jnp.clip(i - lat_tiles, 0, ctx_tiles - 1), 0)),
                  pl.BlockSpec((ROW_TILE, ROW_TILE), lambda i: (0, 0)),
                  pl.BlockSpec((tm_out, gla_dv), lambda i: (i, 0)),
                  pl.BlockSpec((tm_out, gla_dv), lambda i: (i, 0)),
                  pl.BlockSpec((tm_out, gla_dv), lambda i: (i, gate0 * LANE // gla_dv))]
    xs, h2, lg = _out_proj(functools.partial(_out_proj_ab_kernel, lat_tiles=lat_tiles),
                           (ya_l, ya_c, _time_permutation().T.astype(BF16), o_f, o_b, u0), lead_specs,
                           ab_w_out[0].astype(BF16), xs, tabs[0], post_gain[0, 0][None, :],
                           pre_gain[0, 1][None, :], moe_router[0].T, tm_out, ntok // tm_out)
    cwt = _route(lg, moe_bias[0], tm_route)
    xs_ctx = moe(0, h2, cwt, xs, n_lat // n_ctx_tok, 1, n_ctx_tok)
    xs = moe(0, h2, cwt, xs, 0, n_lat // tm_moe, tm_moe)

    hd = c_w_out.shape[1]
    tn1 = 1280
    w_in1 = c_w_in[0].astype(BF16)
    c_cols = w_in1.shape[1]
    nsec = c_cols // hd
    perm = _grid_permutation(band)
    gain1 = pre_gain[1, 0][None, :]
    u1l = _in_proj_grid(xs, tabs[1], gain1, perm, w_in1, tn1, n_lat // tm_grid, band,
                        rows_img // band)
    u1c = _in_proj(xs_ctx, tabs[1], gain1, w_in1, tm_proj, tn1, n_lat // tm_proj)
    lb_sm = jax.nn.softmax(hgrn_lb_logits.astype(F32), axis=0)
    lb_cum = jnp.cumsum(lb_sm, axis=0)
    lb = (lb_cum[1] - lb_cum[0])[None, :]
    ub = 1.0 - lb
    dh = hd // HGRN_HEADS
    s_zero = jnp.zeros((2, batch, HGRN_HEADS, dh, dh), F32)
    u1l_flat = u1l.reshape(n_lat, c_cols)
    (s_c,) = _hgrn_segment(u1c, s_zero, lb, ub, batch=batch, seg_len=n_ctx, emit_o=False)
    o_f, o_b, _ = _hgrn_segment(u1l_flat, s_c, lb, ub, batch=batch, seg_len=seq, emit_o=True)
    o_dir = [o.reshape(n_lat // rows_img, rows_img // band, band, hd) for o in (o_f, o_b)]

    bpi = rows_img // band
    grid_blk = (GRID_W, None, band, hd)
    lead_specs = [pl.BlockSpec(grid_blk, lambda i: (i // bpi, i % bpi, 0, 0)),
                  pl.BlockSpec(grid_blk, lambda i: (i // bpi, i % bpi, 0, 0)),
                  pl.BlockSpec(grid_blk, lambda i: (i // bpi, i % bpi, 0, nsec - 1)),
                  pl.BlockSpec((tm_grid, tm_grid), lambda i: (0, 0))]
    xs, h2, lg = _out_proj(_out_proj_c_kernel, (o_dir[0], o_dir[1], u1l, perm.T), lead_specs,
                           c_w_out[0].astype(BF16), xs, tabs[1], post_gain[1, 0][None, :],
                           pre_gain[1, 1][None, :], moe_router[1].T, tm_grid, n_lat // tm_grid)
    out = moe(1, h2, _route(lg, moe_bias[1], tm_route), xs, 0, n_lat // tm_moe, tm_moe)
    return out.reshape(batch, seq, d)
```

```python
import functools

import jax
import jax.numpy as jnp
from jax import lax
from jax.experimental import pallas as pl
from jax.experimental.pallas import tpu as pltpu

F32 = jnp.float32
BF16 = jnp.bfloat16

EPS = 1e-6
N_MOD = 6
ROW_TILE = 256
CHUNK = 64
CHUNK_SHIFT = 6
GRID_W = 64
LRU_HEADS = 8
LRU_C = 8.0
CONV_W = 4
GLA_HEADS = 4
GLA_RANK = 16
GLA_TAU = 16.0
HGRN_HEADS = 8
N_EXPERTS = 64
TOP_K = 8
N_GROUPS = 8
TOPK_GROUPS = 4
ROUTED_SCALE = 2.5
LANE = 128
SUBLANE = 8
BF16_ROWS = 16
VMEM_LIMIT = 48 * 1024 * 1024
VMEM_LIMIT_MOE = 58 * 1024 * 1024

_NT = (((1,), (1,)), ((), ()))
_TN = (((0,), (0,)), ((), ()))


def _params(*sem, vmem=VMEM_LIMIT):
    return pltpu.CompilerParams(dimension_semantics=sem, vmem_limit_bytes=vmem)


def _rms(x, gain):
    ms = jnp.mean(x * x, axis=-1, keepdims=True)
    return x * lax.rsqrt(ms + EPS) * gain


def _sigmoid(x):
    return jax.nn.sigmoid(x)


def _silu(x):
    return x * _sigmoid(x)


def _log_sigmoid(x):
    return jnp.minimum(x, 0.0) - jnp.log1p(jnp.exp(-jnp.abs(x)))


def _softplus(x):
    return jnp.maximum(x, 0.0) + jnp.log1p(jnp.exp(-jnp.abs(x)))


def _modulate(xn, tab_ref, shift_row, scale_row):
    parts = []
    for s in range(tab_ref.shape[0]):
        t = tab_ref[s]
        rows = xn[s * ROW_TILE:(s + 1) * ROW_TILE]
        parts.append(rows * (1.0 + t[scale_row:scale_row + 1]) + t[shift_row:shift_row + 1])
    return parts[0] if len(parts) == 1 else jnp.concatenate(parts, axis=0)


def _gated_residual(x, branch, tab_ref, gate_row):
    parts = []
    for s in range(tab_ref.shape[0]):
        sl = slice(s * ROW_TILE, (s + 1) * ROW_TILE)
        parts.append(x[sl] + tab_ref[s][gate_row:gate_row + 1] * branch[sl])
    return parts[0] if len(parts) == 1 else jnp.concatenate(parts, axis=0)


def _mod_kernel(c_ref, w_ref, b_ref, o_ref):
    s = _silu(c_ref[...])
    o_ref[0] = jnp.dot(s, w_ref[0], precision=lax.Precision.HIGHEST,
                       preferred_element_type=F32) + b_ref[0]


def _modulation(cvec, mod_w, mod_b):
    depth, d, n = mod_w.shape
    tn = 1024
    return pl.pallas_call(
        _mod_kernel,
        out_shape=jax.ShapeDtypeStruct((depth, 8, n), F32),
        grid=(depth, n // tn),
        in_specs=[pl.BlockSpec((8, d), lambda l, j: (0, 0)),
                  pl.BlockSpec((1, d, tn), lambda l, j: (l, 0, j)),
                  pl.BlockSpec((1, 1, tn), lambda l, j: (l, 0, j))],
        out_specs=pl.BlockSpec((1, 8, tn), lambda l, j: (l, 0, j)),
        compiler_params=_params("parallel", "parallel"),
        name="adaln_modulation",
    )(cvec, mod_w, mod_b.reshape(depth, 1, n))


def _in_proj_kernel(x_ref, tab_ref, gain_ref, w_ref, o_ref, *, tn):
    h = _modulate(_rms(x_ref[...], gain_ref[...]), tab_ref, 0, 1).astype(BF16)
    for j in range(w_ref.shape[1] // tn):
        cs = slice(j * tn, (j + 1) * tn)
        o_ref[:, cs] = jnp.dot(h, w_ref[:, cs], preferred_element_type=F32).astype(o_ref.dtype)


def _in_proj(x, tab, gain, w, tm, tn, tab0):
    d = x.shape[1]
    n = w.shape[1]
    nsub = tm // ROW_TILE
    ntiles = x.shape[0] // tm
    return pl.pallas_call(
        functools.partial(_in_proj_kernel, tn=tn),
        out_shape=jax.ShapeDtypeStruct((ntiles * tm, n), BF16),
        grid=(ntiles,),
        in_specs=[pl.BlockSpec((tm, d), lambda i: (i, 0)),
                  pl.BlockSpec((nsub, 8, d), lambda i: (tab0 + i, 0, 0)),
                  pl.BlockSpec((1, d), lambda i: (0, 0)),
                  pl.BlockSpec((d, n), lambda i: (0, 0), pipeline_mode=pl.Buffered(1))],
        out_specs=pl.BlockSpec((tm, n), lambda i: (i, 0)),
        compiler_params=_params("parallel"),
        name="norm_mod_in_proj",
    )(x, tab, gain, w)


def _in_proj_grid_kernel(x_ref, tab_ref, gain_ref, p_ref, w_ref, o_ref, *, tn):
    h = _modulate(_rms(x_ref[...], gain_ref[...]), tab_ref, 0, 1).astype(BF16)
    hp = jnp.dot(p_ref[...], h, preferred_element_type=F32).astype(BF16)
    for j in range(w_ref.shape[1] // tn):
        cs = slice(j * tn, (j + 1) * tn)
        o = jnp.dot(hp, w_ref[:, cs], preferred_element_type=F32).astype(o_ref.dtype)
        o_ref[:, :, cs] = o.reshape(o_ref.shape[:2] + (tn,))


def _in_proj_grid(x, tab, gain, perm, w, tn, ntiles, band, bands_per_image):
    d = x.shape[1]
    n = w.shape[1]
    tm = band * GRID_W
    nsub = tm // ROW_TILE
    images = ntiles // bands_per_image
    once = pl.Buffered(1)
    return pl.pallas_call(
        functools.partial(_in_proj_grid_kernel, tn=tn),
        out_shape=jax.ShapeDtypeStruct((images * GRID_W, bands_per_image, band, n), F32),
        grid=(ntiles,),
        in_specs=[pl.BlockSpec((tm, d), lambda i: (i, 0)),
                  pl.BlockSpec((nsub, 8, d), lambda i: (i, 0, 0)),
                  pl.BlockSpec((1, d), lambda i: (0, 0)),
                  pl.BlockSpec((tm, tm), lambda i: (0, 0), pipeline_mode=once),
                  pl.BlockSpec((d, n), lambda i: (0, 0), pipeline_mode=once)],
        out_specs=pl.BlockSpec((GRID_W, None, band, n),
                               lambda i: (i // bands_per_image, i % bands_per_image, 0, 0)),
        compiler_params=_params("parallel"),
        name="norm_mod_in_proj_grid",
    )(x, tab, gain, perm, w)


def _lru_kernel(gc_ref, xc_ref, gl_ref, xl_ref, p_ref, cw_ref, cb_ref, wg_ref, bg_ref,
                lam_ref, yc_ref, yl_ref, xs_scr, hf_scr):
    T = ROW_TILE
    V = T // SUBLANE
    H = BF16_ROWS
    sub = lax.broadcasted_iota(jnp.int32, (SUBLANE, LANE), 0)
    cw = cw_ref[...]
    cb = cb_ref[...]
    sp = _softplus(-lam_ref[...])

    def vreg(a, v):
        return a[v * SUBLANE:(v + 1) * SUBLANE]

    def shift_down(y, first):
        return jnp.where(sub == 0, first, pltpu.roll(y, 1, 0))

    def shift_up(y, last):
        return jnp.where(sub == SUBLANE - 1, last, pltpu.roll(y, SUBLANE - 1, 0))

    def conv_tile(x_ref, t, nt, seg_len):
        t0 = pl.multiple_of(t * T, T)
        xp = jnp.dot(p_ref[...], x_ref[pl.ds(t0, T), :], preferred_element_type=F32)
        p0 = pl.multiple_of(jnp.maximum(t0 - H, 0), H)
        n0 = pl.multiple_of(jnp.minimum(t0 + T, seg_len - H), H)
        prev = x_ref[pl.ds(p0, H), :].astype(F32) * jnp.where(t > 0, 1.0, 0.0)
        nxt = x_ref[pl.ds(n0, H), :].astype(F32) * jnp.where(t < nt - 1, 1.0, 0.0)
        m1, m2, p1 = prev[H - 1:H], prev[H - 2:H - 1], nxt[0:1]
        last_m1 = shift_down(vreg(xp, V - 1), m1)
        xm1 = jnp.concatenate([last_m1, xp[:T - SUBLANE]], axis=0)
        xm2 = jnp.concatenate([shift_down(vreg(xp, V - 2), m2), last_m1, xp[:T - 2 * SUBLANE]], axis=0)
        xp1 = jnp.concatenate([xp[SUBLANE:], shift_up(vreg(xp, 0), p1)], axis=0)
        return cb + xm2 * cw[0:1] + xm1 * cw[1:2] + xp * cw[2:3] + xp1 * cw[3:4]

    def gates(xc, d):
        z = jnp.dot(xc.astype(BF16), wg_ref[d, 0], preferred_element_type=F32) + bg_ref[d, 0]
        r = _sigmoid(z[:, :LANE])
        i = _sigmoid(z[:, LANE:])
        log_a = (-LRU_C) * r * sp[d:d + 1]
        a = jnp.exp(log_a)
        return a, jnp.sqrt(1.0 - a * a) * (i * xc)

    def scan_tile(a, u, carry, reverse):
        hs, cum = [None] * V, [None] * V
        hp = ap = None
        for v in (range(V - 1, -1, -1) if reverse else range(V)):
            av, uv = vreg(a, v), vreg(u, v)
            hp, ap = (uv, av) if hp is None else (av * hp + uv, av * ap)
            hs[v], cum[v] = hp, ap
        s = 1
        while s < SUBLANE:
            keep = (sub < SUBLANE - s) if reverse else (sub >= s)
            sh = SUBLANE - s if reverse else s
            h_sh = jnp.where(keep, pltpu.roll(hp, sh, 0), 0.0)
            a_sh = jnp.where(keep, pltpu.roll(ap, sh, 0), 1.0)
            hp = hp + ap * h_sh
            ap = ap * a_sh
            s *= 2
        state = hp + ap * carry
        if reverse:
            start, out = shift_up(state, carry), state[0:1]
        else:
            start, out = shift_down(state, carry), state[SUBLANE - 1:SUBLANE]
        h = jnp.concatenate([hs[v] + cum[v] * start for v in range(V)], axis=0)
        return h, out

    def run_segment(g_ref, x_ref, y_ref, seg_len, carry_f, carry_b):
        nt = seg_len // T
        unroll = 4 if nt % 4 == 0 else (2 if nt % 2 == 0 else 1)

        def fwd_body(i, carry):
            for j in range(unroll):
                t = i * unroll + j
                t0 = pl.multiple_of(t * T, T)
                xc = conv_tile(x_ref, t, nt, seg_len)
                xs_scr[pl.ds(t0, T), :] = xc
                a, u = gates(xc, 0)
                h, carry = scan_tile(a, u, carry, False)
                hf_scr[pl.ds(t0, T), :] = h
            return carry

        def bwd_body(i, carry):
            for j in range(unroll):
                t = nt - 1 - (i * unroll + j)
                t0 = pl.multiple_of(t * T, T)
                a, u = gates(xs_scr[pl.ds(t0, T), :], 1)
                h, carry = scan_tile(a, u, carry, True)
                g = jnp.dot(p_ref[...], g_ref[pl.ds(t0, T), :], preferred_element_type=F32)
                rec = hf_scr[pl.ds(t0, T), :] + h
                y_ref[pl.ds(t0, T), :] = (jax.nn.gelu(g) * rec).astype(y_ref.dtype)
            return carry

        steps = nt // unroll
        return (lax.fori_loop(0, steps, fwd_body, carry_f),
                lax.fori_loop(0, steps, bwd_body, carry_b))

    zero = jnp.zeros((1, LANE), F32)
    hf, hb = run_segment(gc_ref, xc_ref, yc_ref, xc_ref.shape[0], zero, zero)
    run_segment(gl_ref, xl_ref, yl_ref, xl_ref.shape[0], hf, hb)


def _time_permutation():
    v = ROW_TILE // SUBLANE
    dst = jnp.arange(ROW_TILE)
    src = (dst % SUBLANE) * v + dst // SUBLANE
    return src[:, None] == jnp.arange(ROW_TILE)[None, :]


def _lru(u, lru_w, *, batch, seq, n_ctx, g_col0, x_col0):
    width = LRU_HEADS * LANE
    cw, cb, wg, bg, lam = lru_w
    ctx_blk0 = batch * seq // n_ctx
    perm = _time_permutation()
    once = pl.Buffered(1)
    return pl.pallas_call(
        _lru_kernel,
        out_shape=(jax.ShapeDtypeStruct((batch * n_ctx, width), BF16),
                   jax.ShapeDtypeStruct((batch * seq, width), BF16)),
        grid=(batch, LRU_HEADS),
        in_specs=[pl.BlockSpec((n_ctx, LANE), lambda b, h: (ctx_blk0 + b, g_col0 + h)),
                  pl.BlockSpec((n_ctx, LANE), lambda b, h: (ctx_blk0 + b, x_col0 + h)),
                  pl.BlockSpec((seq, LANE), lambda b, h: (b, g_col0 + h)),
                  pl.BlockSpec((seq, LANE), lambda b, h: (b, x_col0 + h)),
                  pl.BlockSpec((ROW_TILE, ROW_TILE), lambda b, h: (0, 0), pipeline_mode=once),
                  pl.BlockSpec((8, LANE), lambda b, h: (0, h)),
                  pl.BlockSpec((1, LANE), lambda b, h: (0, h)),
                  pl.BlockSpec((2, 1, LANE, 2 * LANE), lambda b, h: (0, h, 0, 0)),
                  pl.BlockSpec((2, 1, 1, 2 * LANE), lambda b, h: (0, h, 0, 0)),
                  pl.BlockSpec((2, LANE), lambda b, h: (0, h))],
        out_specs=(pl.BlockSpec((n_ctx, LANE), lambda b, h: (b, h)),
                   pl.BlockSpec((seq, LANE), lambda b, h: (b, h))),
        scratch_shapes=[pltpu.VMEM((seq, LANE), F32), pltpu.VMEM((seq, LANE), F32)],
        compiler_params=_params("parallel", "parallel"),
        name="rglru",
    )(u, u, u, u, perm.astype(BF16), cw, cb, wg, bg, lam)


def _chunk_mask(rows, reverse):
    r = lax.broadcasted_iota(jnp.int32, (rows, rows), 0)
    c = lax.broadcasted_iota(jnp.int32, (rows, rows), 1)
    same = (r >> CHUNK_SHIFT) == (c >> CHUNK_SHIFT)
    tri = (c >= r) if reverse else (c <= r)
    return jnp.where(same, jnp.where(tri, 1.0, 0.0), 0.0)


def _split3(x):
    hi = x.astype(BF16)
    r1 = x - hi.astype(F32)
    mid = r1.astype(BF16)
    lo = (r1 - mid.astype(F32)).astype(BF16)
    return hi, mid, lo


def _block_scan(g_all, head_qkv, st_scr, o_ref, *, heads, dv, reverse, batched):
    rows, wall = g_all.shape
    width = wall // heads
    nchunk = rows // CHUNK
    mask = _chunk_mask(rows, reverse)
    mask_b = mask.astype(BF16)
    keep = mask > 0.5
    keep_c = keep[:CHUNK, :CHUNK]
    iref = CHUNK // 2 if reverse else CHUNK // 2 - 1
    ilast = 0 if reverse else CHUNK - 1
    order = range(nchunk - 1, -1, -1) if reverse else range(nchunk)

    def per_chunk(rows_1, width):
        return jnp.concatenate([jnp.broadcast_to(r, (CHUNK, width)) for r in rows_1], axis=0)

    pieces = jnp.concatenate(_split3(g_all), axis=1)
    sums = jnp.dot(mask_b, pieces, preferred_element_type=F32)
    b_all = sums[:, :wall] + sums[:, wall:2 * wall] + sums[:, 2 * wall:]

    def prepare(h):
        q, k, v = head_qkv(h)
        b = b_all[:, h * width:(h + 1) * width]
        b_ref = [b[c * CHUNK + iref:c * CHUNK + iref + 1] for c in range(nchunk)]
        b_last = [b[c * CHUNK + ilast:c * CHUNK + ilast + 1] for c in range(nchunk)]
        e1 = jnp.exp(b - per_chunk(b_ref, width))
        r1 = 1.0 / e1
        qe = q * e1
        ke = k * r1
        qd = (qe * per_chunk([jnp.exp(r) for r in b_ref], width)).astype(BF16)
        kd = (ke * per_chunk([jnp.exp(l - r) for l, r in zip(b_last, b_ref)], width)).astype(BF16)
        decays = [jnp.exp(l) for l in b_last]
        return qe.astype(BF16), ke.astype(BF16), qd, kd, v.astype(BF16), decays

    nxt = prepare(0)
    for h in range(heads):
        qe, ke, qd, kd, vb, decays = nxt
        if h + 1 < heads:
            nxt = prepare(h + 1)
        if batched:
            sc = lax.dot_general(qe, ke, _NT, preferred_element_type=F32)
            sc = jnp.where(keep, sc, 0.0).astype(BF16)
            o = jnp.dot(sc, vb, preferred_element_type=F32)
        st = st_scr[h]
        parts = [None] * nchunk
        for ci in order:
            rs = slice(ci * CHUNK, (ci + 1) * CHUNK)
            if batched:
                o_c = o[rs]
            else:
                sc = lax.dot_general(qe[rs], ke[rs], _NT, preferred_element_type=F32)
                sc = jnp.where(keep_c, sc, 0.0).astype(BF16)
                o_c = jnp.dot(sc, vb[rs], preferred_element_type=F32)
            parts[ci] = o_c + lax.dot_general(qd[rs], st.astype(BF16), _NT,
                                              preferred_element_type=F32)
            st = st * decays[ci] + lax.dot_general(vb[rs], kd[rs], _TN, preferred_element_type=F32)
        if o_ref is not None:
            o_ref[:, h * dv:(h + 1) * dv] = jnp.concatenate(parts, axis=0).astype(o_ref.dtype)
        st_scr[h] = st


def _gla_kernel(q_ref, k_ref, v_ref, a_ref, wa_ref, ba_ref, o_ref, st_scr):
    t = pl.program_id(1)
    d = pl.program_id(2)
    dk = q_ref.shape[1] // GLA_HEADS
    dv = v_ref.shape[1] // GLA_HEADS

    @pl.when((t == 0) & (d == 0))
    def _():
        st_scr[...] = jnp.zeros(st_scr.shape, F32)

    def head_qkv(h):
        ks = slice(h * dk, (h + 1) * dk)
        q = q_ref[:, ks].astype(F32) * (dk ** -0.5)
        return q, k_ref[:, ks].astype(F32), v_ref[:, h * dv:(h + 1) * dv]

    for dd in range(2):
        @pl.when(d == dd)
        def _(dd=dd):
            z = jnp.dot(a_ref[...], wa_ref[dd], preferred_element_type=F32) + ba_ref[dd]
            g_all = _log_sigmoid(z) * (1.0 / GLA_TAU)
            _block_scan(g_all, head_qkv, st_scr.at[dd], o_ref,
                        heads=GLA_HEADS, dv=dv, reverse=dd == 1, batched=True)


def _gla(u, wa, ba, *, batch, seq, n_ctx, cols, dv_total):
    ntok = u.shape[0]
    q0, k0, v0, a0 = cols
    dkt = wa.shape[2]
    dvt = dv_total
    nblk = seq // ROW_TILE
    ctx_blk0 = batch * nblk
    assert n_ctx == ROW_TILE

    def blk(b, t, d):
        lat = b * nblk + jnp.where(d == 0, t - 1, nblk - t)
        return jnp.where(t == 0, ctx_blk0 + b, lat)

    in_specs = [pl.BlockSpec((ROW_TILE, dkt), lambda b, t, d: (blk(b, t, d), q0 * LANE // dkt)),
                pl.BlockSpec((ROW_TILE, dkt), lambda b, t, d: (blk(b, t, d), k0 * LANE // dkt)),
                pl.BlockSpec((ROW_TILE, dvt), lambda b, t, d: (blk(b, t, d), v0 * LANE // dvt)),
                pl.BlockSpec((ROW_TILE, LANE), lambda b, t, d: (blk(b, t, d), a0)),
                pl.BlockSpec(wa.shape, lambda b, t, d: (0, 0, 0)),
                pl.BlockSpec(ba.shape, lambda b, t, d: (0, 0, 0))]
    return pl.pallas_call(
        _gla_kernel,
        out_shape=jax.ShapeDtypeStruct((2, ntok, dvt), BF16),
        grid=(batch, nblk + 1, 2),
        in_specs=in_specs,
        out_specs=pl.BlockSpec((None, ROW_TILE, dvt), lambda b, t, d: (d, blk(b, t, d), 0)),
        scratch_shapes=[pltpu.VMEM((2, GLA_HEADS, dvt // GLA_HEADS, dkt // GLA_HEADS), F32)],
        compiler_params=_params("parallel", "arbitrary", "arbitrary"),
        name="gla",
    )(u, u, u, u, wa, ba)


def _hgrn_kernel(*refs, emit_o):
    q_ref, z_ref, v_ref, lb_ref, ub_ref, s0_ref = refs[:6]
    if emit_o:
        o_ref, st_ref, st_scr = refs[6:]
    else:
        st_ref, st_scr = refs[6:]
        o_ref = None
    t = pl.program_id(1)
    d = pl.program_id(2)
    dh = q_ref.shape[1] // HGRN_HEADS

    @pl.when((t == 0) & (d == 0))
    def _():
        st_scr[...] = s0_ref[:, 0]

    lb = lb_ref[...]
    ub = ub_ref[...]

    for dd in range(2):
        @pl.when(d == dd)
        def _(dd=dd):
            z = z_ref[...].astype(F32)
            e = jnp.exp(-jnp.abs(z))
            r = 1.0 / (1.0 + e)
            er = e * r
            pos = z >= 0.0
            log_f = jnp.log(lb + ub * jnp.where(pos, r, er))
            k_all = ub * jnp.where(pos, er, r)

            def head_qkv(h):
                hs = slice(h * dh, (h + 1) * dh)
                return q_ref[:, hs].astype(F32), k_all[:, hs], v_ref[:, hs]

            _block_scan(log_f, head_qkv, st_scr.at[dd], o_ref,
                        heads=HGRN_HEADS, dv=dh, reverse=dd == 1, batched=True)

    @pl.when((t == pl.num_programs(1) - 1) & (d == 1))
    def _():
        st_ref[:, 0] = st_scr[...]


def _hgrn_segment(u, s0, lb, ub, *, batch, seg_len, emit_o):
    ntok = u.shape[0]
    width = lb.shape[1]
    nblk = seg_len // ROW_TILE

    def blk(b, t, d):
        return b * nblk + jnp.where(d == 0, t, nblk - 1 - t)

    st_spec = pl.BlockSpec((2, 1) + s0.shape[2:], lambda b, t, d: (0, b, 0, 0, 0))
    in_specs = [pl.BlockSpec((ROW_TILE, width), lambda b, t, d: (blk(b, t, d), 0)),
                pl.BlockSpec((ROW_TILE, width), lambda b, t, d: (blk(b, t, d), 1 + d)),
                pl.BlockSpec((ROW_TILE, width), lambda b, t, d: (blk(b, t, d), 3)),
                pl.BlockSpec((1, width), lambda b, t, d: (0, 0)),
                pl.BlockSpec((1, width), lambda b, t, d: (0, 0)),
                st_spec]
    st_shape = jax.ShapeDtypeStruct(s0.shape, F32)
    if emit_o:
        out_shape = (jax.ShapeDtypeStruct((2, ntok, width), F32), st_shape)
        out_specs = (pl.BlockSpec((None, ROW_TILE, width), lambda b, t, d: (d, blk(b, t, d), 0)),
                     st_spec)
    else:
        out_shape = (st_shape,)
        out_specs = (st_spec,)
    return pl.pallas_call(
        functools.partial(_hgrn_kernel, emit_o=emit_o),
        out_shape=out_shape,
        grid=(batch, nblk, 2),
        in_specs=in_specs,
        out_specs=out_specs,
        scratch_shapes=[pltpu.VMEM((2,) + s0.shape[2:], F32)],
        compiler_params=_params("parallel", "arbitrary", "arbitrary"),
        name="hgrn_segment",
    )(u, u, u, lb, ub, s0)


def _head_norm_gate(o, gate, heads):
    dh = o.shape[1] // heads
    parts = []
    for h in range(heads):
        oh = o[:, h * dh:(h + 1) * dh]
        ms = jnp.mean(oh * oh, axis=-1, keepdims=True)
        parts.append((oh * lax.rsqrt(ms + EPS) * _silu(gate[:, h * dh:(h + 1) * dh])).astype(BF16))
    return jnp.concatenate(parts, axis=1)


def _out_proj_tail(o, x_ref, tab_ref, pg_ref, ng_ref, rt_ref, xo_ref, h_ref, lg_ref):
    xn = _gated_residual(x_ref[...], _rms(o, pg_ref[...]), tab_ref, 2)
    xo_ref[...] = xn
    h2 = _modulate(_rms(xn, ng_ref[...]), tab_ref, 3, 4)
    h_ref[...] = h2.astype(BF16)
    lg_ref[...] = lax.dot_general(rt_ref[...], h2, _NT, precision=lax.Precision.HIGHEST,
                                  preferred_element_type=F32)


def _out_proj_ab_kernel(yl_ref, yc_ref, pt_ref, of_ref, ob_ref, gate_ref, w_ref, x_ref, tab_ref,
                        pg_ref, ng_ref, rt_ref, xo_ref, h_ref, lg_ref, *, lat_tiles):
    o_att = of_ref[...].astype(F32) + ob_ref[...].astype(F32)
    yb = _head_norm_gate(o_att, gate_ref[...].astype(F32), GLA_HEADS)
    ya = jnp.where(pl.program_id(0) < lat_tiles, yl_ref[...], yc_ref[...])
    ya = jnp.concatenate(
        [jnp.dot(pt_ref[...], ya[s:s + ROW_TILE], preferred_element_type=F32).astype(BF16)
         for s in range(0, ya.shape[0], ROW_TILE)], axis=0)
    da = ya.shape[1]
    o = jnp.dot(ya, w_ref[:da], preferred_element_type=F32)
    o = o + jnp.dot(yb, w_ref[da:], preferred_element_type=F32)
    _out_proj_tail(o, x_ref, tab_ref, pg_ref, ng_ref, rt_ref, xo_ref, h_ref, lg_ref)


def _out_proj_c_kernel(of_ref, ob_ref, gate_ref, p_ref, w_ref, x_ref, tab_ref, pg_ref, ng_ref,
                       rt_ref, xo_ref, h_ref, lg_ref):
    tm = x_ref.shape[0]
    width = of_ref.shape[-1]
    o_att = of_ref[...].astype(F32) + ob_ref[...].astype(F32)
    y = _head_norm_gate(o_att.reshape(tm, width), gate_ref[...].astype(F32).reshape(tm, width),
                        HGRN_HEADS)
    y = jnp.dot(p_ref[...], y, preferred_element_type=F32).astype(BF16)
    o = jnp.dot(y, w_ref[...], preferred_element_type=F32)
    _out_proj_tail(o, x_ref, tab_ref, pg_ref, ng_ref, rt_ref, xo_ref, h_ref, lg_ref)


def _out_proj(kernel, lead, lead_specs, w, x, tab, post_gain, next_gain, router_t, tm, ntiles):
    d = x.shape[1]
    ntok = ntiles * tm
    nsub = tm // ROW_TILE
    ne = router_t.shape[0]
    in_specs = list(lead_specs) + [
        pl.BlockSpec(w.shape, lambda i: (0, 0), pipeline_mode=pl.Buffered(1)),
        pl.BlockSpec((tm, d), lambda i: (i, 0)),
        pl.BlockSpec((nsub, 8, d), lambda i: (i, 0, 0)),
        pl.BlockSpec((1, d), lambda i: (0, 0)),
        pl.BlockSpec((1, d), lambda i: (0, 0)),
        pl.BlockSpec((ne, d), lambda i: (0, 0))]
    return pl.pallas_call(
        kernel,
        out_shape=(jax.ShapeDtypeStruct((ntok, d), F32),
                   jax.ShapeDtypeStruct((ntok, d), BF16),
                   jax.ShapeDtypeStruct((ne, ntok), F32)),
        grid=(ntiles,),
        in_specs=in_specs,
        out_specs=(pl.BlockSpec((tm, d), lambda i: (i, 0)),
                   pl.BlockSpec((tm, d), lambda i: (i, 0)),
                   pl.BlockSpec((ne, tm), lambda i: (0, i))),
        compiler_params=_params("parallel"),
        name="out_proj_norm_residual",
    )(*lead, w, x, tab, post_gain, next_gain, router_t)


def _route_kernel(lg_ref, bias_ref, cw_ref):
    tm = lg_ref.shape[1]
    per_group = N_EXPERTS // N_GROUPS
    shape3 = (N_GROUPS, per_group, tm)
    aff = _sigmoid(lg_ref[...]).reshape(shape3)
    biased = aff + bias_ref[...].reshape(shape3)
    neg = -jnp.inf
    sub = lax.broadcasted_iota(jnp.int32, shape3, 1)
    grp = lax.broadcasted_iota(jnp.int32, shape3, 0)
    m1 = jnp.max(biased, axis=1, keepdims=True)
    i1 = jnp.min(jnp.where(biased == m1, sub, per_group), axis=1, keepdims=True)
    m2 = jnp.max(jnp.where(sub == i1, neg, biased), axis=1, keepdims=True)
    score = m1 + m2
    gidx = lax.broadcasted_iota(jnp.int32, score.shape, 0)
    keep = jnp.zeros(score.shape, F32)
    for _ in range(TOPK_GROUPS):
        m = jnp.max(score, axis=0, keepdims=True)
        im = jnp.min(jnp.where(score == m, gidx, N_GROUPS), axis=0, keepdims=True)
        sel = gidx == im
        keep = jnp.where(sel, 1.0, keep)
        score = jnp.where(sel, neg, score)
    work = jnp.where(jnp.broadcast_to(keep, shape3) > 0.0, biased, neg)
    eidx = grp * per_group + sub
    chosen = jnp.zeros(shape3, F32)
    for _ in range(TOP_K):
        m = jnp.max(jnp.max(work, axis=0, keepdims=True), axis=1, keepdims=True)
        cand = jnp.where(work == m, eidx, N_EXPERTS)
        im = jnp.min(jnp.min(cand, axis=0, keepdims=True), axis=1, keepdims=True)
        sel = eidx == im
        chosen = jnp.where(sel, 1.0, chosen)
        work = jnp.where(sel, neg, work)
    gate = aff * chosen
    den = jnp.sum(jnp.sum(gate, axis=0, keepdims=True), axis=1, keepdims=True)
    cw = (ROUTED_SCALE * gate / den).reshape(N_EXPERTS, tm)
    cw = jnp.concatenate([cw, jnp.zeros((LANE - N_EXPERTS, tm), F32)], axis=0)
    cw_ref[...] = cw.T


def _route(logits_t, bias, tm):
    ne, ntok = logits_t.shape
    bias_b = jnp.broadcast_to(bias.astype(F32)[:, None], (ne, tm))
    return pl.pallas_call(
        _route_kernel,
        out_shape=jax.ShapeDtypeStruct((ntok, LANE), F32),
        grid=(ntok // tm,),
        in_specs=[pl.BlockSpec((ne, tm), lambda i: (0, i)),
                  pl.BlockSpec((ne, tm), lambda i: (0, 0))],
        out_specs=pl.BlockSpec((tm, LANE), lambda i: (i, 0)),
        compiler_params=_params("parallel"),
        name="moe_route",
    )(logits_t, bias_b)


def _moe_kernel(h_ref, cw_ref, wg_ref, wu_ref, wd_ref, sg_ref, su_ref, sd_ref, x_ref, tab_ref,
                pg_ref, o_ref, *, epb):
    j = pl.program_id(1)
    h = h_ref[...]

    @pl.when(j == 0)
    def _():
        g = jnp.dot(h, sg_ref[...], preferred_element_type=F32)
        u = jnp.dot(h, su_ref[...], preferred_element_type=F32)
        o_ref[...] = jnp.dot((_silu(g) * u).astype(BF16), sd_ref[...], preferred_element_type=F32)

    cw = cw_ref[...]
    lane = lax.broadcasted_iota(jnp.int32, cw.shape, 1)
    for e in range(epb):
        we = jnp.sum(jnp.where(lane == j * epb + e, cw, 0.0), axis=1, keepdims=True)
        g = jnp.dot(h, wg_ref[e].astype(BF16), preferred_element_type=F32)
        u = jnp.dot(h, wu_ref[e].astype(BF16), preferred_element_type=F32)
        a = (_silu(g) * u * we).astype(BF16)
        o_ref[...] += jnp.dot(a, wd_ref[e].astype(BF16), preferred_element_type=F32)

    @pl.when(j == pl.num_programs(1) - 1)
    def _():
        o_ref[...] = _gated_residual(x_ref[...], _rms(o_ref[...], pg_ref[...]), tab_ref, 5)


def _moe(h, cw, x, tab, post_gain, layer, wg, wu, wd, sg, su, sd, tile0, ntiles, tm, epb):
    d = h.shape[1]
    _, ne, _, ff = wg.shape
    nsub = tm // ROW_TILE
    once = pl.Buffered(1)
    return pl.pallas_call(
        functools.partial(_moe_kernel, epb=epb),
        out_shape=jax.ShapeDtypeStruct((ntiles * tm, d), F32),
        grid=(ntiles, ne // epb),
        in_specs=[pl.BlockSpec((tm, d), lambda i, j: (tile0 + i, 0), pipeline_mode=once),
                  pl.BlockSpec((tm, LANE), lambda i, j: (tile0 + i, 0), pipeline_mode=once),
                  pl.BlockSpec((None, epb, d, ff), lambda i, j: (layer, j, 0, 0)),
                  pl.BlockSpec((None, epb, d, ff), lambda i, j: (layer, j, 0, 0)),
                  pl.BlockSpec((None, epb, ff, d), lambda i, j: (layer, j, 0, 0)),
                  pl.BlockSpec(sg.shape, lambda i, j: (0, 0), pipeline_mode=once),
                  pl.BlockSpec(su.shape, lambda i, j: (0, 0), pipeline_mode=once),
                  pl.BlockSpec(sd.shape, lambda i, j: (0, 0), pipeline_mode=once),
                  pl.BlockSpec((tm, d), lambda i, j: (tile0 + i, 0), pipeline_mode=once),
                  pl.BlockSpec((nsub, 8, d), lambda i, j: (tile0 + i, 0, 0)),
                  pl.BlockSpec((1, d), lambda i, j: (0, 0))],
        out_specs=pl.BlockSpec((tm, d), lambda i, j: (i, 0)),
        compiler_params=_params("parallel", "arbitrary", vmem=VMEM_LIMIT_MOE),
        name="moe_experts",
    )(h, cw, wg, wu, wd, sg, su, sd, x, tab, post_gain)


def _grid_permutation(band):
    tm = band * GRID_W
    dst = jnp.arange(tm)
    src = (dst % band) * GRID_W + dst // band
    return (src[:, None] == jnp.arange(tm)[None, :]).astype(BF16)


def kernel(x, c, ctx, c_ctx, mod_w, mod_b, pre_gain, post_gain, ab_w_in, ab_conv_w, ab_conv_b, ab_lru_wa, ab_lru_ba, ab_lru_wi, ab_lru_bi, ab_lru_lambda, ab_gla_wa2, ab_gla_ba, ab_w_out, c_w_in, hgrn_lb_logits, c_w_out, moe_router, moe_bias, moe_w_gate, moe_w_up, moe_w_down, moe_ws_gate, moe_ws_up, moe_ws_down):
    batch, seq, d = x.shape
    n_ctx = ctx.shape[1]
    depth = mod_w.shape[0]
    band = SUBLANE
    tm_grid = band * GRID_W
    rows_img = seq // GRID_W
    assert n_ctx == ROW_TILE and seq % (2 * ROW_TILE) == 0 and d % LANE == 0
    assert depth == 2 and batch + 1 <= 8 and rows_img % band == 0 and tm_grid % ROW_TILE == 0
    n_lat = batch * seq
    n_ctx_tok = batch * n_ctx
    ntok = n_lat + n_ctx_tok

    xs = jnp.concatenate([x.reshape(n_lat, d), ctx.reshape(n_ctx_tok, d)], axis=0)

    cvec = jnp.concatenate([c, c_ctx[None, :], jnp.zeros((8 - batch - 1, d), F32)], axis=0)
    mods = _modulation(cvec, mod_w, mod_b).reshape(depth, 8, N_MOD, d)
    tile_row = jnp.concatenate([jnp.repeat(jnp.arange(batch), seq // ROW_TILE),
                                jnp.full((batch,), batch)]).astype(jnp.int32)
    tabs = jnp.pad(mods[:, tile_row], ((0, 0), (0, 0), (0, 8 - N_MOD), (0, 0)))

    def moe(layer, h2, cwt, xs_in, tile0, ntiles, tm):
        return _moe(h2, cwt, xs_in, tabs[layer], post_gain[layer, 1][None, :], layer,
                    moe_w_gate, moe_w_up, moe_w_down, moe_ws_gate[layer].astype(BF16),
                    moe_ws_up[layer].astype(BF16), moe_ws_down[layer].astype(BF16),
                    tile0, ntiles, tm, epb)

    tm_proj, tm_out, tm_route, tm_moe, epb = 512, 512, 512, 1024, 2
    assert n_lat % tm_moe == 0 and n_lat % n_ctx_tok == 0

    ab_cols = ab_w_in.shape[2]
    gla_dk = ab_gla_wa2.shape[3]
    lru_w = d
    gla_dv = (ab_cols - 2 * lru_w - 2 * gla_dk - 2 * GLA_RANK) // 2
    tn0 = 1792
    n0 = -(-(ab_cols) // tn0) * tn0
    w_in0 = jnp.pad(ab_w_in[0], ((0, 0), (0, n0 - ab_cols))).astype(BF16)
    u0 = _in_proj(xs, tabs[0], pre_gain[0, 0][None, :], w_in0, tm_proj, tn0, 0)

    cw = jnp.pad(ab_conv_w[0], ((0, 8 - CONV_W), (0, 0)))
    cb = ab_conv_b[0][None, :]
    wg = jnp.concatenate([ab_lru_wa[0], ab_lru_wi[0]], axis=-1).astype(BF16)
    bg = jnp.concatenate([ab_lru_ba[0].reshape(2, LRU_HEADS, 1, LANE),
                          ab_lru_bi[0].reshape(2, LRU_HEADS, 1, LANE)], axis=-1)
    ya_c, ya_l = _lru(u0, (cw, cb, wg, bg, ab_lru_lambda[0]), batch=batch, seq=seq, n_ctx=n_ctx,
                      g_col0=0, x_col0=lru_w // LANE)

    q0 = 2 * lru_w // LANE
    k0 = q0 + gla_dk // LANE
    v0 = k0 + gla_dk // LANE
    gate0 = v0 + gla_dv // LANE
    a0 = gate0 + gla_dv // LANE
    wa = jnp.stack([jnp.zeros((LANE, gla_dk), F32).at[dr * GLA_RANK:(dr + 1) * GLA_RANK]
                    .set(ab_gla_wa2[0, dr]) for dr in range(2)]).astype(BF16)
    ba = ab_gla_ba[0][:, None, :]
    o_gla = _gla(u0, wa, ba, batch=batch, seq=seq, n_ctx=n_ctx, cols=(q0, k0, v0, a0),
                 dv_total=gla_dv)

    lat_tiles = n_lat // tm_out
    ctx_tiles = n_ctx_tok // tm_out
    lead_specs = [pl.BlockSpec((tm_out, lru_w), lambda i: (jnp.minimum(i, lat_tiles - 1), 0)),
                  pl.BlockSpec((tm_out, lru_w),
                               lambda i: (jnp.clip(i - lat_tiles, 0, ctx_tiles - 1), 0)),
                  pl.BlockSpec((ROW_TILE, ROW_TILE), lambda i: (0, 0)),
                  pl.BlockSpec((None, tm_out, gla_dv), lambda i: (0, i, 0)),
                  pl.BlockSpec((None, tm_out, gla_dv), lambda i: (1, i, 0)),
                  pl.BlockSpec((tm_out, gla_dv), lambda i: (i, gate0 * LANE // gla_dv))]
    xs, h2, lg = _out_proj(functools.partial(_out_proj_ab_kernel, lat_tiles=lat_tiles),
                           (ya_l, ya_c, _time_permutation().T.astype(BF16), o_gla, o_gla, u0),
                           lead_specs,
                           ab_w_out[0].astype(BF16), xs, tabs[0], post_gain[0, 0][None, :],
                           pre_gain[0, 1][None, :], moe_router[0].T, tm_out, ntok // tm_out)
    cwt = _route(lg, moe_bias[0], tm_route)
    xs_ctx = moe(0, h2, cwt, xs, n_lat // n_ctx_tok, 1, n_ctx_tok)
    xs = moe(0, h2, cwt, xs, 0, n_lat // tm_moe, tm_moe)

    hd = c_w_out.shape[1]
    tn1 = 1280
    w_in1 = c_w_in[0].astype(BF16)
    c_cols = w_in1.shape[1]
    nsec = c_cols // hd
    perm = _grid_permutation(band)
    gain1 = pre_gain[1, 0][None, :]
    u1l = _in_proj_grid(xs, tabs[1], gain1, perm, w_in1, tn1, n_lat // tm_grid, band,
                        rows_img // band)
    u1c = _in_proj(xs_ctx, tabs[1], gain1, w_in1, tm_proj, tn1, n_lat // tm_proj)
    lb_sm = jax.nn.softmax(hgrn_lb_logits.astype(F32), axis=0)
    lb_cum = jnp.cumsum(lb_sm, axis=0)
    lb = (lb_cum[1] - lb_cum[0])[None, :]
    ub = 1.0 - lb
    dh = hd // HGRN_HEADS
    s_zero = jnp.zeros((2, batch, HGRN_HEADS, dh, dh), F32)
    u1l_flat = u1l.reshape(n_lat, c_cols)
    (s_c,) = _hgrn_segment(u1c, s_zero, lb, ub, batch=batch, seg_len=n_ctx, emit_o=False)
    o_hgrn, _ = _hgrn_segment(u1l_flat, s_c, lb, ub, batch=batch, seg_len=seq, emit_o=True)
    o_hgrn = o_hgrn.reshape(2, n_lat // rows_img, rows_img // band, band, hd)

    bpi = rows_img // band
    grid_blk = (GRID_W, None, band, hd)
    lead_specs = [pl.BlockSpec((None,) + grid_blk, lambda i: (0, i // bpi, i % bpi, 0, 0)),
                  pl.BlockSpec((None,) + grid_blk, lambda i: (1, i // bpi, i % bpi, 0, 0)),
                  pl.BlockSpec(grid_blk, lambda i: (i // bpi, i % bpi, 0, nsec - 1)),
                  pl.BlockSpec((tm_grid, tm_grid), lambda i: (0, 0))]
    xs, h2, lg = _out_proj(_out_proj_c_kernel, (o_hgrn, o_hgrn, u1l, perm.T), lead_specs,
                           c_w_out[0].astype(BF16), xs, tabs[1], post_gain[1, 0][None, :],
                           pre_gain[1, 1][None, :], moe_router[1].T, tm_grid, n_lat // tm_grid)
    out = moe(1, h2, _route(lg, moe_bias[1], tm_route), xs, 0, n_lat // tm_moe, tm_moe)
    return out.reshape(batch, seq, d)
```

```python
import functools

import jax
import jax.numpy as jnp
from jax import lax
from jax.experimental import pallas as pl
from jax.experimental.pallas import tpu as pltpu

F32 = jnp.float32
BF16 = jnp.bfloat16

EPS = 1e-6
N_MOD = 6
ROW_TILE = 256
CHUNK = 64
CHUNK_SHIFT = 6
GRID_W = 64
LRU_HEADS = 8
LRU_C = 8.0
CONV_W = 4
GLA_HEADS = 4
GLA_RANK = 16
GLA_TAU = 16.0
HGRN_HEADS = 8
N_EXPERTS = 64
TOP_K = 8
N_GROUPS = 8
TOPK_GROUPS = 4
ROUTED_SCALE = 2.5
LANE = 128
SUBLANE = 8
BF16_ROWS = 16
VMEM_LIMIT = 48 * 1024 * 1024
VMEM_LIMIT_MOE = 58 * 1024 * 1024

_NT = (((1,), (1,)), ((), ()))
_TN = (((0,), (0,)), ((), ()))


def _params(*sem, vmem=VMEM_LIMIT):
    return pltpu.CompilerParams(dimension_semantics=sem, vmem_limit_bytes=vmem)


def _rms(x, gain):
    ms = jnp.mean(x * x, axis=-1, keepdims=True)
    return x * lax.rsqrt(ms + EPS) * gain


def _sigmoid(x):
    return jax.nn.sigmoid(x)


def _silu(x):
    return x * _sigmoid(x)


def _log_sigmoid(x):
    return jnp.minimum(x, 0.0) - jnp.log1p(jnp.exp(-jnp.abs(x)))


def _softplus(x):
    return jnp.maximum(x, 0.0) + jnp.log1p(jnp.exp(-jnp.abs(x)))


def _modulate(xn, tab_ref, shift_row, scale_row):
    parts = []
    for s in range(tab_ref.shape[0]):
        t = tab_ref[s]
        rows = xn[s * ROW_TILE:(s + 1) * ROW_TILE]
        parts.append(rows * (1.0 + t[scale_row:scale_row + 1]) + t[shift_row:shift_row + 1])
    return parts[0] if len(parts) == 1 else jnp.concatenate(parts, axis=0)


def _gated_residual(x, branch, tab_ref, gate_row):
    parts = []
    for s in range(tab_ref.shape[0]):
        sl = slice(s * ROW_TILE, (s + 1) * ROW_TILE)
        parts.append(x[sl] + tab_ref[s][gate_row:gate_row + 1] * branch[sl])
    return parts[0] if len(parts) == 1 else jnp.concatenate(parts, axis=0)


def _mod_kernel(c_ref, w_ref, b_ref, o_ref):
    s = _silu(c_ref[...])
    o_ref[0] = jnp.dot(s, w_ref[0], precision=lax.Precision.HIGHEST,
                       preferred_element_type=F32) + b_ref[0]


def _modulation(cvec, mod_w, mod_b):
    depth, d, n = mod_w.shape
    tn = 1024
    return pl.pallas_call(
        _mod_kernel,
        out_shape=jax.ShapeDtypeStruct((depth, 8, n), F32),
        grid=(depth, n // tn),
        in_specs=[pl.BlockSpec((8, d), lambda l, j: (0, 0)),
                  pl.BlockSpec((1, d, tn), lambda l, j: (l, 0, j)),
                  pl.BlockSpec((1, 1, tn), lambda l, j: (l, 0, j))],
        out_specs=pl.BlockSpec((1, 8, tn), lambda l, j: (l, 0, j)),
        compiler_params=_params("parallel", "parallel"),
        name="adaln_modulation",
    )(cvec, mod_w, mod_b.reshape(depth, 1, n))


def _in_proj_kernel(x_ref, tab_ref, gain_ref, w_ref, o_ref, *, tn):
    h = _modulate(_rms(x_ref[...], gain_ref[...]), tab_ref, 0, 1).astype(BF16)
    for j in range(w_ref.shape[1] // tn):
        cs = slice(j * tn, (j + 1) * tn)
        o_ref[:, cs] = jnp.dot(h, w_ref[:, cs], preferred_element_type=F32).astype(o_ref.dtype)


def _in_proj(x, tab, gain, w, tm, tn, tab0):
    d = x.shape[1]
    n = w.shape[1]
    nsub = tm // ROW_TILE
    ntiles = x.shape[0] // tm
    return pl.pallas_call(
        functools.partial(_in_proj_kernel, tn=tn),
        out_shape=jax.ShapeDtypeStruct((ntiles * tm, n), BF16),
        grid=(ntiles,),
        in_specs=[pl.BlockSpec((tm, d), lambda i: (i, 0)),
                  pl.BlockSpec((nsub, 8, d), lambda i: (tab0 + i, 0, 0)),
                  pl.BlockSpec((1, d), lambda i: (0, 0)),
                  pl.BlockSpec((d, n), lambda i: (0, 0), pipeline_mode=pl.Buffered(1))],
        out_specs=pl.BlockSpec((tm, n), lambda i: (i, 0)),
        compiler_params=_params("parallel"),
        name="norm_mod_in_proj",
    )(x, tab, gain, w)


def _in_proj_grid_kernel(x_ref, tab_ref, gain_ref, p_ref, w_ref, o_ref, *, tn):
    h = _modulate(_rms(x_ref[...], gain_ref[...]), tab_ref, 0, 1).astype(BF16)
    hp = jnp.dot(p_ref[...], h, preferred_element_type=F32).astype(BF16)
    for j in range(w_ref.shape[1] // tn):
        cs = slice(j * tn, (j + 1) * tn)
        o = jnp.dot(hp, w_ref[:, cs], preferred_element_type=F32).astype(o_ref.dtype)
        o_ref[:, :, cs] = o.reshape(o_ref.shape[:2] + (tn,))


def _in_proj_grid(x, tab, gain, perm, w, tn, ntiles, band, bands_per_image):
    d = x.shape[1]
    n = w.shape[1]
    tm = band * GRID_W
    nsub = tm // ROW_TILE
    images = ntiles // bands_per_image
    once = pl.Buffered(1)
    return pl.pallas_call(
        functools.partial(_in_proj_grid_kernel, tn=tn),
        out_shape=jax.ShapeDtypeStruct((images * GRID_W, bands_per_image, band, n), F32),
        grid=(ntiles,),
        in_specs=[pl.BlockSpec((tm, d), lambda i: (i, 0)),
                  pl.BlockSpec((nsub, 8, d), lambda i: (i, 0, 0)),
                  pl.BlockSpec((1, d), lambda i: (0, 0)),
                  pl.BlockSpec((tm, tm), lambda i: (0, 0), pipeline_mode=once),
                  pl.BlockSpec((d, n), lambda i: (0, 0), pipeline_mode=once)],
        out_specs=pl.BlockSpec((GRID_W, None, band, n),
                               lambda i: (i // bands_per_image, i % bands_per_image, 0, 0)),
        compiler_params=_params("parallel"),
        name="norm_mod_in_proj_grid",
    )(x, tab, gain, perm, w)


def _lru_kernel(gc_ref, xc_ref, gl_ref, xl_ref, p_ref, cw_ref, cb_ref, wg_ref, bg_ref,
                lam_ref, yc_ref, yl_ref, xs_scr, hf_scr):
    T = ROW_TILE
    V = T // SUBLANE
    H = BF16_ROWS
    sub = lax.broadcasted_iota(jnp.int32, (SUBLANE, LANE), 0)
    cw = cw_ref[...]
    cb = cb_ref[...]
    sp = _softplus(-lam_ref[...])

    def vreg(a, v):
        return a[v * SUBLANE:(v + 1) * SUBLANE]

    def shift_down(y, first):
        return jnp.where(sub == 0, first, pltpu.roll(y, 1, 0))

    def shift_up(y, last):
        return jnp.where(sub == SUBLANE - 1, last, pltpu.roll(y, SUBLANE - 1, 0))

    def conv_tile(x_ref, t, nt, seg_len):
        t0 = pl.multiple_of(t * T, T)
        xp = jnp.dot(p_ref[...], x_ref[pl.ds(t0, T), :], preferred_element_type=F32)
        p0 = pl.multiple_of(jnp.maximum(t0 - H, 0), H)
        n0 = pl.multiple_of(jnp.minimum(t0 + T, seg_len - H), H)
        prev = x_ref[pl.ds(p0, H), :].astype(F32) * jnp.where(t > 0, 1.0, 0.0)
        nxt = x_ref[pl.ds(n0, H), :].astype(F32) * jnp.where(t < nt - 1, 1.0, 0.0)
        m1, m2, p1 = prev[H - 1:H], prev[H - 2:H - 1], nxt[0:1]
        last_m1 = shift_down(vreg(xp, V - 1), m1)
        xm1 = jnp.concatenate([last_m1, xp[:T - SUBLANE]], axis=0)
        xm2 = jnp.concatenate([shift_down(vreg(xp, V - 2), m2), last_m1, xp[:T - 2 * SUBLANE]], axis=0)
        xp1 = jnp.concatenate([xp[SUBLANE:], shift_up(vreg(xp, 0), p1)], axis=0)
        return cb + xm2 * cw[0:1] + xm1 * cw[1:2] + xp * cw[2:3] + xp1 * cw[3:4]

    def gates(xc, d):
        z = jnp.dot(xc.astype(BF16), wg_ref[d, 0], preferred_element_type=F32) + bg_ref[d, 0]
        r = _sigmoid(z[:, :LANE])
        i = _sigmoid(z[:, LANE:])
        log_a = (-LRU_C) * r * sp[d:d + 1]
        a = jnp.exp(log_a)
        return a, jnp.sqrt(1.0 - a * a) * (i * xc)

    def scan_tile(a, u, carry, reverse):
        hs, cum = [None] * V, [None] * V
        hp = ap = None
        for v in (range(V - 1, -1, -1) if reverse else range(V)):
            av, uv = vreg(a, v), vreg(u, v)
            hp, ap = (uv, av) if hp is None else (av * hp + uv, av * ap)
            hs[v], cum[v] = hp, ap
        s = 1
        while s < SUBLANE:
            keep = (sub < SUBLANE - s) if reverse else (sub >= s)
            sh = SUBLANE - s if reverse else s
            h_sh = jnp.where(keep, pltpu.roll(hp, sh, 0), 0.0)
            a_sh = jnp.where(keep, pltpu.roll(ap, sh, 0), 1.0)
            hp = hp + ap * h_sh
            ap = ap * a_sh
            s *= 2
        state = hp + ap * carry
        if reverse:
            start, out = shift_up(state, carry), state[0:1]
        else:
            start, out = shift_down(state, carry), state[SUBLANE - 1:SUBLANE]
        h = jnp.concatenate([hs[v] + cum[v] * start for v in range(V)], axis=0)
        return h, out

    def run_segment(g_ref, x_ref, y_ref, seg_len, carry_f, carry_b):
        nt = seg_len // T
        unroll = 4 if nt % 4 == 0 else (2 if nt % 2 == 0 else 1)

        def fwd_body(i, carry):
            for j in range(unroll):
                t = i * unroll + j
                t0 = pl.multiple_of(t * T, T)
                xc = conv_tile(x_ref, t, nt, seg_len)
                xs_scr[pl.ds(t0, T), :] = xc
                a, u = gates(xc, 0)
                h, carry = scan_tile(a, u, carry, False)
                hf_scr[pl.ds(t0, T), :] = h
            return carry

        def bwd_body(i, carry):
            for j in range(unroll):
                t = nt - 1 - (i * unroll + j)
                t0 = pl.multiple_of(t * T, T)
                a, u = gates(xs_scr[pl.ds(t0, T), :], 1)
                h, carry = scan_tile(a, u, carry, True)
                g = jnp.dot(p_ref[...], g_ref[pl.ds(t0, T), :], preferred_element_type=F32)
                rec = hf_scr[pl.ds(t0, T), :] + h
                y_ref[pl.ds(t0, T), :] = (jax.nn.gelu(g) * rec).astype(y_ref.dtype)
            return carry

        steps = nt // unroll
        return (lax.fori_loop(0, steps, fwd_body, carry_f),
                lax.fori_loop(0, steps, bwd_body, carry_b))

    zero = jnp.zeros((1, LANE), F32)
    hf, hb = run_segment(gc_ref, xc_ref, yc_ref, xc_ref.shape[0], zero, zero)
    run_segment(gl_ref, xl_ref, yl_ref, xl_ref.shape[0], hf, hb)


def _time_permutation():
    v = ROW_TILE // SUBLANE
    dst = jnp.arange(ROW_TILE)
    src = (dst % SUBLANE) * v + dst // SUBLANE
    return src[:, None] == jnp.arange(ROW_TILE)[None, :]


def _lru(u, lru_w, *, batch, seq, n_ctx, g_col0, x_col0):
    width = LRU_HEADS * LANE
    cw, cb, wg, bg, lam = lru_w
    ctx_blk0 = batch * seq // n_ctx
    perm = _time_permutation()
    once = pl.Buffered(1)
    return pl.pallas_call(
        _lru_kernel,
        out_shape=(jax.ShapeDtypeStruct((batch * n_ctx, width), BF16),
                   jax.ShapeDtypeStruct((batch * seq, width), BF16)),
        grid=(batch, LRU_HEADS),
        in_specs=[pl.BlockSpec((n_ctx, LANE), lambda b, h: (ctx_blk0 + b, g_col0 + h)),
                  pl.BlockSpec((n_ctx, LANE), lambda b, h: (ctx_blk0 + b, x_col0 + h)),
                  pl.BlockSpec((seq, LANE), lambda b, h: (b, g_col0 + h)),
                  pl.BlockSpec((seq, LANE), lambda b, h: (b, x_col0 + h)),
                  pl.BlockSpec((ROW_TILE, ROW_TILE), lambda b, h: (0, 0), pipeline_mode=once),
                  pl.BlockSpec((8, LANE), lambda b, h: (0, h)),
                  pl.BlockSpec((1, LANE), lambda b, h: (0, h)),
                  pl.BlockSpec((2, 1, LANE, 2 * LANE), lambda b, h: (0, h, 0, 0)),
                  pl.BlockSpec((2, 1, 1, 2 * LANE), lambda b, h: (0, h, 0, 0)),
                  pl.BlockSpec((2, LANE), lambda b, h: (0, h))],
        out_specs=(pl.BlockSpec((n_ctx, LANE), lambda b, h: (b, h)),
                   pl.BlockSpec((seq, LANE), lambda b, h: (b, h))),
        scratch_shapes=[pltpu.VMEM((seq, LANE), F32), pltpu.VMEM((seq, LANE), F32)],
        compiler_params=_params("parallel", "parallel"),
        name="rglru",
    )(u, u, u, u, perm.astype(BF16), cw, cb, wg, bg, lam)


def _chunk_mask(rows, reverse):
    r = lax.broadcasted_iota(jnp.int32, (rows, rows), 0)
    c = lax.broadcasted_iota(jnp.int32, (rows, rows), 1)
    same = (r >> CHUNK_SHIFT) == (c >> CHUNK_SHIFT)
    tri = (c >= r) if reverse else (c <= r)
    return jnp.where(same, jnp.where(tri, 1.0, 0.0), 0.0)


def _split3(x):
    hi = x.astype(BF16)
    r1 = x - hi.astype(F32)
    mid = r1.astype(BF16)
    lo = (r1 - mid.astype(F32)).astype(BF16)
    return hi, mid, lo


def _block_scan(g_all, head_qkv, st_scr, o_ref, *, heads, dv, reverse, batched):
    rows, wall = g_all.shape
    width = wall // heads
    nchunk = rows // CHUNK
    mask = _chunk_mask(rows, reverse)
    mask_b = mask.astype(BF16)
    keep = mask > 0.5
    keep_c = keep[:CHUNK, :CHUNK]
    iref = CHUNK // 2 if reverse else CHUNK // 2 - 1
    ilast = 0 if reverse else CHUNK - 1
    order = range(nchunk - 1, -1, -1) if reverse else range(nchunk)

    def per_chunk(rows_1, width):
        return jnp.concatenate([jnp.broadcast_to(r, (CHUNK, width)) for r in rows_1], axis=0)

    pieces = jnp.concatenate(_split3(g_all), axis=1)
    sums = jnp.dot(mask_b, pieces, preferred_element_type=F32)
    b_all = sums[:, :wall] + sums[:, wall:2 * wall] + sums[:, 2 * wall:]

    def prepare(h):
        q, k, v = head_qkv(h)
        b = b_all[:, h * width:(h + 1) * width]
        b_ref = [b[c * CHUNK + iref:c * CHUNK + iref + 1] for c in range(nchunk)]
        b_last = [b[c * CHUNK + ilast:c * CHUNK + ilast + 1] for c in range(nchunk)]
        e1 = jnp.exp(b - per_chunk(b_ref, width))
        r1 = 1.0 / e1
        qe = q * e1
        ke = k * r1
        qd = (qe * per_chunk([jnp.exp(r) for r in b_ref], width)).astype(BF16)
        kd = (ke * per_chunk([jnp.exp(l - r) for l, r in zip(b_last, b_ref)], width)).astype(BF16)
        decays = [jnp.exp(l) for l in b_last]
        return qe.astype(BF16), ke.astype(BF16), qd, kd, v.astype(BF16), decays

    nxt = prepare(0)
    for h in range(heads):
        qe, ke, qd, kd, vb, decays = nxt
        if h + 1 < heads:
            nxt = prepare(h + 1)
        if batched:
            sc = lax.dot_general(qe, ke, _NT, preferred_element_type=F32)
            sc = jnp.where(keep, sc, 0.0).astype(BF16)
            o = jnp.dot(sc, vb, preferred_element_type=F32)
        st = st_scr[h]
        parts = [None] * nchunk
        for ci in order:
            rs = slice(ci * CHUNK, (ci + 1) * CHUNK)
            if batched:
                o_c = o[rs]
            else:
                sc = lax.dot_general(qe[rs], ke[rs], _NT, preferred_element_type=F32)
                sc = jnp.where(keep_c, sc, 0.0).astype(BF16)
                o_c = jnp.dot(sc, vb[rs], preferred_element_type=F32)
            parts[ci] = o_c + lax.dot_general(qd[rs], st.astype(BF16), _NT,
                                              preferred_element_type=F32)
            st = st * decays[ci] + lax.dot_general(vb[rs], kd[rs], _TN, preferred_element_type=F32)
        if o_ref is not None:
            o_ref[:, h * dv:(h + 1) * dv] = jnp.concatenate(parts, axis=0).astype(o_ref.dtype)
        st_scr[h] = st


def _gla_kernel(q_ref, k_ref, v_ref, a_ref, wa_ref, ba_ref, o_ref, st_scr):
    t = pl.program_id(1)
    d = pl.program_id(2)
    dk = q_ref.shape[1] // GLA_HEADS
    dv = v_ref.shape[1] // GLA_HEADS

    @pl.when((t == 0) & (d == 0))
    def _():
        st_scr[...] = jnp.zeros(st_scr.shape, F32)

    def head_qkv(h):
        ks = slice(h * dk, (h + 1) * dk)
        q = q_ref[:, ks].astype(F32) * (dk ** -0.5)
        return q, k_ref[:, ks].astype(F32), v_ref[:, h * dv:(h + 1) * dv]

    for dd in range(2):
        @pl.when(d == dd)
        def _(dd=dd):
            z = jnp.dot(a_ref[...], wa_ref[dd], preferred_element_type=F32) + ba_ref[dd]
            g_all = _log_sigmoid(z) * (1.0 / GLA_TAU)
            _block_scan(g_all, head_qkv, st_scr.at[dd], o_ref,
                        heads=GLA_HEADS, dv=dv, reverse=dd == 1, batched=True)


def _gla(u, wa, ba, *, batch, seq, n_ctx, cols, dv_total):
    ntok = u.shape[0]
    q0, k0, v0, a0 = cols
    dkt = wa.shape[2]
    dvt = dv_total
    nblk = seq // ROW_TILE
    ctx_blk0 = batch * nblk
    assert n_ctx == ROW_TILE

    def blk(b, t, d):
        lat = b * nblk + jnp.where(d == 0, t - 1, nblk - t)
        return jnp.where(t == 0, ctx_blk0 + b, lat)

    in_specs = [pl.BlockSpec((ROW_TILE, dkt), lambda b, t, d: (blk(b, t, d), q0 * LANE // dkt)),
                pl.BlockSpec((ROW_TILE, dkt), lambda b, t, d: (blk(b, t, d), k0 * LANE // dkt)),
                pl.BlockSpec((ROW_TILE, dvt), lambda b, t, d: (blk(b, t, d), v0 * LANE // dvt)),
                pl.BlockSpec((ROW_TILE, LANE), lambda b, t, d: (blk(b, t, d), a0)),
                pl.BlockSpec(wa.shape, lambda b, t, d: (0, 0, 0)),
                pl.BlockSpec(ba.shape, lambda b, t, d: (0, 0, 0))]
    return pl.pallas_call(
        _gla_kernel,
        out_shape=jax.ShapeDtypeStruct((2, ntok, dvt), BF16),
        grid=(batch, nblk + 1, 2),
        in_specs=in_specs,
        out_specs=pl.BlockSpec((None, ROW_TILE, dvt), lambda b, t, d: (d, blk(b, t, d), 0)),
        scratch_shapes=[pltpu.VMEM((2, GLA_HEADS, dvt // GLA_HEADS, dkt // GLA_HEADS), F32)],
        compiler_params=_params("parallel", "arbitrary", "arbitrary"),
        name="gla",
    )(u, u, u, u, wa, ba)


def _hgrn_kernel(*refs, emit_o):
    q_ref, z_ref, v_ref, lb_ref, ub_ref, s0_ref = refs[:6]
    if emit_o:
        o_ref, st_ref, st_scr = refs[6:]
    else:
        st_ref, st_scr = refs[6:]
        o_ref = None
    t = pl.program_id(1)
    d = pl.program_id(2)
    dh = q_ref.shape[1] // HGRN_HEADS

    @pl.when((t == 0) & (d == 0))
    def _():
        st_scr[...] = s0_ref[:, 0]

    lb = lb_ref[...]
    ub = ub_ref[...]

    for dd in range(2):
        @pl.when(d == dd)
        def _(dd=dd):
            z = z_ref[...].astype(F32)
            e = jnp.exp(-jnp.abs(z))
            r = 1.0 / (1.0 + e)
            er = e * r
            pos = z >= 0.0
            log_f = jnp.log(lb + ub * jnp.where(pos, r, er))
            k_all = ub * jnp.where(pos, er, r)

            def head_qkv(h):
                hs = slice(h * dh, (h + 1) * dh)
                return q_ref[:, hs].astype(F32), k_all[:, hs], v_ref[:, hs]

            _block_scan(log_f, head_qkv, st_scr.at[dd], o_ref,
                        heads=HGRN_HEADS, dv=dh, reverse=dd == 1, batched=True)

    @pl.when((t == pl.num_programs(1) - 1) & (d == 1))
    def _():
        st_ref[:, 0] = st_scr[...]


def _hgrn_segment(u, s0, lb, ub, *, batch, seg_len, emit_o):
    ntok = u.shape[0]
    width = lb.shape[1]
    nblk = seg_len // ROW_TILE

    def blk(b, t, d):
        return b * nblk + jnp.where(d == 0, t, nblk - 1 - t)

    st_spec = pl.BlockSpec((2, 1) + s0.shape[2:], lambda b, t, d: (0, b, 0, 0, 0))
    in_specs = [pl.BlockSpec((ROW_TILE, width), lambda b, t, d: (blk(b, t, d), 0)),
                pl.BlockSpec((ROW_TILE, width), lambda b, t, d: (blk(b, t, d), 1 + d)),
                pl.BlockSpec((ROW_TILE, width), lambda b, t, d: (blk(b, t, d), 3)),
                pl.BlockSpec((1, width), lambda b, t, d: (0, 0)),
                pl.BlockSpec((1, width), lambda b, t, d: (0, 0)),
                st_spec]
    st_shape = jax.ShapeDtypeStruct(s0.shape, F32)
    if emit_o:
        out_shape = (jax.ShapeDtypeStruct((2, ntok, width), F32), st_shape)
        out_specs = (pl.BlockSpec((None, ROW_TILE, width), lambda b, t, d: (d, blk(b, t, d), 0)),
                     st_spec)
    else:
        out_shape = (st_shape,)
        out_specs = (st_spec,)
    return pl.pallas_call(
        functools.partial(_hgrn_kernel, emit_o=emit_o),
        out_shape=out_shape,
        grid=(batch, nblk, 2),
        in_specs=in_specs,
        out_specs=out_specs,
        scratch_shapes=[pltpu.VMEM((2,) + s0.shape[2:], F32)],
        compiler_params=_params("parallel", "arbitrary", "arbitrary"),
        name="hgrn_segment",
    )(u, u, u, lb, ub, s0)


def _head_norm_gate(o, gate, heads):
    dh = o.shape[1] // heads
    parts = []
    for h in range(heads):
        oh = o[:, h * dh:(h + 1) * dh]
        ms = jnp.mean(oh * oh, axis=-1, keepdims=True)
        parts.append((oh * lax.rsqrt(ms + EPS) * _silu(gate[:, h * dh:(h + 1) * dh])).astype(BF16))
    return jnp.concatenate(parts, axis=1)


def _out_proj_tail(o, x_ref, tab_ref, pg_ref, ng_ref, rt_ref, xo_ref, h_ref, lg_ref):
    xn = _gated_residual(x_ref[...], _rms(o, pg_ref[...]), tab_ref, 2)
    xo_ref[...] = xn
    h2 = _modulate(_rms(xn, ng_ref[...]), tab_ref, 3, 4)
    h_ref[...] = h2.astype(BF16)
    lg_ref[...] = lax.dot_general(rt_ref[...], h2, _NT, precision=lax.Precision.HIGHEST,
                                  preferred_element_type=F32)


def _out_proj_ab_kernel(yl_ref, yc_ref, pt_ref, of_ref, ob_ref, gate_ref, w_ref, x_ref, tab_ref,
                        pg_ref, ng_ref, rt_ref, xo_ref, h_ref, lg_ref, *, lat_tiles):
    o_att = of_ref[...].astype(F32) + ob_ref[...].astype(F32)
    yb = _head_norm_gate(o_att, gate_ref[...].astype(F32), GLA_HEADS)
    ya = jnp.where(pl.program_id(0) < lat_tiles, yl_ref[...], yc_ref[...])
    ya = jnp.concatenate(
        [jnp.dot(pt_ref[...], ya[s:s + ROW_TILE], preferred_element_type=F32).astype(BF16)
         for s in range(0, ya.shape[0], ROW_TILE)], axis=0)
    da = ya.shape[1]
    o = jnp.dot(ya, w_ref[:da], preferred_element_type=F32)
    o = o + jnp.dot(yb, w_ref[da:], preferred_element_type=F32)
    _out_proj_tail(o, x_ref, tab_ref, pg_ref, ng_ref, rt_ref, xo_ref, h_ref, lg_ref)


def _out_proj_c_kernel(of_ref, ob_ref, gate_ref, p_ref, w_ref, x_ref, tab_ref, pg_ref, ng_ref,
                       rt_ref, xo_ref, h_ref, lg_ref):
    tm = x_ref.shape[0]
    width = of_ref.shape[-1]
    o_att = of_ref[...].astype(F32) + ob_ref[...].astype(F32)
    y = _head_norm_gate(o_att.reshape(tm, width), gate_ref[...].astype(F32).reshape(tm, width),
                        HGRN_HEADS)
    y = jnp.dot(p_ref[...], y, preferred_element_type=F32).astype(BF16)
    o = jnp.dot(y, w_ref[...], preferred_element_type=F32)
    _out_proj_tail(o, x_ref, tab_ref, pg_ref, ng_ref, rt_ref, xo_ref, h_ref, lg_ref)


def _out_proj(kernel, lead, lead_specs, w, x, tab, post_gain, next_gain, router_t, tm, ntiles):
    d = x.shape[1]
    ntok = ntiles * tm
    nsub = tm // ROW_TILE
    ne = router_t.shape[0]
    in_specs = list(lead_specs) + [
        pl.BlockSpec(w.shape, lambda i: (0, 0), pipeline_mode=pl.Buffered(1)),
        pl.BlockSpec((tm, d), lambda i: (i, 0)),
        pl.BlockSpec((nsub, 8, d), lambda i: (i, 0, 0)),
        pl.BlockSpec((1, d), lambda i: (0, 0)),
        pl.BlockSpec((1, d), lambda i: (0, 0)),
        pl.BlockSpec((ne, d), lambda i: (0, 0))]
    return pl.pallas_call(
        kernel,
        out_shape=(jax.ShapeDtypeStruct((ntok, d), F32),
                   jax.ShapeDtypeStruct((ntok, d), BF16),
                   jax.ShapeDtypeStruct((ne, ntok), F32)),
        grid=(ntiles,),
        in_specs=in_specs,
        out_specs=(pl.BlockSpec((tm, d), lambda i: (i, 0)),
                   pl.BlockSpec((tm, d), lambda i: (i, 0)),
                   pl.BlockSpec((ne, tm), lambda i: (0, i))),
        compiler_params=_params("parallel"),
        name="out_proj_norm_residual",
    )(*lead, w, x, tab, post_gain, next_gain, router_t)


def _route_kernel(lg_ref, bias_ref, cw_ref):
    tm = lg_ref.shape[1]
    per_group = N_EXPERTS // N_GROUPS
    shape3 = (N_GROUPS, per_group, tm)
    aff = _sigmoid(lg_ref[...]).reshape(shape3)
    biased = aff + bias_ref[...].reshape(shape3)
    neg = -jnp.inf
    sub = lax.broadcasted_iota(jnp.int32, shape3, 1)
    grp = lax.broadcasted_iota(jnp.int32, shape3, 0)
    m1 = jnp.max(biased, axis=1, keepdims=True)
    i1 = jnp.min(jnp.where(biased == m1, sub, per_group), axis=1, keepdims=True)
    m2 = jnp.max(jnp.where(sub == i1, neg, biased), axis=1, keepdims=True)
    score = m1 + m2
    gidx = lax.broadcasted_iota(jnp.int32, score.shape, 0)
    keep = jnp.zeros(score.shape, F32)
    for _ in range(TOPK_GROUPS):
        m = jnp.max(score, axis=0, keepdims=True)
        im = jnp.min(jnp.where(score == m, gidx, N_GROUPS), axis=0, keepdims=True)
        sel = gidx == im
        keep = jnp.where(sel, 1.0, keep)
        score = jnp.where(sel, neg, score)
    work = jnp.where(jnp.broadcast_to(keep, shape3) > 0.0, biased, neg)
    eidx = grp * per_group + sub
    chosen = jnp.zeros(shape3, F32)
    for _ in range(TOP_K):
        m = jnp.max(jnp.max(work, axis=0, keepdims=True), axis=1, keepdims=True)
        cand = jnp.where(work == m, eidx, N_EXPERTS)
        im = jnp.min(jnp.min(cand, axis=0, keepdims=True), axis=1, keepdims=True)
        sel = eidx == im
        chosen = jnp.where(sel, 1.0, chosen)
        work = jnp.where(sel, neg, work)
    gate = aff * chosen
    den = jnp.sum(jnp.sum(gate, axis=0, keepdims=True), axis=1, keepdims=True)
    cw = (ROUTED_SCALE * gate / den).reshape(N_EXPERTS, tm)
    cw = jnp.concatenate([cw, jnp.zeros((LANE - N_EXPERTS, tm), F32)], axis=0)
    cw_ref[...] = cw.T


def _route(logits_t, bias, tm):
    ne, ntok = logits_t.shape
    bias_b = jnp.broadcast_to(bias.astype(F32)[:, None], (ne, tm))
    return pl.pallas_call(
        _route_kernel,
        out_shape=jax.ShapeDtypeStruct((ntok, LANE), F32),
        grid=(ntok // tm,),
        in_specs=[pl.BlockSpec((ne, tm), lambda i: (0, i)),
                  pl.BlockSpec((ne, tm), lambda i: (0, 0))],
        out_specs=pl.BlockSpec((tm, LANE), lambda i: (i, 0)),
        compiler_params=_params("parallel"),
        name="moe_route",
    )(logits_t, bias_b)


def _moe_kernel(h_ref, cw_ref, wg_ref, wu_ref, wd_ref, sg_ref, su_ref, sd_ref, x_ref, tab_ref,
                pg_ref, o_ref, acc_ref, *, epb):
    j = pl.program_id(1)
    h = h_ref[...]

    @pl.when(j == 0)
    def _():
        g = jnp.dot(h, sg_ref[...], preferred_element_type=F32)
        u = jnp.dot(h, su_ref[...], preferred_element_type=F32)
        acc_ref[...] = jnp.dot((_silu(g) * u).astype(BF16), sd_ref[...], preferred_element_type=F32)

    cw = cw_ref[...]
    lane = lax.broadcasted_iota(jnp.int32, cw.shape, 1)
    for e in range(epb):
        we = jnp.sum(jnp.where(lane == j * epb + e, cw, 0.0), axis=1, keepdims=True)
        g = jnp.dot(h, wg_ref[e].astype(BF16), preferred_element_type=F32)
        u = jnp.dot(h, wu_ref[e].astype(BF16), preferred_element_type=F32)
        a = (_silu(g) * u * we).astype(BF16)
        acc_ref[...] += jnp.dot(a, wd_ref[e].astype(BF16), preferred_element_type=F32)

    @pl.when(j == pl.num_programs(1) - 1)
    def _():
        o_ref[...] = _gated_residual(x_ref[...], _rms(acc_ref[...], pg_ref[...]), tab_ref, 5)


def _moe(h, cw, x, tab, post_gain, layer, wg, wu, wd, sg, su, sd, tile0, ntiles, tm, epb):
    d = h.shape[1]
    _, ne, _, ff = wg.shape
    nsub = tm // ROW_TILE
    once = pl.Buffered(1)
    return pl.pallas_call(
        functools.partial(_moe_kernel, epb=epb),
        out_shape=jax.ShapeDtypeStruct((ntiles * tm, d), F32),
        grid=(ntiles, ne // epb),
        in_specs=[pl.BlockSpec((tm, d), lambda i, j: (tile0 + i, 0), pipeline_mode=once),
                  pl.BlockSpec((tm, LANE), lambda i, j: (tile0 + i, 0), pipeline_mode=once),
                  pl.BlockSpec((None, epb, d, ff), lambda i, j: (layer, j, 0, 0)),
                  pl.BlockSpec((None, epb, d, ff), lambda i, j: (layer, j, 0, 0)),
                  pl.BlockSpec((None, epb, ff, d), lambda i, j: (layer, j, 0, 0)),
                  pl.BlockSpec(sg.shape, lambda i, j: (0, 0), pipeline_mode=once),
                  pl.BlockSpec(su.shape, lambda i, j: (0, 0), pipeline_mode=once),
                  pl.BlockSpec(sd.shape, lambda i, j: (0, 0), pipeline_mode=once),
                  pl.BlockSpec((tm, d), lambda i, j: (tile0 + i, 0), pipeline_mode=once),
                  pl.BlockSpec((nsub, 8, d), lambda i, j: (tile0 + i, 0, 0)),
                  pl.BlockSpec((1, d), lambda i, j: (0, 0))],
        out_specs=pl.BlockSpec((tm, d), lambda i, j: (i, 0)),
        scratch_shapes=[pltpu.VMEM((tm, d), F32)],
        compiler_params=_params("parallel", "arbitrary", vmem=VMEM_LIMIT_MOE),
        name="moe_experts",
    )(h, cw, wg, wu, wd, sg, su, sd, x, tab, post_gain)


def _grid_permutation(band):
    tm = band * GRID_W
    dst = jnp.arange(tm)
    src = (dst % band) * GRID_W + dst // band
    return (src[:, None] == jnp.arange(tm)[None, :]).astype(BF16)


def kernel(x, c, ctx, c_ctx, mod_w, mod_b, pre_gain, post_gain, ab_w_in, ab_conv_w, ab_conv_b, ab_lru_wa, ab_lru_ba, ab_lru_wi, ab_lru_bi, ab_lru_lambda, ab_gla_wa2, ab_gla_ba, ab_w_out, c_w_in, hgrn_lb_logits, c_w_out, moe_router, moe_bias, moe_w_gate, moe_w_up, moe_w_down, moe_ws_gate, moe_ws_up, moe_ws_down):
    batch, seq, d = x.shape
    n_ctx = ctx.shape[1]
    depth = mod_w.shape[0]
    band = SUBLANE
    tm_grid = band * GRID_W
    rows_img = seq // GRID_W
    assert n_ctx == ROW_TILE and seq % (2 * ROW_TILE) == 0 and d % LANE == 0
    assert depth == 2 and batch + 1 <= 8 and rows_img % band == 0 and tm_grid % ROW_TILE == 0
    n_lat = batch * seq
    n_ctx_tok = batch * n_ctx
    ntok = n_lat + n_ctx_tok

    xs = jnp.concatenate([x.reshape(n_lat, d), ctx.reshape(n_ctx_tok, d)], axis=0)

    cvec = jnp.concatenate([c, c_ctx[None, :], jnp.zeros((8 - batch - 1, d), F32)], axis=0)
    mods = _modulation(cvec, mod_w, mod_b).reshape(depth, 8, N_MOD, d)
    tile_row = jnp.concatenate([jnp.repeat(jnp.arange(batch), seq // ROW_TILE),
                                jnp.full((batch,), batch)]).astype(jnp.int32)
    tabs = jnp.pad(mods[:, tile_row], ((0, 0), (0, 0), (0, 8 - N_MOD), (0, 0)))

    def moe(layer, h2, cwt, xs_in, tile0, ntiles, tm):
        return _moe(h2, cwt, xs_in, tabs[layer], post_gain[layer, 1][None, :], layer,
                    moe_w_gate, moe_w_up, moe_w_down, moe_ws_gate[layer].astype(BF16),
                    moe_ws_up[layer].astype(BF16), moe_ws_down[layer].astype(BF16),
                    tile0, ntiles, tm, epb)

    tm_proj, tm_out, tm_route, tm_moe, epb = 512, 512, 512, 1024, 4
    assert n_lat % tm_moe == 0 and n_lat % n_ctx_tok == 0

    ab_cols = ab_w_in.shape[2]
    gla_dk = ab_gla_wa2.shape[3]
    lru_w = d
    gla_dv = (ab_cols - 2 * lru_w - 2 * gla_dk - 2 * GLA_RANK) // 2
    tn0 = 1792
    n0 = -(-(ab_cols) // tn0) * tn0
    w_in0 = jnp.pad(ab_w_in[0], ((0, 0), (0, n0 - ab_cols))).astype(BF16)
    u0 = _in_proj(xs, tabs[0], pre_gain[0, 0][None, :], w_in0, tm_proj, tn0, 0)

    cw = jnp.pad(ab_conv_w[0], ((0, 8 - CONV_W), (0, 0)))
    cb = ab_conv_b[0][None, :]
    wg = jnp.concatenate([ab_lru_wa[0], ab_lru_wi[0]], axis=-1).astype(BF16)
    bg = jnp.concatenate([ab_lru_ba[0].reshape(2, LRU_HEADS, 1, LANE),
                          ab_lru_bi[0].reshape(2, LRU_HEADS, 1, LANE)], axis=-1)
    ya_c, ya_l = _lru(u0, (cw, cb, wg, bg, ab_lru_lambda[0]), batch=batch, seq=seq, n_ctx=n_ctx,
                      g_col0=0, x_col0=lru_w // LANE)

    q0 = 2 * lru_w // LANE
    k0 = q0 + gla_dk // LANE
    v0 = k0 + gla_dk // LANE
    gate0 = v0 + gla_dv // LANE
    a0 = gate0 + gla_dv // LANE
    wa = jnp.stack([jnp.zeros((LANE, gla_dk), F32).at[dr * GLA_RANK:(dr + 1) * GLA_RANK]
                    .set(ab_gla_wa2[0, dr]) for dr in range(2)]).astype(BF16)
    ba = ab_gla_ba[0][:, None, :]
    o_gla = _gla(u0, wa, ba, batch=batch, seq=seq, n_ctx=n_ctx, cols=(q0, k0, v0, a0),
                 dv_total=gla_dv)

    lat_tiles = n_lat // tm_out
    ctx_tiles = n_ctx_tok // tm_out
    lead_specs = [pl.BlockSpec((tm_out, lru_w), lambda i: (jnp.minimum(i, lat_tiles - 1), 0)),
                  pl.BlockSpec((tm_out, lru_w),
                               lambda i: (jnp.clip(i - lat_tiles, 0, ctx_tiles - 1), 0)),
                  pl.BlockSpec((ROW_TILE, ROW_TILE), lambda i: (0, 0)),
                  pl.BlockSpec((None, tm_out, gla_dv), lambda i: (0, i, 0)),
                  pl.BlockSpec((None, tm_out, gla_dv), lambda i: (1, i, 0)),
                  pl.BlockSpec((tm_out, gla_dv), lambda i: (i, gate0 * LANE // gla_dv))]
    xs, h2, lg = _out_proj(functools.partial(_out_proj_ab_kernel, lat_tiles=lat_tiles),
                           (ya_l, ya_c, _time_permutation().T.astype(BF16), o_gla, o_gla, u0),
                           lead_specs,
                           ab_w_out[0].astype(BF16), xs, tabs[0], post_gain[0, 0][None, :],
                           pre_gain[0, 1][None, :], moe_router[0].T, tm_out, ntok // tm_out)
    cwt = _route(lg, moe_bias[0], tm_route)
    xs_ctx = moe(0, h2, cwt, xs, n_lat // n_ctx_tok, 1, n_ctx_tok)
    xs = moe(0, h2, cwt, xs, 0, n_lat // tm_moe, tm_moe)

    hd = c_w_out.shape[1]
    tn1 = 1280
    w_in1 = c_w_in[0].astype(BF16)
    c_cols = w_in1.shape[1]
    nsec = c_cols // hd
    perm = _grid_permutation(band)
    gain1 = pre_gain[1, 0][None, :]
    u1l = _in_proj_grid(xs, tabs[1], gain1, perm, w_in1, tn1, n_lat // tm_grid, band,
                        rows_img // band)
    u1c = _in_proj(xs_ctx, tabs[1], gain1, w_in1, tm_proj, tn1, n_lat // tm_proj)
    lb_sm = jax.nn.softmax(hgrn_lb_logits.astype(F32), axis=0)
    lb_cum = jnp.cumsum(lb_sm, axis=0)
    lb = (lb_cum[1] - lb_cum[0])[None, :]
    ub = 1.0 - lb
    dh = hd // HGRN_HEADS
    s_zero = jnp.zeros((2, batch, HGRN_HEADS, dh, dh), F32)
    u1l_flat = u1l.reshape(n_lat, c_cols)
    (s_c,) = _hgrn_segment(u1c, s_zero, lb, ub, batch=batch, seg_len=n_ctx, emit_o=False)
    o_hgrn, _ = _hgrn_segment(u1l_flat, s_c, lb, ub, batch=batch, seg_len=seq, emit_o=True)
    o_hgrn = o_hgrn.reshape(2, n_lat // rows_img, rows_img // band, band, hd)

    bpi = rows_img // band
    grid_blk = (GRID_W, None, band, hd)
    lead_specs = [pl.BlockSpec((None,) + grid_blk, lambda i: (0, i // bpi, i % bpi, 0, 0)),
                  pl.BlockSpec((None,) + grid_blk, lambda i: (1, i // bpi, i % bpi, 0, 0)),
                  pl.BlockSpec(grid_blk, lambda i: (i // bpi, i % bpi, 0, nsec - 1)),
                  pl.BlockSpec((tm_grid, tm_grid), lambda i: (0, 0))]
    xs, h2, lg = _out_proj(_out_proj_c_kernel, (o_hgrn, o_hgrn, u1l, perm.T), lead_specs,
                           c_w_out[0].astype(BF16), xs, tabs[1], post_gain[1, 0][None, :],
                           pre_gain[1, 1][None, :], moe_router[1].T, tm_grid, n_lat // tm_grid)
    out = moe(1, h2, _route(lg, moe_bias[1], tm_route), xs, 0, n_lat // tm_moe, tm_moe)
    return out.reshape(batch, seq, d)
```

```python
import functools

import jax
import jax.numpy as jnp
from jax import lax
from jax.experimental import pallas as pl
from jax.experimental.pallas import tpu as pltpu

F32 = jnp.float32
BF16 = jnp.bfloat16

EPS = 1e-6
N_MOD = 6
ROW_TILE = 256
CHUNK = 64
CHUNK_SHIFT = 6
GRID_W = 64
LRU_HEADS = 8
LRU_C = 8.0
CONV_W = 4
GLA_HEADS = 4
GLA_RANK = 16
GLA_TAU = 16.0
HGRN_HEADS = 8
N_EXPERTS = 64
TOP_K = 8
N_GROUPS = 8
TOPK_GROUPS = 4
ROUTED_SCALE = 2.5
LANE = 128
SUBLANE = 8
BF16_ROWS = 16
CHUNKS_PER_ROW = 8
MOE_BLOCK = 128
VMEM_LIMIT = 48 * 1024 * 1024
VMEM_LIMIT_MOE = 58 * 1024 * 1024

_NT = (((1,), (1,)), ((), ()))
_TN = (((0,), (0,)), ((), ()))


def _params(*sem, vmem=VMEM_LIMIT):
    return pltpu.CompilerParams(dimension_semantics=sem, vmem_limit_bytes=vmem)


def _rms(x, gain):
    ms = jnp.mean(x * x, axis=-1, keepdims=True)
    return x * lax.rsqrt(ms + EPS) * gain


def _sigmoid(x):
    return jax.nn.sigmoid(x)


def _silu(x):
    return x * _sigmoid(x)


def _log_sigmoid(x):
    return jnp.minimum(x, 0.0) - jnp.log1p(jnp.exp(-jnp.abs(x)))


def _softplus(x):
    return jnp.maximum(x, 0.0) + jnp.log1p(jnp.exp(-jnp.abs(x)))


def _modulate(xn, tab_ref, shift_row, scale_row):
    parts = []
    for s in range(tab_ref.shape[0]):
        t = tab_ref[s]
        rows = xn[s * ROW_TILE:(s + 1) * ROW_TILE]
        parts.append(rows * (1.0 + t[scale_row:scale_row + 1]) + t[shift_row:shift_row + 1])
    return parts[0] if len(parts) == 1 else jnp.concatenate(parts, axis=0)


def _gated_residual(x, branch, tab_ref, gate_row):
    parts = []
    for s in range(tab_ref.shape[0]):
        sl = slice(s * ROW_TILE, (s + 1) * ROW_TILE)
        parts.append(x[sl] + tab_ref[s][gate_row:gate_row + 1] * branch[sl])
    return parts[0] if len(parts) == 1 else jnp.concatenate(parts, axis=0)


def _mod_kernel(c_ref, w_ref, b_ref, o_ref):
    s = _silu(c_ref[...])
    o_ref[0] = jnp.dot(s, w_ref[0], precision=lax.Precision.HIGHEST,
                       preferred_element_type=F32) + b_ref[0]


def _modulation(cvec, mod_w, mod_b):
    depth, d, n = mod_w.shape
    tn = 1024
    return pl.pallas_call(
        _mod_kernel,
        out_shape=jax.ShapeDtypeStruct((depth, 8, n), F32),
        grid=(depth, n // tn),
        in_specs=[pl.BlockSpec((8, d), lambda l, j: (0, 0)),
                  pl.BlockSpec((1, d, tn), lambda l, j: (l, 0, j)),
                  pl.BlockSpec((1, 1, tn), lambda l, j: (l, 0, j))],
        out_specs=pl.BlockSpec((1, 8, tn), lambda l, j: (l, 0, j)),
        compiler_params=_params("parallel", "parallel"),
        name="adaln_modulation",
    )(cvec, mod_w, mod_b.reshape(depth, 1, n))


def _in_proj_kernel(x_ref, tab_ref, gain_ref, w_ref, o_ref, *, tn):
    h = _modulate(_rms(x_ref[...], gain_ref[...]), tab_ref, 0, 1).astype(BF16)
    for j in range(w_ref.shape[1] // tn):
        cs = slice(j * tn, (j + 1) * tn)
        o_ref[:, cs] = jnp.dot(h, w_ref[:, cs], preferred_element_type=F32).astype(o_ref.dtype)


def _in_proj(x, tab, gain, w, tm, tn, tab0):
    d = x.shape[1]
    n = w.shape[1]
    nsub = tm // ROW_TILE
    ntiles = x.shape[0] // tm
    return pl.pallas_call(
        functools.partial(_in_proj_kernel, tn=tn),
        out_shape=jax.ShapeDtypeStruct((ntiles * tm, n), BF16),
        grid=(ntiles,),
        in_specs=[pl.BlockSpec((tm, d), lambda i: (i, 0)),
                  pl.BlockSpec((nsub, 8, d), lambda i: (tab0 + i, 0, 0)),
                  pl.BlockSpec((1, d), lambda i: (0, 0)),
                  pl.BlockSpec((d, n), lambda i: (0, 0), pipeline_mode=pl.Buffered(1))],
        out_specs=pl.BlockSpec((tm, n), lambda i: (i, 0)),
        compiler_params=_params("parallel"),
        name="norm_mod_in_proj",
    )(x, tab, gain, w)


def _in_proj_grid_kernel(x_ref, tab_ref, gain_ref, p_ref, w_ref, o_ref, *, tn):
    h = _modulate(_rms(x_ref[...], gain_ref[...]), tab_ref, 0, 1).astype(BF16)
    hp = jnp.dot(p_ref[...], h, preferred_element_type=F32).astype(BF16)
    for j in range(w_ref.shape[1] // tn):
        cs = slice(j * tn, (j + 1) * tn)
        o = jnp.dot(hp, w_ref[:, cs], preferred_element_type=F32).astype(o_ref.dtype)
        o_ref[:, :, cs] = o.reshape(o_ref.shape[:2] + (tn,))


def _in_proj_grid(x, tab, gain, perm, w, tn, ntiles, band, bands_per_image):
    d = x.shape[1]
    n = w.shape[1]
    tm = band * GRID_W
    nsub = tm // ROW_TILE
    images = ntiles // bands_per_image
    once = pl.Buffered(1)
    return pl.pallas_call(
        functools.partial(_in_proj_grid_kernel, tn=tn),
        out_shape=jax.ShapeDtypeStruct((images * GRID_W, bands_per_image, band, n), F32),
        grid=(ntiles,),
        in_specs=[pl.BlockSpec((tm, d), lambda i: (i, 0)),
                  pl.BlockSpec((nsub, 8, d), lambda i: (i, 0, 0)),
                  pl.BlockSpec((1, d), lambda i: (0, 0)),
                  pl.BlockSpec((tm, tm), lambda i: (0, 0), pipeline_mode=once),
                  pl.BlockSpec((d, n), lambda i: (0, 0), pipeline_mode=once)],
        out_specs=pl.BlockSpec((GRID_W, None, band, n),
                               lambda i: (i // bands_per_image, i % bands_per_image, 0, 0)),
        compiler_params=_params("parallel"),
        name="norm_mod_in_proj_grid",
    )(x, tab, gain, perm, w)


def _lru_kernel(gc_ref, xc_ref, gl_ref, xl_ref, p_ref, cw_ref, cb_ref, wg_ref, bg_ref,
                lam_ref, yc_ref, yl_ref, xs_scr, hf_scr):
    T = ROW_TILE
    V = T // SUBLANE
    H = BF16_ROWS
    sub = lax.broadcasted_iota(jnp.int32, (SUBLANE, LANE), 0)
    cw = cw_ref[...]
    cb = cb_ref[...]
    sp = _softplus(-lam_ref[...])

    def vreg(a, v):
        return a[v * SUBLANE:(v + 1) * SUBLANE]

    def shift_down(y, first):
        return jnp.where(sub == 0, first, pltpu.roll(y, 1, 0))

    def shift_up(y, last):
        return jnp.where(sub == SUBLANE - 1, last, pltpu.roll(y, SUBLANE - 1, 0))

    def conv_tile(x_ref, t, nt, seg_len):
        t0 = pl.multiple_of(t * T, T)
        xp = jnp.dot(p_ref[...], x_ref[pl.ds(t0, T), :], preferred_element_type=F32)
        p0 = pl.multiple_of(jnp.maximum(t0 - H, 0), H)
        n0 = pl.multiple_of(jnp.minimum(t0 + T, seg_len - H), H)
        prev = x_ref[pl.ds(p0, H), :].astype(F32) * jnp.where(t > 0, 1.0, 0.0)
        nxt = x_ref[pl.ds(n0, H), :].astype(F32) * jnp.where(t < nt - 1, 1.0, 0.0)
        m1, m2, p1 = prev[H - 1:H], prev[H - 2:H - 1], nxt[0:1]
        last_m1 = shift_down(vreg(xp, V - 1), m1)
        xm1 = jnp.concatenate([last_m1, xp[:T - SUBLANE]], axis=0)
        xm2 = jnp.concatenate([shift_down(vreg(xp, V - 2), m2), last_m1, xp[:T - 2 * SUBLANE]], axis=0)
        xp1 = jnp.concatenate([xp[SUBLANE:], shift_up(vreg(xp, 0), p1)], axis=0)
        return cb + xm2 * cw[0:1] + xm1 * cw[1:2] + xp * cw[2:3] + xp1 * cw[3:4]

    def gates(xc, d):
        z = jnp.dot(xc.astype(BF16), wg_ref[d, 0], preferred_element_type=F32) + bg_ref[d, 0]
        r = _sigmoid(z[:, :LANE])
        i = _sigmoid(z[:, LANE:])
        log_a = (-LRU_C) * r * sp[d:d + 1]
        a = jnp.exp(log_a)
        return a, jnp.sqrt(1.0 - a * a) * (i * xc)

    def scan_tile(a, u, carry, reverse):
        hs, cum = [None] * V, [None] * V
        hp = ap = None
        for v in (range(V - 1, -1, -1) if reverse else range(V)):
            av, uv = vreg(a, v), vreg(u, v)
            hp, ap = (uv, av) if hp is None else (av * hp + uv, av * ap)
            hs[v], cum[v] = hp, ap
        s = 1
        while s < SUBLANE:
            keep = (sub < SUBLANE - s) if reverse else (sub >= s)
            sh = SUBLANE - s if reverse else s
            h_sh = jnp.where(keep, pltpu.roll(hp, sh, 0), 0.0)
            a_sh = jnp.where(keep, pltpu.roll(ap, sh, 0), 1.0)
            hp = hp + ap * h_sh
            ap = ap * a_sh
            s *= 2
        state = hp + ap * carry
        if reverse:
            start, out = shift_up(state, carry), state[0:1]
        else:
            start, out = shift_down(state, carry), state[SUBLANE - 1:SUBLANE]
        h = jnp.concatenate([hs[v] + cum[v] * start for v in range(V)], axis=0)
        return h, out

    def run_segment(g_ref, x_ref, y_ref, seg_len, carry_f, carry_b):
        nt = seg_len // T
        unroll = 4 if nt % 4 == 0 else (2 if nt % 2 == 0 else 1)

        def fwd_body(i, carry):
            for j in range(unroll):
                t = i * unroll + j
                t0 = pl.multiple_of(t * T, T)
                xc = conv_tile(x_ref, t, nt, seg_len)
                xs_scr[pl.ds(t0, T), :] = xc
                a, u = gates(xc, 0)
                h, carry = scan_tile(a, u, carry, False)
                hf_scr[pl.ds(t0, T), :] = h
            return carry

        def bwd_body(i, carry):
            for j in range(unroll):
                t = nt - 1 - (i * unroll + j)
                t0 = pl.multiple_of(t * T, T)
                a, u = gates(xs_scr[pl.ds(t0, T), :], 1)
                h, carry = scan_tile(a, u, carry, True)
                g = jnp.dot(p_ref[...], g_ref[pl.ds(t0, T), :], preferred_element_type=F32)
                rec = hf_scr[pl.ds(t0, T), :] + h
                y_ref[pl.ds(t0, T), :] = (jax.nn.gelu(g) * rec).astype(y_ref.dtype)
            return carry

        steps = nt // unroll
        return (lax.fori_loop(0, steps, fwd_body, carry_f),
                lax.fori_loop(0, steps, bwd_body, carry_b))

    zero = jnp.zeros((1, LANE), F32)
    hf, hb = run_segment(gc_ref, xc_ref, yc_ref, xc_ref.shape[0], zero, zero)
    run_segment(gl_ref, xl_ref, yl_ref, xl_ref.shape[0], hf, hb)


def _time_permutation():
    v = ROW_TILE // SUBLANE
    dst = jnp.arange(ROW_TILE)
    src = (dst % SUBLANE) * v + dst // SUBLANE
    return src[:, None] == jnp.arange(ROW_TILE)[None, :]


def _lru(u, lru_w, *, batch, seq, n_ctx, g_col0, x_col0):
    width = LRU_HEADS * LANE
    cw, cb, wg, bg, lam = lru_w
    ctx_blk0 = batch * seq // n_ctx
    perm = _time_permutation()
    once = pl.Buffered(1)
    return pl.pallas_call(
        _lru_kernel,
        out_shape=(jax.ShapeDtypeStruct((batch * n_ctx, width), BF16),
                   jax.ShapeDtypeStruct((batch * seq, width), BF16)),
        grid=(batch, LRU_HEADS),
        in_specs=[pl.BlockSpec((n_ctx, LANE), lambda b, h: (ctx_blk0 + b, g_col0 + h)),
                  pl.BlockSpec((n_ctx, LANE), lambda b, h: (ctx_blk0 + b, x_col0 + h)),
                  pl.BlockSpec((seq, LANE), lambda b, h: (b, g_col0 + h)),
                  pl.BlockSpec((seq, LANE), lambda b, h: (b, x_col0 + h)),
                  pl.BlockSpec((ROW_TILE, ROW_TILE), lambda b, h: (0, 0), pipeline_mode=once),
                  pl.BlockSpec((8, LANE), lambda b, h: (0, h)),
                  pl.BlockSpec((1, LANE), lambda b, h: (0, h)),
                  pl.BlockSpec((2, 1, LANE, 2 * LANE), lambda b, h: (0, h, 0, 0)),
                  pl.BlockSpec((2, 1, 1, 2 * LANE), lambda b, h: (0, h, 0, 0)),
                  pl.BlockSpec((2, LANE), lambda b, h: (0, h))],
        out_specs=(pl.BlockSpec((n_ctx, LANE), lambda b, h: (b, h)),
                   pl.BlockSpec((seq, LANE), lambda b, h: (b, h))),
        scratch_shapes=[pltpu.VMEM((seq, LANE), F32), pltpu.VMEM((seq, LANE), F32)],
        compiler_params=_params("parallel", "parallel"),
        name="rglru",
    )(u, u, u, u, perm.astype(BF16), cw, cb, wg, bg, lam)


def _chunk_mask(rows, reverse):
    r = lax.broadcasted_iota(jnp.int32, (rows, rows), 0)
    c = lax.broadcasted_iota(jnp.int32, (rows, rows), 1)
    same = (r >> CHUNK_SHIFT) == (c >> CHUNK_SHIFT)
    tri = (c >= r) if reverse else (c <= r)
    return jnp.where(same, jnp.where(tri, 1.0, 0.0), 0.0)


def _split3(x):
    hi = x.astype(BF16)
    r1 = x - hi.astype(F32)
    mid = r1.astype(BF16)
    lo = (r1 - mid.astype(F32)).astype(BF16)
    return hi, mid, lo


def _block_scan(g_all, head_qkv, st_scr, o_ref, *, heads, dv, reverse, batched):
    rows, wall = g_all.shape
    width = wall // heads
    nchunk = rows // CHUNK
    mask = _chunk_mask(rows, reverse)
    mask_b = mask.astype(BF16)
    keep = mask > 0.5
    keep_c = keep[:CHUNK, :CHUNK]
    iref = CHUNK // 2 if reverse else CHUNK // 2 - 1
    ilast = 0 if reverse else CHUNK - 1
    order = range(nchunk - 1, -1, -1) if reverse else range(nchunk)

    def per_chunk(rows_1, width):
        return jnp.concatenate([jnp.broadcast_to(r, (CHUNK, width)) for r in rows_1], axis=0)

    pieces = jnp.concatenate(_split3(g_all), axis=1)
    sums = jnp.dot(mask_b, pieces, preferred_element_type=F32)
    b_all = sums[:, :wall] + sums[:, wall:2 * wall] + sums[:, 2 * wall:]

    def prepare(h):
        q, k, v = head_qkv(h)
        b = b_all[:, h * width:(h + 1) * width]
        b_ref = [b[c * CHUNK + iref:c * CHUNK + iref + 1] for c in range(nchunk)]
        b_last = [b[c * CHUNK + ilast:c * CHUNK + ilast + 1] for c in range(nchunk)]
        e1 = jnp.exp(b - per_chunk(b_ref, width))
        r1 = 1.0 / e1
        qe = q * e1
        ke = k * r1
        qd = (qe * per_chunk([jnp.exp(r) for r in b_ref], width)).astype(BF16)
        kd = (ke * per_chunk([jnp.exp(l - r) for l, r in zip(b_last, b_ref)], width)).astype(BF16)
        decays = [jnp.exp(l) for l in b_last]
        return qe.astype(BF16), ke.astype(BF16), qd, kd, v.astype(BF16), decays

    nxt = prepare(0)
    for h in range(heads):
        qe, ke, qd, kd, vb, decays = nxt
        if h + 1 < heads:
            nxt = prepare(h + 1)
        if batched:
            sc = lax.dot_general(qe, ke, _NT, preferred_element_type=F32)
            sc = jnp.where(keep, sc, 0.0).astype(BF16)
            o = jnp.dot(sc, vb, preferred_element_type=F32)
        st = st_scr[h]
        parts = [None] * nchunk
        for ci in order:
            rs = slice(ci * CHUNK, (ci + 1) * CHUNK)
            if batched:
                o_c = o[rs]
            else:
                sc = lax.dot_general(qe[rs], ke[rs], _NT, preferred_element_type=F32)
                sc = jnp.where(keep_c, sc, 0.0).astype(BF16)
                o_c = jnp.dot(sc, vb[rs], preferred_element_type=F32)
            parts[ci] = o_c + lax.dot_general(qd[rs], st.astype(BF16), _NT,
                                              preferred_element_type=F32)
            st = st * decays[ci] + lax.dot_general(vb[rs], kd[rs], _TN, preferred_element_type=F32)
        if o_ref is not None:
            o_ref[:, h * dv:(h + 1) * dv] = jnp.concatenate(parts, axis=0).astype(o_ref.dtype)
        st_scr[h] = st


def _gla_kernel(q_ref, k_ref, v_ref, a_ref, wa_ref, ba_ref, o_ref, st_scr):
    t = pl.program_id(1)
    d = pl.program_id(2)
    dk = q_ref.shape[1] // GLA_HEADS
    dv = v_ref.shape[1] // GLA_HEADS

    @pl.when((t == 0) & (d == 0))
    def _():
        st_scr[...] = jnp.zeros(st_scr.shape, F32)

    def head_qkv(h):
        ks = slice(h * dk, (h + 1) * dk)
        q = q_ref[:, ks].astype(F32) * (dk ** -0.5)
        return q, k_ref[:, ks].astype(F32), v_ref[:, h * dv:(h + 1) * dv]

    for dd in range(2):
        @pl.when(d == dd)
        def _(dd=dd):
            z = jnp.dot(a_ref[...], wa_ref[dd], preferred_element_type=F32) + ba_ref[dd]
            g_all = _log_sigmoid(z) * (1.0 / GLA_TAU)
            _block_scan(g_all, head_qkv, st_scr.at[dd], o_ref,
                        heads=GLA_HEADS, dv=dv, reverse=dd == 1, batched=True)


def _gla(u, wa, ba, *, batch, seq, n_ctx, cols, dv_total):
    ntok = u.shape[0]
    q0, k0, v0, a0 = cols
    dkt = wa.shape[2]
    dvt = dv_total
    nblk = seq // ROW_TILE
    ctx_blk0 = batch * nblk
    assert n_ctx == ROW_TILE

    def blk(b, t, d):
        lat = b * nblk + jnp.where(d == 0, t - 1, nblk - t)
        return jnp.where(t == 0, ctx_blk0 + b, lat)

    in_specs = [pl.BlockSpec((ROW_TILE, dkt), lambda b, t, d: (blk(b, t, d), q0 * LANE // dkt)),
                pl.BlockSpec((ROW_TILE, dkt), lambda b, t, d: (blk(b, t, d), k0 * LANE // dkt)),
                pl.BlockSpec((ROW_TILE, dvt), lambda b, t, d: (blk(b, t, d), v0 * LANE // dvt)),
                pl.BlockSpec((ROW_TILE, LANE), lambda b, t, d: (blk(b, t, d), a0)),
                pl.BlockSpec(wa.shape, lambda b, t, d: (0, 0, 0)),
                pl.BlockSpec(ba.shape, lambda b, t, d: (0, 0, 0))]
    return pl.pallas_call(
        _gla_kernel,
        out_shape=jax.ShapeDtypeStruct((2, ntok, dvt), BF16),
        grid=(batch, nblk + 1, 2),
        in_specs=in_specs,
        out_specs=pl.BlockSpec((None, ROW_TILE, dvt), lambda b, t, d: (d, blk(b, t, d), 0)),
        scratch_shapes=[pltpu.VMEM((2, GLA_HEADS, dvt // GLA_HEADS, dkt // GLA_HEADS), F32)],
        compiler_params=_params("parallel", "arbitrary", "arbitrary"),
        name="gla",
    )(u, u, u, u, wa, ba)


def _hgrn_kernel(*refs, emit_o):
    q_ref, z_ref, v_ref, lb_ref, ub_ref, s0_ref = refs[:6]
    if emit_o:
        o_ref, st_ref, st_scr = refs[6:]
    else:
        st_ref, st_scr = refs[6:]
        o_ref = None
    t = pl.program_id(1)
    d = pl.program_id(2)
    dh = q_ref.shape[1] // HGRN_HEADS

    @pl.when((t == 0) & (d == 0))
    def _():
        st_scr[...] = s0_ref[:, 0]

    lb = lb_ref[...]
    ub = ub_ref[...]

    for dd in range(2):
        @pl.when(d == dd)
        def _(dd=dd):
            z = z_ref[...].astype(F32)
            e = jnp.exp(-jnp.abs(z))
            r = 1.0 / (1.0 + e)
            er = e * r
            pos = z >= 0.0
            log_f = jnp.log(lb + ub * jnp.where(pos, r, er))
            k_all = ub * jnp.where(pos, er, r)

            def head_qkv(h):
                hs = slice(h * dh, (h + 1) * dh)
                return q_ref[:, hs].astype(F32), k_all[:, hs], v_ref[:, hs]

            _block_scan(log_f, head_qkv, st_scr.at[dd], o_ref,
                        heads=HGRN_HEADS, dv=dh, reverse=dd == 1, batched=True)

    @pl.when((t == pl.num_programs(1) - 1) & (d == 1))
    def _():
        st_ref[:, 0] = st_scr[...]


def _hgrn_segment(u, s0, lb, ub, *, batch, seg_len, emit_o):
    ntok = u.shape[0]
    width = lb.shape[1]
    nblk = seg_len // ROW_TILE

    def blk(b, t, d):
        return b * nblk + jnp.where(d == 0, t, nblk - 1 - t)

    st_spec = pl.BlockSpec((2, 1) + s0.shape[2:], lambda b, t, d: (0, b, 0, 0, 0))
    in_specs = [pl.BlockSpec((ROW_TILE, width), lambda b, t, d: (blk(b, t, d), 0)),
                pl.BlockSpec((ROW_TILE, width), lambda b, t, d: (blk(b, t, d), 1 + d)),
                pl.BlockSpec((ROW_TILE, width), lambda b, t, d: (blk(b, t, d), 3)),
                pl.BlockSpec((1, width), lambda b, t, d: (0, 0)),
                pl.BlockSpec((1, width), lambda b, t, d: (0, 0)),
                st_spec]
    st_shape = jax.ShapeDtypeStruct(s0.shape, F32)
    if emit_o:
        out_shape = (jax.ShapeDtypeStruct((2, ntok, width), F32), st_shape)
        out_specs = (pl.BlockSpec((None, ROW_TILE, width), lambda b, t, d: (d, blk(b, t, d), 0)),
                     st_spec)
    else:
        out_shape = (st_shape,)
        out_specs = (st_spec,)
    return pl.pallas_call(
        functools.partial(_hgrn_kernel, emit_o=emit_o),
        out_shape=out_shape,
        grid=(batch, nblk, 2),
        in_specs=in_specs,
        out_specs=out_specs,
        scratch_shapes=[pltpu.VMEM((2,) + s0.shape[2:], F32)],
        compiler_params=_params("parallel", "arbitrary", "arbitrary"),
        name="hgrn_segment",
    )(u, u, u, lb, ub, s0)


def _head_norm_gate(o, gate, heads):
    dh = o.shape[1] // heads
    parts = []
    for h in range(heads):
        oh = o[:, h * dh:(h + 1) * dh]
        ms = jnp.mean(oh * oh, axis=-1, keepdims=True)
        parts.append((oh * lax.rsqrt(ms + EPS) * _silu(gate[:, h * dh:(h + 1) * dh])).astype(BF16))
    return jnp.concatenate(parts, axis=1)


def _load_rows(ref, r0, nrows):
    s_n = CHUNKS_PER_ROW
    return jnp.concatenate([ref[pl.ds(r0 * s_n + s, nrows, stride=s_n), :] for s in range(s_n)],
                           axis=1)


def _store_rows(ref, r0, val):
    s_n = CHUNKS_PER_ROW
    for s in range(s_n):
        ref[pl.ds(r0 * s_n + s, val.shape[0], stride=s_n), :] = val[:, s * LANE:(s + 1) * LANE]


def _out_proj_tail(o, x_ref, tab_ref, pg_ref, ng_ref, rt_ref, xo_ref, h_ref, lg_ref):
    xn = _gated_residual(x_ref[...], _rms(o, pg_ref[...]), tab_ref, 2)
    xo_ref[...] = xn
    h2 = _modulate(_rms(xn, ng_ref[...]), tab_ref, 3, 4)
    _store_rows(h_ref, 0, h2)
    lg_ref[...] = lax.dot_general(rt_ref[...], h2, _NT, precision=lax.Precision.HIGHEST,
                                  preferred_element_type=F32)


def _out_proj_ab_kernel(yl_ref, yc_ref, pt_ref, of_ref, ob_ref, gate_ref, w_ref, x_ref, tab_ref,
                        pg_ref, ng_ref, rt_ref, xo_ref, h_ref, lg_ref, *, lat_tiles):
    o_att = of_ref[...].astype(F32) + ob_ref[...].astype(F32)
    yb = _head_norm_gate(o_att, gate_ref[...].astype(F32), GLA_HEADS)
    ya = jnp.where(pl.program_id(0) < lat_tiles, yl_ref[...], yc_ref[...])
    ya = jnp.concatenate(
        [jnp.dot(pt_ref[...], ya[s:s + ROW_TILE], preferred_element_type=F32).astype(BF16)
         for s in range(0, ya.shape[0], ROW_TILE)], axis=0)
    da = ya.shape[1]
    o = jnp.dot(ya, w_ref[:da], preferred_element_type=F32)
    o = o + jnp.dot(yb, w_ref[da:], preferred_element_type=F32)
    _out_proj_tail(o, x_ref, tab_ref, pg_ref, ng_ref, rt_ref, xo_ref, h_ref, lg_ref)


def _out_proj_c_kernel(of_ref, ob_ref, gate_ref, p_ref, w_ref, x_ref, tab_ref, pg_ref, ng_ref,
                       rt_ref, xo_ref, h_ref, lg_ref):
    tm = x_ref.shape[0]
    width = of_ref.shape[-1]
    o_att = of_ref[...].astype(F32) + ob_ref[...].astype(F32)
    y = _head_norm_gate(o_att.reshape(tm, width), gate_ref[...].astype(F32).reshape(tm, width),
                        HGRN_HEADS)
    y = jnp.dot(p_ref[...], y, preferred_element_type=F32).astype(BF16)
    o = jnp.dot(y, w_ref[...], preferred_element_type=F32)
    _out_proj_tail(o, x_ref, tab_ref, pg_ref, ng_ref, rt_ref, xo_ref, h_ref, lg_ref)


def _out_proj(kernel, lead, lead_specs, w, x, tab, post_gain, next_gain, router_t, tm, ntiles):
    d = x.shape[1]
    ntok = ntiles * tm
    nsub = tm // ROW_TILE
    ne = router_t.shape[0]
    in_specs = list(lead_specs) + [
        pl.BlockSpec(w.shape, lambda i: (0, 0), pipeline_mode=pl.Buffered(1)),
        pl.BlockSpec((tm, d), lambda i: (i, 0)),
        pl.BlockSpec((nsub, 8, d), lambda i: (i, 0, 0)),
        pl.BlockSpec((1, d), lambda i: (0, 0)),
        pl.BlockSpec((1, d), lambda i: (0, 0)),
        pl.BlockSpec((ne, d), lambda i: (0, 0))]
    return pl.pallas_call(
        kernel,
        out_shape=(jax.ShapeDtypeStruct((ntok, d), F32),
                   jax.ShapeDtypeStruct((ntok * CHUNKS_PER_ROW, LANE), F32),
                   jax.ShapeDtypeStruct((ne, ntok), F32)),
        grid=(ntiles,),
        in_specs=in_specs,
        out_specs=(pl.BlockSpec((tm, d), lambda i: (i, 0)),
                   pl.BlockSpec((tm * CHUNKS_PER_ROW, LANE), lambda i: (i, 0)),
                   pl.BlockSpec((ne, tm), lambda i: (0, i))),
        compiler_params=_params("parallel"),
        name="out_proj_norm_residual",
    )(*lead, w, x, tab, post_gain, next_gain, router_t)


def _route_kernel(lg_ref, bias_ref, ids_ref, w_ref):
    tm = lg_ref.shape[1]
    per_group = N_EXPERTS // N_GROUPS
    shape3 = (N_GROUPS, per_group, tm)
    aff = _sigmoid(lg_ref[...]).reshape(shape3)
    biased = aff + bias_ref[...].reshape(shape3)
    neg = -jnp.inf
    sub = lax.broadcasted_iota(jnp.int32, shape3, 1)
    grp = lax.broadcasted_iota(jnp.int32, shape3, 0)
    m1 = jnp.max(biased, axis=1, keepdims=True)
    i1 = jnp.min(jnp.where(biased == m1, sub, per_group), axis=1, keepdims=True)
    m2 = jnp.max(jnp.where(sub == i1, neg, biased), axis=1, keepdims=True)
    score = m1 + m2
    gidx = lax.broadcasted_iota(jnp.int32, score.shape, 0)
    keep = jnp.zeros(score.shape, F32)
    for _ in range(TOPK_GROUPS):
        m = jnp.max(score, axis=0, keepdims=True)
        im = jnp.min(jnp.where(score == m, gidx, N_GROUPS), axis=0, keepdims=True)
        sel = gidx == im
        keep = jnp.where(sel, 1.0, keep)
        score = jnp.where(sel, neg, score)
    work = jnp.where(jnp.broadcast_to(keep, shape3) > 0.0, biased, neg)
    eidx = grp * per_group + sub
    ids, gates = [], []
    for _ in range(TOP_K):
        m = jnp.max(jnp.max(work, axis=0, keepdims=True), axis=1, keepdims=True)
        cand = jnp.where(work == m, eidx, N_EXPERTS)
        im = jnp.min(jnp.min(cand, axis=0, keepdims=True), axis=1, keepdims=True)
        sel = eidx == im
        ids.append(im)
        gates.append(jnp.sum(jnp.sum(jnp.where(sel, aff, 0.0), axis=0, keepdims=True),
                             axis=1, keepdims=True))
        work = jnp.where(sel, neg, work)
    gate = jnp.concatenate(gates, axis=1)
    den = jnp.sum(gate, axis=1, keepdims=True)
    w_ref[...] = (ROUTED_SCALE * gate / den).reshape(TOP_K, tm)
    ids_ref[...] = jnp.concatenate(ids, axis=1).reshape(TOP_K, tm)


def _route(logits_t, bias, tm):
    ne, ntok = logits_t.shape
    bias_b = jnp.broadcast_to(bias.astype(F32)[:, None], (ne, tm))
    return pl.pallas_call(
        _route_kernel,
        out_shape=(jax.ShapeDtypeStruct((TOP_K, ntok), jnp.int32),
                   jax.ShapeDtypeStruct((TOP_K, ntok), F32)),
        grid=(ntok // tm,),
        in_specs=[pl.BlockSpec((ne, tm), lambda i: (0, i)),
                  pl.BlockSpec((ne, tm), lambda i: (0, 0))],
        out_specs=(pl.BlockSpec((TOP_K, tm), lambda i: (0, i)),
                   pl.BlockSpec((TOP_K, tm), lambda i: (0, i))),
        compiler_params=_params("parallel"),
        name="moe_route",
    )(logits_t, bias_b)


LIST_PAD = 1024


def _dispatch_plan(ids, w, t0, tile, ntiles):
    n = ntiles * tile
    e = ids[:, t0:t0 + n].T.reshape(ntiles, tile * TOP_K)
    g = w[:, t0:t0 + n].T.reshape(ntiles, tile * TOP_K)
    order = jnp.argsort(e, axis=1, stable=True).astype(jnp.int32)
    rows = (order // TOP_K) * CHUNKS_PER_ROW
    wts = jnp.take_along_axis(g, order, axis=1)
    experts = jnp.arange(N_EXPERTS, dtype=jnp.int32)
    counts = jnp.sum((e[:, :, None] == experts).astype(jnp.int32), axis=1)
    starts = jnp.concatenate([jnp.zeros((ntiles, 1), jnp.int32), jnp.cumsum(counts, axis=1)], axis=1)
    rows = jnp.pad(rows, ((0, 0), (0, LIST_PAD)))
    wts = jnp.pad(wts, ((0, 0), (0, LIST_PAD)))
    starts = jnp.pad(starts, ((0, 0), (0, LANE - N_EXPERTS - 1)))
    return rows, wts, starts


def _moe_kernel(rows_hbm, wts_hbm, starts_hbm, h_ref, wg_ref, wu_ref, wd_ref, sg_ref, su_ref,
                sd_ref, o_ref, rows_s, wts_s, starts_s, sem, gbuf, ybuf, *, epb):
    i = pl.program_id(0)
    j = pl.program_id(1)
    s_n = CHUNKS_PER_ROW
    blk = MOE_BLOCK
    tile = h_ref.shape[0] // s_n

    def swiglu(x, w_gate, w_up, w_down):
        g = jnp.dot(x, w_gate, preferred_element_type=F32)
        u = jnp.dot(x, w_up, preferred_element_type=F32)
        return jnp.dot((_silu(g) * u).astype(BF16), w_down, preferred_element_type=F32)

    @pl.when(j == 0)
    def _():
        copies = [pltpu.make_async_copy(rows_hbm.at[i], rows_s, sem.at[0]),
                  pltpu.make_async_copy(wts_hbm.at[i], wts_s, sem.at[1]),
                  pltpu.make_async_copy(starts_hbm.at[i], starts_s, sem.at[2])]
        for c in copies:
            c.start()
        for r0 in range(0, tile, ROW_TILE):
            x = _load_rows(h_ref, r0, ROW_TILE).astype(BF16)
            _store_rows(o_ref, r0, swiglu(x, sg_ref[...], su_ref[...], sd_ref[...]))
        for c in copies:
            c.wait()

    for e in range(epb):
        base = starts_s[j * epb + e]
        count = starts_s[j * epb + e + 1] - base
        w_gate = wg_ref[e].astype(BF16)
        w_up = wu_ref[e].astype(BF16)
        w_down = wd_ref[e].astype(BF16)

        def block(bi, carry, base=base, count=count, w_gate=w_gate, w_up=w_up, w_down=w_down):
            p0 = base + bi * blk
            left = count - bi * blk
            offs = [pl.multiple_of(rows_s[p0 + r], s_n) for r in range(blk)]
            for r in range(blk):
                gbuf[pl.ds(r * s_n, s_n), :] = h_ref[pl.ds(offs[r], s_n), :]
            x = _load_rows(gbuf, 0, blk).astype(BF16)
            y = swiglu(x, w_gate, w_up, w_down)
            y = jnp.where(lax.broadcasted_iota(jnp.int32, y.shape, 0) < left, y, 0.0)
            _store_rows(ybuf, 0, y)
            for r0 in range(0, blk, SUBLANE):
                vals = [o_ref[pl.ds(offs[r], s_n), :] + wts_s[p0 + r] * ybuf[pl.ds(r * s_n, s_n), :]
                        for r in range(r0, r0 + SUBLANE)]
                for r in reversed(range(r0, r0 + SUBLANE)):
                    o_ref[pl.ds(offs[r], s_n), :] = vals[r - r0]
            return carry

        nblk = lax.shift_right_logical(count + (blk - 1), blk.bit_length() - 1)
        lax.fori_loop(0, nblk, block, 0)


def _moe(h, plan, layer, wg, wu, wd, sg, su, sd, tile0, ntiles, tile, epb):
    rows, wts, starts = plan
    s_n = CHUNKS_PER_ROW
    _, ne, d, ff = wg.shape
    once = pl.Buffered(1)
    hbm = pl.BlockSpec(memory_space=pl.ANY)
    return pl.pallas_call(
        functools.partial(_moe_kernel, epb=epb),
        out_shape=jax.ShapeDtypeStruct((ntiles * tile * s_n, LANE), F32),
        grid=(ntiles, ne // epb),
        in_specs=[hbm, hbm, hbm,
                  pl.BlockSpec((tile * s_n, LANE), lambda i, j: (tile0 + i, 0), pipeline_mode=once),
                  pl.BlockSpec((None, epb, d, ff), lambda i, j: (layer, j, 0, 0)),
                  pl.BlockSpec((None, epb, d, ff), lambda i, j: (layer, j, 0, 0)),
                  pl.BlockSpec((None, epb, ff, d), lambda i, j: (layer, j, 0, 0)),
                  pl.BlockSpec(sg.shape, lambda i, j: (0, 0), pipeline_mode=once),
                  pl.BlockSpec(su.shape, lambda i, j: (0, 0), pipeline_mode=once),
                  pl.BlockSpec(sd.shape, lambda i, j: (0, 0), pipeline_mode=once)],
        out_specs=pl.BlockSpec((tile * s_n, LANE), lambda i, j: (i, 0)),
        scratch_shapes=[pltpu.SMEM((rows.shape[1],), jnp.int32),
                        pltpu.SMEM((wts.shape[1],), F32),
                        pltpu.SMEM((starts.shape[1],), jnp.int32),
                        pltpu.SemaphoreType.DMA((3,)),
                        pltpu.VMEM((MOE_BLOCK * s_n, LANE), F32),
                        pltpu.VMEM((MOE_BLOCK * s_n, LANE), F32)],
        compiler_params=_params("parallel", "arbitrary", vmem=VMEM_LIMIT_MOE),
        name="moe_experts",
    )(rows, wts, starts, h, wg, wu, wd, sg, su, sd)


def _moe_finish_kernel(f_ref, x_ref, tab_ref, pg_ref, o_ref):
    f = _load_rows(f_ref, 0, x_ref.shape[0])
    o_ref[...] = _gated_residual(x_ref[...], _rms(f, pg_ref[...]), tab_ref, 5)


def _moe_finish(f, x, tab, post_gain, tm, x_tile0, tab0):
    d = x.shape[1]
    s_n = CHUNKS_PER_ROW
    nsub = tm // ROW_TILE
    ntiles = f.shape[0] // (tm * s_n)
    return pl.pallas_call(
        _moe_finish_kernel,
        out_shape=jax.ShapeDtypeStruct((ntiles * tm, d), F32),
        grid=(ntiles,),
        in_specs=[pl.BlockSpec((tm * s_n, LANE), lambda i: (i, 0)),
                  pl.BlockSpec((tm, d), lambda i: (x_tile0 + i, 0)),
                  pl.BlockSpec((nsub, 8, d), lambda i: (tab0 + i, 0, 0)),
                  pl.BlockSpec((1, d), lambda i: (0, 0))],
        out_specs=pl.BlockSpec((tm, d), lambda i: (i, 0)),
        compiler_params=_params("parallel"),
        name="moe_norm_residual",
    )(f, x, tab, post_gain)


def _grid_permutation(band):
    tm = band * GRID_W
    dst = jnp.arange(tm)
    src = (dst % band) * GRID_W + dst // band
    return (src[:, None] == jnp.arange(tm)[None, :]).astype(BF16)


def kernel(x, c, ctx, c_ctx, mod_w, mod_b, pre_gain, post_gain, ab_w_in, ab_conv_w, ab_conv_b, ab_lru_wa, ab_lru_ba, ab_lru_wi, ab_lru_bi, ab_lru_lambda, ab_gla_wa2, ab_gla_ba, ab_w_out, c_w_in, hgrn_lb_logits, c_w_out, moe_router, moe_bias, moe_w_gate, moe_w_up, moe_w_down, moe_ws_gate, moe_ws_up, moe_ws_down):
    batch, seq, d = x.shape
    n_ctx = ctx.shape[1]
    depth = mod_w.shape[0]
    band = SUBLANE
    tm_grid = band * GRID_W
    rows_img = seq // GRID_W
    assert n_ctx == ROW_TILE and seq % (2 * ROW_TILE) == 0 and d % LANE == 0
    assert depth == 2 and batch + 1 <= 8 and rows_img % band == 0 and tm_grid % ROW_TILE == 0
    n_lat = batch * seq
    n_ctx_tok = batch * n_ctx
    ntok = n_lat + n_ctx_tok

    xs = jnp.concatenate([x.reshape(n_lat, d), ctx.reshape(n_ctx_tok, d)], axis=0)

    cvec = jnp.concatenate([c, c_ctx[None, :], jnp.zeros((8 - batch - 1, d), F32)], axis=0)
    mods = _modulation(cvec, mod_w, mod_b).reshape(depth, 8, N_MOD, d)
    tile_row = jnp.concatenate([jnp.repeat(jnp.arange(batch), seq // ROW_TILE),
                                jnp.full((batch,), batch)]).astype(jnp.int32)
    tabs = jnp.pad(mods[:, tile_row], ((0, 0), (0, 0), (0, 8 - N_MOD), (0, 0)))

    def moe(layer, h2, routed, xs_in, t0, ntiles, tile):
        plan = _dispatch_plan(*routed, t0, tile, ntiles)
        f = _moe(h2, plan, layer, moe_w_gate, moe_w_up, moe_w_down,
                 moe_ws_gate[layer].astype(BF16), moe_ws_up[layer].astype(BF16),
                 moe_ws_down[layer].astype(BF16), t0 // tile, ntiles, tile, epb)
        return _moe_finish(f, xs_in, tabs[layer], post_gain[layer, 1][None, :], tm_out,
                           t0 // tm_out, t0 // tm_out)

    tm_proj, tm_out, tm_route, tm_moe, epb = 512, 512, 512, 2048, 2
    assert n_lat % tm_moe == 0 and n_lat % n_ctx_tok == 0

    ab_cols = ab_w_in.shape[2]
    gla_dk = ab_gla_wa2.shape[3]
    lru_w = d
    gla_dv = (ab_cols - 2 * lru_w - 2 * gla_dk - 2 * GLA_RANK) // 2
    tn0 = 1792
    n0 = -(-(ab_cols) // tn0) * tn0
    w_in0 = jnp.pad(ab_w_in[0], ((0, 0), (0, n0 - ab_cols))).astype(BF16)
    u0 = _in_proj(xs, tabs[0], pre_gain[0, 0][None, :], w_in0, tm_proj, tn0, 0)

    cw = jnp.pad(ab_conv_w[0], ((0, 8 - CONV_W), (0, 0)))
    cb = ab_conv_b[0][None, :]
    wg = jnp.concatenate([ab_lru_wa[0], ab_lru_wi[0]], axis=-1).astype(BF16)
    bg = jnp.concatenate([ab_lru_ba[0].reshape(2, LRU_HEADS, 1, LANE),
                          ab_lru_bi[0].reshape(2, LRU_HEADS, 1, LANE)], axis=-1)
    ya_c, ya_l = _lru(u0, (cw, cb, wg, bg, ab_lru_lambda[0]), batch=batch, seq=seq, n_ctx=n_ctx,
                      g_col0=0, x_col0=lru_w // LANE)

    q0 = 2 * lru_w // LANE
    k0 = q0 + gla_dk // LANE
    v0 = k0 + gla_dk // LANE
    gate0 = v0 + gla_dv // LANE
    a0 = gate0 + gla_dv // LANE
    wa = jnp.stack([jnp.zeros((LANE, gla_dk), F32).at[dr * GLA_RANK:(dr + 1) * GLA_RANK]
                    .set(ab_gla_wa2[0, dr]) for dr in range(2)]).astype(BF16)
    ba = ab_gla_ba[0][:, None, :]
    o_gla = _gla(u0, wa, ba, batch=batch, seq=seq, n_ctx=n_ctx, cols=(q0, k0, v0, a0),
                 dv_total=gla_dv)

    lat_tiles = n_lat // tm_out
    ctx_tiles = n_ctx_tok // tm_out
    lead_specs = [pl.BlockSpec((tm_out, lru_w), lambda i: (jnp.minimum(i, lat_tiles - 1), 0)),
                  pl.BlockSpec((tm_out, lru_w),
                               lambda i: (jnp.clip(i - lat_tiles, 0, ctx_tiles - 1), 0)),
                  pl.BlockSpec((ROW_TILE, ROW_TILE), lambda i: (0, 0)),
                  pl.BlockSpec((None, tm_out, gla_dv), lambda i: (0, i, 0)),
                  pl.BlockSpec((None, tm_out, gla_dv), lambda i: (1, i, 0)),
                  pl.BlockSpec((tm_out, gla_dv), lambda i: (i, gate0 * LANE // gla_dv))]
    xs, h2, lg = _out_proj(functools.partial(_out_proj_ab_kernel, lat_tiles=lat_tiles),
                           (ya_l, ya_c, _time_permutation().T.astype(BF16), o_gla, o_gla, u0),
                           lead_specs,
                           ab_w_out[0].astype(BF16), xs, tabs[0], post_gain[0, 0][None, :],
                           pre_gain[0, 1][None, :], moe_router[0].T, tm_out, ntok // tm_out)
    cwt = _route(lg, moe_bias[0], tm_route)
    xs_ctx = moe(0, h2, cwt, xs, n_lat, 1, n_ctx_tok)
    xs = moe(0, h2, cwt, xs, 0, n_lat // tm_moe, tm_moe)

    hd = c_w_out.shape[1]
    tn1 = 1280
    w_in1 = c_w_in[0].astype(BF16)
    c_cols = w_in1.shape[1]
    nsec = c_cols // hd
    perm = _grid_permutation(band)
    gain1 = pre_gain[1, 0][None, :]
    u1l = _in_proj_grid(xs, tabs[1], gain1, perm, w_in1, tn1, n_lat // tm_grid, band,
                        rows_img // band)
    u1c = _in_proj(xs_ctx, tabs[1], gain1, w_in1, tm_proj, tn1, n_lat // tm_proj)
    lb_sm = jax.nn.softmax(hgrn_lb_logits.astype(F32), axis=0)
    lb_cum = jnp.cumsum(lb_sm, axis=0)
    lb = (lb_cum[1] - lb_cum[0])[None, :]
    ub = 1.0 - lb
    dh = hd // HGRN_HEADS
    s_zero = jnp.zeros((2, batch, HGRN_HEADS, dh, dh), F32)
    u1l_flat = u1l.reshape(n_lat, c_cols)
    (s_c,) = _hgrn_segment(u1c, s_zero, lb, ub, batch=batch, seg_len=n_ctx, emit_o=False)
    o_hgrn, _ = _hgrn_segment(u1l_flat, s_c, lb, ub, batch=batch, seg_len=seq, emit_o=True)
    o_hgrn = o_hgrn.reshape(2, n_lat // rows_img, rows_img // band, band, hd)

    bpi = rows_img // band
    grid_blk = (GRID_W, None, band, hd)
    lead_specs = [pl.BlockSpec((None,) + grid_blk, lambda i: (0, i // bpi, i % bpi, 0, 0)),
                  pl.BlockSpec((None,) + grid_blk, lambda i: (1, i // bpi, i % bpi, 0, 0)),
                  pl.BlockSpec(grid_blk, lambda i: (i // bpi, i % bpi, 0, nsec - 1)),
                  pl.BlockSpec((tm_grid, tm_grid), lambda i: (0, 0))]
    xs, h2, lg = _out_proj(_out_proj_c_kernel, (o_hgrn, o_hgrn, u1l, perm.T), lead_specs,
                           c_w_out[0].astype(BF16), xs, tabs[1], post_gain[1, 0][None, :],
                           pre_gain[1, 1][None, :], moe_router[1].T, tm_grid, n_lat // tm_grid)
    out = moe(1, h2, _route(lg, moe_bias[1], tm_route), xs, 0, n_lat // tm_moe, tm_moe)
    return out.reshape(batch, seq, d)
```

```python
import functools

import jax
import jax.numpy as jnp
from jax import lax
from jax.experimental import pallas as pl
from jax.experimental.pallas import tpu as pltpu

F32 = jnp.float32
BF16 = jnp.bfloat16

EPS = 1e-6
N_MOD = 6
ROW_TILE = 256
CHUNK = 64
CHUNK_SHIFT = 6
GRID_W = 64
LRU_HEADS = 8
LRU_C = 8.0
CONV_W = 4
GLA_HEADS = 4
GLA_RANK = 16
GLA_TAU = 16.0
HGRN_HEADS = 8
N_EXPERTS = 64
TOP_K = 8
N_GROUPS = 8
TOPK_GROUPS = 4
ROUTED_SCALE = 2.5
LANE = 128
SUBLANE = 8
BF16_ROWS = 16
CHUNKS_PER_ROW = 8
MOE_BLOCK = 128
VMEM_LIMIT = 48 * 1024 * 1024
VMEM_LIMIT_MOE = 58 * 1024 * 1024

_NT = (((1,), (1,)), ((), ()))
_TN = (((0,), (0,)), ((), ()))


def _params(*sem, vmem=VMEM_LIMIT):
    return pltpu.CompilerParams(dimension_semantics=sem, vmem_limit_bytes=vmem)


def _rms(x, gain):
    ms = jnp.mean(x * x, axis=-1, keepdims=True)
    return x * lax.rsqrt(ms + EPS) * gain


def _sigmoid(x):
    return jax.nn.sigmoid(x)


def _silu(x):
    return x * _sigmoid(x)


def _log_sigmoid(x):
    return jnp.minimum(x, 0.0) - jnp.log1p(jnp.exp(-jnp.abs(x)))


def _softplus(x):
    return jnp.maximum(x, 0.0) + jnp.log1p(jnp.exp(-jnp.abs(x)))


def _modulate(xn, tab_ref, shift_row, scale_row):
    parts = []
    for s in range(tab_ref.shape[0]):
        t = tab_ref[s]
        rows = xn[s * ROW_TILE:(s + 1) * ROW_TILE]
        parts.append(rows * (1.0 + t[scale_row:scale_row + 1]) + t[shift_row:shift_row + 1])
    return parts[0] if len(parts) == 1 else jnp.concatenate(parts, axis=0)


def _gated_residual(x, branch, tab_ref, gate_row):
    parts = []
    for s in range(tab_ref.shape[0]):
        sl = slice(s * ROW_TILE, (s + 1) * ROW_TILE)
        parts.append(x[sl] + tab_ref[s][gate_row:gate_row + 1] * branch[sl])
    return parts[0] if len(parts) == 1 else jnp.concatenate(parts, axis=0)


def _mod_kernel(c_ref, w_ref, b_ref, o_ref):
    s = _silu(c_ref[...])
    o_ref[0] = jnp.dot(s, w_ref[0], precision=lax.Precision.HIGHEST,
                       preferred_element_type=F32) + b_ref[0]


def _modulation(cvec, mod_w, mod_b):
    depth, d, n = mod_w.shape
    tn = 1024
    return pl.pallas_call(
        _mod_kernel,
        out_shape=jax.ShapeDtypeStruct((depth, 8, n), F32),
        grid=(depth, n // tn),
        in_specs=[pl.BlockSpec((8, d), lambda l, j: (0, 0)),
                  pl.BlockSpec((1, d, tn), lambda l, j: (l, 0, j)),
                  pl.BlockSpec((1, 1, tn), lambda l, j: (l, 0, j))],
        out_specs=pl.BlockSpec((1, 8, tn), lambda l, j: (l, 0, j)),
        compiler_params=_params("parallel", "parallel"),
        name="adaln_modulation",
    )(cvec, mod_w, mod_b.reshape(depth, 1, n))


def _in_proj_kernel(x_ref, tab_ref, gain_ref, w_ref, o_ref, *, tn):
    h = _modulate(_rms(x_ref[...], gain_ref[...]), tab_ref, 0, 1).astype(BF16)
    for j in range(w_ref.shape[1] // tn):
        cs = slice(j * tn, (j + 1) * tn)
        o_ref[:, cs] = jnp.dot(h, w_ref[:, cs], preferred_element_type=F32).astype(o_ref.dtype)


def _in_proj(x, tab, gain, w, tm, tn, tab0):
    d = x.shape[1]
    n = w.shape[1]
    nsub = tm // ROW_TILE
    ntiles = x.shape[0] // tm
    return pl.pallas_call(
        functools.partial(_in_proj_kernel, tn=tn),
        out_shape=jax.ShapeDtypeStruct((ntiles * tm, n), BF16),
        grid=(ntiles,),
        in_specs=[pl.BlockSpec((tm, d), lambda i: (i, 0)),
                  pl.BlockSpec((nsub, 8, d), lambda i: (tab0 + i, 0, 0)),
                  pl.BlockSpec((1, d), lambda i: (0, 0)),
                  pl.BlockSpec((d, n), lambda i: (0, 0), pipeline_mode=pl.Buffered(1))],
        out_specs=pl.BlockSpec((tm, n), lambda i: (i, 0)),
        compiler_params=_params("parallel"),
        name="norm_mod_in_proj",
    )(x, tab, gain, w)


def _in_proj_grid_kernel(x_ref, tab_ref, gain_ref, p_ref, w_ref, o_ref, *, tn):
    h = _modulate(_rms(x_ref[...], gain_ref[...]), tab_ref, 0, 1).astype(BF16)
    hp = jnp.dot(p_ref[...], h, preferred_element_type=F32).astype(BF16)
    for j in range(w_ref.shape[1] // tn):
        cs = slice(j * tn, (j + 1) * tn)
        o = jnp.dot(hp, w_ref[:, cs], preferred_element_type=F32).astype(o_ref.dtype)
        o_ref[:, :, cs] = o.reshape(o_ref.shape[:2] + (tn,))


def _in_proj_grid(x, tab, gain, perm, w, tn, ntiles, band, bands_per_image):
    d = x.shape[1]
    n = w.shape[1]
    tm = band * GRID_W
    nsub = tm // ROW_TILE
    images = ntiles // bands_per_image
    once = pl.Buffered(1)
    return pl.pallas_call(
        functools.partial(_in_proj_grid_kernel, tn=tn),
        out_shape=jax.ShapeDtypeStruct((images * GRID_W, bands_per_image, band, n), F32),
        grid=(ntiles,),
        in_specs=[pl.BlockSpec((tm, d), lambda i: (i, 0)),
                  pl.BlockSpec((nsub, 8, d), lambda i: (i, 0, 0)),
                  pl.BlockSpec((1, d), lambda i: (0, 0)),
                  pl.BlockSpec((tm, tm), lambda i: (0, 0), pipeline_mode=once),
                  pl.BlockSpec((d, n), lambda i: (0, 0), pipeline_mode=once)],
        out_specs=pl.BlockSpec((GRID_W, None, band, n),
                               lambda i: (i // bands_per_image, i % bands_per_image, 0, 0)),
        compiler_params=_params("parallel"),
        name="norm_mod_in_proj_grid",
    )(x, tab, gain, perm, w)


def _lru_kernel(gc_ref, xc_ref, gl_ref, xl_ref, p_ref, cw_ref, cb_ref, wg_ref, bg_ref,
                lam_ref, yc_ref, yl_ref, xs_scr, hf_scr):
    T = ROW_TILE
    V = T // SUBLANE
    H = BF16_ROWS
    sub = lax.broadcasted_iota(jnp.int32, (SUBLANE, LANE), 0)
    cw = cw_ref[...]
    cb = cb_ref[...]
    sp = _softplus(-lam_ref[...])

    def vreg(a, v):
        return a[v * SUBLANE:(v + 1) * SUBLANE]

    def shift_down(y, first):
        return jnp.where(sub == 0, first, pltpu.roll(y, 1, 0))

    def shift_up(y, last):
        return jnp.where(sub == SUBLANE - 1, last, pltpu.roll(y, SUBLANE - 1, 0))

    def conv_tile(x_ref, t, nt, seg_len):
        t0 = pl.multiple_of(t * T, T)
        xp = jnp.dot(p_ref[...], x_ref[pl.ds(t0, T), :], preferred_element_type=F32)
        p0 = pl.multiple_of(jnp.maximum(t0 - H, 0), H)
        n0 = pl.multiple_of(jnp.minimum(t0 + T, seg_len - H), H)
        prev = x_ref[pl.ds(p0, H), :].astype(F32) * jnp.where(t > 0, 1.0, 0.0)
        nxt = x_ref[pl.ds(n0, H), :].astype(F32) * jnp.where(t < nt - 1, 1.0, 0.0)
        m1, m2, p1 = prev[H - 1:H], prev[H - 2:H - 1], nxt[0:1]
        last_m1 = shift_down(vreg(xp, V - 1), m1)
        xm1 = jnp.concatenate([last_m1, xp[:T - SUBLANE]], axis=0)
        xm2 = jnp.concatenate([shift_down(vreg(xp, V - 2), m2), last_m1, xp[:T - 2 * SUBLANE]], axis=0)
        xp1 = jnp.concatenate([xp[SUBLANE:], shift_up(vreg(xp, 0), p1)], axis=0)
        return cb + xm2 * cw[0:1] + xm1 * cw[1:2] + xp * cw[2:3] + xp1 * cw[3:4]

    def gates(xc, d):
        z = jnp.dot(xc.astype(BF16), wg_ref[d, 0], preferred_element_type=F32) + bg_ref[d, 0]
        r = _sigmoid(z[:, :LANE])
        i = _sigmoid(z[:, LANE:])
        log_a = (-LRU_C) * r * sp[d:d + 1]
        a = jnp.exp(log_a)
        return a, jnp.sqrt(1.0 - a * a) * (i * xc)

    def scan_tile(a, u, carry, reverse):
        hs, cum = [None] * V, [None] * V
        hp = ap = None
        for v in (range(V - 1, -1, -1) if reverse else range(V)):
            av, uv = vreg(a, v), vreg(u, v)
            hp, ap = (uv, av) if hp is None else (av * hp + uv, av * ap)
            hs[v], cum[v] = hp, ap
        s = 1
        while s < SUBLANE:
            keep = (sub < SUBLANE - s) if reverse else (sub >= s)
            sh = SUBLANE - s if reverse else s
            h_sh = jnp.where(keep, pltpu.roll(hp, sh, 0), 0.0)
            a_sh = jnp.where(keep, pltpu.roll(ap, sh, 0), 1.0)
            hp = hp + ap * h_sh
            ap = ap * a_sh
            s *= 2
        state = hp + ap * carry
        if reverse:
            start, out = shift_up(state, carry), state[0:1]
        else:
            start, out = shift_down(state, carry), state[SUBLANE - 1:SUBLANE]
        h = jnp.concatenate([hs[v] + cum[v] * start for v in range(V)], axis=0)
        return h, out

    def run_segment(g_ref, x_ref, y_ref, seg_len, carry_f, carry_b):
        nt = seg_len // T
        unroll = 4 if nt % 4 == 0 else (2 if nt % 2 == 0 else 1)

        def fwd_body(i, carry):
            for j in range(unroll):
                t = i * unroll + j
                t0 = pl.multiple_of(t * T, T)
                xc = conv_tile(x_ref, t, nt, seg_len)
                xs_scr[pl.ds(t0, T), :] = xc
                a, u = gates(xc, 0)
                h, carry = scan_tile(a, u, carry, False)
                hf_scr[pl.ds(t0, T), :] = h
            return carry

        def bwd_body(i, carry):
            for j in range(unroll):
                t = nt - 1 - (i * unroll + j)
                t0 = pl.multiple_of(t * T, T)
                a, u = gates(xs_scr[pl.ds(t0, T), :], 1)
                h, carry = scan_tile(a, u, carry, True)
                g = jnp.dot(p_ref[...], g_ref[pl.ds(t0, T), :], preferred_element_type=F32)
                rec = hf_scr[pl.ds(t0, T), :] + h
                y_ref[pl.ds(t0, T), :] = (jax.nn.gelu(g) * rec).astype(y_ref.dtype)
            return carry

        steps = nt // unroll
        return (lax.fori_loop(0, steps, fwd_body, carry_f),
                lax.fori_loop(0, steps, bwd_body, carry_b))

    zero = jnp.zeros((1, LANE), F32)
    hf, hb = run_segment(gc_ref, xc_ref, yc_ref, xc_ref.shape[0], zero, zero)
    run_segment(gl_ref, xl_ref, yl_ref, xl_ref.shape[0], hf, hb)


def _time_permutation():
    v = ROW_TILE // SUBLANE
    dst = jnp.arange(ROW_TILE)
    src = (dst % SUBLANE) * v + dst // SUBLANE
    return src[:, None] == jnp.arange(ROW_TILE)[None, :]


def _lru(u, lru_w, *, batch, seq, n_ctx, g_col0, x_col0):
    width = LRU_HEADS * LANE
    cw, cb, wg, bg, lam = lru_w
    ctx_blk0 = batch * seq // n_ctx
    perm = _time_permutation()
    once = pl.Buffered(1)
    return pl.pallas_call(
        _lru_kernel,
        out_shape=(jax.ShapeDtypeStruct((batch * n_ctx, width), BF16),
                   jax.ShapeDtypeStruct((batch * seq, width), BF16)),
        grid=(batch, LRU_HEADS),
        in_specs=[pl.BlockSpec((n_ctx, LANE), lambda b, h: (ctx_blk0 + b, g_col0 + h)),
                  pl.BlockSpec((n_ctx, LANE), lambda b, h: (ctx_blk0 + b, x_col0 + h)),
                  pl.BlockSpec((seq, LANE), lambda b, h: (b, g_col0 + h)),
                  pl.BlockSpec((seq, LANE), lambda b, h: (b, x_col0 + h)),
                  pl.BlockSpec((ROW_TILE, ROW_TILE), lambda b, h: (0, 0), pipeline_mode=once),
                  pl.BlockSpec((8, LANE), lambda b, h: (0, h)),
                  pl.BlockSpec((1, LANE), lambda b, h: (0, h)),
                  pl.BlockSpec((2, 1, LANE, 2 * LANE), lambda b, h: (0, h, 0, 0)),
                  pl.BlockSpec((2, 1, 1, 2 * LANE), lambda b, h: (0, h, 0, 0)),
                  pl.BlockSpec((2, LANE), lambda b, h: (0, h))],
        out_specs=(pl.BlockSpec((n_ctx, LANE), lambda b, h: (b, h)),
                   pl.BlockSpec((seq, LANE), lambda b, h: (b, h))),
        scratch_shapes=[pltpu.VMEM((seq, LANE), F32), pltpu.VMEM((seq, LANE), F32)],
        compiler_params=_params("parallel", "parallel"),
        name="rglru",
    )(u, u, u, u, perm.astype(BF16), cw, cb, wg, bg, lam)


def _chunk_mask(rows, reverse):
    r = lax.broadcasted_iota(jnp.int32, (rows, rows), 0)
    c = lax.broadcasted_iota(jnp.int32, (rows, rows), 1)
    same = (r >> CHUNK_SHIFT) == (c >> CHUNK_SHIFT)
    tri = (c >= r) if reverse else (c <= r)
    return jnp.where(same, jnp.where(tri, 1.0, 0.0), 0.0)


def _split3(x):
    hi = x.astype(BF16)
    r1 = x - hi.astype(F32)
    mid = r1.astype(BF16)
    lo = (r1 - mid.astype(F32)).astype(BF16)
    return hi, mid, lo


def _block_scan(g_all, head_qkv, st_scr, o_ref, *, heads, dv, reverse, batched):
    rows, wall = g_all.shape
    width = wall // heads
    nchunk = rows // CHUNK
    mask = _chunk_mask(rows, reverse)
    mask_b = mask.astype(BF16)
    keep = mask > 0.5
    keep_c = keep[:CHUNK, :CHUNK]
    iref = CHUNK // 2 if reverse else CHUNK // 2 - 1
    ilast = 0 if reverse else CHUNK - 1
    order = range(nchunk - 1, -1, -1) if reverse else range(nchunk)

    def per_chunk(rows_1, width):
        return jnp.concatenate([jnp.broadcast_to(r, (CHUNK, width)) for r in rows_1], axis=0)

    pieces = jnp.concatenate(_split3(g_all), axis=1)
    sums = jnp.dot(mask_b, pieces, preferred_element_type=F32)
    b_all = sums[:, :wall] + sums[:, wall:2 * wall] + sums[:, 2 * wall:]

    def prepare(h):
        q, k, v = head_qkv(h)
        b = b_all[:, h * width:(h + 1) * width]
        b_ref = [b[c * CHUNK + iref:c * CHUNK + iref + 1] for c in range(nchunk)]
        b_last = [b[c * CHUNK + ilast:c * CHUNK + ilast + 1] for c in range(nchunk)]
        e1 = jnp.exp(b - per_chunk(b_ref, width))
        r1 = 1.0 / e1
        qe = q * e1
        ke = k * r1
        qd = (qe * per_chunk([jnp.exp(r) for r in b_ref], width)).astype(BF16)
        kd = (ke * per_chunk([jnp.exp(l - r) for l, r in zip(b_last, b_ref)], width)).astype(BF16)
        decays = [jnp.exp(l) for l in b_last]
        return qe.astype(BF16), ke.astype(BF16), qd, kd, v.astype(BF16), decays

    nxt = prepare(0)
    for h in range(heads):
        qe, ke, qd, kd, vb, decays = nxt
        if h + 1 < heads:
            nxt = prepare(h + 1)
        if batched:
            sc = lax.dot_general(qe, ke, _NT, preferred_element_type=F32)
            sc = jnp.where(keep, sc, 0.0).astype(BF16)
            o = jnp.dot(sc, vb, preferred_element_type=F32)
        st = st_scr[h]
        parts = [None] * nchunk
        for ci in order:
            rs = slice(ci * CHUNK, (ci + 1) * CHUNK)
            if batched:
                o_c = o[rs]
            else:
                sc = lax.dot_general(qe[rs], ke[rs], _NT, preferred_element_type=F32)
                sc = jnp.where(keep_c, sc, 0.0).astype(BF16)
                o_c = jnp.dot(sc, vb[rs], preferred_element_type=F32)
            parts[ci] = o_c + lax.dot_general(qd[rs], st.astype(BF16), _NT,
                                              preferred_element_type=F32)
            st = st * decays[ci] + lax.dot_general(vb[rs], kd[rs], _TN, preferred_element_type=F32)
        if o_ref is not None:
            o_ref[:, h * dv:(h + 1) * dv] = jnp.concatenate(parts, axis=0).astype(o_ref.dtype)
        st_scr[h] = st


def _gla_kernel(q_ref, k_ref, v_ref, a_ref, wa_ref, ba_ref, o_ref, st_scr):
    t = pl.program_id(1)
    d = pl.program_id(2)
    dk = q_ref.shape[1] // GLA_HEADS
    dv = v_ref.shape[1] // GLA_HEADS

    @pl.when((t == 0) & (d == 0))
    def _():
        st_scr[...] = jnp.zeros(st_scr.shape, F32)

    def head_qkv(h):
        ks = slice(h * dk, (h + 1) * dk)
        q = q_ref[:, ks].astype(F32) * (dk ** -0.5)
        return q, k_ref[:, ks].astype(F32), v_ref[:, h * dv:(h + 1) * dv]

    for dd in range(2):
        @pl.when(d == dd)
        def _(dd=dd):
            z = jnp.dot(a_ref[...], wa_ref[dd], preferred_element_type=F32) + ba_ref[dd]
            g_all = _log_sigmoid(z) * (1.0 / GLA_TAU)
            _block_scan(g_all, head_qkv, st_scr.at[dd], o_ref,
                        heads=GLA_HEADS, dv=dv, reverse=dd == 1, batched=True)


def _gla(u, wa, ba, *, batch, seq, n_ctx, cols, dv_total):
    ntok = u.shape[0]
    q0, k0, v0, a0 = cols
    dkt = wa.shape[2]
    dvt = dv_total
    nblk = seq // ROW_TILE
    ctx_blk0 = batch * nblk
    assert n_ctx == ROW_TILE

    def blk(b, t, d):
        lat = b * nblk + jnp.where(d == 0, t - 1, nblk - t)
        return jnp.where(t == 0, ctx_blk0 + b, lat)

    in_specs = [pl.BlockSpec((ROW_TILE, dkt), lambda b, t, d: (blk(b, t, d), q0 * LANE // dkt)),
                pl.BlockSpec((ROW_TILE, dkt), lambda b, t, d: (blk(b, t, d), k0 * LANE // dkt)),
                pl.BlockSpec((ROW_TILE, dvt), lambda b, t, d: (blk(b, t, d), v0 * LANE // dvt)),
                pl.BlockSpec((ROW_TILE, LANE), lambda b, t, d: (blk(b, t, d), a0)),
                pl.BlockSpec(wa.shape, lambda b, t, d: (0, 0, 0)),
                pl.BlockSpec(ba.shape, lambda b, t, d: (0, 0, 0))]
    return pl.pallas_call(
        _gla_kernel,
        out_shape=jax.ShapeDtypeStruct((2, ntok, dvt), BF16),
        grid=(batch, nblk + 1, 2),
        in_specs=in_specs,
        out_specs=pl.BlockSpec((None, ROW_TILE, dvt), lambda b, t, d: (d, blk(b, t, d), 0)),
        scratch_shapes=[pltpu.VMEM((2, GLA_HEADS, dvt // GLA_HEADS, dkt // GLA_HEADS), F32)],
        compiler_params=_params("parallel", "arbitrary", "arbitrary"),
        name="gla",
    )(u, u, u, u, wa, ba)


def _hgrn_kernel(*refs, emit_o):
    q_ref, z_ref, v_ref, lb_ref, ub_ref, s0_ref = refs[:6]
    if emit_o:
        o_ref, st_ref, st_scr = refs[6:]
    else:
        st_ref, st_scr = refs[6:]
        o_ref = None
    t = pl.program_id(1)
    d = pl.program_id(2)
    dh = q_ref.shape[1] // HGRN_HEADS

    @pl.when((t == 0) & (d == 0))
    def _():
        st_scr[...] = s0_ref[:, 0]

    lb = lb_ref[...]
    ub = ub_ref[...]

    for dd in range(2):
        @pl.when(d == dd)
        def _(dd=dd):
            z = z_ref[...].astype(F32)
            e = jnp.exp(-jnp.abs(z))
            r = 1.0 / (1.0 + e)
            er = e * r
            pos = z >= 0.0
            log_f = jnp.log(lb + ub * jnp.where(pos, r, er))
            k_all = ub * jnp.where(pos, er, r)

            def head_qkv(h):
                hs = slice(h * dh, (h + 1) * dh)
                return q_ref[:, hs].astype(F32), k_all[:, hs], v_ref[:, hs]

            _block_scan(log_f, head_qkv, st_scr.at[dd], o_ref,
                        heads=HGRN_HEADS, dv=dh, reverse=dd == 1, batched=True)

    @pl.when((t == pl.num_programs(1) - 1) & (d == 1))
    def _():
        st_ref[:, 0] = st_scr[...]


def _hgrn_segment(u, s0, lb, ub, *, batch, seg_len, emit_o):
    ntok = u.shape[0]
    width = lb.shape[1]
    nblk = seg_len // ROW_TILE

    def blk(b, t, d):
        return b * nblk + jnp.where(d == 0, t, nblk - 1 - t)

    st_spec = pl.BlockSpec((2, 1) + s0.shape[2:], lambda b, t, d: (0, b, 0, 0, 0))
    in_specs = [pl.BlockSpec((ROW_TILE, width), lambda b, t, d: (blk(b, t, d), 0)),
                pl.BlockSpec((ROW_TILE, width), lambda b, t, d: (blk(b, t, d), 1 + d)),
                pl.BlockSpec((ROW_TILE, width), lambda b, t, d: (blk(b, t, d), 3)),
                pl.BlockSpec((1, width), lambda b, t, d: (0, 0)),
                pl.BlockSpec((1, width), lambda b, t, d: (0, 0)),
                st_spec]
    st_shape = jax.ShapeDtypeStruct(s0.shape, F32)
    if emit_o:
        out_shape = (jax.ShapeDtypeStruct((2, ntok, width), F32), st_shape)
        out_specs = (pl.BlockSpec((None, ROW_TILE, width), lambda b, t, d: (d, blk(b, t, d), 0)),
                     st_spec)
    else:
        out_shape = (st_shape,)
        out_specs = (st_spec,)
    return pl.pallas_call(
        functools.partial(_hgrn_kernel, emit_o=emit_o),
        out_shape=out_shape,
        grid=(batch, nblk, 2),
        in_specs=in_specs,
        out_specs=out_specs,
        scratch_shapes=[pltpu.VMEM((2,) + s0.shape[2:], F32)],
        compiler_params=_params("parallel", "arbitrary", "arbitrary"),
        name="hgrn_segment",
    )(u, u, u, lb, ub, s0)


def _head_norm_gate(o, gate, heads):
    dh = o.shape[1] // heads
    parts = []
    for h in range(heads):
        oh = o[:, h * dh:(h + 1) * dh]
        ms = jnp.mean(oh * oh, axis=-1, keepdims=True)
        parts.append((oh * lax.rsqrt(ms + EPS) * _silu(gate[:, h * dh:(h + 1) * dh])).astype(BF16))
    return jnp.concatenate(parts, axis=1)


def _load_rows(ref, r0, nrows):
    s_n = CHUNKS_PER_ROW
    return jnp.concatenate([ref[pl.ds(r0 * s_n + s, nrows, stride=s_n), :] for s in range(s_n)],
                           axis=1)


def _store_rows(ref, r0, val):
    s_n = CHUNKS_PER_ROW
    for s in range(s_n):
        ref[pl.ds(r0 * s_n + s, val.shape[0], stride=s_n), :] = val[:, s * LANE:(s + 1) * LANE]


def _out_proj_tail(o, x_ref, tab_ref, pg_ref, ng_ref, rt_ref, xo_ref, h_ref, lg_ref):
    xn = _gated_residual(x_ref[...], _rms(o, pg_ref[...]), tab_ref, 2)
    xo_ref[...] = xn
    h2 = _modulate(_rms(xn, ng_ref[...]), tab_ref, 3, 4)
    _store_rows(h_ref, 0, h2)
    lg_ref[...] = lax.dot_general(rt_ref[...], h2, _NT, precision=lax.Precision.HIGHEST,
                                  preferred_element_type=F32)


def _out_proj_ab_kernel(yl_ref, yc_ref, pt_ref, of_ref, ob_ref, gate_ref, w_ref, x_ref, tab_ref,
                        pg_ref, ng_ref, rt_ref, xo_ref, h_ref, lg_ref, *, lat_tiles):
    o_att = of_ref[...].astype(F32) + ob_ref[...].astype(F32)
    yb = _head_norm_gate(o_att, gate_ref[...].astype(F32), GLA_HEADS)
    ya = jnp.where(pl.program_id(0) < lat_tiles, yl_ref[...], yc_ref[...])
    ya = jnp.concatenate(
        [jnp.dot(pt_ref[...], ya[s:s + ROW_TILE], preferred_element_type=F32).astype(BF16)
         for s in range(0, ya.shape[0], ROW_TILE)], axis=0)
    da = ya.shape[1]
    o = jnp.dot(ya, w_ref[:da], preferred_element_type=F32)
    o = o + jnp.dot(yb, w_ref[da:], preferred_element_type=F32)
    _out_proj_tail(o, x_ref, tab_ref, pg_ref, ng_ref, rt_ref, xo_ref, h_ref, lg_ref)


def _out_proj_c_kernel(of_ref, ob_ref, gate_ref, p_ref, w_ref, x_ref, tab_ref, pg_ref, ng_ref,
                       rt_ref, xo_ref, h_ref, lg_ref):
    tm = x_ref.shape[0]
    width = of_ref.shape[-1]
    o_att = of_ref[...].astype(F32) + ob_ref[...].astype(F32)
    y = _head_norm_gate(o_att.reshape(tm, width), gate_ref[...].astype(F32).reshape(tm, width),
                        HGRN_HEADS)
    y = jnp.dot(p_ref[...], y, preferred_element_type=F32).astype(BF16)
    o = jnp.dot(y, w_ref[...], preferred_element_type=F32)
    _out_proj_tail(o, x_ref, tab_ref, pg_ref, ng_ref, rt_ref, xo_ref, h_ref, lg_ref)


def _out_proj(kernel, lead, lead_specs, w, x, tab, post_gain, next_gain, router_t, tm, ntiles):
    d = x.shape[1]
    ntok = ntiles * tm
    nsub = tm // ROW_TILE
    ne = router_t.shape[0]
    in_specs = list(lead_specs) + [
        pl.BlockSpec(w.shape, lambda i: (0, 0), pipeline_mode=pl.Buffered(1)),
        pl.BlockSpec((tm, d), lambda i: (i, 0)),
        pl.BlockSpec((nsub, 8, d), lambda i: (i, 0, 0)),
        pl.BlockSpec((1, d), lambda i: (0, 0)),
        pl.BlockSpec((1, d), lambda i: (0, 0)),
        pl.BlockSpec((ne, d), lambda i: (0, 0))]
    return pl.pallas_call(
        kernel,
        out_shape=(jax.ShapeDtypeStruct((ntok, d), F32),
                   jax.ShapeDtypeStruct((ntok * CHUNKS_PER_ROW, LANE), F32),
                   jax.ShapeDtypeStruct((ne, ntok), F32)),
        grid=(ntiles,),
        in_specs=in_specs,
        out_specs=(pl.BlockSpec((tm, d), lambda i: (i, 0)),
                   pl.BlockSpec((tm * CHUNKS_PER_ROW, LANE), lambda i: (i, 0)),
                   pl.BlockSpec((ne, tm), lambda i: (0, i))),
        compiler_params=_params("parallel"),
        name="out_proj_norm_residual",
    )(*lead, w, x, tab, post_gain, next_gain, router_t)


def _route_kernel(lg_ref, bias_ref, ids_ref, w_ref):
    tm = lg_ref.shape[1]
    per_group = N_EXPERTS // N_GROUPS
    shape3 = (N_GROUPS, per_group, tm)
    aff = _sigmoid(lg_ref[...]).reshape(shape3)
    biased = aff + bias_ref[...].reshape(shape3)
    neg = -jnp.inf
    sub = lax.broadcasted_iota(jnp.int32, shape3, 1)
    grp = lax.broadcasted_iota(jnp.int32, shape3, 0)
    m1 = jnp.max(biased, axis=1, keepdims=True)
    i1 = jnp.min(jnp.where(biased == m1, sub, per_group), axis=1, keepdims=True)
    m2 = jnp.max(jnp.where(sub == i1, neg, biased), axis=1, keepdims=True)
    score = m1 + m2
    gidx = lax.broadcasted_iota(jnp.int32, score.shape, 0)
    keep = jnp.zeros(score.shape, F32)
    for _ in range(TOPK_GROUPS):
        m = jnp.max(score, axis=0, keepdims=True)
        im = jnp.min(jnp.where(score == m, gidx, N_GROUPS), axis=0, keepdims=True)
        sel = gidx == im
        keep = jnp.where(sel, 1.0, keep)
        score = jnp.where(sel, neg, score)
    work = jnp.where(jnp.broadcast_to(keep, shape3) > 0.0, biased, neg)
    eidx = grp * per_group + sub
    ids, gates = [], []
    for _ in range(TOP_K):
        m = jnp.max(jnp.max(work, axis=0, keepdims=True), axis=1, keepdims=True)
        cand = jnp.where(work == m, eidx, N_EXPERTS)
        im = jnp.min(jnp.min(cand, axis=0, keepdims=True), axis=1, keepdims=True)
        sel = eidx == im
        ids.append(im)
        gates.append(jnp.sum(jnp.sum(jnp.where(sel, aff, 0.0), axis=0, keepdims=True),
                             axis=1, keepdims=True))
        work = jnp.where(sel, neg, work)
    gate = jnp.concatenate(gates, axis=1)
    den = jnp.sum(gate, axis=1, keepdims=True)
    w_ref[...] = (ROUTED_SCALE * gate / den).reshape(TOP_K, tm)
    ids_ref[...] = jnp.concatenate(ids, axis=1).reshape(TOP_K, tm)


def _route(logits_t, bias, tm):
    ne, ntok = logits_t.shape
    bias_b = jnp.broadcast_to(bias.astype(F32)[:, None], (ne, tm))
    return pl.pallas_call(
        _route_kernel,
        out_shape=(jax.ShapeDtypeStruct((TOP_K, ntok), jnp.int32),
                   jax.ShapeDtypeStruct((TOP_K, ntok), F32)),
        grid=(ntok // tm,),
        in_specs=[pl.BlockSpec((ne, tm), lambda i: (0, i)),
                  pl.BlockSpec((ne, tm), lambda i: (0, 0))],
        out_specs=(pl.BlockSpec((TOP_K, tm), lambda i: (0, i)),
                   pl.BlockSpec((TOP_K, tm), lambda i: (0, i))),
        compiler_params=_params("parallel"),
        name="moe_route",
    )(logits_t, bias_b)


LIST_GRAIN = 1024


def _dispatch_plan(ids, w, t0, tile, ntiles):
    n = ntiles * tile
    nassign = tile * TOP_K
    e = ids[:, t0:t0 + n].T.reshape(ntiles, nassign)
    g = w[:, t0:t0 + n].T.reshape(ntiles, nassign)
    order = jnp.argsort(e, axis=1, stable=True).astype(jnp.int32)
    rows_sorted = (order // TOP_K) * CHUNKS_PER_ROW
    wts_sorted = jnp.take_along_axis(g, order, axis=1)
    experts = jnp.arange(N_EXPERTS, dtype=jnp.int32)
    counts = jnp.sum((e[:, :, None] == experts).astype(jnp.int32), axis=1)
    nblocks = (counts + MOE_BLOCK - 1) // MOE_BLOCK
    starts = jnp.cumsum(counts, axis=1) - counts
    bstarts = jnp.cumsum(nblocks, axis=1) - nblocks
    length = -(-(nassign + (N_EXPERTS + 1) * MOE_BLOCK) // LIST_GRAIN) * LIST_GRAIN
    q = jnp.arange(length, dtype=jnp.int32)
    eq = jnp.sum((q[None, :, None] >= (bstarts * MOE_BLOCK)[:, None, :]).astype(jnp.int32), axis=2) - 1
    off = q[None, :] - jnp.take_along_axis(bstarts, eq, axis=1) * MOE_BLOCK
    valid = off < jnp.take_along_axis(counts, eq, axis=1)
    src = jnp.minimum(jnp.take_along_axis(starts, eq, axis=1) + off, nassign - 1)
    rows = jnp.where(valid, jnp.take_along_axis(rows_sorted, src, axis=1), 0)
    wts = jnp.where(valid, jnp.take_along_axis(wts_sorted, src, axis=1), 0.0)
    blocks = jnp.concatenate([bstarts, jnp.sum(nblocks, axis=1, keepdims=True)], axis=1)
    blocks = jnp.pad(blocks, ((0, 0), (0, LANE - N_EXPERTS - 1)))
    return rows, wts, blocks


def _moe_kernel(rows_hbm, wts_hbm, starts_hbm, h_ref, wg_ref, wu_ref, wd_ref, sg_ref, su_ref,
                sd_ref, o_ref, rows_s, wts_s, starts_s, sem, gbuf, ybuf, *, epb):
    i = pl.program_id(0)
    j = pl.program_id(1)
    s_n = CHUNKS_PER_ROW
    blk = MOE_BLOCK
    tile = h_ref.shape[0] // s_n

    def swiglu(x, w_gate, w_up, w_down):
        g = jnp.dot(x, w_gate, preferred_element_type=F32)
        u = jnp.dot(x, w_up, preferred_element_type=F32)
        return jnp.dot((_silu(g) * u).astype(BF16), w_down, preferred_element_type=F32)

    @pl.when(j == 0)
    def _():
        copies = [pltpu.make_async_copy(rows_hbm.at[i], rows_s, sem.at[0]),
                  pltpu.make_async_copy(wts_hbm.at[i], wts_s, sem.at[1]),
                  pltpu.make_async_copy(starts_hbm.at[i], starts_s, sem.at[2])]
        for c in copies:
            c.start()
        for r0 in range(0, tile, ROW_TILE):
            x = _load_rows(h_ref, r0, ROW_TILE).astype(BF16)
            _store_rows(o_ref, r0, swiglu(x, sg_ref[...], su_ref[...], sd_ref[...]))
        for c in copies:
            c.wait()

    experts = []
    for e in range(epb):
        first = starts_s[j * epb + e]
        experts.append((first, starts_s[j * epb + e + 1] - first, wg_ref[e].astype(BF16),
                        wu_ref[e].astype(BF16), wd_ref[e].astype(BF16)))
    pad_block = rows_s.shape[0] // blk - 1

    def block(bi, carry):
        windows = []
        for e, (first, nblocks, _, _, _) in enumerate(experts):
            p0 = pl.multiple_of(jnp.where(bi < nblocks, first + bi, pad_block) * blk, blk)
            rows_w = rows_s.at[pl.ds(p0, blk)]
            for r in range(blk):
                off = pl.multiple_of(rows_w[r], s_n)
                gbuf[e, pl.ds(r * s_n, s_n), :] = h_ref[pl.ds(off, s_n), :]
            windows.append((rows_w, wts_s.at[pl.ds(p0, blk)]))
        for e, (_, _, w_gate, w_up, w_down) in enumerate(experts):
            x = _load_rows(gbuf.at[e], 0, blk).astype(BF16)
            _store_rows(ybuf.at[e], 0, swiglu(x, w_gate, w_up, w_down))
        group = 2 * SUBLANE
        for e, (rows_w, wts_w) in enumerate(windows):
            for r0 in range(0, blk, group):
                offs = [pl.multiple_of(rows_w[r], s_n) for r in range(r0, r0 + group)]
                vals = [o_ref[pl.ds(off, s_n), :] + wts_w[r0 + k] * ybuf[e, pl.ds((r0 + k) * s_n, s_n), :]
                        for k, off in enumerate(offs)]
                for off, val in reversed(list(zip(offs, vals))):
                    o_ref[pl.ds(off, s_n), :] = val
        return carry

    nblk = functools.reduce(jnp.maximum, [nblocks for _, nblocks, _, _, _ in experts])
    lax.fori_loop(0, nblk, block, 0)


def _moe(h, plan, layer, wg, wu, wd, sg, su, sd, tile0, ntiles, tile, epb):
    rows, wts, starts = plan
    s_n = CHUNKS_PER_ROW
    _, ne, d, ff = wg.shape
    once = pl.Buffered(1)
    hbm = pl.BlockSpec(memory_space=pl.ANY)
    return pl.pallas_call(
        functools.partial(_moe_kernel, epb=epb),
        out_shape=jax.ShapeDtypeStruct((ntiles * tile * s_n, LANE), F32),
        grid=(ntiles, ne // epb),
        in_specs=[hbm, hbm, hbm,
                  pl.BlockSpec((tile * s_n, LANE), lambda i, j: (tile0 + i, 0), pipeline_mode=once),
                  pl.BlockSpec((None, epb, d, ff), lambda i, j: (layer, j, 0, 0)),
                  pl.BlockSpec((None, epb, d, ff), lambda i, j: (layer, j, 0, 0)),
                  pl.BlockSpec((None, epb, ff, d), lambda i, j: (layer, j, 0, 0)),
                  pl.BlockSpec(sg.shape, lambda i, j: (0, 0), pipeline_mode=once),
                  pl.BlockSpec(su.shape, lambda i, j: (0, 0), pipeline_mode=once),
                  pl.BlockSpec(sd.shape, lambda i, j: (0, 0), pipeline_mode=once)],
        out_specs=pl.BlockSpec((tile * s_n, LANE), lambda i, j: (i, 0)),
        scratch_shapes=[pltpu.SMEM((rows.shape[1],), jnp.int32),
                        pltpu.SMEM((wts.shape[1],), F32),
                        pltpu.SMEM((starts.shape[1],), jnp.int32),
                        pltpu.SemaphoreType.DMA((3,)),
                        pltpu.VMEM((epb, MOE_BLOCK * s_n, LANE), F32),
                        pltpu.VMEM((epb, MOE_BLOCK * s_n, LANE), F32)],
        compiler_params=_params("parallel", "arbitrary", vmem=VMEM_LIMIT_MOE),
        name="moe_experts",
    )(rows, wts, starts, h, wg, wu, wd, sg, su, sd)


def _moe_finish_kernel(f_ref, x_ref, tab_ref, pg_ref, o_ref):
    f = _load_rows(f_ref, 0, x_ref.shape[0])
    o_ref[...] = _gated_residual(x_ref[...], _rms(f, pg_ref[...]), tab_ref, 5)


def _moe_finish(f, x, tab, post_gain, tm, x_tile0, tab0):
    d = x.shape[1]
    s_n = CHUNKS_PER_ROW
    nsub = tm // ROW_TILE
    ntiles = f.shape[0] // (tm * s_n)
    return pl.pallas_call(
        _moe_finish_kernel,
        out_shape=jax.ShapeDtypeStruct((ntiles * tm, d), F32),
        grid=(ntiles,),
        in_specs=[pl.BlockSpec((tm * s_n, LANE), lambda i: (i, 0)),
                  pl.BlockSpec((tm, d), lambda i: (x_tile0 + i, 0)),
                  pl.BlockSpec((nsub, 8, d), lambda i: (tab0 + i, 0, 0)),
                  pl.BlockSpec((1, d), lambda i: (0, 0))],
        out_specs=pl.BlockSpec((tm, d), lambda i: (i, 0)),
        compiler_params=_params("parallel"),
        name="moe_norm_residual",
    )(f, x, tab, post_gain)


def _grid_permutation(band):
    tm = band * GRID_W
    dst = jnp.arange(tm)
    src = (dst % band) * GRID_W + dst // band
    return (src[:, None] == jnp.arange(tm)[None, :]).astype(BF16)


def kernel(x, c, ctx, c_ctx, mod_w, mod_b, pre_gain, post_gain, ab_w_in, ab_conv_w, ab_conv_b, ab_lru_wa, ab_lru_ba, ab_lru_wi, ab_lru_bi, ab_lru_lambda, ab_gla_wa2, ab_gla_ba, ab_w_out, c_w_in, hgrn_lb_logits, c_w_out, moe_router, moe_bias, moe_w_gate, moe_w_up, moe_w_down, moe_ws_gate, moe_ws_up, moe_ws_down):
    batch, seq, d = x.shape
    n_ctx = ctx.shape[1]
    depth = mod_w.shape[0]
    band = SUBLANE
    tm_grid = band * GRID_W
    rows_img = seq // GRID_W
    assert n_ctx == ROW_TILE and seq % (2 * ROW_TILE) == 0 and d % LANE == 0
    assert depth == 2 and batch + 1 <= 8 and rows_img % band == 0 and tm_grid % ROW_TILE == 0
    n_lat = batch * seq
    n_ctx_tok = batch * n_ctx
    ntok = n_lat + n_ctx_tok

    xs = jnp.concatenate([x.reshape(n_lat, d), ctx.reshape(n_ctx_tok, d)], axis=0)

    cvec = jnp.concatenate([c, c_ctx[None, :], jnp.zeros((8 - batch - 1, d), F32)], axis=0)
    mods = _modulation(cvec, mod_w, mod_b).reshape(depth, 8, N_MOD, d)
    tile_row = jnp.concatenate([jnp.repeat(jnp.arange(batch), seq // ROW_TILE),
                                jnp.full((batch,), batch)]).astype(jnp.int32)
    tabs = jnp.pad(mods[:, tile_row], ((0, 0), (0, 0), (0, 8 - N_MOD), (0, 0)))

    def moe(layer, h2, routed, xs_in, t0, ntiles, tile):
        plan = _dispatch_plan(*routed, t0, tile, ntiles)
        f = _moe(h2, plan, layer, moe_w_gate, moe_w_up, moe_w_down,
                 moe_ws_gate[layer].astype(BF16), moe_ws_up[layer].astype(BF16),
                 moe_ws_down[layer].astype(BF16), t0 // tile, ntiles, tile, epb)
        return _moe_finish(f, xs_in, tabs[layer], post_gain[layer, 1][None, :], tm_out,
                           t0 // tm_out, t0 // tm_out)

    tm_proj, tm_out, tm_route, tm_moe, epb = 512, 512, 512, 2048, 2
    assert n_lat % tm_moe == 0 and n_lat % n_ctx_tok == 0

    ab_cols = ab_w_in.shape[2]
    gla_dk = ab_gla_wa2.shape[3]
    lru_w = d
    gla_dv = (ab_cols - 2 * lru_w - 2 * gla_dk - 2 * GLA_RANK) // 2
    tn0 = 1792
    n0 = -(-(ab_cols) // tn0) * tn0
    w_in0 = jnp.pad(ab_w_in[0], ((0, 0), (0, n0 - ab_cols))).astype(BF16)
    u0 = _in_proj(xs, tabs[0], pre_gain[0, 0][None, :], w_in0, tm_proj, tn0, 0)

    cw = jnp.pad(ab_conv_w[0], ((0, 8 - CONV_W), (0, 0)))
    cb = ab_conv_b[0][None, :]
    wg = jnp.concatenate([ab_lru_wa[0], ab_lru_wi[0]], axis=-1).astype(BF16)
    bg = jnp.concatenate([ab_lru_ba[0].reshape(2, LRU_HEADS, 1, LANE),
                          ab_lru_bi[0].reshape(2, LRU_HEADS, 1, LANE)], axis=-1)
    ya_c, ya_l = _lru(u0, (cw, cb, wg, bg, ab_lru_lambda[0]), batch=batch, seq=seq, n_ctx=n_ctx,
                      g_col0=0, x_col0=lru_w // LANE)

    q0 = 2 * lru_w // LANE
    k0 = q0 + gla_dk // LANE
    v0 = k0 + gla_dk // LANE
    gate0 = v0 + gla_dv // LANE
    a0 = gate0 + gla_dv // LANE
    wa = jnp.stack([jnp.zeros((LANE, gla_dk), F32).at[dr * GLA_RANK:(dr + 1) * GLA_RANK]
                    .set(ab_gla_wa2[0, dr]) for dr in range(2)]).astype(BF16)
    ba = ab_gla_ba[0][:, None, :]
    o_gla = _gla(u0, wa, ba, batch=batch, seq=seq, n_ctx=n_ctx, cols=(q0, k0, v0, a0),
                 dv_total=gla_dv)

    lat_tiles = n_lat // tm_out
    ctx_tiles = n_ctx_tok // tm_out
    lead_specs = [pl.BlockSpec((tm_out, lru_w), lambda i: (jnp.minimum(i, lat_tiles - 1), 0)),
                  pl.BlockSpec((tm_out, lru_w),
                               lambda i: (jnp.clip(i - lat_tiles, 0, ctx_tiles - 1), 0)),
                  pl.BlockSpec((ROW_TILE, ROW_TILE), lambda i: (0, 0)),
                  pl.BlockSpec((None, tm_out, gla_dv), lambda i: (0, i, 0)),
                  pl.BlockSpec((None, tm_out, gla_dv), lambda i: (1, i, 0)),
                  pl.BlockSpec((tm_out, gla_dv), lambda i: (i, gate0 * LANE // gla_dv))]
    xs, h2, lg = _out_proj(functools.partial(_out_proj_ab_kernel, lat_tiles=lat_tiles),
                           (ya_l, ya_c, _time_permutation().T.astype(BF16), o_gla, o_gla, u0),
                           lead_specs,
                           ab_w_out[0].astype(BF16), xs, tabs[0], post_gain[0, 0][None, :],
                           pre_gain[0, 1][None, :], moe_router[0].T, tm_out, ntok // tm_out)
    cwt = _route(lg, moe_bias[0], tm_route)
    xs_ctx = moe(0, h2, cwt, xs, n_lat, 1, n_ctx_tok)
    xs = moe(0, h2, cwt, xs, 0, n_lat // tm_moe, tm_moe)

    hd = c_w_out.shape[1]
    tn1 = 1280
    w_in1 = c_w_in[0].astype(BF16)
    c_cols = w_in1.shape[1]
    nsec = c_cols // hd
    perm = _grid_permutation(band)
    gain1 = pre_gain[1, 0][None, :]
    u1l = _in_proj_grid(xs, tabs[1], gain1, perm, w_in1, tn1, n_lat // tm_grid, band,
                        rows_img // band)
    u1c = _in_proj(xs_ctx, tabs[1], gain1, w_in1, tm_proj, tn1, n_lat // tm_proj)
    lb_sm = jax.nn.softmax(hgrn_lb_logits.astype(F32), axis=0)
    lb_cum = jnp.cumsum(lb_sm, axis=0)
    lb = (lb_cum[1] - lb_cum[0])[None, :]
    ub = 1.0 - lb
    dh = hd // HGRN_HEADS
    s_zero = jnp.zeros((2, batch, HGRN_HEADS, dh, dh), F32)
    u1l_flat = u1l.reshape(n_lat, c_cols)
    (s_c,) = _hgrn_segment(u1c, s_zero, lb, ub, batch=batch, seg_len=n_ctx, emit_o=False)
    o_hgrn, _ = _hgrn_segment(u1l_flat, s_c, lb, ub, batch=batch, seg_len=seq, emit_o=True)
    o_hgrn = o_hgrn.reshape(2, n_lat // rows_img, rows_img // band, band, hd)

    bpi = rows_img // band
    grid_blk = (GRID_W, None, band, hd)
    lead_specs = [pl.BlockSpec((None,) + grid_blk, lambda i: (0, i // bpi, i % bpi, 0, 0)),
                  pl.BlockSpec((None,) + grid_blk, lambda i: (1, i // bpi, i % bpi, 0, 0)),
                  pl.BlockSpec(grid_blk, lambda i: (i // bpi, i % bpi, 0, nsec - 1)),
                  pl.BlockSpec((tm_grid, tm_grid), lambda i: (0, 0))]
    xs, h2, lg = _out_proj(_out_proj_c_kernel, (o_hgrn, o_hgrn, u1l, perm.T), lead_specs,
                           c_w_out[0].astype(BF16), xs, tabs[1], post_gain[1, 0][None, :],
                           pre_gain[1, 1][None, :], moe_router[1].T, tm_grid, n_lat // tm_grid)
    out = moe(1, h2, _route(lg, moe_bias[1], tm_route), xs, 0, n_lat // tm_moe, tm_moe)
    return out.reshape(batch, seq, d)
```

```python
import functools

import jax
import jax.numpy as jnp
from jax import lax
from jax.experimental import pallas as pl
from jax.experimental.pallas import tpu as pltpu

F32 = jnp.float32
BF16 = jnp.bfloat16

EPS = 1e-6
N_MOD = 6
ROW_TILE = 256
CHUNK = 64
CHUNK_SHIFT = 6
GRID_W = 64
LRU_HEADS = 8
LRU_C = 8.0
CONV_W = 4
GLA_HEADS = 4
GLA_RANK = 16
GLA_TAU = 16.0
HGRN_HEADS = 8
N_EXPERTS = 64
TOP_K = 8
N_GROUPS = 8
TOPK_GROUPS = 4
ROUTED_SCALE = 2.5
LANE = 128
SUBLANE = 8
BF16_ROWS = 16
CHUNKS_PER_ROW = 8
MOE_BLOCK = 128
VMEM_LIMIT = 48 * 1024 * 1024
VMEM_LIMIT_MOE = 58 * 1024 * 1024

_NT = (((1,), (1,)), ((), ()))
_TN = (((0,), (0,)), ((), ()))


def _params(*sem, vmem=VMEM_LIMIT):
    return pltpu.CompilerParams(dimension_semantics=sem, vmem_limit_bytes=vmem)


def _rms(x, gain):
    ms = jnp.mean(x * x, axis=-1, keepdims=True)
    return x * lax.rsqrt(ms + EPS) * gain


def _sigmoid(x):
    return jax.nn.sigmoid(x)


def _silu(x):
    return x * _sigmoid(x)


def _log_sigmoid(x):
    return jnp.minimum(x, 0.0) - jnp.log1p(jnp.exp(-jnp.abs(x)))


def _softplus(x):
    return jnp.maximum(x, 0.0) + jnp.log1p(jnp.exp(-jnp.abs(x)))


def _modulate(xn, tab_ref, shift_row, scale_row):
    parts = []
    for s in range(tab_ref.shape[0]):
        t = tab_ref[s]
        rows = xn[s * ROW_TILE:(s + 1) * ROW_TILE]
        parts.append(rows * (1.0 + t[scale_row:scale_row + 1]) + t[shift_row:shift_row + 1])
    return parts[0] if len(parts) == 1 else jnp.concatenate(parts, axis=0)


def _gated_residual(x, branch, tab_ref, gate_row):
    parts = []
    for s in range(tab_ref.shape[0]):
        sl = slice(s * ROW_TILE, (s + 1) * ROW_TILE)
        parts.append(x[sl] + tab_ref[s][gate_row:gate_row + 1] * branch[sl])
    return parts[0] if len(parts) == 1 else jnp.concatenate(parts, axis=0)


def _mod_kernel(c_ref, w_ref, b_ref, o_ref):
    s = _silu(c_ref[...])
    o_ref[0] = jnp.dot(s, w_ref[0], precision=lax.Precision.HIGHEST,
                       preferred_element_type=F32) + b_ref[0]


def _modulation(cvec, mod_w, mod_b):
    depth, d, n = mod_w.shape
    tn = 1024
    return pl.pallas_call(
        _mod_kernel,
        out_shape=jax.ShapeDtypeStruct((depth, 8, n), F32),
        grid=(depth, n // tn),
        in_specs=[pl.BlockSpec((8, d), lambda l, j: (0, 0)),
                  pl.BlockSpec((1, d, tn), lambda l, j: (l, 0, j)),
                  pl.BlockSpec((1, 1, tn), lambda l, j: (l, 0, j))],
        out_specs=pl.BlockSpec((1, 8, tn), lambda l, j: (l, 0, j)),
        compiler_params=_params("parallel", "parallel"),
        name="adaln_modulation",
    )(cvec, mod_w, mod_b.reshape(depth, 1, n))


def _in_proj_kernel(x_ref, tab_ref, gain_ref, w_ref, o_ref, *, tn):
    h = _modulate(_rms(x_ref[...], gain_ref[...]), tab_ref, 0, 1).astype(BF16)
    for j in range(w_ref.shape[1] // tn):
        cs = slice(j * tn, (j + 1) * tn)
        o_ref[:, cs] = jnp.dot(h, w_ref[:, cs], preferred_element_type=F32).astype(o_ref.dtype)


def _in_proj(x, tab, gain, w, tm, tn, tab0):
    d = x.shape[1]
    n = w.shape[1]
    nsub = tm // ROW_TILE
    ntiles = x.shape[0] // tm
    return pl.pallas_call(
        functools.partial(_in_proj_kernel, tn=tn),
        out_shape=jax.ShapeDtypeStruct((ntiles * tm, n), BF16),
        grid=(ntiles,),
        in_specs=[pl.BlockSpec((tm, d), lambda i: (i, 0)),
                  pl.BlockSpec((nsub, 8, d), lambda i: (tab0 + i, 0, 0)),
                  pl.BlockSpec((1, d), lambda i: (0, 0)),
                  pl.BlockSpec((d, n), lambda i: (0, 0), pipeline_mode=pl.Buffered(1))],
        out_specs=pl.BlockSpec((tm, n), lambda i: (i, 0)),
        compiler_params=_params("parallel"),
        name="norm_mod_in_proj",
    )(x, tab, gain, w)


def _in_proj_grid_kernel(x_ref, tab_ref, gain_ref, p_ref, w_ref, o_ref, *, tn):
    h = _modulate(_rms(x_ref[...], gain_ref[...]), tab_ref, 0, 1).astype(BF16)
    hp = jnp.dot(p_ref[...], h, preferred_element_type=F32).astype(BF16)
    for j in range(w_ref.shape[1] // tn):
        cs = slice(j * tn, (j + 1) * tn)
        o = jnp.dot(hp, w_ref[:, cs], preferred_element_type=F32).astype(o_ref.dtype)
        o_ref[:, :, cs] = o.reshape(o_ref.shape[:2] + (tn,))


def _in_proj_grid(x, tab, gain, perm, w, tn, ntiles, band, bands_per_image):
    d = x.shape[1]
    n = w.shape[1]
    tm = band * GRID_W
    nsub = tm // ROW_TILE
    images = ntiles // bands_per_image
    once = pl.Buffered(1)
    return pl.pallas_call(
        functools.partial(_in_proj_grid_kernel, tn=tn),
        out_shape=jax.ShapeDtypeStruct((images * GRID_W, bands_per_image, band, n), F32),
        grid=(ntiles,),
        in_specs=[pl.BlockSpec((tm, d), lambda i: (i, 0)),
                  pl.BlockSpec((nsub, 8, d), lambda i: (i, 0, 0)),
                  pl.BlockSpec((1, d), lambda i: (0, 0)),
                  pl.BlockSpec((tm, tm), lambda i: (0, 0), pipeline_mode=once),
                  pl.BlockSpec((d, n), lambda i: (0, 0), pipeline_mode=once)],
        out_specs=pl.BlockSpec((GRID_W, None, band, n),
                               lambda i: (i // bands_per_image, i % bands_per_image, 0, 0)),
        compiler_params=_params("parallel"),
        name="norm_mod_in_proj_grid",
    )(x, tab, gain, perm, w)


def _lru_kernel(gc_ref, xc_ref, gl_ref, xl_ref, p_ref, cw_ref, cb_ref, wg_ref, bg_ref,
                lam_ref, yc_ref, yl_ref, xs_scr, hf_scr):
    T = ROW_TILE
    V = T // SUBLANE
    H = BF16_ROWS
    sub = lax.broadcasted_iota(jnp.int32, (SUBLANE, LANE), 0)
    cw = cw_ref[...]
    cb = cb_ref[...]
    sp = _softplus(-lam_ref[...])

    def vreg(a, v):
        return a[v * SUBLANE:(v + 1) * SUBLANE]

    def shift_down(y, first):
        return jnp.where(sub == 0, first, pltpu.roll(y, 1, 0))

    def shift_up(y, last):
        return jnp.where(sub == SUBLANE - 1, last, pltpu.roll(y, SUBLANE - 1, 0))

    def conv_tile(x_ref, t, nt, seg_len):
        t0 = pl.multiple_of(t * T, T)
        xp = jnp.dot(p_ref[...], x_ref[pl.ds(t0, T), :], preferred_element_type=F32)
        p0 = pl.multiple_of(jnp.maximum(t0 - H, 0), H)
        n0 = pl.multiple_of(jnp.minimum(t0 + T, seg_len - H), H)
        prev = x_ref[pl.ds(p0, H), :].astype(F32) * jnp.where(t > 0, 1.0, 0.0)
        nxt = x_ref[pl.ds(n0, H), :].astype(F32) * jnp.where(t < nt - 1, 1.0, 0.0)
        m1, m2, p1 = prev[H - 1:H], prev[H - 2:H - 1], nxt[0:1]
        last_m1 = shift_down(vreg(xp, V - 1), m1)
        xm1 = jnp.concatenate([last_m1, xp[:T - SUBLANE]], axis=0)
        xm2 = jnp.concatenate([shift_down(vreg(xp, V - 2), m2), last_m1, xp[:T - 2 * SUBLANE]], axis=0)
        xp1 = jnp.concatenate([xp[SUBLANE:], shift_up(vreg(xp, 0), p1)], axis=0)
        return cb + xm2 * cw[0:1] + xm1 * cw[1:2] + xp * cw[2:3] + xp1 * cw[3:4]

    def gates(xc, d):
        z = jnp.dot(xc.astype(BF16), wg_ref[d, 0], preferred_element_type=F32) + bg_ref[d, 0]
        r = _sigmoid(z[:, :LANE])
        i = _sigmoid(z[:, LANE:])
        log_a = (-LRU_C) * r * sp[d:d + 1]
        a = jnp.exp(log_a)
        return a, jnp.sqrt(1.0 - a * a) * (i * xc)

    def scan_tile(a, u, carry, reverse):
        hs, cum = [None] * V, [None] * V
        hp = ap = None
        for v in (range(V - 1, -1, -1) if reverse else range(V)):
            av, uv = vreg(a, v), vreg(u, v)
            hp, ap = (uv, av) if hp is None else (av * hp + uv, av * ap)
            hs[v], cum[v] = hp, ap
        s = 1
        while s < SUBLANE:
            keep = (sub < SUBLANE - s) if reverse else (sub >= s)
            sh = SUBLANE - s if reverse else s
            h_sh = jnp.where(keep, pltpu.roll(hp, sh, 0), 0.0)
            a_sh = jnp.where(keep, pltpu.roll(ap, sh, 0), 1.0)
            hp = hp + ap * h_sh
            ap = ap * a_sh
            s *= 2
        state = hp + ap * carry
        if reverse:
            start, out = shift_up(state, carry), state[0:1]
        else:
            start, out = shift_down(state, carry), state[SUBLANE - 1:SUBLANE]
        h = jnp.concatenate([hs[v] + cum[v] * start for v in range(V)], axis=0)
        return h, out

    def run_segment(g_ref, x_ref, y_ref, seg_len, carry_f, carry_b):
        nt = seg_len // T
        unroll = 4 if nt % 4 == 0 else (2 if nt % 2 == 0 else 1)

        def fwd_body(i, carry):
            for j in range(unroll):
                t = i * unroll + j
                t0 = pl.multiple_of(t * T, T)
                xc = conv_tile(x_ref, t, nt, seg_len)
                xs_scr[pl.ds(t0, T), :] = xc
                a, u = gates(xc, 0)
                h, carry = scan_tile(a, u, carry, False)
                hf_scr[pl.ds(t0, T), :] = h
            return carry

        def bwd_body(i, carry):
            for j in range(unroll):
                t = nt - 1 - (i * unroll + j)
                t0 = pl.multiple_of(t * T, T)
                a, u = gates(xs_scr[pl.ds(t0, T), :], 1)
                h, carry = scan_tile(a, u, carry, True)
                g = jnp.dot(p_ref[...], g_ref[pl.ds(t0, T), :], preferred_element_type=F32)
                rec = hf_scr[pl.ds(t0, T), :] + h
                y_ref[pl.ds(t0, T), :] = (jax.nn.gelu(g) * rec).astype(y_ref.dtype)
            return carry

        steps = nt // unroll
        return (lax.fori_loop(0, steps, fwd_body, carry_f),
                lax.fori_loop(0, steps, bwd_body, carry_b))

    zero = jnp.zeros((1, LANE), F32)
    hf, hb = run_segment(gc_ref, xc_ref, yc_ref, xc_ref.shape[0], zero, zero)
    run_segment(gl_ref, xl_ref, yl_ref, xl_ref.shape[0], hf, hb)


def _time_permutation():
    v = ROW_TILE // SUBLANE
    dst = jnp.arange(ROW_TILE)
    src = (dst % SUBLANE) * v + dst // SUBLANE
    return src[:, None] == jnp.arange(ROW_TILE)[None, :]


def _lru(u, lru_w, *, batch, seq, n_ctx, g_col0, x_col0):
    width = LRU_HEADS * LANE
    cw, cb, wg, bg, lam = lru_w
    ctx_blk0 = batch * seq // n_ctx
    perm = _time_permutation()
    once = pl.Buffered(1)
    return pl.pallas_call(
        _lru_kernel,
        out_shape=(jax.ShapeDtypeStruct((batch * n_ctx, width), BF16),
                   jax.ShapeDtypeStruct((batch * seq, width), BF16)),
        grid=(batch, LRU_HEADS),
        in_specs=[pl.BlockSpec((n_ctx, LANE), lambda b, h: (ctx_blk0 + b, g_col0 + h)),
                  pl.BlockSpec((n_ctx, LANE), lambda b, h: (ctx_blk0 + b, x_col0 + h)),
                  pl.BlockSpec((seq, LANE), lambda b, h: (b, g_col0 + h)),
                  pl.BlockSpec((seq, LANE), lambda b, h: (b, x_col0 + h)),
                  pl.BlockSpec((ROW_TILE, ROW_TILE), lambda b, h: (0, 0), pipeline_mode=once),
                  pl.BlockSpec((8, LANE), lambda b, h: (0, h)),
                  pl.BlockSpec((1, LANE), lambda b, h: (0, h)),
                  pl.BlockSpec((2, 1, LANE, 2 * LANE), lambda b, h: (0, h, 0, 0)),
                  pl.BlockSpec((2, 1, 1, 2 * LANE), lambda b, h: (0, h, 0, 0)),
                  pl.BlockSpec((2, LANE), lambda b, h: (0, h))],
        out_specs=(pl.BlockSpec((n_ctx, LANE), lambda b, h: (b, h)),
                   pl.BlockSpec((seq, LANE), lambda b, h: (b, h))),
        scratch_shapes=[pltpu.VMEM((seq, LANE), F32), pltpu.VMEM((seq, LANE), F32)],
        compiler_params=_params("parallel", "parallel"),
        name="rglru",
    )(u, u, u, u, perm.astype(BF16), cw, cb, wg, bg, lam)


def _chunk_mask(rows, reverse):
    r = lax.broadcasted_iota(jnp.int32, (rows, rows), 0)
    c = lax.broadcasted_iota(jnp.int32, (rows, rows), 1)
    same = (r >> CHUNK_SHIFT) == (c >> CHUNK_SHIFT)
    tri = (c >= r) if reverse else (c <= r)
    return jnp.where(same, jnp.where(tri, 1.0, 0.0), 0.0)


def _split3(x):
    hi = x.astype(BF16)
    r1 = x - hi.astype(F32)
    mid = r1.astype(BF16)
    lo = (r1 - mid.astype(F32)).astype(BF16)
    return hi, mid, lo


def _block_scan(g_all, head_qkv, st_scr, o_ref, *, heads, dv, reverse, batched):
    rows, wall = g_all.shape
    width = wall // heads
    nchunk = rows // CHUNK
    mask = _chunk_mask(rows, reverse)
    mask_b = mask.astype(BF16)
    keep = mask > 0.5
    keep_c = keep[:CHUNK, :CHUNK]
    iref = CHUNK // 2 if reverse else CHUNK // 2 - 1
    ilast = 0 if reverse else CHUNK - 1
    order = range(nchunk - 1, -1, -1) if reverse else range(nchunk)

    def per_chunk(rows_1, width):
        return jnp.concatenate([jnp.broadcast_to(r, (CHUNK, width)) for r in rows_1], axis=0)

    pieces = jnp.concatenate(_split3(g_all), axis=1)
    sums = jnp.dot(mask_b, pieces, preferred_element_type=F32)
    b_all = sums[:, :wall] + sums[:, wall:2 * wall] + sums[:, 2 * wall:]

    def prepare(h):
        q, k, v = head_qkv(h)
        b = b_all[:, h * width:(h + 1) * width]
        b_ref = [b[c * CHUNK + iref:c * CHUNK + iref + 1] for c in range(nchunk)]
        b_last = [b[c * CHUNK + ilast:c * CHUNK + ilast + 1] for c in range(nchunk)]
        e1 = jnp.exp(b - per_chunk(b_ref, width))
        r1 = 1.0 / e1
        qe = q * e1
        ke = k * r1
        qd = (qe * per_chunk([jnp.exp(r) for r in b_ref], width)).astype(BF16)
        kd = (ke * per_chunk([jnp.exp(l - r) for l, r in zip(b_last, b_ref)], width)).astype(BF16)
        decays = [jnp.exp(l) for l in b_last]
        return qe.astype(BF16), ke.astype(BF16), qd, kd, v.astype(BF16), decays

    nxt = prepare(0)
    for h in range(heads):
        qe, ke, qd, kd, vb, decays = nxt
        if h + 1 < heads:
            nxt = prepare(h + 1)
        if batched:
            sc = lax.dot_general(qe, ke, _NT, preferred_element_type=F32)
            sc = jnp.where(keep, sc, 0.0).astype(BF16)
            o = jnp.dot(sc, vb, preferred_element_type=F32)
        st = st_scr[h]
        parts = [None] * nchunk
        for ci in order:
            rs = slice(ci * CHUNK, (ci + 1) * CHUNK)
            if batched:
                o_c = o[rs]
            else:
                sc = lax.dot_general(qe[rs], ke[rs], _NT, preferred_element_type=F32)
                sc = jnp.where(keep_c, sc, 0.0).astype(BF16)
                o_c = jnp.dot(sc, vb[rs], preferred_element_type=F32)
            parts[ci] = o_c + lax.dot_general(qd[rs], st.astype(BF16), _NT,
                                              preferred_element_type=F32)
            st = st * decays[ci] + lax.dot_general(vb[rs], kd[rs], _TN, preferred_element_type=F32)
        if o_ref is not None:
            o_ref[:, h * dv:(h + 1) * dv] = jnp.concatenate(parts, axis=0).astype(o_ref.dtype)
        st_scr[h] = st


def _gla_kernel(q_ref, k_ref, v_ref, a_ref, wa_ref, ba_ref, o_ref, st_scr):
    t = pl.program_id(1)
    d = pl.program_id(2)
    dk = q_ref.shape[1] // GLA_HEADS
    dv = v_ref.shape[1] // GLA_HEADS

    @pl.when((t == 0) & (d == 0))
    def _():
        st_scr[...] = jnp.zeros(st_scr.shape, F32)

    def head_qkv(h):
        ks = slice(h * dk, (h + 1) * dk)
        q = q_ref[:, ks].astype(F32) * (dk ** -0.5)
        return q, k_ref[:, ks].astype(F32), v_ref[:, h * dv:(h + 1) * dv]

    for dd in range(2):
        @pl.when(d == dd)
        def _(dd=dd):
            z = jnp.dot(a_ref[...], wa_ref[dd], preferred_element_type=F32) + ba_ref[dd]
            g_all = _log_sigmoid(z) * (1.0 / GLA_TAU)
            _block_scan(g_all, head_qkv, st_scr.at[dd], o_ref,
                        heads=GLA_HEADS, dv=dv, reverse=dd == 1, batched=True)


def _gla(u, wa, ba, *, batch, seq, n_ctx, cols, dv_total):
    ntok = u.shape[0]
    q0, k0, v0, a0 = cols
    dkt = wa.shape[2]
    dvt = dv_total
    nblk = seq // ROW_TILE
    ctx_blk0 = batch * nblk
    assert n_ctx == ROW_TILE

    def blk(b, t, d):
        lat = b * nblk + jnp.where(d == 0, t - 1, nblk - t)
        return jnp.where(t == 0, ctx_blk0 + b, lat)

    in_specs = [pl.BlockSpec((ROW_TILE, dkt), lambda b, t, d: (blk(b, t, d), q0 * LANE // dkt)),
                pl.BlockSpec((ROW_TILE, dkt), lambda b, t, d: (blk(b, t, d), k0 * LANE // dkt)),
                pl.BlockSpec((ROW_TILE, dvt), lambda b, t, d: (blk(b, t, d), v0 * LANE // dvt)),
                pl.BlockSpec((ROW_TILE, LANE), lambda b, t, d: (blk(b, t, d), a0)),
                pl.BlockSpec(wa.shape, lambda b, t, d: (0, 0, 0)),
                pl.BlockSpec(ba.shape, lambda b, t, d: (0, 0, 0))]
    return pl.pallas_call(
        _gla_kernel,
        out_shape=jax.ShapeDtypeStruct((2, ntok, dvt), BF16),
        grid=(batch, nblk + 1, 2),
        in_specs=in_specs,
        out_specs=pl.BlockSpec((None, ROW_TILE, dvt), lambda b, t, d: (d, blk(b, t, d), 0)),
        scratch_shapes=[pltpu.VMEM((2, GLA_HEADS, dvt // GLA_HEADS, dkt // GLA_HEADS), F32)],
        compiler_params=_params("parallel", "arbitrary", "arbitrary"),
        name="gla",
    )(u, u, u, u, wa, ba)


def _hgrn_kernel(*refs, emit_o):
    q_ref, z_ref, v_ref, lb_ref, ub_ref, s0_ref = refs[:6]
    if emit_o:
        o_ref, st_ref, st_scr = refs[6:]
    else:
        st_ref, st_scr = refs[6:]
        o_ref = None
    t = pl.program_id(1)
    d = pl.program_id(2)
    dh = q_ref.shape[1] // HGRN_HEADS

    @pl.when((t == 0) & (d == 0))
    def _():
        st_scr[...] = s0_ref[:, 0]

    lb = lb_ref[...]
    ub = ub_ref[...]

    for dd in range(2):
        @pl.when(d == dd)
        def _(dd=dd):
            z = z_ref[...].astype(F32)
            e = jnp.exp(-jnp.abs(z))
            r = 1.0 / (1.0 + e)
            er = e * r
            pos = z >= 0.0
            log_f = jnp.log(lb + ub * jnp.where(pos, r, er))
            k_all = ub * jnp.where(pos, er, r)

            def head_qkv(h):
                hs = slice(h * dh, (h + 1) * dh)
                return q_ref[:, hs].astype(F32), k_all[:, hs], v_ref[:, hs]

            _block_scan(log_f, head_qkv, st_scr.at[dd], o_ref,
                        heads=HGRN_HEADS, dv=dh, reverse=dd == 1, batched=True)

    @pl.when((t == pl.num_programs(1) - 1) & (d == 1))
    def _():
        st_ref[:, 0] = st_scr[...]


def _hgrn_segment(u, s0, lb, ub, *, batch, seg_len, emit_o):
    ntok = u.shape[0]
    width = lb.shape[1]
    nblk = seg_len // ROW_TILE

    def blk(b, t, d):
        return b * nblk + jnp.where(d == 0, t, nblk - 1 - t)

    st_spec = pl.BlockSpec((2, 1) + s0.shape[2:], lambda b, t, d: (0, b, 0, 0, 0))
    in_specs = [pl.BlockSpec((ROW_TILE, width), lambda b, t, d: (blk(b, t, d), 0)),
                pl.BlockSpec((ROW_TILE, width), lambda b, t, d: (blk(b, t, d), 1 + d)),
                pl.BlockSpec((ROW_TILE, width), lambda b, t, d: (blk(b, t, d), 3)),
                pl.BlockSpec((1, width), lambda b, t, d: (0, 0)),
                pl.BlockSpec((1, width), lambda b, t, d: (0, 0)),
                st_spec]
    st_shape = jax.ShapeDtypeStruct(s0.shape, F32)
    if emit_o:
        out_shape = (jax.ShapeDtypeStruct((2, ntok, width), F32), st_shape)
        out_specs = (pl.BlockSpec((None, ROW_TILE, width), lambda b, t, d: (d, blk(b, t, d), 0)),
                     st_spec)
    else:
        out_shape = (st_shape,)
        out_specs = (st_spec,)
    return pl.pallas_call(
        functools.partial(_hgrn_kernel, emit_o=emit_o),
        out_shape=out_shape,
        grid=(batch, nblk, 2),
        in_specs=in_specs,
        out_specs=out_specs,
        scratch_shapes=[pltpu.VMEM((2,) + s0.shape[2:], F32)],
        compiler_params=_params("parallel", "arbitrary", "arbitrary"),
        name="hgrn_segment",
    )(u, u, u, lb, ub, s0)


def _head_norm_gate(o, gate, heads):
    dh = o.shape[1] // heads
    parts = []
    for h in range(heads):
        oh = o[:, h * dh:(h + 1) * dh]
        ms = jnp.mean(oh * oh, axis=-1, keepdims=True)
        parts.append((oh * lax.rsqrt(ms + EPS) * _silu(gate[:, h * dh:(h + 1) * dh])).astype(BF16))
    return jnp.concatenate(parts, axis=1)


def _load_rows(ref, r0, nrows):
    s_n = CHUNKS_PER_ROW
    return jnp.concatenate([ref[pl.ds(r0 * s_n + s, nrows, stride=s_n), :] for s in range(s_n)],
                           axis=1)


def _store_rows(ref, r0, val):
    s_n = CHUNKS_PER_ROW
    for s in range(s_n):
        ref[pl.ds(r0 * s_n + s, val.shape[0], stride=s_n), :] = val[:, s * LANE:(s + 1) * LANE]


def _out_proj_tail(o, x_ref, tab_ref, pg_ref, ng_ref, rt_ref, xo_ref, h_ref, lg_ref):
    xn = _gated_residual(x_ref[...], _rms(o, pg_ref[...]), tab_ref, 2)
    xo_ref[...] = xn
    h2 = _modulate(_rms(xn, ng_ref[...]), tab_ref, 3, 4)
    _store_rows(h_ref, 0, h2)
    lg_ref[...] = lax.dot_general(rt_ref[...], h2, _NT, precision=lax.Precision.HIGHEST,
                                  preferred_element_type=F32)


def _out_proj_ab_kernel(yl_ref, yc_ref, pt_ref, of_ref, ob_ref, gate_ref, w_ref, x_ref, tab_ref,
                        pg_ref, ng_ref, rt_ref, xo_ref, h_ref, lg_ref, *, lat_tiles):
    o_att = of_ref[...].astype(F32) + ob_ref[...].astype(F32)
    yb = _head_norm_gate(o_att, gate_ref[...].astype(F32), GLA_HEADS)
    ya = jnp.where(pl.program_id(0) < lat_tiles, yl_ref[...], yc_ref[...])
    ya = jnp.concatenate(
        [jnp.dot(pt_ref[...], ya[s:s + ROW_TILE], preferred_element_type=F32).astype(BF16)
         for s in range(0, ya.shape[0], ROW_TILE)], axis=0)
    da = ya.shape[1]
    o = jnp.dot(ya, w_ref[:da], preferred_element_type=F32)
    o = o + jnp.dot(yb, w_ref[da:], preferred_element_type=F32)
    _out_proj_tail(o, x_ref, tab_ref, pg_ref, ng_ref, rt_ref, xo_ref, h_ref, lg_ref)


def _out_proj_c_kernel(of_ref, ob_ref, gate_ref, p_ref, w_ref, x_ref, tab_ref, pg_ref, ng_ref,
                       rt_ref, xo_ref, h_ref, lg_ref):
    tm = x_ref.shape[0]
    width = of_ref.shape[-1]
    o_att = of_ref[...].astype(F32) + ob_ref[...].astype(F32)
    y = _head_norm_gate(o_att.reshape(tm, width), gate_ref[...].astype(F32).reshape(tm, width),
                        HGRN_HEADS)
    y = jnp.dot(p_ref[...], y, preferred_element_type=F32).astype(BF16)
    o = jnp.dot(y, w_ref[...], preferred_element_type=F32)
    _out_proj_tail(o, x_ref, tab_ref, pg_ref, ng_ref, rt_ref, xo_ref, h_ref, lg_ref)


def _out_proj(kernel, lead, lead_specs, w, x, tab, post_gain, next_gain, router_t, tm, ntiles):
    d = x.shape[1]
    ntok = ntiles * tm
    nsub = tm // ROW_TILE
    ne = router_t.shape[0]
    in_specs = list(lead_specs) + [
        pl.BlockSpec(w.shape, lambda i: (0, 0), pipeline_mode=pl.Buffered(1)),
        pl.BlockSpec((tm, d), lambda i: (i, 0)),
        pl.BlockSpec((nsub, 8, d), lambda i: (i, 0, 0)),
        pl.BlockSpec((1, d), lambda i: (0, 0)),
        pl.BlockSpec((1, d), lambda i: (0, 0)),
        pl.BlockSpec((ne, d), lambda i: (0, 0))]
    return pl.pallas_call(
        kernel,
        out_shape=(jax.ShapeDtypeStruct((ntok, d), F32),
                   jax.ShapeDtypeStruct((ntok * CHUNKS_PER_ROW, LANE), F32),
                   jax.ShapeDtypeStruct((ne, ntok), F32)),
        grid=(ntiles,),
        in_specs=in_specs,
        out_specs=(pl.BlockSpec((tm, d), lambda i: (i, 0)),
                   pl.BlockSpec((tm * CHUNKS_PER_ROW, LANE), lambda i: (i, 0)),
                   pl.BlockSpec((ne, tm), lambda i: (0, i))),
        compiler_params=_params("parallel"),
        name="out_proj_norm_residual",
    )(*lead, w, x, tab, post_gain, next_gain, router_t)


def _route_kernel(lg_ref, bias_ref, ids_ref, w_ref):
    tm = lg_ref.shape[1]
    per_group = N_EXPERTS // N_GROUPS
    shape3 = (N_GROUPS, per_group, tm)
    aff = _sigmoid(lg_ref[...]).reshape(shape3)
    biased = aff + bias_ref[...].reshape(shape3)
    neg = -jnp.inf
    sub = lax.broadcasted_iota(jnp.int32, shape3, 1)
    grp = lax.broadcasted_iota(jnp.int32, shape3, 0)
    m1 = jnp.max(biased, axis=1, keepdims=True)
    i1 = jnp.min(jnp.where(biased == m1, sub, per_group), axis=1, keepdims=True)
    m2 = jnp.max(jnp.where(sub == i1, neg, biased), axis=1, keepdims=True)
    score = m1 + m2
    gidx = lax.broadcasted_iota(jnp.int32, score.shape, 0)
    keep = jnp.zeros(score.shape, F32)
    for _ in range(TOPK_GROUPS):
        m = jnp.max(score, axis=0, keepdims=True)
        im = jnp.min(jnp.where(score == m, gidx, N_GROUPS), axis=0, keepdims=True)
        sel = gidx == im
        keep = jnp.where(sel, 1.0, keep)
        score = jnp.where(sel, neg, score)
    work = jnp.where(jnp.broadcast_to(keep, shape3) > 0.0, biased, neg)
    eidx = grp * per_group + sub
    ids, gates = [], []
    for _ in range(TOP_K):
        m = jnp.max(jnp.max(work, axis=0, keepdims=True), axis=1, keepdims=True)
        cand = jnp.where(work == m, eidx, N_EXPERTS)
        im = jnp.min(jnp.min(cand, axis=0, keepdims=True), axis=1, keepdims=True)
        sel = eidx == im
        ids.append(im)
        gates.append(jnp.sum(jnp.sum(jnp.where(sel, aff, 0.0), axis=0, keepdims=True),
                             axis=1, keepdims=True))
        work = jnp.where(sel, neg, work)
    gate = jnp.concatenate(gates, axis=1)
    den = jnp.sum(gate, axis=1, keepdims=True)
    w_ref[...] = (ROUTED_SCALE * gate / den).reshape(TOP_K, tm)
    ids_ref[...] = jnp.concatenate(ids, axis=1).reshape(TOP_K, tm)


def _route(logits_t, bias, tm):
    ne, ntok = logits_t.shape
    bias_b = jnp.broadcast_to(bias.astype(F32)[:, None], (ne, tm))
    return pl.pallas_call(
        _route_kernel,
        out_shape=(jax.ShapeDtypeStruct((TOP_K, ntok), jnp.int32),
                   jax.ShapeDtypeStruct((TOP_K, ntok), F32)),
        grid=(ntok // tm,),
        in_specs=[pl.BlockSpec((ne, tm), lambda i: (0, i)),
                  pl.BlockSpec((ne, tm), lambda i: (0, 0))],
        out_specs=(pl.BlockSpec((TOP_K, tm), lambda i: (0, i)),
                   pl.BlockSpec((TOP_K, tm), lambda i: (0, i))),
        compiler_params=_params("parallel"),
        name="moe_route",
    )(logits_t, bias_b)


LIST_PAD = 1024


def _dispatch_plan(ids, w, t0, tile, ntiles):
    n = ntiles * tile
    e = ids[:, t0:t0 + n].T.reshape(ntiles, tile * TOP_K)
    g = w[:, t0:t0 + n].T.reshape(ntiles, tile * TOP_K)
    order = jnp.argsort(e, axis=1, stable=True).astype(jnp.int32)
    rows = (order // TOP_K) * CHUNKS_PER_ROW
    wts = jnp.take_along_axis(g, order, axis=1)
    experts = jnp.arange(N_EXPERTS, dtype=jnp.int32)
    counts = jnp.sum((e[:, :, None] == experts).astype(jnp.int32), axis=1)
    starts = jnp.concatenate([jnp.zeros((ntiles, 1), jnp.int32), jnp.cumsum(counts, axis=1)], axis=1)
    rows = jnp.pad(rows, ((0, 0), (0, LIST_PAD)))
    wts = jnp.pad(wts, ((0, 0), (0, LIST_PAD)))
    starts = jnp.pad(starts, ((0, 0), (0, LANE - N_EXPERTS - 1)))
    return rows, wts, starts


def _moe_kernel(rows_hbm, wts_hbm, starts_hbm, h_ref, wg_ref, wu_ref, wd_ref, sg_ref, su_ref,
                sd_ref, o_ref, rows_s, wts_s, starts_s, sem, gbuf, ybuf, *, epb):
    i = pl.program_id(0)
    j = pl.program_id(1)
    s_n = CHUNKS_PER_ROW
    blk = MOE_BLOCK
    tile = h_ref.shape[0] // s_n

    def swiglu(x, w_gate, w_up, w_down):
        g = jnp.dot(x, w_gate, preferred_element_type=F32)
        u = jnp.dot(x, w_up, preferred_element_type=F32)
        return jnp.dot((_silu(g) * u).astype(BF16), w_down, preferred_element_type=F32)

    @pl.when(j == 0)
    def _():
        copies = [pltpu.make_async_copy(rows_hbm.at[i], rows_s, sem.at[0]),
                  pltpu.make_async_copy(wts_hbm.at[i], wts_s, sem.at[1]),
                  pltpu.make_async_copy(starts_hbm.at[i], starts_s, sem.at[2])]
        for c in copies:
            c.start()
        for r0 in range(0, tile, ROW_TILE):
            x = _load_rows(h_ref, r0, ROW_TILE).astype(BF16)
            _store_rows(o_ref, r0, swiglu(x, sg_ref[...], su_ref[...], sd_ref[...]))
        for c in copies:
            c.wait()

    experts = []
    for e in range(epb):
        base = starts_s[j * epb + e]
        count = starts_s[j * epb + e + 1] - base
        experts.append((base, count, wg_ref[e].astype(BF16), wu_ref[e].astype(BF16),
                        wd_ref[e].astype(BF16)))
    last_block = rows_s.shape[0] - blk

    def block(bi, carry):
        staged = []
        for e, (base, count, _, _, _) in enumerate(experts):
            p0 = jnp.minimum(base + bi * blk, last_block)
            left = count - bi * blk
            offs = [pl.multiple_of(rows_s[p0 + r], s_n) for r in range(blk)]
            for r in range(blk):
                gbuf[e, pl.ds(r * s_n, s_n), :] = h_ref[pl.ds(offs[r], s_n), :]
            staged.append((p0, left, offs))
        for e, (_, _, w_gate, w_up, w_down) in enumerate(experts):
            y = swiglu(_load_rows(gbuf.at[e], 0, blk).astype(BF16), w_gate, w_up, w_down)
            y = jnp.where(lax.broadcasted_iota(jnp.int32, y.shape, 0) < staged[e][1], y, 0.0)
            _store_rows(ybuf.at[e], 0, y)
        group = 2 * SUBLANE
        for e, (p0, _, offs) in enumerate(staged):
            for r0 in range(0, blk, group):
                vals = [o_ref[pl.ds(offs[r], s_n), :]
                        + wts_s[p0 + r] * ybuf[e, pl.ds(r * s_n, s_n), :]
                        for r in range(r0, r0 + group)]
                for r in reversed(range(r0, r0 + group)):
                    o_ref[pl.ds(offs[r], s_n), :] = vals[r - r0]
        return carry

    nblk = functools.reduce(jnp.maximum, [
        lax.shift_right_logical(count + (blk - 1), blk.bit_length() - 1)
        for _, count, _, _, _ in experts])
    lax.fori_loop(0, nblk, block, 0)


def _moe(h, plan, layer, wg, wu, wd, sg, su, sd, tile0, ntiles, tile, epb):
    rows, wts, starts = plan
    s_n = CHUNKS_PER_ROW
    _, ne, d, ff = wg.shape
    once = pl.Buffered(1)
    hbm = pl.BlockSpec(memory_space=pl.ANY)
    return pl.pallas_call(
        functools.partial(_moe_kernel, epb=epb),
        out_shape=jax.ShapeDtypeStruct((ntiles * tile * s_n, LANE), F32),
        grid=(ntiles, ne // epb),
        in_specs=[hbm, hbm, hbm,
                  pl.BlockSpec((tile * s_n, LANE), lambda i, j: (tile0 + i, 0), pipeline_mode=once),
                  pl.BlockSpec((None, epb, d, ff), lambda i, j: (layer, j, 0, 0)),
                  pl.BlockSpec((None, epb, d, ff), lambda i, j: (layer, j, 0, 0)),
                  pl.BlockSpec((None, epb, ff, d), lambda i, j: (layer, j, 0, 0)),
                  pl.BlockSpec(sg.shape, lambda i, j: (0, 0), pipeline_mode=once),
                  pl.BlockSpec(su.shape, lambda i, j: (0, 0), pipeline_mode=once),
                  pl.BlockSpec(sd.shape, lambda i, j: (0, 0), pipeline_mode=once)],
        out_specs=pl.BlockSpec((tile * s_n, LANE), lambda i, j: (i, 0)),
        scratch_shapes=[pltpu.SMEM((rows.shape[1],), jnp.int32),
                        pltpu.SMEM((wts.shape[1],), F32),
                        pltpu.SMEM((starts.shape[1],), jnp.int32),
                        pltpu.SemaphoreType.DMA((3,)),
                        pltpu.VMEM((epb, MOE_BLOCK * s_n, LANE), F32),
                        pltpu.VMEM((epb, MOE_BLOCK * s_n, LANE), F32)],
        compiler_params=_params("parallel", "arbitrary", vmem=VMEM_LIMIT_MOE),
        name="moe_experts",
    )(rows, wts, starts, h, wg, wu, wd, sg, su, sd)


def _moe_finish_kernel(f_ref, x_ref, tab_ref, pg_ref, o_ref):
    f = _load_rows(f_ref, 0, x_ref.shape[0])
    o_ref[...] = _gated_residual(x_ref[...], _rms(f, pg_ref[...]), tab_ref, 5)


def _moe_finish(f, x, tab, post_gain, tm, x_tile0, tab0):
    d = x.shape[1]
    s_n = CHUNKS_PER_ROW
    nsub = tm // ROW_TILE
    ntiles = f.shape[0] // (tm * s_n)
    return pl.pallas_call(
        _moe_finish_kernel,
        out_shape=jax.ShapeDtypeStruct((ntiles * tm, d), F32),
        grid=(ntiles,),
        in_specs=[pl.BlockSpec((tm * s_n, LANE), lambda i: (i, 0)),
                  pl.BlockSpec((tm, d), lambda i: (x_tile0 + i, 0)),
                  pl.BlockSpec((nsub, 8, d), lambda i: (tab0 + i, 0, 0)),
                  pl.BlockSpec((1, d), lambda i: (0, 0))],
        out_specs=pl.BlockSpec((tm, d), lambda i: (i, 0)),
        compiler_params=_params("parallel"),
        name="moe_norm_residual",
    )(f, x, tab, post_gain)


def _grid_permutation(band):
    tm = band * GRID_W
    dst = jnp.arange(tm)
    src = (dst % band) * GRID_W + dst // band
    return (src[:, None] == jnp.arange(tm)[None, :]).astype(BF16)


def kernel(x, c, ctx, c_ctx, mod_w, mod_b, pre_gain, post_gain, ab_w_in, ab_conv_w, ab_conv_b, ab_lru_wa, ab_lru_ba, ab_lru_wi, ab_lru_bi, ab_lru_lambda, ab_gla_wa2, ab_gla_ba, ab_w_out, c_w_in, hgrn_lb_logits, c_w_out, moe_router, moe_bias, moe_w_gate, moe_w_up, moe_w_down, moe_ws_gate, moe_ws_up, moe_ws_down):
    batch, seq, d = x.shape
    n_ctx = ctx.shape[1]
    depth = mod_w.shape[0]
    band = SUBLANE
    tm_grid = band * GRID_W
    rows_img = seq // GRID_W
    assert n_ctx == ROW_TILE and seq % (2 * ROW_TILE) == 0 and d % LANE == 0
    assert depth == 2 and batch + 1 <= 8 and rows_img % band == 0 and tm_grid % ROW_TILE == 0
    n_lat = batch * seq
    n_ctx_tok = batch * n_ctx
    ntok = n_lat + n_ctx_tok

    xs = jnp.concatenate([x.reshape(n_lat, d), ctx.reshape(n_ctx_tok, d)], axis=0)

    cvec = jnp.concatenate([c, c_ctx[None, :], jnp.zeros((8 - batch - 1, d), F32)], axis=0)
    mods = _modulation(cvec, mod_w, mod_b).reshape(depth, 8, N_MOD, d)
    tile_row = jnp.concatenate([jnp.repeat(jnp.arange(batch), seq // ROW_TILE),
                                jnp.full((batch,), batch)]).astype(jnp.int32)
    tabs = jnp.pad(mods[:, tile_row], ((0, 0), (0, 0), (0, 8 - N_MOD), (0, 0)))

    def moe(layer, h2, routed, xs_in, t0, ntiles, tile):
        plan = _dispatch_plan(*routed, t0, tile, ntiles)
        f = _moe(h2, plan, layer, moe_w_gate, moe_w_up, moe_w_down,
                 moe_ws_gate[layer].astype(BF16), moe_ws_up[layer].astype(BF16),
                 moe_ws_down[layer].astype(BF16), t0 // tile, ntiles, tile, epb)
        return _moe_finish(f, xs_in, tabs[layer], post_gain[layer, 1][None, :], tm_out,
                           t0 // tm_out, t0 // tm_out)

    tm_proj, tm_out, tm_route, tm_moe, epb = 512, 512, 512, 2048, 2
    assert n_lat % tm_moe == 0 and n_lat % n_ctx_tok == 0

    ab_cols = ab_w_in.shape[2]
    gla_dk = ab_gla_wa2.shape[3]
    lru_w = d
    gla_dv = (ab_cols - 2 * lru_w - 2 * gla_dk - 2 * GLA_RANK) // 2
    tn0 = 1792
    n0 = -(-(ab_cols) // tn0) * tn0
    w_in0 = jnp.pad(ab_w_in[0], ((0, 0), (0, n0 - ab_cols))).astype(BF16)
    u0 = _in_proj(xs, tabs[0], pre_gain[0, 0][None, :], w_in0, tm_proj, tn0, 0)

    cw = jnp.pad(ab_conv_w[0], ((0, 8 - CONV_W), (0, 0)))
    cb = ab_conv_b[0][None, :]
    wg = jnp.concatenate([ab_lru_wa[0], ab_lru_wi[0]], axis=-1).astype(BF16)
    bg = jnp.concatenate([ab_lru_ba[0].reshape(2, LRU_HEADS, 1, LANE),
                          ab_lru_bi[0].reshape(2, LRU_HEADS, 1, LANE)], axis=-1)
    ya_c, ya_l = _lru(u0, (cw, cb, wg, bg, ab_lru_lambda[0]), batch=batch, seq=seq, n_ctx=n_ctx,
                      g_col0=0, x_col0=lru_w // LANE)

    q0 = 2 * lru_w // LANE
    k0 = q0 + gla_dk // LANE
    v0 = k0 + gla_dk // LANE
    gate0 = v0 + gla_dv // LANE
    a0 = gate0 + gla_dv // LANE
    wa = jnp.stack([jnp.zeros((LANE, gla_dk), F32).at[dr * GLA_RANK:(dr + 1) * GLA_RANK]
                    .set(ab_gla_wa2[0, dr]) for dr in range(2)]).astype(BF16)
    ba = ab_gla_ba[0][:, None, :]
    o_gla = _gla(u0, wa, ba, batch=batch, seq=seq, n_ctx=n_ctx, cols=(q0, k0, v0, a0),
                 dv_total=gla_dv)

    lat_tiles = n_lat // tm_out
    ctx_tiles = n_ctx_tok // tm_out
    lead_specs = [pl.BlockSpec((tm_out, lru_w), lambda i: (jnp.minimum(i, lat_tiles - 1), 0)),
                  pl.BlockSpec((tm_out, lru_w),
                               lambda i: (jnp.clip(i - lat_tiles, 0, ctx_tiles - 1), 0)),
                  pl.BlockSpec((ROW_TILE, ROW_TILE), lambda i: (0, 0)),
                  pl.BlockSpec((None, tm_out, gla_dv), lambda i: (0, i, 0)),
                  pl.BlockSpec((None, tm_out, gla_dv), lambda i: (1, i, 0)),
                  pl.BlockSpec((tm_out, gla_dv), lambda i: (i, gate0 * LANE // gla_dv))]
    xs, h2, lg = _out_proj(functools.partial(_out_proj_ab_kernel, lat_tiles=lat_tiles),
                           (ya_l, ya_c, _time_permutation().T.astype(BF16), o_gla, o_gla, u0),
                           lead_specs,
                           ab_w_out[0].astype(BF16), xs, tabs[0], post_gain[0, 0][None, :],
                           pre_gain[0, 1][None, :], moe_router[0].T, tm_out, ntok // tm_out)
    cwt = _route(lg, moe_bias[0], tm_route)
    xs_ctx = moe(0, h2, cwt, xs, n_lat, 1, n_ctx_tok)
    xs = moe(0, h2, cwt, xs, 0, n_lat // tm_moe, tm_moe)

    hd = c_w_out.shape[1]
    tn1 = 1280
    w_in1 = c_w_in[0].astype(BF16)
    c_cols = w_in1.shape[1]
    nsec = c_cols // hd
    perm = _grid_permutation(band)
    gain1 = pre_gain[1, 0][None, :]
    u1l = _in_proj_grid(xs, tabs[1], gain1, perm, w_in1, tn1, n_lat // tm_grid, band,
                        rows_img // band)
    u1c = _in_proj(xs_ctx, tabs[1], gain1, w_in1, tm_proj, tn1, n_lat // tm_proj)
    lb_sm = jax.nn.softmax(hgrn_lb_logits.astype(F32), axis=0)
    lb_cum = jnp.cumsum(lb_sm, axis=0)
    lb = (lb_cum[1] - lb_cum[0])[None, :]
    ub = 1.0 - lb
    dh = hd // HGRN_HEADS
    s_zero = jnp.zeros((2, batch, HGRN_HEADS, dh, dh), F32)
    u1l_flat = u1l.reshape(n_lat, c_cols)
    (s_c,) = _hgrn_segment(u1c, s_zero, lb, ub, batch=batch, seg_len=n_ctx, emit_o=False)
    o_hgrn, _ = _hgrn_segment(u1l_flat, s_c, lb, ub, batch=batch, seg_len=seq, emit_o=True)
    o_hgrn = o_hgrn.reshape(2, n_lat // rows_img, rows_img // band, band, hd)

    bpi = rows_img // band
    grid_blk = (GRID_W, None, band, hd)
    lead_specs = [pl.BlockSpec((None,) + grid_blk, lambda i: (0, i // bpi, i % bpi, 0, 0)),
                  pl.BlockSpec((None,) + grid_blk, lambda i: (1, i // bpi, i % bpi, 0, 0)),
                  pl.BlockSpec(grid_blk, lambda i: (i // bpi, i % bpi, 0, nsec - 1)),
                  pl.BlockSpec((tm_grid, tm_grid), lambda i: (0, 0))]
    xs, h2, lg = _out_proj(_out_proj_c_kernel, (o_hgrn, o_hgrn, u1l, perm.T), lead_specs,
                           c_w_out[0].astype(BF16), xs, tabs[1], post_gain[1, 0][None, :],
                           pre_gain[1, 1][None, :], moe_router[1].T, tm_grid, n_lat // tm_grid)
    out = moe(1, h2, _route(lg, moe_bias[1], tm_route), xs, 0, n_lat // tm_moe, tm_moe)
    return out.reshape(batch, seq, d)
```

```python
import functools

import jax
import jax.numpy as jnp
from jax import lax
from jax.experimental import pallas as pl
from jax.experimental.pallas import tpu as pltpu

F32 = jnp.float32
BF16 = jnp.bfloat16

EPS = 1e-6
N_MOD = 6
ROW_TILE = 256
CHUNK = 64
CHUNK_SHIFT = 6
GRID_W = 64
LRU_HEADS = 8
LRU_C = 8.0
CONV_W = 4
GLA_HEADS = 4
GLA_RANK = 16
GLA_TAU = 16.0
HGRN_HEADS = 8
N_EXPERTS = 64
TOP_K = 8
N_GROUPS = 8
TOPK_GROUPS = 4
ROUTED_SCALE = 2.5
LANE = 128
SUBLANE = 8
BF16_ROWS = 16
CHUNKS_PER_ROW = 8
MOE_BLOCK = 128
VMEM_LIMIT = 48 * 1024 * 1024
VMEM_LIMIT_MOE = 58 * 1024 * 1024

_NT = (((1,), (1,)), ((), ()))
_TN = (((0,), (0,)), ((), ()))


def _params(*sem, vmem=VMEM_LIMIT):
    return pltpu.CompilerParams(dimension_semantics=sem, vmem_limit_bytes=vmem)


def _rms(x, gain):
    ms = jnp.mean(x * x, axis=-1, keepdims=True)
    return x * lax.rsqrt(ms + EPS) * gain


def _sigmoid(x):
    return jax.nn.sigmoid(x)


def _silu(x):
    return x * _sigmoid(x)


def _log_sigmoid(x):
    return jnp.minimum(x, 0.0) - jnp.log1p(jnp.exp(-jnp.abs(x)))


def _softplus(x):
    return jnp.maximum(x, 0.0) + jnp.log1p(jnp.exp(-jnp.abs(x)))


def _modulate(xn, tab_ref, shift_row, scale_row):
    parts = []
    for s in range(tab_ref.shape[0]):
        t = tab_ref[s]
        rows = xn[s * ROW_TILE:(s + 1) * ROW_TILE]
        parts.append(rows * (1.0 + t[scale_row:scale_row + 1]) + t[shift_row:shift_row + 1])
    return parts[0] if len(parts) == 1 else jnp.concatenate(parts, axis=0)


def _gated_residual(x, branch, tab_ref, gate_row):
    parts = []
    for s in range(tab_ref.shape[0]):
        sl = slice(s * ROW_TILE, (s + 1) * ROW_TILE)
        parts.append(x[sl] + tab_ref[s][gate_row:gate_row + 1] * branch[sl])
    return parts[0] if len(parts) == 1 else jnp.concatenate(parts, axis=0)


def _mod_kernel(c_ref, w_ref, b_ref, o_ref):
    s = _silu(c_ref[...])
    o_ref[0] = jnp.dot(s, w_ref[0], precision=lax.Precision.HIGHEST,
                       preferred_element_type=F32) + b_ref[0]


def _modulation(cvec, mod_w, mod_b):
    depth, d, n = mod_w.shape
    tn = 1024
    return pl.pallas_call(
        _mod_kernel,
        out_shape=jax.ShapeDtypeStruct((depth, 8, n), F32),
        grid=(depth, n // tn),
        in_specs=[pl.BlockSpec((8, d), lambda l, j: (0, 0)),
                  pl.BlockSpec((1, d, tn), lambda l, j: (l, 0, j)),
                  pl.BlockSpec((1, 1, tn), lambda l, j: (l, 0, j))],
        out_specs=pl.BlockSpec((1, 8, tn), lambda l, j: (l, 0, j)),
        compiler_params=_params("parallel", "parallel"),
        name="adaln_modulation",
    )(cvec, mod_w, mod_b.reshape(depth, 1, n))


def _in_proj_kernel(x_ref, tab_ref, gain_ref, w_ref, o_ref, *, tn):
    h = _modulate(_rms(x_ref[...], gain_ref[...]), tab_ref, 0, 1).astype(BF16)
    for j in range(w_ref.shape[1] // tn):
        cs = slice(j * tn, (j + 1) * tn)
        o_ref[:, cs] = jnp.dot(h, w_ref[:, cs], preferred_element_type=F32).astype(o_ref.dtype)


def _in_proj(x, tab, gain, w, tm, tn, tab0):
    d = x.shape[1]
    n = w.shape[1]
    nsub = tm // ROW_TILE
    ntiles = x.shape[0] // tm
    return pl.pallas_call(
        functools.partial(_in_proj_kernel, tn=tn),
        out_shape=jax.ShapeDtypeStruct((ntiles * tm, n), BF16),
        grid=(ntiles,),
        in_specs=[pl.BlockSpec((tm, d), lambda i: (i, 0)),
                  pl.BlockSpec((nsub, 8, d), lambda i: (tab0 + i, 0, 0)),
                  pl.BlockSpec((1, d), lambda i: (0, 0)),
                  pl.BlockSpec((d, n), lambda i: (0, 0), pipeline_mode=pl.Buffered(1))],
        out_specs=pl.BlockSpec((tm, n), lambda i: (i, 0)),
        compiler_params=_params("parallel"),
        name="norm_mod_in_proj",
    )(x, tab, gain, w)


def _in_proj_grid_kernel(x_ref, tab_ref, gain_ref, p_ref, w_ref, o_ref, *, tn):
    h = _modulate(_rms(x_ref[...], gain_ref[...]), tab_ref, 0, 1).astype(BF16)
    hp = jnp.dot(p_ref[...], h, preferred_element_type=F32).astype(BF16)
    for j in range(w_ref.shape[1] // tn):
        cs = slice(j * tn, (j + 1) * tn)
        o = jnp.dot(hp, w_ref[:, cs], preferred_element_type=F32).astype(o_ref.dtype)
        o_ref[:, :, cs] = o.reshape(o_ref.shape[:2] + (tn,))


def _in_proj_grid(x, tab, gain, perm, w, tn, ntiles, band, bands_per_image):
    d = x.shape[1]
    n = w.shape[1]
    tm = band * GRID_W
    nsub = tm // ROW_TILE
    images = ntiles // bands_per_image
    once = pl.Buffered(1)
    return pl.pallas_call(
        functools.partial(_in_proj_grid_kernel, tn=tn),
        out_shape=jax.ShapeDtypeStruct((images * GRID_W, bands_per_image, band, n), F32),
        grid=(ntiles,),
        in_specs=[pl.BlockSpec((tm, d), lambda i: (i, 0)),
                  pl.BlockSpec((nsub, 8, d), lambda i: (i, 0, 0)),
                  pl.BlockSpec((1, d), lambda i: (0, 0)),
                  pl.BlockSpec((tm, tm), lambda i: (0, 0), pipeline_mode=once),
                  pl.BlockSpec((d, n), lambda i: (0, 0), pipeline_mode=once)],
        out_specs=pl.BlockSpec((GRID_W, None, band, n),
                               lambda i: (i // bands_per_image, i % bands_per_image, 0, 0)),
        compiler_params=_params("parallel"),
        name="norm_mod_in_proj_grid",
    )(x, tab, gain, perm, w)


def _lru_kernel(gc_ref, xc_ref, gl_ref, xl_ref, p_ref, cw_ref, cb_ref, wg_ref, bg_ref,
                lam_ref, yc_ref, yl_ref, xs_scr, hf_scr):
    T = ROW_TILE
    V = T // SUBLANE
    H = BF16_ROWS
    sub = lax.broadcasted_iota(jnp.int32, (SUBLANE, LANE), 0)
    cw = cw_ref[...]
    cb = cb_ref[...]
    sp = _softplus(-lam_ref[...])

    def vreg(a, v):
        return a[v * SUBLANE:(v + 1) * SUBLANE]

    def shift_down(y, first):
        return jnp.where(sub == 0, first, pltpu.roll(y, 1, 0))

    def shift_up(y, last):
        return jnp.where(sub == SUBLANE - 1, last, pltpu.roll(y, SUBLANE - 1, 0))

    def conv_tile(x_ref, t, nt, seg_len):
        t0 = pl.multiple_of(t * T, T)
        xp = jnp.dot(p_ref[...], x_ref[pl.ds(t0, T), :], preferred_element_type=F32)
        p0 = pl.multiple_of(jnp.maximum(t0 - H, 0), H)
        n0 = pl.multiple_of(jnp.minimum(t0 + T, seg_len - H), H)
        prev = x_ref[pl.ds(p0, H), :].astype(F32) * jnp.where(t > 0, 1.0, 0.0)
        nxt = x_ref[pl.ds(n0, H), :].astype(F32) * jnp.where(t < nt - 1, 1.0, 0.0)
        m1, m2, p1 = prev[H - 1:H], prev[H - 2:H - 1], nxt[0:1]
        last_m1 = shift_down(vreg(xp, V - 1), m1)
        xm1 = jnp.concatenate([last_m1, xp[:T - SUBLANE]], axis=0)
        xm2 = jnp.concatenate([shift_down(vreg(xp, V - 2), m2), last_m1, xp[:T - 2 * SUBLANE]], axis=0)
        xp1 = jnp.concatenate([xp[SUBLANE:], shift_up(vreg(xp, 0), p1)], axis=0)
        return cb + xm2 * cw[0:1] + xm1 * cw[1:2] + xp * cw[2:3] + xp1 * cw[3:4]

    def gates(xc, d):
        z = jnp.dot(xc.astype(BF16), wg_ref[d, 0], preferred_element_type=F32) + bg_ref[d, 0]
        r = _sigmoid(z[:, :LANE])
        i = _sigmoid(z[:, LANE:])
        log_a = (-LRU_C) * r * sp[d:d + 1]
        a = jnp.exp(log_a)
        return a, jnp.sqrt(1.0 - a * a) * (i * xc)

    def scan_tile(a, u, carry, reverse):
        hs, cum = [None] * V, [None] * V
        hp = ap = None
        for v in (range(V - 1, -1, -1) if reverse else range(V)):
            av, uv = vreg(a, v), vreg(u, v)
            hp, ap = (uv, av) if hp is None else (av * hp + uv, av * ap)
            hs[v], cum[v] = hp, ap
        s = 1
        while s < SUBLANE:
            keep = (sub < SUBLANE - s) if reverse else (sub >= s)
            sh = SUBLANE - s if reverse else s
            h_sh = jnp.where(keep, pltpu.roll(hp, sh, 0), 0.0)
            a_sh = jnp.where(keep, pltpu.roll(ap, sh, 0), 1.0)
            hp = hp + ap * h_sh
            ap = ap * a_sh
            s *= 2
        state = hp + ap * carry
        if reverse:
            start, out = shift_up(state, carry), state[0:1]
        else:
            start, out = shift_down(state, carry), state[SUBLANE - 1:SUBLANE]
        h = jnp.concatenate([hs[v] + cum[v] * start for v in range(V)], axis=0)
        return h, out

    def run_segment(g_ref, x_ref, y_ref, seg_len, carry_f, carry_b):
        nt = seg_len // T
        unroll = 4 if nt % 4 == 0 else (2 if nt % 2 == 0 else 1)

        def fwd_body(i, carry):
            for j in range(unroll):
                t = i * unroll + j
                t0 = pl.multiple_of(t * T, T)
                xc = conv_tile(x_ref, t, nt, seg_len)
                xs_scr[pl.ds(t0, T), :] = xc
                a, u = gates(xc, 0)
                h, carry = scan_tile(a, u, carry, False)
                hf_scr[pl.ds(t0, T), :] = h
            return carry

        def bwd_body(i, carry):
            for j in range(unroll):
                t = nt - 1 - (i * unroll + j)
                t0 = pl.multiple_of(t * T, T)
                a, u = gates(xs_scr[pl.ds(t0, T), :], 1)
                h, carry = scan_tile(a, u, carry, True)
                g = jnp.dot(p_ref[...], g_ref[pl.ds(t0, T), :], preferred_element_type=F32)
                rec = hf_scr[pl.ds(t0, T), :] + h
                y_ref[pl.ds(t0, T), :] = (jax.nn.gelu(g) * rec).astype(y_ref.dtype)
            return carry

        steps = nt // unroll
        return (lax.fori_loop(0, steps, fwd_body, carry_f),
                lax.fori_loop(0, steps, bwd_body, carry_b))

    zero = jnp.zeros((1, LANE), F32)
    hf, hb = run_segment(gc_ref, xc_ref, yc_ref, xc_ref.shape[0], zero, zero)
    run_segment(gl_ref, xl_ref, yl_ref, xl_ref.shape[0], hf, hb)


def _time_permutation():
    v = ROW_TILE // SUBLANE
    dst = jnp.arange(ROW_TILE)
    src = (dst % SUBLANE) * v + dst // SUBLANE
    return src[:, None] == jnp.arange(ROW_TILE)[None, :]


def _lru(u, lru_w, *, batch, seq, n_ctx, g_col0, x_col0):
    width = LRU_HEADS * LANE
    cw, cb, wg, bg, lam = lru_w
    ctx_blk0 = batch * seq // n_ctx
    perm = _time_permutation()
    once = pl.Buffered(1)
    return pl.pallas_call(
        _lru_kernel,
        out_shape=(jax.ShapeDtypeStruct((batch * n_ctx, width), BF16),
                   jax.ShapeDtypeStruct((batch * seq, width), BF16)),
        grid=(batch, LRU_HEADS),
        in_specs=[pl.BlockSpec((n_ctx, LANE), lambda b, h: (ctx_blk0 + b, g_col0 + h)),
                  pl.BlockSpec((n_ctx, LANE), lambda b, h: (ctx_blk0 + b, x_col0 + h)),
                  pl.BlockSpec((seq, LANE), lambda b, h: (b, g_col0 + h)),
                  pl.BlockSpec((seq, LANE), lambda b, h: (b, x_col0 + h)),
                  pl.BlockSpec((ROW_TILE, ROW_TILE), lambda b, h: (0, 0), pipeline_mode=once),
                  pl.BlockSpec((8, LANE), lambda b, h: (0, h)),
                  pl.BlockSpec((1, LANE), lambda b, h: (0, h)),
                  pl.BlockSpec((2, 1, LANE, 2 * LANE), lambda b, h: (0, h, 0, 0)),
                  pl.BlockSpec((2, 1, 1, 2 * LANE), lambda b, h: (0, h, 0, 0)),
                  pl.BlockSpec((2, LANE), lambda b, h: (0, h))],
        out_specs=(pl.BlockSpec((n_ctx, LANE), lambda b, h: (b, h)),
                   pl.BlockSpec((seq, LANE), lambda b, h: (b, h))),
        scratch_shapes=[pltpu.VMEM((seq, LANE), F32), pltpu.VMEM((seq, LANE), F32)],
        compiler_params=_params("parallel", "parallel"),
        name="rglru",
    )(u, u, u, u, perm.astype(BF16), cw, cb, wg, bg, lam)


def _chunk_mask(rows, reverse):
    r = lax.broadcasted_iota(jnp.int32, (rows, rows), 0)
    c = lax.broadcasted_iota(jnp.int32, (rows, rows), 1)
    same = (r >> CHUNK_SHIFT) == (c >> CHUNK_SHIFT)
    tri = (c >= r) if reverse else (c <= r)
    return jnp.where(same, jnp.where(tri, 1.0, 0.0), 0.0)


def _split3(x):
    hi = x.astype(BF16)
    r1 = x - hi.astype(F32)
    mid = r1.astype(BF16)
    lo = (r1 - mid.astype(F32)).astype(BF16)
    return hi, mid, lo


def _block_scan(g_all, head_qkv, st_scr, o_ref, *, heads, dv, reverse, batched):
    rows, wall = g_all.shape
    width = wall // heads
    nchunk = rows // CHUNK
    mask = _chunk_mask(rows, reverse)
    mask_b = mask.astype(BF16)
    keep = mask > 0.5
    keep_c = keep[:CHUNK, :CHUNK]
    iref = CHUNK // 2 if reverse else CHUNK // 2 - 1
    ilast = 0 if reverse else CHUNK - 1
    order = range(nchunk - 1, -1, -1) if reverse else range(nchunk)

    def per_chunk(rows_1, width):
        return jnp.concatenate([jnp.broadcast_to(r, (CHUNK, width)) for r in rows_1], axis=0)

    pieces = jnp.concatenate(_split3(g_all), axis=1)
    sums = jnp.dot(mask_b, pieces, preferred_element_type=F32)
    b_all = sums[:, :wall] + sums[:, wall:2 * wall] + sums[:, 2 * wall:]

    def prepare(h):
        q, k, v = head_qkv(h)
        b = b_all[:, h * width:(h + 1) * width]
        b_ref = [b[c * CHUNK + iref:c * CHUNK + iref + 1] for c in range(nchunk)]
        b_last = [b[c * CHUNK + ilast:c * CHUNK + ilast + 1] for c in range(nchunk)]
        e1 = jnp.exp(b - per_chunk(b_ref, width))
        r1 = 1.0 / e1
        qe = q * e1
        ke = k * r1
        qd = (qe * per_chunk([jnp.exp(r) for r in b_ref], width)).astype(BF16)
        kd = (ke * per_chunk([jnp.exp(l - r) for l, r in zip(b_last, b_ref)], width)).astype(BF16)
        decays = [jnp.exp(l) for l in b_last]
        return qe.astype(BF16), ke.astype(BF16), qd, kd, v.astype(BF16), decays

    nxt = prepare(0)
    for h in range(heads):
        qe, ke, qd, kd, vb, decays = nxt
        if h + 1 < heads:
            nxt = prepare(h + 1)
        if batched:
            sc = lax.dot_general(qe, ke, _NT, preferred_element_type=F32)
            sc = jnp.where(keep, sc, 0.0).astype(BF16)
            o = jnp.dot(sc, vb, preferred_element_type=F32)
        st = st_scr[h]
        parts = [None] * nchunk
        for ci in order:
            rs = slice(ci * CHUNK, (ci + 1) * CHUNK)
            if batched:
                o_c = o[rs]
            else:
                sc = lax.dot_general(qe[rs], ke[rs], _NT, preferred_element_type=F32)
                sc = jnp.where(keep_c, sc, 0.0).astype(BF16)
                o_c = jnp.dot(sc, vb[rs], preferred_element_type=F32)
            parts[ci] = o_c + lax.dot_general(qd[rs], st.astype(BF16), _NT,
                                              preferred_element_type=F32)
            st = st * decays[ci] + lax.dot_general(vb[rs], kd[rs], _TN, preferred_element_type=F32)
        if o_ref is not None:
            o_ref[:, h * dv:(h + 1) * dv] = jnp.concatenate(parts, axis=0).astype(o_ref.dtype)
        st_scr[h] = st


def _gla_kernel(q_ref, k_ref, v_ref, a_ref, wa_ref, ba_ref, o_ref, st_scr):
    t = pl.program_id(1)
    d = pl.program_id(2)
    dk = q_ref.shape[1] // GLA_HEADS
    dv = v_ref.shape[1] // GLA_HEADS

    @pl.when((t == 0) & (d == 0))
    def _():
        st_scr[...] = jnp.zeros(st_scr.shape, F32)

    def head_qkv(h):
        ks = slice(h * dk, (h + 1) * dk)
        q = q_ref[:, ks].astype(F32) * (dk ** -0.5)
        return q, k_ref[:, ks].astype(F32), v_ref[:, h * dv:(h + 1) * dv]

    for dd in range(2):
        @pl.when(d == dd)
        def _(dd=dd):
            z = jnp.dot(a_ref[...], wa_ref[dd], preferred_element_type=F32) + ba_ref[dd]
            g_all = _log_sigmoid(z) * (1.0 / GLA_TAU)
            _block_scan(g_all, head_qkv, st_scr.at[dd], o_ref,
                        heads=GLA_HEADS, dv=dv, reverse=dd == 1, batched=True)


def _gla(u, wa, ba, *, batch, seq, n_ctx, cols, dv_total):
    ntok = u.shape[0]
    q0, k0, v0, a0 = cols
    dkt = wa.shape[2]
    dvt = dv_total
    nblk = seq // ROW_TILE
    ctx_blk0 = batch * nblk
    assert n_ctx == ROW_TILE

    def blk(b, t, d):
        lat = b * nblk + jnp.where(d == 0, t - 1, nblk - t)
        return jnp.where(t == 0, ctx_blk0 + b, lat)

    in_specs = [pl.BlockSpec((ROW_TILE, dkt), lambda b, t, d: (blk(b, t, d), q0 * LANE // dkt)),
                pl.BlockSpec((ROW_TILE, dkt), lambda b, t, d: (blk(b, t, d), k0 * LANE // dkt)),
                pl.BlockSpec((ROW_TILE, dvt), lambda b, t, d: (blk(b, t, d), v0 * LANE // dvt)),
                pl.BlockSpec((ROW_TILE, LANE), lambda b, t, d: (blk(b, t, d), a0)),
                pl.BlockSpec(wa.shape, lambda b, t, d: (0, 0, 0)),
                pl.BlockSpec(ba.shape, lambda b, t, d: (0, 0, 0))]
    return pl.pallas_call(
        _gla_kernel,
        out_shape=jax.ShapeDtypeStruct((2, ntok, dvt), BF16),
        grid=(batch, nblk + 1, 2),
        in_specs=in_specs,
        out_specs=pl.BlockSpec((None, ROW_TILE, dvt), lambda b, t, d: (d, blk(b, t, d), 0)),
        scratch_shapes=[pltpu.VMEM((2, GLA_HEADS, dvt // GLA_HEADS, dkt // GLA_HEADS), F32)],
        compiler_params=_params("parallel", "arbitrary", "arbitrary"),
        name="gla",
    )(u, u, u, u, wa, ba)


def _hgrn_kernel(*refs, emit_o):
    q_ref, z_ref, v_ref, lb_ref, ub_ref, s0_ref = refs[:6]
    if emit_o:
        o_ref, st_ref, st_scr = refs[6:]
    else:
        st_ref, st_scr = refs[6:]
        o_ref = None
    t = pl.program_id(1)
    d = pl.program_id(2)
    dh = q_ref.shape[1] // HGRN_HEADS

    @pl.when((t == 0) & (d == 0))
    def _():
        st_scr[...] = s0_ref[:, 0]

    lb = lb_ref[...]
    ub = ub_ref[...]

    for dd in range(2):
        @pl.when(d == dd)
        def _(dd=dd):
            z = z_ref[...].astype(F32)
            e = jnp.exp(-jnp.abs(z))
            r = 1.0 / (1.0 + e)
            er = e * r
            pos = z >= 0.0
            log_f = jnp.log(lb + ub * jnp.where(pos, r, er))
            k_all = ub * jnp.where(pos, er, r)

            def head_qkv(h):
                hs = slice(h * dh, (h + 1) * dh)
                return q_ref[:, hs].astype(F32), k_all[:, hs], v_ref[:, hs]

            _block_scan(log_f, head_qkv, st_scr.at[dd], o_ref,
                        heads=HGRN_HEADS, dv=dh, reverse=dd == 1, batched=True)

    @pl.when((t == pl.num_programs(1) - 1) & (d == 1))
    def _():
        st_ref[:, 0] = st_scr[...]


def _hgrn_segment(u, s0, lb, ub, *, batch, seg_len, emit_o):
    ntok = u.shape[0]
    width = lb.shape[1]
    nblk = seg_len // ROW_TILE

    def blk(b, t, d):
        return b * nblk + jnp.where(d == 0, t, nblk - 1 - t)

    st_spec = pl.BlockSpec((2, 1) + s0.shape[2:], lambda b, t, d: (0, b, 0, 0, 0))
    in_specs = [pl.BlockSpec((ROW_TILE, width), lambda b, t, d: (blk(b, t, d), 0)),
                pl.BlockSpec((ROW_TILE, width), lambda b, t, d: (blk(b, t, d), 1 + d)),
                pl.BlockSpec((ROW_TILE, width), lambda b, t, d: (blk(b, t, d), 3)),
                pl.BlockSpec((1, width), lambda b, t, d: (0, 0)),
                pl.BlockSpec((1, width), lambda b, t, d: (0, 0)),
                st_spec]
    st_shape = jax.ShapeDtypeStruct(s0.shape, F32)
    if emit_o:
        out_shape = (jax.ShapeDtypeStruct((2, ntok, width), F32), st_shape)
        out_specs = (pl.BlockSpec((None, ROW_TILE, width), lambda b, t, d: (d, blk(b, t, d), 0)),
                     st_spec)
    else:
        out_shape = (st_shape,)
        out_specs = (st_spec,)
    return pl.pallas_call(
        functools.partial(_hgrn_kernel, emit_o=emit_o),
        out_shape=out_shape,
        grid=(batch, nblk, 2),
        in_specs=in_specs,
        out_specs=out_specs,
        scratch_shapes=[pltpu.VMEM((2,) + s0.shape[2:], F32)],
        compiler_params=_params("parallel", "arbitrary", "arbitrary"),
        name="hgrn_segment",
    )(u, u, u, lb, ub, s0)


def _head_norm_gate(o, gate, heads):
    dh = o.shape[1] // heads
    parts = []
    for h in range(heads):
        oh = o[:, h * dh:(h + 1) * dh]
        ms = jnp.mean(oh * oh, axis=-1, keepdims=True)
        parts.append((oh * lax.rsqrt(ms + EPS) * _silu(gate[:, h * dh:(h + 1) * dh])).astype(BF16))
    return jnp.concatenate(parts, axis=1)


def _load_rows(ref, r0, nrows):
    s_n = CHUNKS_PER_ROW
    return jnp.concatenate([ref[pl.ds(r0 * s_n + s, nrows, stride=s_n), :] for s in range(s_n)],
                           axis=1)


def _store_rows(ref, r0, val):
    s_n = CHUNKS_PER_ROW
    for s in range(s_n):
        ref[pl.ds(r0 * s_n + s, val.shape[0], stride=s_n), :] = val[:, s * LANE:(s + 1) * LANE]


def _out_proj_tail(o, x_ref, tab_ref, pg_ref, ng_ref, rt_ref, xo_ref, h_ref, lg_ref):
    xn = _gated_residual(x_ref[...], _rms(o, pg_ref[...]), tab_ref, 2)
    xo_ref[...] = xn
    h2 = _modulate(_rms(xn, ng_ref[...]), tab_ref, 3, 4)
    _store_rows(h_ref, 0, h2)
    lg_ref[...] = lax.dot_general(rt_ref[...], h2, _NT, precision=lax.Precision.HIGHEST,
                                  preferred_element_type=F32)


def _out_proj_ab_kernel(yl_ref, yc_ref, pt_ref, of_ref, ob_ref, gate_ref, w_ref, x_ref, tab_ref,
                        pg_ref, ng_ref, rt_ref, xo_ref, h_ref, lg_ref, *, lat_tiles):
    o_att = of_ref[...].astype(F32) + ob_ref[...].astype(F32)
    yb = _head_norm_gate(o_att, gate_ref[...].astype(F32), GLA_HEADS)
    ya = jnp.where(pl.program_id(0) < lat_tiles, yl_ref[...], yc_ref[...])
    ya = jnp.concatenate(
        [jnp.dot(pt_ref[...], ya[s:s + ROW_TILE], preferred_element_type=F32).astype(BF16)
         for s in range(0, ya.shape[0], ROW_TILE)], axis=0)
    da = ya.shape[1]
    o = jnp.dot(ya, w_ref[:da], preferred_element_type=F32)
    o = o + jnp.dot(yb, w_ref[da:], preferred_element_type=F32)
    _out_proj_tail(o, x_ref, tab_ref, pg_ref, ng_ref, rt_ref, xo_ref, h_ref, lg_ref)


def _out_proj_c_kernel(of_ref, ob_ref, gate_ref, p_ref, w_ref, x_ref, tab_ref, pg_ref, ng_ref,
                       rt_ref, xo_ref, h_ref, lg_ref):
    tm = x_ref.shape[0]
    width = of_ref.shape[-1]
    o_att = of_ref[...].astype(F32) + ob_ref[...].astype(F32)
    y = _head_norm_gate(o_att.reshape(tm, width), gate_ref[...].astype(F32).reshape(tm, width),
                        HGRN_HEADS)
    y = jnp.dot(p_ref[...], y, preferred_element_type=F32).astype(BF16)
    o = jnp.dot(y, w_ref[...], preferred_element_type=F32)
    _out_proj_tail(o, x_ref, tab_ref, pg_ref, ng_ref, rt_ref, xo_ref, h_ref, lg_ref)


def _out_proj(kernel, lead, lead_specs, w, x, tab, post_gain, next_gain, router_t, tm, ntiles):
    d = x.shape[1]
    ntok = ntiles * tm
    nsub = tm // ROW_TILE
    ne = router_t.shape[0]
    in_specs = list(lead_specs) + [
        pl.BlockSpec(w.shape, lambda i: (0, 0), pipeline_mode=pl.Buffered(1)),
        pl.BlockSpec((tm, d), lambda i: (i, 0)),
        pl.BlockSpec((nsub, 8, d), lambda i: (i, 0, 0)),
        pl.BlockSpec((1, d), lambda i: (0, 0)),
        pl.BlockSpec((1, d), lambda i: (0, 0)),
        pl.BlockSpec((ne, d), lambda i: (0, 0))]
    return pl.pallas_call(
        kernel,
        out_shape=(jax.ShapeDtypeStruct((ntok, d), F32),
                   jax.ShapeDtypeStruct((ntok * CHUNKS_PER_ROW, LANE), F32),
                   jax.ShapeDtypeStruct((ne, ntok), F32)),
        grid=(ntiles,),
        in_specs=in_specs,
        out_specs=(pl.BlockSpec((tm, d), lambda i: (i, 0)),
                   pl.BlockSpec((tm * CHUNKS_PER_ROW, LANE), lambda i: (i, 0)),
                   pl.BlockSpec((ne, tm), lambda i: (0, i))),
        compiler_params=_params("parallel"),
        name="out_proj_norm_residual",
    )(*lead, w, x, tab, post_gain, next_gain, router_t)


def _route_kernel(lg_ref, bias_ref, ids_ref, w_ref):
    tm = lg_ref.shape[1]
    per_group = N_EXPERTS // N_GROUPS
    shape3 = (N_GROUPS, per_group, tm)
    aff = _sigmoid(lg_ref[...]).reshape(shape3)
    biased = aff + bias_ref[...].reshape(shape3)
    neg = -jnp.inf
    sub = lax.broadcasted_iota(jnp.int32, shape3, 1)
    grp = lax.broadcasted_iota(jnp.int32, shape3, 0)
    m1 = jnp.max(biased, axis=1, keepdims=True)
    i1 = jnp.min(jnp.where(biased == m1, sub, per_group), axis=1, keepdims=True)
    m2 = jnp.max(jnp.where(sub == i1, neg, biased), axis=1, keepdims=True)
    score = m1 + m2
    gidx = lax.broadcasted_iota(jnp.int32, score.shape, 0)
    keep = jnp.zeros(score.shape, F32)
    for _ in range(TOPK_GROUPS):
        m = jnp.max(score, axis=0, keepdims=True)
        im = jnp.min(jnp.where(score == m, gidx, N_GROUPS), axis=0, keepdims=True)
        sel = gidx == im
        keep = jnp.where(sel, 1.0, keep)
        score = jnp.where(sel, neg, score)
    work = jnp.where(jnp.broadcast_to(keep, shape3) > 0.0, biased, neg)
    eidx = grp * per_group + sub
    ids, gates = [], []
    for _ in range(TOP_K):
        m = jnp.max(jnp.max(work, axis=0, keepdims=True), axis=1, keepdims=True)
        cand = jnp.where(work == m, eidx, N_EXPERTS)
        im = jnp.min(jnp.min(cand, axis=0, keepdims=True), axis=1, keepdims=True)
        sel = eidx == im
        ids.append(im)
        gates.append(jnp.sum(jnp.sum(jnp.where(sel, aff, 0.0), axis=0, keepdims=True),
                             axis=1, keepdims=True))
        work = jnp.where(sel, neg, work)
    gate = jnp.concatenate(gates, axis=1)
    den = jnp.sum(gate, axis=1, keepdims=True)
    w_ref[...] = (ROUTED_SCALE * gate / den).reshape(TOP_K, tm)
    ids_ref[...] = jnp.concatenate(ids, axis=1).reshape(TOP_K, tm)


def _route(logits_t, bias, tm):
    ne, ntok = logits_t.shape
    bias_b = jnp.broadcast_to(bias.astype(F32)[:, None], (ne, tm))
    return pl.pallas_call(
        _route_kernel,
        out_shape=(jax.ShapeDtypeStruct((TOP_K, ntok), jnp.int32),
                   jax.ShapeDtypeStruct((TOP_K, ntok), F32)),
        grid=(ntok // tm,),
        in_specs=[pl.BlockSpec((ne, tm), lambda i: (0, i)),
                  pl.BlockSpec((ne, tm), lambda i: (0, 0))],
        out_specs=(pl.BlockSpec((TOP_K, tm), lambda i: (0, i)),
                   pl.BlockSpec((TOP_K, tm), lambda i: (0, i))),
        compiler_params=_params("parallel"),
        name="moe_route",
    )(logits_t, bias_b)


LIST_PAD = 1024


def _dispatch_plan(ids, w, t0, tile, ntiles):
    n = ntiles * tile
    e = ids[:, t0:t0 + n].T.reshape(ntiles, tile * TOP_K)
    g = w[:, t0:t0 + n].T.reshape(ntiles, tile * TOP_K)
    order = jnp.argsort(e, axis=1, stable=True).astype(jnp.int32)
    rows = (order // TOP_K) * CHUNKS_PER_ROW
    wts = jnp.take_along_axis(g, order, axis=1)
    experts = jnp.arange(N_EXPERTS, dtype=jnp.int32)
    counts = jnp.sum((e[:, :, None] == experts).astype(jnp.int32), axis=1)
    starts = jnp.concatenate([jnp.zeros((ntiles, 1), jnp.int32), jnp.cumsum(counts, axis=1)], axis=1)
    rows = jnp.pad(rows, ((0, 0), (0, LIST_PAD)))
    wts = jnp.pad(wts, ((0, 0), (0, LIST_PAD)))
    starts = jnp.pad(starts, ((0, 0), (0, LANE - N_EXPERTS - 1)))
    return rows, wts, starts


def _moe_kernel(rows_hbm, wts_hbm, starts_hbm, h_ref, wg_ref, wu_ref, wd_ref, sg_ref, su_ref,
                sd_ref, o_ref, rows_s, wts_s, starts_s, sem, gbuf, ybuf, *, epb):
    i = pl.program_id(0)
    j = pl.program_id(1)
    s_n = CHUNKS_PER_ROW
    blk = MOE_BLOCK
    tile = h_ref.shape[0] // s_n

    def swiglu(x, w_gate, w_up, w_down):
        g = jnp.dot(x, w_gate, preferred_element_type=F32)
        u = jnp.dot(x, w_up, preferred_element_type=F32)
        return jnp.dot((_silu(g) * u).astype(BF16), w_down, preferred_element_type=F32)

    @pl.when(j == 0)
    def _():
        copies = [pltpu.make_async_copy(rows_hbm.at[i], rows_s, sem.at[0]),
                  pltpu.make_async_copy(wts_hbm.at[i], wts_s, sem.at[1]),
                  pltpu.make_async_copy(starts_hbm.at[i], starts_s, sem.at[2])]
        for c in copies:
            c.start()
        for r0 in range(0, tile, ROW_TILE):
            x = _load_rows(h_ref, r0, ROW_TILE).astype(BF16)
            _store_rows(o_ref, r0, swiglu(x, sg_ref[...], su_ref[...], sd_ref[...]))
        for c in copies:
            c.wait()

    for e in range(epb):
        base = starts_s[j * epb + e]
        count = starts_s[j * epb + e + 1] - base
        w_gate = wg_ref[e].astype(BF16)
        w_up = wu_ref[e].astype(BF16)
        w_down = wd_ref[e].astype(BF16)

        def block(bi, carry, base=base, count=count, w_gate=w_gate, w_up=w_up, w_down=w_down):
            p0 = base + bi * blk
            left = count - bi * blk
            offs = [pl.multiple_of(rows_s[p0 + r], s_n) for r in range(blk)]
            for r in range(blk):
                gbuf[pl.ds(r * s_n, s_n), :] = h_ref[pl.ds(offs[r], s_n), :]
            x = _load_rows(gbuf, 0, blk).astype(BF16)
            y = swiglu(x, w_gate, w_up, w_down)
            y = jnp.where(lax.broadcasted_iota(jnp.int32, y.shape, 0) < left, y, 0.0)
            _store_rows(ybuf, 0, y)
            for r0 in range(0, blk, SUBLANE):
                vals = [o_ref[pl.ds(offs[r], s_n), :] + wts_s[p0 + r] * ybuf[pl.ds(r * s_n, s_n), :]
                        for r in range(r0, r0 + SUBLANE)]
                for r in reversed(range(r0, r0 + SUBLANE)):
                    o_ref[pl.ds(offs[r], s_n), :] = vals[r - r0]
            return carry

        nblk = lax.shift_right_logical(count + (blk - 1), blk.bit_length() - 1)
        lax.fori_loop(0, nblk, block, 0)


def _moe(h, plan, layer, wg, wu, wd, sg, su, sd, tile0, ntiles, tile, epb):
    rows, wts, starts = plan
    s_n = CHUNKS_PER_ROW
    _, ne, d, ff = wg.shape
    once = pl.Buffered(1)
    hbm = pl.BlockSpec(memory_space=pl.ANY)
    return pl.pallas_call(
        functools.partial(_moe_kernel, epb=epb),
        out_shape=jax.ShapeDtypeStruct((ntiles * tile * s_n, LANE), F32),
        grid=(ntiles, ne // epb),
        in_specs=[hbm, hbm, hbm,
                  pl.BlockSpec((tile * s_n, LANE), lambda i, j: (tile0 + i, 0), pipeline_mode=once),
                  pl.BlockSpec((None, epb, d, ff), lambda i, j: (layer, j, 0, 0)),
                  pl.BlockSpec((None, epb, d, ff), lambda i, j: (layer, j, 0, 0)),
                  pl.BlockSpec((None, epb, ff, d), lambda i, j: (layer, j, 0, 0)),
                  pl.BlockSpec(sg.shape, lambda i, j: (0, 0), pipeline_mode=once),
                  pl.BlockSpec(su.shape, lambda i, j: (0, 0), pipeline_mode=once),
                  pl.BlockSpec(sd.shape, lambda i, j: (0, 0), pipeline_mode=once)],
        out_specs=pl.BlockSpec((tile * s_n, LANE), lambda i, j: (i, 0)),
        scratch_shapes=[pltpu.SMEM((rows.shape[1],), jnp.int32),
                        pltpu.SMEM((wts.shape[1],), F32),
                        pltpu.SMEM((starts.shape[1],), jnp.int32),
                        pltpu.SemaphoreType.DMA((3,)),
                        pltpu.VMEM((MOE_BLOCK * s_n, LANE), F32),
                        pltpu.VMEM((MOE_BLOCK * s_n, LANE), F32)],
        compiler_params=_params("parallel", "arbitrary", vmem=VMEM_LIMIT_MOE),
        name="moe_experts",
    )(rows, wts, starts, h, wg, wu, wd, sg, su, sd)


def _moe_finish_kernel(f_ref, x_ref, tab_ref, pg_ref, o_ref):
    f = _load_rows(f_ref, 0, x_ref.shape[0])
    o_ref[...] = _gated_residual(x_ref[...], _rms(f, pg_ref[...]), tab_ref, 5)


def _moe_finish(f, x, tab, post_gain, tm, x_tile0, tab0):
    d = x.shape[1]
    s_n = CHUNKS_PER_ROW
    nsub = tm // ROW_TILE
    ntiles = f.shape[0] // (tm * s_n)
    return pl.pallas_call(
        _moe_finish_kernel,
        out_shape=jax.ShapeDtypeStruct((ntiles * tm, d), F32),
        grid=(ntiles,),
        in_specs=[pl.BlockSpec((tm * s_n, LANE), lambda i: (i, 0)),
                  pl.BlockSpec((tm, d), lambda i: (x_tile0 + i, 0)),
                  pl.BlockSpec((nsub, 8, d), lambda i: (tab0 + i, 0, 0)),
                  pl.BlockSpec((1, d), lambda i: (0, 0))],
        out_specs=pl.BlockSpec((tm, d), lambda i: (i, 0)),
        compiler_params=_params("parallel"),
        name="moe_norm_residual",
    )(f, x, tab, post_gain)


def _grid_permutation(band):
    tm = band * GRID_W
    dst = jnp.arange(tm)
    src = (dst % band) * GRID_W + dst // band
    return (src[:, None] == jnp.arange(tm)[None, :]).astype(BF16)


def kernel(x, c, ctx, c_ctx, mod_w, mod_b, pre_gain, post_gain, ab_w_in, ab_conv_w, ab_conv_b, ab_lru_wa, ab_lru_ba, ab_lru_wi, ab_lru_bi, ab_lru_lambda, ab_gla_wa2, ab_gla_ba, ab_w_out, c_w_in, hgrn_lb_logits, c_w_out, moe_router, moe_bias, moe_w_gate, moe_w_up, moe_w_down, moe_ws_gate, moe_ws_up, moe_ws_down):
    batch, seq, d = x.shape
    n_ctx = ctx.shape[1]
    depth = mod_w.shape[0]
    band = SUBLANE
    tm_grid = band * GRID_W
    rows_img = seq // GRID_W
    assert n_ctx == ROW_TILE and seq % (2 * ROW_TILE) == 0 and d % LANE == 0
    assert depth == 2 and batch + 1 <= 8 and rows_img % band == 0 and tm_grid % ROW_TILE == 0
    n_lat = batch * seq
    n_ctx_tok = batch * n_ctx
    ntok = n_lat + n_ctx_tok

    xs = jnp.concatenate([x.reshape(n_lat, d), ctx.reshape(n_ctx_tok, d)], axis=0)

    cvec = jnp.concatenate([c, c_ctx[None, :], jnp.zeros((8 - batch - 1, d), F32)], axis=0)
    mods = _modulation(cvec, mod_w, mod_b).reshape(depth, 8, N_MOD, d)
    tile_row = jnp.concatenate([jnp.repeat(jnp.arange(batch), seq // ROW_TILE),
                                jnp.full((batch,), batch)]).astype(jnp.int32)
    tabs = jnp.pad(mods[:, tile_row], ((0, 0), (0, 0), (0, 8 - N_MOD), (0, 0)))

    def moe(layer, h2, routed, xs_in, t0, ntiles, tile):
        plan = _dispatch_plan(*routed, t0, tile, ntiles)
        f = _moe(h2, plan, layer, moe_w_gate, moe_w_up, moe_w_down,
                 moe_ws_gate[layer].astype(BF16), moe_ws_up[layer].astype(BF16),
                 moe_ws_down[layer].astype(BF16), t0 // tile, ntiles, tile, epb)
        return _moe_finish(f, xs_in, tabs[layer], post_gain[layer, 1][None, :], tm_out,
                           t0 // tm_out, t0 // tm_out)

    tm_proj, tm_out, tm_route, tm_moe, epb = 512, 512, 512, 2048, 4
    assert n_lat % tm_moe == 0 and n_lat % n_ctx_tok == 0

    ab_cols = ab_w_in.shape[2]
    gla_dk = ab_gla_wa2.shape[3]
    lru_w = d
    gla_dv = (ab_cols - 2 * lru_w - 2 * gla_dk - 2 * GLA_RANK) // 2
    tn0 = 1792
    n0 = -(-(ab_cols) // tn0) * tn0
    w_in0 = jnp.pad(ab_w_in[0], ((0, 0), (0, n0 - ab_cols))).astype(BF16)
    u0 = _in_proj(xs, tabs[0], pre_gain[0, 0][None, :], w_in0, tm_proj, tn0, 0)

    cw = jnp.pad(ab_conv_w[0], ((0, 8 - CONV_W), (0, 0)))
    cb = ab_conv_b[0][None, :]
    wg = jnp.concatenate([ab_lru_wa[0], ab_lru_wi[0]], axis=-1).astype(BF16)
    bg = jnp.concatenate([ab_lru_ba[0].reshape(2, LRU_HEADS, 1, LANE),
                          ab_lru_bi[0].reshape(2, LRU_HEADS, 1, LANE)], axis=-1)
    ya_c, ya_l = _lru(u0, (cw, cb, wg, bg, ab_lru_lambda[0]), batch=batch, seq=seq, n_ctx=n_ctx,
                      g_col0=0, x_col0=lru_w // LANE)

    q0 = 2 * lru_w // LANE
    k0 = q0 + gla_dk // LANE
    v0 = k0 + gla_dk // LANE
    gate0 = v0 + gla_dv // LANE
    a0 = gate0 + gla_dv // LANE
    wa = jnp.stack([jnp.zeros((LANE, gla_dk), F32).at[dr * GLA_RANK:(dr + 1) * GLA_RANK]
                    .set(ab_gla_wa2[0, dr]) for dr in range(2)]).astype(BF16)
    ba = ab_gla_ba[0][:, None, :]
    o_gla = _gla(u0, wa, ba, batch=batch, seq=seq, n_ctx=n_ctx, cols=(q0, k0, v0, a0),
                 dv_total=gla_dv)

    lat_tiles = n_lat // tm_out
    ctx_tiles = n_ctx_tok // tm_out
    lead_specs = [pl.BlockSpec((tm_out, lru_w), lambda i: (jnp.minimum(i, lat_tiles - 1), 0)),
                  pl.BlockSpec((tm_out, lru_w),
                               lambda i: (jnp.clip(i - lat_tiles, 0, ctx_tiles - 1), 0)),
                  pl.BlockSpec((ROW_TILE, ROW_TILE), lambda i: (0, 0)),
                  pl.BlockSpec((None, tm_out, gla_dv), lambda i: (0, i, 0)),
                  pl.BlockSpec((None, tm_out, gla_dv), lambda i: (1, i, 0)),
                  pl.BlockSpec((tm_out, gla_dv), lambda i: (i, gate0 * LANE // gla_dv))]
    xs, h2, lg = _out_proj(functools.partial(_out_proj_ab_kernel, lat_tiles=lat_tiles),
                           (ya_l, ya_c, _time_permutation().T.astype(BF16), o_gla, o_gla, u0),
                           lead_specs,
                           ab_w_out[0].astype(BF16), xs, tabs[0], post_gain[0, 0][None, :],
                           pre_gain[0, 1][None, :], moe_router[0].T, tm_out, ntok // tm_out)
    cwt = _route(lg, moe_bias[0], tm_route)
    xs_ctx = moe(0, h2, cwt, xs, n_lat, 1, n_ctx_tok)
    xs = moe(0, h2, cwt, xs, 0, n_lat // tm_moe, tm_moe)

    hd = c_w_out.shape[1]
    tn1 = 1280
    w_in1 = c_w_in[0].astype(BF16)
    c_cols = w_in1.shape[1]
    nsec = c_cols // hd
    perm = _grid_permutation(band)
    gain1 = pre_gain[1, 0][None, :]
    u1l = _in_proj_grid(xs, tabs[1], gain1, perm, w_in1, tn1, n_lat // tm_grid, band,
                        rows_img // band)
    u1c = _in_proj(xs_ctx, tabs[1], gain1, w_in1, tm_proj, tn1, n_lat // tm_proj)
    lb_sm = jax.nn.softmax(hgrn_lb_logits.astype(F32), axis=0)
    lb_cum = jnp.cumsum(lb_sm, axis=0)
    lb = (lb_cum[1] - lb_cum[0])[None, :]
    ub = 1.0 - lb
    dh = hd // HGRN_HEADS
    s_zero = jnp.zeros((2, batch, HGRN_HEADS, dh, dh), F32)
    u1l_flat = u1l.reshape(n_lat, c_cols)
    (s_c,) = _hgrn_segment(u1c, s_zero, lb, ub, batch=batch, seg_len=n_ctx, emit_o=False)
    o_hgrn, _ = _hgrn_segment(u1l_flat, s_c, lb, ub, batch=batch, seg_len=seq, emit_o=True)
    o_hgrn = o_hgrn.reshape(2, n_lat // rows_img, rows_img // band, band, hd)

    bpi = rows_img // band
    grid_blk = (GRID_W, None, band, hd)
    lead_specs = [pl.BlockSpec((None,) + grid_blk, lambda i: (0, i // bpi, i % bpi, 0, 0)),
                  pl.BlockSpec((None,) + grid_blk, lambda i: (1, i // bpi, i % bpi, 0, 0)),
                  pl.BlockSpec(grid_blk, lambda i: (i // bpi, i % bpi, 0, nsec - 1)),
                  pl.BlockSpec((tm_grid, tm_grid), lambda i: (0, 0))]
    xs, h2, lg = _out_proj(_out_proj_c_kernel, (o_hgrn, o_hgrn, u1l, perm.T), lead_specs,
                           c_w_out[0].astype(BF16), xs, tabs[1], post_gain[1, 0][None, :],
                           pre_gain[1, 1][None, :], moe_router[1].T, tm_grid, n_lat // tm_grid)
    out = moe(1, h2, _route(lg, moe_bias[1], tm_route), xs, 0, n_lat // tm_moe, tm_moe)
    return out.reshape(batch, seq, d)
```

```python
import functools

import jax
import jax.numpy as jnp
from jax import lax
from jax.experimental import pallas as pl
from jax.experimental.pallas import tpu as pltpu

F32 = jnp.float32
BF16 = jnp.bfloat16

EPS = 1e-6
N_MOD = 6
ROW_TILE = 256
CHUNK = 64
CHUNK_SHIFT = 6
GRID_W = 64
LRU_HEADS = 8
LRU_C = 8.0
CONV_W = 4
GLA_HEADS = 4
GLA_RANK = 16
GLA_TAU = 16.0
HGRN_HEADS = 8
N_EXPERTS = 64
TOP_K = 8
N_GROUPS = 8
TOPK_GROUPS = 4
ROUTED_SCALE = 2.5
LANE = 128
SUBLANE = 8
BF16_ROWS = 16
CHUNKS_PER_ROW = 8
MOE_BLOCK = 128
VMEM_LIMIT = 48 * 1024 * 1024
VMEM_LIMIT_MOE = 58 * 1024 * 1024

_NT = (((1,), (1,)), ((), ()))
_TN = (((0,), (0,)), ((), ()))


def _params(*sem, vmem=VMEM_LIMIT):
    return pltpu.CompilerParams(dimension_semantics=sem, vmem_limit_bytes=vmem)


def _rms(x, gain):
    ms = jnp.mean(x * x, axis=-1, keepdims=True)
    return x * lax.rsqrt(ms + EPS) * gain


def _sigmoid(x):
    return jax.nn.sigmoid(x)


def _silu(x):
    return x * _sigmoid(x)


def _log_sigmoid(x):
    return jnp.minimum(x, 0.0) - jnp.log1p(jnp.exp(-jnp.abs(x)))


def _softplus(x):
    return jnp.maximum(x, 0.0) + jnp.log1p(jnp.exp(-jnp.abs(x)))


def _modulate(xn, tab_ref, shift_row, scale_row):
    parts = []
    for s in range(tab_ref.shape[0]):
        t = tab_ref[s]
        rows = xn[s * ROW_TILE:(s + 1) * ROW_TILE]
        parts.append(rows * (1.0 + t[scale_row:scale_row + 1]) + t[shift_row:shift_row + 1])
    return parts[0] if len(parts) == 1 else jnp.concatenate(parts, axis=0)


def _gated_residual(x, branch, tab_ref, gate_row):
    parts = []
    for s in range(tab_ref.shape[0]):
        sl = slice(s * ROW_TILE, (s + 1) * ROW_TILE)
        parts.append(x[sl] + tab_ref[s][gate_row:gate_row + 1] * branch[sl])
    return parts[0] if len(parts) == 1 else jnp.concatenate(parts, axis=0)


def _mod_kernel(c_ref, w_ref, b_ref, o_ref):
    s = _silu(c_ref[...])
    o_ref[0] = jnp.dot(s, w_ref[0], precision=lax.Precision.HIGHEST,
                       preferred_element_type=F32) + b_ref[0]


def _modulation(cvec, mod_w, mod_b):
    depth, d, n = mod_w.shape
    tn = 1024
    return pl.pallas_call(
        _mod_kernel,
        out_shape=jax.ShapeDtypeStruct((depth, 8, n), F32),
        grid=(depth, n // tn),
        in_specs=[pl.BlockSpec((8, d), lambda l, j: (0, 0)),
                  pl.BlockSpec((1, d, tn), lambda l, j: (l, 0, j)),
                  pl.BlockSpec((1, 1, tn), lambda l, j: (l, 0, j))],
        out_specs=pl.BlockSpec((1, 8, tn), lambda l, j: (l, 0, j)),
        compiler_params=_params("parallel", "parallel"),
        name="adaln_modulation",
    )(cvec, mod_w, mod_b.reshape(depth, 1, n))


def _in_proj_kernel(x_ref, tab_ref, gain_ref, w_ref, o_ref, *, tn):
    h = _modulate(_rms(x_ref[...], gain_ref[...]), tab_ref, 0, 1).astype(BF16)
    for j in range(w_ref.shape[1] // tn):
        cs = slice(j * tn, (j + 1) * tn)
        o_ref[:, cs] = jnp.dot(h, w_ref[:, cs], preferred_element_type=F32).astype(o_ref.dtype)


def _in_proj(x, tab, gain, w, tm, tn, tab0):
    d = x.shape[1]
    n = w.shape[1]
    nsub = tm // ROW_TILE
    ntiles = x.shape[0] // tm
    return pl.pallas_call(
        functools.partial(_in_proj_kernel, tn=tn),
        out_shape=jax.ShapeDtypeStruct((ntiles * tm, n), BF16),
        grid=(ntiles,),
        in_specs=[pl.BlockSpec((tm, d), lambda i: (i, 0)),
                  pl.BlockSpec((nsub, 8, d), lambda i: (tab0 + i, 0, 0)),
                  pl.BlockSpec((1, d), lambda i: (0, 0)),
                  pl.BlockSpec((d, n), lambda i: (0, 0), pipeline_mode=pl.Buffered(1))],
        out_specs=pl.BlockSpec((tm, n), lambda i: (i, 0)),
        compiler_params=_params("parallel"),
        name="norm_mod_in_proj",
    )(x, tab, gain, w)


def _after_moe(f_ref, x_ref, ptab_ref, pg_ref):
    f = _load_rows(f_ref, 0, x_ref.shape[0])
    return _gated_residual(x_ref[...], _rms(f, pg_ref[...]), ptab_ref, 5)


def _in_proj_ctx_kernel(f_ref, x_ref, ptab_ref, pg_ref, tab_ref, gain_ref, w_ref, o_ref, *, tn):
    xn = _after_moe(f_ref, x_ref, ptab_ref, pg_ref)
    h = _modulate(_rms(xn, gain_ref[...]), tab_ref, 0, 1).astype(BF16)
    for j in range(w_ref.shape[1] // tn):
        cs = slice(j * tn, (j + 1) * tn)
        o_ref[:, cs] = jnp.dot(h, w_ref[:, cs], preferred_element_type=F32).astype(o_ref.dtype)


def _in_proj_ctx(f, x, ptab, post_gain, tab, gain, w, tm, tn, tile0):
    d = x.shape[1]
    n = w.shape[1]
    s_n = CHUNKS_PER_ROW
    nsub = tm // ROW_TILE
    ntiles = f.shape[0] // (tm * s_n)
    tabspec = pl.BlockSpec((nsub, 8, d), lambda i: (tile0 + i, 0, 0))
    vec = pl.BlockSpec((1, d), lambda i: (0, 0))
    return pl.pallas_call(
        functools.partial(_in_proj_ctx_kernel, tn=tn),
        out_shape=jax.ShapeDtypeStruct((ntiles * tm, n), BF16),
        grid=(ntiles,),
        in_specs=[pl.BlockSpec((tm * s_n, LANE), lambda i: (i, 0)),
                  pl.BlockSpec((tm, d), lambda i: (tile0 + i, 0)),
                  tabspec, vec, tabspec, vec,
                  pl.BlockSpec((d, n), lambda i: (0, 0), pipeline_mode=pl.Buffered(1))],
        out_specs=pl.BlockSpec((tm, n), lambda i: (i, 0)),
        compiler_params=_params("parallel"),
        name="moe_close_in_proj",
    )(f, x, ptab, post_gain, tab, gain, w)


def _in_proj_grid_kernel(f_ref, x_ref, ptab_ref, pg_ref, tab_ref, gain_ref, p_ref, w_ref,
                         o_ref, xo_ref, *, tn):
    xn = _after_moe(f_ref, x_ref, ptab_ref, pg_ref)
    xo_ref[...] = xn
    h = _modulate(_rms(xn, gain_ref[...]), tab_ref, 0, 1).astype(BF16)
    hp = jnp.dot(p_ref[...], h, preferred_element_type=F32).astype(BF16)
    for j in range(w_ref.shape[1] // tn):
        cs = slice(j * tn, (j + 1) * tn)
        o = jnp.dot(hp, w_ref[:, cs], preferred_element_type=F32).astype(o_ref.dtype)
        o_ref[:, :, cs] = o.reshape(o_ref.shape[:2] + (tn,))


def _in_proj_grid(f, x, ptab, post_gain, tab, gain, perm, w, tn, ntiles, band, bands_per_image):
    d = x.shape[1]
    n = w.shape[1]
    s_n = CHUNKS_PER_ROW
    tm = band * GRID_W
    nsub = tm // ROW_TILE
    images = ntiles // bands_per_image
    once = pl.Buffered(1)
    tabspec = pl.BlockSpec((nsub, 8, d), lambda i: (i, 0, 0))
    vec = pl.BlockSpec((1, d), lambda i: (0, 0))
    return pl.pallas_call(
        functools.partial(_in_proj_grid_kernel, tn=tn),
        out_shape=(jax.ShapeDtypeStruct((images * GRID_W, bands_per_image, band, n), F32),
                   jax.ShapeDtypeStruct((ntiles * tm, d), F32)),
        grid=(ntiles,),
        in_specs=[pl.BlockSpec((tm * s_n, LANE), lambda i: (i, 0)),
                  pl.BlockSpec((tm, d), lambda i: (i, 0)),
                  tabspec, vec, tabspec, vec,
                  pl.BlockSpec((tm, tm), lambda i: (0, 0), pipeline_mode=once),
                  pl.BlockSpec((d, n), lambda i: (0, 0), pipeline_mode=once)],
        out_specs=(pl.BlockSpec((GRID_W, None, band, n),
                                lambda i: (i // bands_per_image, i % bands_per_image, 0, 0)),
                   pl.BlockSpec((tm, d), lambda i: (i, 0))),
        compiler_params=_params("parallel"),
        name="moe_close_in_proj_grid",
    )(f, x, ptab, post_gain, tab, gain, perm, w)


def _lru_kernel(gc_ref, xc_ref, gl_ref, xl_ref, p_ref, cw_ref, cb_ref, wg_ref, bg_ref,
                lam_ref, yc_ref, yl_ref, xs_scr, hf_scr):
    T = ROW_TILE
    V = T // SUBLANE
    H = BF16_ROWS
    sub = lax.broadcasted_iota(jnp.int32, (SUBLANE, LANE), 0)
    cw = cw_ref[...]
    cb = cb_ref[...]
    sp = _softplus(-lam_ref[...])

    def vreg(a, v):
        return a[v * SUBLANE:(v + 1) * SUBLANE]

    def shift_down(y, first):
        return jnp.where(sub == 0, first, pltpu.roll(y, 1, 0))

    def shift_up(y, last):
        return jnp.where(sub == SUBLANE - 1, last, pltpu.roll(y, SUBLANE - 1, 0))

    def conv_tile(x_ref, t, nt, seg_len):
        t0 = pl.multiple_of(t * T, T)
        xp = jnp.dot(p_ref[...], x_ref[pl.ds(t0, T), :], preferred_element_type=F32)
        p0 = pl.multiple_of(jnp.maximum(t0 - H, 0), H)
        n0 = pl.multiple_of(jnp.minimum(t0 + T, seg_len - H), H)
        prev = x_ref[pl.ds(p0, H), :].astype(F32) * jnp.where(t > 0, 1.0, 0.0)
        nxt = x_ref[pl.ds(n0, H), :].astype(F32) * jnp.where(t < nt - 1, 1.0, 0.0)
        m1, m2, p1 = prev[H - 1:H], prev[H - 2:H - 1], nxt[0:1]
        last_m1 = shift_down(vreg(xp, V - 1), m1)
        xm1 = jnp.concatenate([last_m1, xp[:T - SUBLANE]], axis=0)
        xm2 = jnp.concatenate([shift_down(vreg(xp, V - 2), m2), last_m1, xp[:T - 2 * SUBLANE]], axis=0)
        xp1 = jnp.concatenate([xp[SUBLANE:], shift_up(vreg(xp, 0), p1)], axis=0)
        return cb + xm2 * cw[0:1] + xm1 * cw[1:2] + xp * cw[2:3] + xp1 * cw[3:4]

    def gates(xc, d):
        z = jnp.dot(xc.astype(BF16), wg_ref[d, 0], preferred_element_type=F32) + bg_ref[d, 0]
        r = _sigmoid(z[:, :LANE])
        i = _sigmoid(z[:, LANE:])
        log_a = (-LRU_C) * r * sp[d:d + 1]
        a = jnp.exp(log_a)
        return a, jnp.sqrt(1.0 - a * a) * (i * xc)

    def scan_tile(a, u, carry, reverse):
        hs, cum = [None] * V, [None] * V
        hp = ap = None
        for v in (range(V - 1, -1, -1) if reverse else range(V)):
            av, uv = vreg(a, v), vreg(u, v)
            hp, ap = (uv, av) if hp is None else (av * hp + uv, av * ap)
            hs[v], cum[v] = hp, ap
        s = 1
        while s < SUBLANE:
            keep = (sub < SUBLANE - s) if reverse else (sub >= s)
            sh = SUBLANE - s if reverse else s
            h_sh = jnp.where(keep, pltpu.roll(hp, sh, 0), 0.0)
            a_sh = jnp.where(keep, pltpu.roll(ap, sh, 0), 1.0)
            hp = hp + ap * h_sh
            ap = ap * a_sh
            s *= 2
        state = hp + ap * carry
        if reverse:
            start, out = shift_up(state, carry), state[0:1]
        else:
            start, out = shift_down(state, carry), state[SUBLANE - 1:SUBLANE]
        h = jnp.concatenate([hs[v] + cum[v] * start for v in range(V)], axis=0)
        return h, out

    def run_segment(g_ref, x_ref, y_ref, seg_len, carry_f, carry_b):
        nt = seg_len // T
        unroll = 4 if nt % 4 == 0 else (2 if nt % 2 == 0 else 1)

        def fwd_body(i, carry):
            for j in range(unroll):
                t = i * unroll + j
                t0 = pl.multiple_of(t * T, T)
                xc = conv_tile(x_ref, t, nt, seg_len)
                xs_scr[pl.ds(t0, T), :] = xc
                a, u = gates(xc, 0)
                h, carry = scan_tile(a, u, carry, False)
                hf_scr[pl.ds(t0, T), :] = h
            return carry

        def bwd_body(i, carry):
            for j in range(unroll):
                t = nt - 1 - (i * unroll + j)
                t0 = pl.multiple_of(t * T, T)
                a, u = gates(xs_scr[pl.ds(t0, T), :], 1)
                h, carry = scan_tile(a, u, carry, True)
                g = jnp.dot(p_ref[...], g_ref[pl.ds(t0, T), :], preferred_element_type=F32)
                rec = hf_scr[pl.ds(t0, T), :] + h
                y_ref[pl.ds(t0, T), :] = (jax.nn.gelu(g) * rec).astype(y_ref.dtype)
            return carry

        steps = nt // unroll
        return (lax.fori_loop(0, steps, fwd_body, carry_f),
                lax.fori_loop(0, steps, bwd_body, carry_b))

    zero = jnp.zeros((1, LANE), F32)
    hf, hb = run_segment(gc_ref, xc_ref, yc_ref, xc_ref.shape[0], zero, zero)
    run_segment(gl_ref, xl_ref, yl_ref, xl_ref.shape[0], hf, hb)


def _time_permutation():
    v = ROW_TILE // SUBLANE
    dst = jnp.arange(ROW_TILE)
    src = (dst % SUBLANE) * v + dst // SUBLANE
    return src[:, None] == jnp.arange(ROW_TILE)[None, :]


def _lru(u, lru_w, *, batch, seq, n_ctx, g_col0, x_col0):
    width = LRU_HEADS * LANE
    cw, cb, wg, bg, lam = lru_w
    ctx_blk0 = batch * seq // n_ctx
    perm = _time_permutation()
    once = pl.Buffered(1)
    return pl.pallas_call(
        _lru_kernel,
        out_shape=(jax.ShapeDtypeStruct((batch * n_ctx, width), BF16),
                   jax.ShapeDtypeStruct((batch * seq, width), BF16)),
        grid=(batch, LRU_HEADS),
        in_specs=[pl.BlockSpec((n_ctx, LANE), lambda b, h: (ctx_blk0 + b, g_col0 + h)),
                  pl.BlockSpec((n_ctx, LANE), lambda b, h: (ctx_blk0 + b, x_col0 + h)),
                  pl.BlockSpec((seq, LANE), lambda b, h: (b, g_col0 + h)),
                  pl.BlockSpec((seq, LANE), lambda b, h: (b, x_col0 + h)),
                  pl.BlockSpec((ROW_TILE, ROW_TILE), lambda b, h: (0, 0), pipeline_mode=once),
                  pl.BlockSpec((8, LANE), lambda b, h: (0, h)),
                  pl.BlockSpec((1, LANE), lambda b, h: (0, h)),
                  pl.BlockSpec((2, 1, LANE, 2 * LANE), lambda b, h: (0, h, 0, 0)),
                  pl.BlockSpec((2, 1, 1, 2 * LANE), lambda b, h: (0, h, 0, 0)),
                  pl.BlockSpec((2, LANE), lambda b, h: (0, h))],
        out_specs=(pl.BlockSpec((n_ctx, LANE), lambda b, h: (b, h)),
                   pl.BlockSpec((seq, LANE), lambda b, h: (b, h))),
        scratch_shapes=[pltpu.VMEM((seq, LANE), F32), pltpu.VMEM((seq, LANE), F32)],
        compiler_params=_params("parallel", "parallel"),
        name="rglru",
    )(u, u, u, u, perm.astype(BF16), cw, cb, wg, bg, lam)


def _chunk_mask(rows, reverse):
    r = lax.broadcasted_iota(jnp.int32, (rows, rows), 0)
    c = lax.broadcasted_iota(jnp.int32, (rows, rows), 1)
    same = (r >> CHUNK_SHIFT) == (c >> CHUNK_SHIFT)
    tri = (c >= r) if reverse else (c <= r)
    return jnp.where(same, jnp.where(tri, 1.0, 0.0), 0.0)


def _split3(x):
    hi = x.astype(BF16)
    r1 = x - hi.astype(F32)
    mid = r1.astype(BF16)
    lo = (r1 - mid.astype(F32)).astype(BF16)
    return hi, mid, lo


def _block_scan(g_all, head_qkv, st_scr, o_ref, *, heads, dv, reverse, batched):
    rows, wall = g_all.shape
    width = wall // heads
    nchunk = rows // CHUNK
    mask = _chunk_mask(rows, reverse)
    mask_b = mask.astype(BF16)
    keep = mask > 0.5
    keep_c = keep[:CHUNK, :CHUNK]
    iref = CHUNK // 2 if reverse else CHUNK // 2 - 1
    ilast = 0 if reverse else CHUNK - 1
    order = range(nchunk - 1, -1, -1) if reverse else range(nchunk)

    def per_chunk(rows_1, width):
        return jnp.concatenate([jnp.broadcast_to(r, (CHUNK, width)) for r in rows_1], axis=0)

    pieces = jnp.concatenate(_split3(g_all), axis=1)
    sums = jnp.dot(mask_b, pieces, preferred_element_type=F32)
    b_all = sums[:, :wall] + sums[:, wall:2 * wall] + sums[:, 2 * wall:]

    def prepare(h):
        q, k, v = head_qkv(h)
        b = b_all[:, h * width:(h + 1) * width]
        b_ref = [b[c * CHUNK + iref:c * CHUNK + iref + 1] for c in range(nchunk)]
        b_last = [b[c * CHUNK + ilast:c * CHUNK + ilast + 1] for c in range(nchunk)]
        e1 = jnp.exp(b - per_chunk(b_ref, width))
        r1 = 1.0 / e1
        qe = q * e1
        ke = k * r1
        qd = (qe * per_chunk([jnp.exp(r) for r in b_ref], width)).astype(BF16)
        kd = (ke * per_chunk([jnp.exp(l - r) for l, r in zip(b_last, b_ref)], width)).astype(BF16)
        decays = [jnp.exp(l) for l in b_last]
        return qe.astype(BF16), ke.astype(BF16), qd, kd, v.astype(BF16), decays

    nxt = prepare(0)
    for h in range(heads):
        qe, ke, qd, kd, vb, decays = nxt
        if h + 1 < heads:
            nxt = prepare(h + 1)
        if batched:
            sc = lax.dot_general(qe, ke, _NT, preferred_element_type=F32)
            sc = jnp.where(keep, sc, 0.0).astype(BF16)
            o = jnp.dot(sc, vb, preferred_element_type=F32)
        st = st_scr[h]
        parts = [None] * nchunk
        for ci in order:
            rs = slice(ci * CHUNK, (ci + 1) * CHUNK)
            if batched:
                o_c = o[rs]
            else:
                sc = lax.dot_general(qe[rs], ke[rs], _NT, preferred_element_type=F32)
                sc = jnp.where(keep_c, sc, 0.0).astype(BF16)
                o_c = jnp.dot(sc, vb[rs], preferred_element_type=F32)
            parts[ci] = o_c + lax.dot_general(qd[rs], st.astype(BF16), _NT,
                                              preferred_element_type=F32)
            st = st * decays[ci] + lax.dot_general(vb[rs], kd[rs], _TN, preferred_element_type=F32)
        if o_ref is not None:
            o_ref[:, h * dv:(h + 1) * dv] = jnp.concatenate(parts, axis=0).astype(o_ref.dtype)
        st_scr[h] = st


def _gla_kernel(q_ref, k_ref, v_ref, a_ref, wa_ref, ba_ref, o_ref, st_scr):
    t = pl.program_id(1)
    d = pl.program_id(2)
    dk = q_ref.shape[1] // GLA_HEADS
    dv = v_ref.shape[1] // GLA_HEADS

    @pl.when((t == 0) & (d == 0))
    def _():
        st_scr[...] = jnp.zeros(st_scr.shape, F32)

    def head_qkv(h):
        ks = slice(h * dk, (h + 1) * dk)
        q = q_ref[:, ks].astype(F32) * (dk ** -0.5)
        return q, k_ref[:, ks].astype(F32), v_ref[:, h * dv:(h + 1) * dv]

    for dd in range(2):
        @pl.when(d == dd)
        def _(dd=dd):
            z = jnp.dot(a_ref[...], wa_ref[dd], preferred_element_type=F32) + ba_ref[dd]
            g_all = _log_sigmoid(z) * (1.0 / GLA_TAU)
            _block_scan(g_all, head_qkv, st_scr.at[dd], o_ref,
                        heads=GLA_HEADS, dv=dv, reverse=dd == 1, batched=True)


def _gla(u, wa, ba, *, batch, seq, n_ctx, cols, dv_total):
    ntok = u.shape[0]
    q0, k0, v0, a0 = cols
    dkt = wa.shape[2]
    dvt = dv_total
    nblk = seq // ROW_TILE
    ctx_blk0 = batch * nblk
    assert n_ctx == ROW_TILE

    def blk(b, t, d):
        lat = b * nblk + jnp.where(d == 0, t - 1, nblk - t)
        return jnp.where(t == 0, ctx_blk0 + b, lat)

    in_specs = [pl.BlockSpec((ROW_TILE, dkt), lambda b, t, d: (blk(b, t, d), q0 * LANE // dkt)),
                pl.BlockSpec((ROW_TILE, dkt), lambda b, t, d: (blk(b, t, d), k0 * LANE // dkt)),
                pl.BlockSpec((ROW_TILE, dvt), lambda b, t, d: (blk(b, t, d), v0 * LANE // dvt)),
                pl.BlockSpec((ROW_TILE, LANE), lambda b, t, d: (blk(b, t, d), a0)),
                pl.BlockSpec(wa.shape, lambda b, t, d: (0, 0, 0)),
                pl.BlockSpec(ba.shape, lambda b, t, d: (0, 0, 0))]
    return pl.pallas_call(
        _gla_kernel,
        out_shape=jax.ShapeDtypeStruct((2, ntok, dvt), BF16),
        grid=(batch, nblk + 1, 2),
        in_specs=in_specs,
        out_specs=pl.BlockSpec((None, ROW_TILE, dvt), lambda b, t, d: (d, blk(b, t, d), 0)),
        scratch_shapes=[pltpu.VMEM((2, GLA_HEADS, dvt // GLA_HEADS, dkt // GLA_HEADS), F32)],
        compiler_params=_params("parallel", "arbitrary", "arbitrary"),
        name="gla",
    )(u, u, u, u, wa, ba)


def _hgrn_kernel(*refs, emit_o):
    q_ref, z_ref, v_ref, lb_ref, ub_ref, s0_ref = refs[:6]
    if emit_o:
        o_ref, st_ref, st_scr = refs[6:]
    else:
        st_ref, st_scr = refs[6:]
        o_ref = None
    t = pl.program_id(1)
    d = pl.program_id(2)
    dh = q_ref.shape[1] // HGRN_HEADS

    @pl.when((t == 0) & (d == 0))
    def _():
        st_scr[...] = s0_ref[:, 0]

    lb = lb_ref[...]
    ub = ub_ref[...]

    for dd in range(2):
        @pl.when(d == dd)
        def _(dd=dd):
            z = z_ref[...].astype(F32)
            e = jnp.exp(-jnp.abs(z))
            r = 1.0 / (1.0 + e)
            er = e * r
            pos = z >= 0.0
            log_f = jnp.log(lb + ub * jnp.where(pos, r, er))
            k_all = ub * jnp.where(pos, er, r)

            def head_qkv(h):
                hs = slice(h * dh, (h + 1) * dh)
                return q_ref[:, hs].astype(F32), k_all[:, hs], v_ref[:, hs]

            _block_scan(log_f, head_qkv, st_scr.at[dd], o_ref,
                        heads=HGRN_HEADS, dv=dh, reverse=dd == 1, batched=True)

    @pl.when((t == pl.num_programs(1) - 1) & (d == 1))
    def _():
        st_ref[:, 0] = st_scr[...]


def _hgrn_segment(u, s0, lb, ub, *, batch, seg_len, emit_o):
    ntok = u.shape[0]
    width = lb.shape[1]
    nblk = seg_len // ROW_TILE

    def blk(b, t, d):
        return b * nblk + jnp.where(d == 0, t, nblk - 1 - t)

    st_spec = pl.BlockSpec((2, 1) + s0.shape[2:], lambda b, t, d: (0, b, 0, 0, 0))
    in_specs = [pl.BlockSpec((ROW_TILE, width), lambda b, t, d: (blk(b, t, d), 0)),
                pl.BlockSpec((ROW_TILE, width), lambda b, t, d: (blk(b, t, d), 1 + d)),
                pl.BlockSpec((ROW_TILE, width), lambda b, t, d: (blk(b, t, d), 3)),
                pl.BlockSpec((1, width), lambda b, t, d: (0, 0)),
                pl.BlockSpec((1, width), lambda b, t, d: (0, 0)),
                st_spec]
    st_shape = jax.ShapeDtypeStruct(s0.shape, F32)
    if emit_o:
        out_shape = (jax.ShapeDtypeStruct((2, ntok, width), F32), st_shape)
        out_specs = (pl.BlockSpec((None, ROW_TILE, width), lambda b, t, d: (d, blk(b, t, d), 0)),
                     st_spec)
    else:
        out_shape = (st_shape,)
        out_specs = (st_spec,)
    return pl.pallas_call(
        functools.partial(_hgrn_kernel, emit_o=emit_o),
        out_shape=out_shape,
        grid=(batch, nblk, 2),
        in_specs=in_specs,
        out_specs=out_specs,
        scratch_shapes=[pltpu.VMEM((2,) + s0.shape[2:], F32)],
        compiler_params=_params("parallel", "arbitrary", "arbitrary"),
        name="hgrn_segment",
    )(u, u, u, lb, ub, s0)


def _head_norm_gate(o, gate, heads):
    dh = o.shape[1] // heads
    parts = []
    for h in range(heads):
        oh = o[:, h * dh:(h + 1) * dh]
        ms = jnp.mean(oh * oh, axis=-1, keepdims=True)
        parts.append((oh * lax.rsqrt(ms + EPS) * _silu(gate[:, h * dh:(h + 1) * dh])).astype(BF16))
    return jnp.concatenate(parts, axis=1)


def _load_rows(ref, r0, nrows):
    s_n = CHUNKS_PER_ROW
    return jnp.concatenate([ref[pl.ds(r0 * s_n + s, nrows, stride=s_n), :] for s in range(s_n)],
                           axis=1)


def _store_rows(ref, r0, val):
    s_n = CHUNKS_PER_ROW
    for s in range(s_n):
        ref[pl.ds(r0 * s_n + s, val.shape[0], stride=s_n), :] = val[:, s * LANE:(s + 1) * LANE]


def _out_proj_tail(o, x_ref, tab_ref, pg_ref, ng_ref, rt_ref, xo_ref, h_ref, lg_ref):
    xn = _gated_residual(x_ref[...], _rms(o, pg_ref[...]), tab_ref, 2)
    xo_ref[...] = xn
    h2 = _modulate(_rms(xn, ng_ref[...]), tab_ref, 3, 4)
    _store_rows(h_ref, 0, h2)
    lg_ref[...] = lax.dot_general(rt_ref[...], h2, _NT, precision=lax.Precision.HIGHEST,
                                  preferred_element_type=F32)


def _out_proj_ab_kernel(yl_ref, yc_ref, pt_ref, of_ref, ob_ref, gate_ref, w_ref, x_ref, tab_ref,
                        pg_ref, ng_ref, rt_ref, xo_ref, h_ref, lg_ref, *, lat_tiles):
    o_att = of_ref[...].astype(F32) + ob_ref[...].astype(F32)
    yb = _head_norm_gate(o_att, gate_ref[...].astype(F32), GLA_HEADS)
    ya = jnp.where(pl.program_id(0) < lat_tiles, yl_ref[...], yc_ref[...])
    ya = jnp.concatenate(
        [jnp.dot(pt_ref[...], ya[s:s + ROW_TILE], preferred_element_type=F32).astype(BF16)
         for s in range(0, ya.shape[0], ROW_TILE)], axis=0)
    da = ya.shape[1]
    o = jnp.dot(ya, w_ref[:da], preferred_element_type=F32)
    o = o + jnp.dot(yb, w_ref[da:], preferred_element_type=F32)
    _out_proj_tail(o, x_ref, tab_ref, pg_ref, ng_ref, rt_ref, xo_ref, h_ref, lg_ref)


def _out_proj_c_kernel(of_ref, ob_ref, gate_ref, p_ref, w_ref, x_ref, tab_ref, pg_ref, ng_ref,
                       rt_ref, xo_ref, h_ref, lg_ref):
    tm = x_ref.shape[0]
    width = of_ref.shape[-1]
    o_att = of_ref[...].astype(F32) + ob_ref[...].astype(F32)
    y = _head_norm_gate(o_att.reshape(tm, width), gate_ref[...].astype(F32).reshape(tm, width),
                        HGRN_HEADS)
    y = jnp.dot(p_ref[...], y, preferred_element_type=F32).astype(BF16)
    o = jnp.dot(y, w_ref[...], preferred_element_type=F32)
    _out_proj_tail(o, x_ref, tab_ref, pg_ref, ng_ref, rt_ref, xo_ref, h_ref, lg_ref)


def _out_proj(kernel, lead, lead_specs, w, x, tab, post_gain, next_gain, router_t, tm, ntiles):
    d = x.shape[1]
    ntok = ntiles * tm
    nsub = tm // ROW_TILE
    ne = router_t.shape[0]
    in_specs = list(lead_specs) + [
        pl.BlockSpec(w.shape, lambda i: (0, 0), pipeline_mode=pl.Buffered(1)),
        pl.BlockSpec((tm, d), lambda i: (i, 0)),
        pl.BlockSpec((nsub, 8, d), lambda i: (i, 0, 0)),
        pl.BlockSpec((1, d), lambda i: (0, 0)),
        pl.BlockSpec((1, d), lambda i: (0, 0)),
        pl.BlockSpec((ne, d), lambda i: (0, 0))]
    return pl.pallas_call(
        kernel,
        out_shape=(jax.ShapeDtypeStruct((ntok, d), F32),
                   jax.ShapeDtypeStruct((ntok * CHUNKS_PER_ROW, LANE), F32),
                   jax.ShapeDtypeStruct((ne, ntok), F32)),
        grid=(ntiles,),
        in_specs=in_specs,
        out_specs=(pl.BlockSpec((tm, d), lambda i: (i, 0)),
                   pl.BlockSpec((tm * CHUNKS_PER_ROW, LANE), lambda i: (i, 0)),
                   pl.BlockSpec((ne, tm), lambda i: (0, i))),
        compiler_params=_params("parallel"),
        name="out_proj_norm_residual",
    )(*lead, w, x, tab, post_gain, next_gain, router_t)


def _route_kernel(lg_ref, bias_ref, ids_ref, w_ref):
    tm = lg_ref.shape[1]
    per_group = N_EXPERTS // N_GROUPS
    shape3 = (N_GROUPS, per_group, tm)
    aff = _sigmoid(lg_ref[...]).reshape(shape3)
    biased = aff + bias_ref[...].reshape(shape3)
    neg = -jnp.inf
    sub = lax.broadcasted_iota(jnp.int32, shape3, 1)
    grp = lax.broadcasted_iota(jnp.int32, shape3, 0)
    m1 = jnp.max(biased, axis=1, keepdims=True)
    i1 = jnp.min(jnp.where(biased == m1, sub, per_group), axis=1, keepdims=True)
    m2 = jnp.max(jnp.where(sub == i1, neg, biased), axis=1, keepdims=True)
    score = m1 + m2
    gidx = lax.broadcasted_iota(jnp.int32, score.shape, 0)
    keep = jnp.zeros(score.shape, F32)
    for _ in range(TOPK_GROUPS):
        m = jnp.max(score, axis=0, keepdims=True)
        im = jnp.min(jnp.where(score == m, gidx, N_GROUPS), axis=0, keepdims=True)
        sel = gidx == im
        keep = jnp.where(sel, 1.0, keep)
        score = jnp.where(sel, neg, score)
    work = jnp.where(jnp.broadcast_to(keep, shape3) > 0.0, biased, neg)
    eidx = grp * per_group + sub
    ids, gates = [], []
    for _ in range(TOP_K):
        m = jnp.max(jnp.max(work, axis=0, keepdims=True), axis=1, keepdims=True)
        cand = jnp.where(work == m, eidx, N_EXPERTS)
        im = jnp.min(jnp.min(cand, axis=0, keepdims=True), axis=1, keepdims=True)
        sel = eidx == im
        ids.append(im)
        gates.append(jnp.sum(jnp.sum(jnp.where(sel, aff, 0.0), axis=0, keepdims=True),
                             axis=1, keepdims=True))
        work = jnp.where(sel, neg, work)
    gate = jnp.concatenate(gates, axis=1)
    den = jnp.sum(gate, axis=1, keepdims=True)
    w_ref[...] = (ROUTED_SCALE * gate / den).reshape(TOP_K, tm)
    ids_ref[...] = jnp.concatenate(ids, axis=1).reshape(TOP_K, tm)


def _route(logits_t, bias, tm):
    ne, ntok = logits_t.shape
    bias_b = jnp.broadcast_to(bias.astype(F32)[:, None], (ne, tm))
    return pl.pallas_call(
        _route_kernel,
        out_shape=(jax.ShapeDtypeStruct((TOP_K, ntok), jnp.int32),
                   jax.ShapeDtypeStruct((TOP_K, ntok), F32)),
        grid=(ntok // tm,),
        in_specs=[pl.BlockSpec((ne, tm), lambda i: (0, i)),
                  pl.BlockSpec((ne, tm), lambda i: (0, 0))],
        out_specs=(pl.BlockSpec((TOP_K, tm), lambda i: (0, i)),
                   pl.BlockSpec((TOP_K, tm), lambda i: (0, i))),
        compiler_params=_params("parallel"),
        name="moe_route",
    )(logits_t, bias_b)


LIST_PAD = 1024


def _dispatch_plan(ids, w, t0, tile, ntiles):
    n = ntiles * tile
    e = ids[:, t0:t0 + n].T.reshape(ntiles, tile * TOP_K)
    g = w[:, t0:t0 + n].T.reshape(ntiles, tile * TOP_K)
    order = jnp.argsort(e, axis=1, stable=True).astype(jnp.int32)
    rows = (order // TOP_K) * CHUNKS_PER_ROW
    wts = jnp.take_along_axis(g, order, axis=1)
    experts = jnp.arange(N_EXPERTS, dtype=jnp.int32)
    counts = jnp.sum((e[:, :, None] == experts).astype(jnp.int32), axis=1)
    starts = jnp.concatenate([jnp.zeros((ntiles, 1), jnp.int32), jnp.cumsum(counts, axis=1)], axis=1)
    rows = jnp.pad(rows, ((0, 0), (0, LIST_PAD)))
    wts = jnp.pad(wts, ((0, 0), (0, LIST_PAD)))
    starts = jnp.pad(starts, ((0, 0), (0, LANE - N_EXPERTS - 1)))
    return rows, wts, starts


def _moe_kernel(rows_hbm, wts_hbm, starts_hbm, h_ref, wg_ref, wu_ref, wd_ref, sg_ref, su_ref,
                sd_ref, o_ref, rows_s, wts_s, starts_s, sem, gbuf, ybuf, *, epb):
    i = pl.program_id(0)
    j = pl.program_id(1)
    s_n = CHUNKS_PER_ROW
    blk = MOE_BLOCK
    tile = h_ref.shape[0] // s_n

    def swiglu(x, w_gate, w_up, w_down):
        g = jnp.dot(x, w_gate, preferred_element_type=F32)
        u = jnp.dot(x, w_up, preferred_element_type=F32)
        return jnp.dot((_silu(g) * u).astype(BF16), w_down, preferred_element_type=F32)

    @pl.when(j == 0)
    def _():
        copies = [pltpu.make_async_copy(rows_hbm.at[i], rows_s, sem.at[0]),
                  pltpu.make_async_copy(wts_hbm.at[i], wts_s, sem.at[1]),
                  pltpu.make_async_copy(starts_hbm.at[i], starts_s, sem.at[2])]
        for c in copies:
            c.start()
        for r0 in range(0, tile, ROW_TILE):
            x = _load_rows(h_ref, r0, ROW_TILE).astype(BF16)
            _store_rows(o_ref, r0, swiglu(x, sg_ref[...], su_ref[...], sd_ref[...]))
        for c in copies:
            c.wait()

    for e in range(epb):
        base = starts_s[j * epb + e]
        count = starts_s[j * epb + e + 1] - base
        w_gate = wg_ref[e].astype(BF16)
        w_up = wu_ref[e].astype(BF16)
        w_down = wd_ref[e].astype(BF16)

        def block(bi, carry, base=base, count=count, w_gate=w_gate, w_up=w_up, w_down=w_down):
            p0 = base + bi * blk
            left = count - bi * blk
            offs = [pl.multiple_of(rows_s[p0 + r], s_n) for r in range(blk)]
            for r in range(blk):
                gbuf[pl.ds(r * s_n, s_n), :] = h_ref[pl.ds(offs[r], s_n), :]
            x = _load_rows(gbuf, 0, blk).astype(BF16)
            y = swiglu(x, w_gate, w_up, w_down)
            y = jnp.where(lax.broadcasted_iota(jnp.int32, y.shape, 0) < left, y, 0.0)
            _store_rows(ybuf, 0, y)
            group = 2 * SUBLANE
            for r0 in range(0, blk, group):
                vals = [o_ref[pl.ds(offs[r], s_n), :] + wts_s[p0 + r] * ybuf[pl.ds(r * s_n, s_n), :]
                        for r in range(r0, r0 + group)]
                for r in reversed(range(r0, r0 + group)):
                    o_ref[pl.ds(offs[r], s_n), :] = vals[r - r0]
            return carry

        nblk = lax.shift_right_logical(count + (blk - 1), blk.bit_length() - 1)
        lax.fori_loop(0, nblk, block, 0)


def _moe(h, plan, layer, wg, wu, wd, sg, su, sd, tile0, ntiles, tile, epb):
    rows, wts, starts = plan
    s_n = CHUNKS_PER_ROW
    _, ne, d, ff = wg.shape
    once = pl.Buffered(1)
    hbm = pl.BlockSpec(memory_space=pl.ANY)
    return pl.pallas_call(
        functools.partial(_moe_kernel, epb=epb),
        out_shape=jax.ShapeDtypeStruct((ntiles * tile * s_n, LANE), F32),
        grid=(ntiles, ne // epb),
        in_specs=[hbm, hbm, hbm,
                  pl.BlockSpec((tile * s_n, LANE), lambda i, j: (tile0 + i, 0), pipeline_mode=once),
                  pl.BlockSpec((None, epb, d, ff), lambda i, j: (layer, j, 0, 0)),
                  pl.BlockSpec((None, epb, d, ff), lambda i, j: (layer, j, 0, 0)),
                  pl.BlockSpec((None, epb, ff, d), lambda i, j: (layer, j, 0, 0)),
                  pl.BlockSpec(sg.shape, lambda i, j: (0, 0), pipeline_mode=once),
                  pl.BlockSpec(su.shape, lambda i, j: (0, 0), pipeline_mode=once),
                  pl.BlockSpec(sd.shape, lambda i, j: (0, 0), pipeline_mode=once)],
        out_specs=pl.BlockSpec((tile * s_n, LANE), lambda i, j: (i, 0)),
        scratch_shapes=[pltpu.SMEM((rows.shape[1],), jnp.int32),
                        pltpu.SMEM((wts.shape[1],), F32),
                        pltpu.SMEM((starts.shape[1],), jnp.int32),
                        pltpu.SemaphoreType.DMA((3,)),
                        pltpu.VMEM((MOE_BLOCK * s_n, LANE), F32),
                        pltpu.VMEM((MOE_BLOCK * s_n, LANE), F32)],
        compiler_params=_params("parallel", "arbitrary", vmem=VMEM_LIMIT_MOE),
        name="moe_experts",
    )(rows, wts, starts, h, wg, wu, wd, sg, su, sd)


def _moe_finish_kernel(f_ref, x_ref, tab_ref, pg_ref, o_ref):
    f = _load_rows(f_ref, 0, x_ref.shape[0])
    o_ref[...] = _gated_residual(x_ref[...], _rms(f, pg_ref[...]), tab_ref, 5)


def _moe_finish(f, x, tab, post_gain, tm, x_tile0, tab0):
    d = x.shape[1]
    s_n = CHUNKS_PER_ROW
    nsub = tm // ROW_TILE
    ntiles = f.shape[0] // (tm * s_n)
    return pl.pallas_call(
        _moe_finish_kernel,
        out_shape=jax.ShapeDtypeStruct((ntiles * tm, d), F32),
        grid=(ntiles,),
        in_specs=[pl.BlockSpec((tm * s_n, LANE), lambda i: (i, 0)),
                  pl.BlockSpec((tm, d), lambda i: (x_tile0 + i, 0)),
                  pl.BlockSpec((nsub, 8, d), lambda i: (tab0 + i, 0, 0)),
                  pl.BlockSpec((1, d), lambda i: (0, 0))],
        out_specs=pl.BlockSpec((tm, d), lambda i: (i, 0)),
        compiler_params=_params("parallel"),
        name="moe_norm_residual",
    )(f, x, tab, post_gain)


def _grid_permutation(band):
    tm = band * GRID_W
    dst = jnp.arange(tm)
    src = (dst % band) * GRID_W + dst // band
    return (src[:, None] == jnp.arange(tm)[None, :]).astype(BF16)


def kernel(x, c, ctx, c_ctx, mod_w, mod_b, pre_gain, post_gain, ab_w_in, ab_conv_w, ab_conv_b, ab_lru_wa, ab_lru_ba, ab_lru_wi, ab_lru_bi, ab_lru_lambda, ab_gla_wa2, ab_gla_ba, ab_w_out, c_w_in, hgrn_lb_logits, c_w_out, moe_router, moe_bias, moe_w_gate, moe_w_up, moe_w_down, moe_ws_gate, moe_ws_up, moe_ws_down):
    batch, seq, d = x.shape
    n_ctx = ctx.shape[1]
    depth = mod_w.shape[0]
    band = SUBLANE
    tm_grid = band * GRID_W
    rows_img = seq // GRID_W
    assert n_ctx == ROW_TILE and seq % (2 * ROW_TILE) == 0 and d % LANE == 0
    assert depth == 2 and batch + 1 <= 8 and rows_img % band == 0 and tm_grid % ROW_TILE == 0
    n_lat = batch * seq
    n_ctx_tok = batch * n_ctx
    ntok = n_lat + n_ctx_tok

    xs = jnp.concatenate([x.reshape(n_lat, d), ctx.reshape(n_ctx_tok, d)], axis=0)

    cvec = jnp.concatenate([c, c_ctx[None, :], jnp.zeros((8 - batch - 1, d), F32)], axis=0)
    mods = _modulation(cvec, mod_w, mod_b).reshape(depth, 8, N_MOD, d)
    tile_row = jnp.concatenate([jnp.repeat(jnp.arange(batch), seq // ROW_TILE),
                                jnp.full((batch,), batch)]).astype(jnp.int32)
    tabs = jnp.pad(mods[:, tile_row], ((0, 0), (0, 0), (0, 8 - N_MOD), (0, 0)))

    def experts(layer, h2, routed, t0, ntiles, tile):
        plan = _dispatch_plan(*routed, t0, tile, ntiles)
        return _moe(h2, plan, layer, moe_w_gate, moe_w_up, moe_w_down,
                    moe_ws_gate[layer].astype(BF16), moe_ws_up[layer].astype(BF16),
                    moe_ws_down[layer].astype(BF16), t0 // tile, ntiles, tile, epb)

    tm_proj, tm_out, tm_route, tm_moe, epb = 512, 512, 512, 2048, 4
    assert n_lat % tm_moe == 0 and n_lat % n_ctx_tok == 0

    ab_cols = ab_w_in.shape[2]
    gla_dk = ab_gla_wa2.shape[3]
    lru_w = d
    gla_dv = (ab_cols - 2 * lru_w - 2 * gla_dk - 2 * GLA_RANK) // 2
    tn0 = 1792
    n0 = -(-(ab_cols) // tn0) * tn0
    w_in0 = jnp.pad(ab_w_in[0], ((0, 0), (0, n0 - ab_cols))).astype(BF16)
    u0 = _in_proj(xs, tabs[0], pre_gain[0, 0][None, :], w_in0, tm_proj, tn0, 0)

    cw = jnp.pad(ab_conv_w[0], ((0, 8 - CONV_W), (0, 0)))
    cb = ab_conv_b[0][None, :]
    wg = jnp.concatenate([ab_lru_wa[0], ab_lru_wi[0]], axis=-1).astype(BF16)
    bg = jnp.concatenate([ab_lru_ba[0].reshape(2, LRU_HEADS, 1, LANE),
                          ab_lru_bi[0].reshape(2, LRU_HEADS, 1, LANE)], axis=-1)
    ya_c, ya_l = _lru(u0, (cw, cb, wg, bg, ab_lru_lambda[0]), batch=batch, seq=seq, n_ctx=n_ctx,
                      g_col0=0, x_col0=lru_w // LANE)

    q0 = 2 * lru_w // LANE
    k0 = q0 + gla_dk // LANE
    v0 = k0 + gla_dk // LANE
    gate0 = v0 + gla_dv // LANE
    a0 = gate0 + gla_dv // LANE
    wa = jnp.stack([jnp.zeros((LANE, gla_dk), F32).at[dr * GLA_RANK:(dr + 1) * GLA_RANK]
                    .set(ab_gla_wa2[0, dr]) for dr in range(2)]).astype(BF16)
    ba = ab_gla_ba[0][:, None, :]
    o_gla = _gla(u0, wa, ba, batch=batch, seq=seq, n_ctx=n_ctx, cols=(q0, k0, v0, a0),
                 dv_total=gla_dv)

    lat_tiles = n_lat // tm_out
    ctx_tiles = n_ctx_tok // tm_out
    lead_specs = [pl.BlockSpec((tm_out, lru_w), lambda i: (jnp.minimum(i, lat_tiles - 1), 0)),
                  pl.BlockSpec((tm_out, lru_w),
                               lambda i: (jnp.clip(i - lat_tiles, 0, ctx_tiles - 1), 0)),
                  pl.BlockSpec((ROW_TILE, ROW_TILE), lambda i: (0, 0)),
                  pl.BlockSpec((None, tm_out, gla_dv), lambda i: (0, i, 0)),
                  pl.BlockSpec((None, tm_out, gla_dv), lambda i: (1, i, 0)),
                  pl.BlockSpec((tm_out, gla_dv), lambda i: (i, gate0 * LANE // gla_dv))]
    xs, h2, lg = _out_proj(functools.partial(_out_proj_ab_kernel, lat_tiles=lat_tiles),
                           (ya_l, ya_c, _time_permutation().T.astype(BF16), o_gla, o_gla, u0),
                           lead_specs,
                           ab_w_out[0].astype(BF16), xs, tabs[0], post_gain[0, 0][None, :],
                           pre_gain[0, 1][None, :], moe_router[0].T, tm_out, ntok // tm_out)
    cwt = _route(lg, moe_bias[0], tm_route)
    f_ctx = experts(0, h2, cwt, n_lat, 1, n_ctx_tok)
    f_lat = experts(0, h2, cwt, 0, n_lat // tm_moe, tm_moe)
    pg0 = post_gain[0, 1][None, :]

    hd = c_w_out.shape[1]
    tn1 = 1280
    w_in1 = c_w_in[0].astype(BF16)
    c_cols = w_in1.shape[1]
    nsec = c_cols // hd
    perm = _grid_permutation(band)
    gain1 = pre_gain[1, 0][None, :]
    u1c = _in_proj_ctx(f_ctx, xs, tabs[0], pg0, tabs[1], gain1, w_in1, tm_proj, tn1,
                       n_lat // tm_proj)
    u1l, xs = _in_proj_grid(f_lat, xs, tabs[0], pg0, tabs[1], gain1, perm, w_in1, tn1,
                            n_lat // tm_grid, band, rows_img // band)
    lb_sm = jax.nn.softmax(hgrn_lb_logits.astype(F32), axis=0)
    lb_cum = jnp.cumsum(lb_sm, axis=0)
    lb = (lb_cum[1] - lb_cum[0])[None, :]
    ub = 1.0 - lb
    dh = hd // HGRN_HEADS
    s_zero = jnp.zeros((2, batch, HGRN_HEADS, dh, dh), F32)
    u1l_flat = u1l.reshape(n_lat, c_cols)
    (s_c,) = _hgrn_segment(u1c, s_zero, lb, ub, batch=batch, seg_len=n_ctx, emit_o=False)
    o_hgrn, _ = _hgrn_segment(u1l_flat, s_c, lb, ub, batch=batch, seg_len=seq, emit_o=True)
    o_hgrn = o_hgrn.reshape(2, n_lat // rows_img, rows_img // band, band, hd)

    bpi = rows_img // band
    grid_blk = (GRID_W, None, band, hd)
    lead_specs = [pl.BlockSpec((None,) + grid_blk, lambda i: (0, i // bpi, i % bpi, 0, 0)),
                  pl.BlockSpec((None,) + grid_blk, lambda i: (1, i // bpi, i % bpi, 0, 0)),
                  pl.BlockSpec(grid_blk, lambda i: (i // bpi, i % bpi, 0, nsec - 1)),
                  pl.BlockSpec((tm_grid, tm_grid), lambda i: (0, 0))]
    xs, h2, lg = _out_proj(_out_proj_c_kernel, (o_hgrn, o_hgrn, u1l, perm.T), lead_specs,
                           c_w_out[0].astype(BF16), xs, tabs[1], post_gain[1, 0][None, :],
                           pre_gain[1, 1][None, :], moe_router[1].T, tm_grid, n_lat // tm_grid)
    f_lat = experts(1, h2, _route(lg, moe_bias[1], tm_route), 0, n_lat // tm_moe, tm_moe)
    out = _moe_finish(f_lat, xs, tabs[1], post_gain[1, 1][None, :], tm_out, 0, 0)
    return out.reshape(batch, seq, d)
```

```python
import functools

import jax
import jax.numpy as jnp
from jax import lax
from jax.experimental import pallas as pl
from jax.experimental.pallas import tpu as pltpu

F32 = jnp.float32
BF16 = jnp.bfloat16

EPS = 1e-6
N_MOD = 6
ROW_TILE = 256
CHUNK = 64
CHUNK_SHIFT = 6
GRID_W = 64
LRU_HEADS = 8
LRU_C = 8.0
CONV_W = 4
GLA_HEADS = 4
GLA_RANK = 16
GLA_TAU = 16.0
HGRN_HEADS = 8
N_EXPERTS = 64
TOP_K = 8
N_GROUPS = 8
TOPK_GROUPS = 4
ROUTED_SCALE = 2.5
LANE = 128
SUBLANE = 8
BF16_ROWS = 16
CHUNKS_PER_ROW = 8
MOE_BLOCK = 128
VMEM_LIMIT = 48 * 1024 * 1024
VMEM_LIMIT_MOE = 58 * 1024 * 1024

_NT = (((1,), (1,)), ((), ()))
_TN = (((0,), (0,)), ((), ()))


def _params(*sem, vmem=VMEM_LIMIT):
    return pltpu.CompilerParams(dimension_semantics=sem, vmem_limit_bytes=vmem)


def _rms(x, gain):
    ms = jnp.mean(x * x, axis=-1, keepdims=True)
    return x * lax.rsqrt(ms + EPS) * gain


def _sigmoid(x):
    return jax.nn.sigmoid(x)


def _silu(x):
    return x * _sigmoid(x)


def _log_sigmoid(x):
    return jnp.minimum(x, 0.0) - jnp.log1p(jnp.exp(-jnp.abs(x)))


def _softplus(x):
    return jnp.maximum(x, 0.0) + jnp.log1p(jnp.exp(-jnp.abs(x)))


def _modulate(xn, tab_ref, shift_row, scale_row):
    parts = []
    for s in range(tab_ref.shape[0]):
        t = tab_ref[s]
        rows = xn[s * ROW_TILE:(s + 1) * ROW_TILE]
        parts.append(rows * (1.0 + t[scale_row:scale_row + 1]) + t[shift_row:shift_row + 1])
    return parts[0] if len(parts) == 1 else jnp.concatenate(parts, axis=0)


def _gated_residual(x, branch, tab_ref, gate_row):
    parts = []
    for s in range(tab_ref.shape[0]):
        sl = slice(s * ROW_TILE, (s + 1) * ROW_TILE)
        parts.append(x[sl] + tab_ref[s][gate_row:gate_row + 1] * branch[sl])
    return parts[0] if len(parts) == 1 else jnp.concatenate(parts, axis=0)


def _mod_kernel(c_ref, w_ref, b_ref, o_ref):
    s = _silu(c_ref[...])
    o_ref[0] = jnp.dot(s, w_ref[0], precision=lax.Precision.HIGHEST,
                       preferred_element_type=F32) + b_ref[0]


def _modulation(cvec, mod_w, mod_b):
    depth, d, n = mod_w.shape
    tn = 1024
    return pl.pallas_call(
        _mod_kernel,
        out_shape=jax.ShapeDtypeStruct((depth, 8, n), F32),
        grid=(depth, n // tn),
        in_specs=[pl.BlockSpec((8, d), lambda l, j: (0, 0)),
                  pl.BlockSpec((1, d, tn), lambda l, j: (l, 0, j)),
                  pl.BlockSpec((1, 1, tn), lambda l, j: (l, 0, j))],
        out_specs=pl.BlockSpec((1, 8, tn), lambda l, j: (l, 0, j)),
        compiler_params=_params("parallel", "parallel"),
        name="adaln_modulation",
    )(cvec, mod_w, mod_b.reshape(depth, 1, n))


def _in_proj_kernel(x_ref, tab_ref, gain_ref, w_ref, o_ref, *, tn):
    h = _modulate(_rms(x_ref[...], gain_ref[...]), tab_ref, 0, 1).astype(BF16)
    for j in range(w_ref.shape[1] // tn):
        cs = slice(j * tn, (j + 1) * tn)
        o_ref[:, cs] = jnp.dot(h, w_ref[:, cs], preferred_element_type=F32).astype(o_ref.dtype)


def _in_proj(x, tab, gain, w, tm, tn, tab0):
    d = x.shape[1]
    n = w.shape[1]
    nsub = tm // ROW_TILE
    ntiles = x.shape[0] // tm
    return pl.pallas_call(
        functools.partial(_in_proj_kernel, tn=tn),
        out_shape=jax.ShapeDtypeStruct((ntiles * tm, n), BF16),
        grid=(ntiles,),
        in_specs=[pl.BlockSpec((tm, d), lambda i: (i, 0)),
                  pl.BlockSpec((nsub, 8, d), lambda i: (tab0 + i, 0, 0)),
                  pl.BlockSpec((1, d), lambda i: (0, 0)),
                  pl.BlockSpec((d, n), lambda i: (0, 0), pipeline_mode=pl.Buffered(1))],
        out_specs=pl.BlockSpec((tm, n), lambda i: (i, 0)),
        compiler_params=_params("parallel"),
        name="norm_mod_in_proj",
    )(x, tab, gain, w)


def _after_moe(f_ref, x_ref, ptab_ref, pg_ref):
    f = _load_rows(f_ref, 0, x_ref.shape[0])
    return _gated_residual(x_ref[...], _rms(f, pg_ref[...]), ptab_ref, 5)


def _in_proj_ctx_kernel(f_ref, x_ref, ptab_ref, pg_ref, tab_ref, gain_ref, w_ref, o_ref, *, tn):
    xn = _after_moe(f_ref, x_ref, ptab_ref, pg_ref)
    h = _modulate(_rms(xn, gain_ref[...]), tab_ref, 0, 1).astype(BF16)
    for j in range(w_ref.shape[1] // tn):
        cs = slice(j * tn, (j + 1) * tn)
        o_ref[:, cs] = jnp.dot(h, w_ref[:, cs], preferred_element_type=F32).astype(o_ref.dtype)


def _in_proj_ctx(f, x, ptab, post_gain, tab, gain, w, tm, tn, tile0):
    d = x.shape[1]
    n = w.shape[1]
    s_n = CHUNKS_PER_ROW
    nsub = tm // ROW_TILE
    ntiles = f.shape[0] // (tm * s_n)
    tabspec = pl.BlockSpec((nsub, 8, d), lambda i: (tile0 + i, 0, 0))
    vec = pl.BlockSpec((1, d), lambda i: (0, 0))
    return pl.pallas_call(
        functools.partial(_in_proj_ctx_kernel, tn=tn),
        out_shape=jax.ShapeDtypeStruct((ntiles * tm, n), BF16),
        grid=(ntiles,),
        in_specs=[pl.BlockSpec((tm * s_n, LANE), lambda i: (i, 0)),
                  pl.BlockSpec((tm, d), lambda i: (tile0 + i, 0)),
                  tabspec, vec, tabspec, vec,
                  pl.BlockSpec((d, n), lambda i: (0, 0), pipeline_mode=pl.Buffered(1))],
        out_specs=pl.BlockSpec((tm, n), lambda i: (i, 0)),
        compiler_params=_params("parallel"),
        name="moe_close_in_proj",
    )(f, x, ptab, post_gain, tab, gain, w)


def _in_proj_grid_kernel(f_ref, x_ref, ptab_ref, pg_ref, tab_ref, gain_ref, p_ref, w_ref,
                         o_ref, xo_ref, *, tn):
    xn = _after_moe(f_ref, x_ref, ptab_ref, pg_ref)
    xo_ref[...] = xn
    h = _modulate(_rms(xn, gain_ref[...]), tab_ref, 0, 1).astype(BF16)
    hp = jnp.dot(p_ref[...], h, preferred_element_type=F32).astype(BF16)
    for j in range(w_ref.shape[1] // tn):
        cs = slice(j * tn, (j + 1) * tn)
        o = jnp.dot(hp, w_ref[:, cs], preferred_element_type=F32).astype(o_ref.dtype)
        o_ref[:, :, cs] = o.reshape(o_ref.shape[:2] + (tn,))


def _in_proj_grid(f, x, ptab, post_gain, tab, gain, perm, w, tn, ntiles, band, bands_per_image):
    d = x.shape[1]
    n = w.shape[1]
    s_n = CHUNKS_PER_ROW
    tm = band * GRID_W
    nsub = tm // ROW_TILE
    images = ntiles // bands_per_image
    once = pl.Buffered(1)
    tabspec = pl.BlockSpec((nsub, 8, d), lambda i: (i, 0, 0))
    vec = pl.BlockSpec((1, d), lambda i: (0, 0))
    return pl.pallas_call(
        functools.partial(_in_proj_grid_kernel, tn=tn),
        out_shape=(jax.ShapeDtypeStruct((images * GRID_W, bands_per_image, band, n), F32),
                   jax.ShapeDtypeStruct((ntiles * tm, d), F32)),
        grid=(ntiles,),
        in_specs=[pl.BlockSpec((tm * s_n, LANE), lambda i: (i, 0)),
                  pl.BlockSpec((tm, d), lambda i: (i, 0)),
                  tabspec, vec, tabspec, vec,
                  pl.BlockSpec((tm, tm), lambda i: (0, 0), pipeline_mode=once),
                  pl.BlockSpec((d, n), lambda i: (0, 0), pipeline_mode=once)],
        out_specs=(pl.BlockSpec((GRID_W, None, band, n),
                                lambda i: (i // bands_per_image, i % bands_per_image, 0, 0)),
                   pl.BlockSpec((tm, d), lambda i: (i, 0))),
        compiler_params=_params("parallel"),
        name="moe_close_in_proj_grid",
    )(f, x, ptab, post_gain, tab, gain, perm, w)


def _lru_kernel(gc_ref, xc_ref, gl_ref, xl_ref, p_ref, cw_ref, cb_ref, wg_ref, bg_ref,
                lam_ref, yc_ref, yl_ref, xs_scr, hf_scr):
    T = ROW_TILE
    V = T // SUBLANE
    H = BF16_ROWS
    sub = lax.broadcasted_iota(jnp.int32, (SUBLANE, LANE), 0)
    cw = cw_ref[...]
    cb = cb_ref[...]
    sp = _softplus(-lam_ref[...])

    def vreg(a, v):
        return a[v * SUBLANE:(v + 1) * SUBLANE]

    def shift_down(y, first):
        return jnp.where(sub == 0, first, pltpu.roll(y, 1, 0))

    def shift_up(y, last):
        return jnp.where(sub == SUBLANE - 1, last, pltpu.roll(y, SUBLANE - 1, 0))

    def conv_tile(x_ref, t, nt, seg_len):
        t0 = pl.multiple_of(t * T, T)
        xp = jnp.dot(p_ref[...], x_ref[pl.ds(t0, T), :], preferred_element_type=F32)
        p0 = pl.multiple_of(jnp.maximum(t0 - H, 0), H)
        n0 = pl.multiple_of(jnp.minimum(t0 + T, seg_len - H), H)
        prev = x_ref[pl.ds(p0, H), :].astype(F32) * jnp.where(t > 0, 1.0, 0.0)
        nxt = x_ref[pl.ds(n0, H), :].astype(F32) * jnp.where(t < nt - 1, 1.0, 0.0)
        m1, m2, p1 = prev[H - 1:H], prev[H - 2:H - 1], nxt[0:1]
        last_m1 = shift_down(vreg(xp, V - 1), m1)
        xm1 = jnp.concatenate([last_m1, xp[:T - SUBLANE]], axis=0)
        xm2 = jnp.concatenate([shift_down(vreg(xp, V - 2), m2), last_m1, xp[:T - 2 * SUBLANE]], axis=0)
        xp1 = jnp.concatenate([xp[SUBLANE:], shift_up(vreg(xp, 0), p1)], axis=0)
        return cb + xm2 * cw[0:1] + xm1 * cw[1:2] + xp * cw[2:3] + xp1 * cw[3:4]

    def gates(xc, d):
        z = jnp.dot(xc.astype(BF16), wg_ref[d, 0], preferred_element_type=F32) + bg_ref[d, 0]
        r = _sigmoid(z[:, :LANE])
        i = _sigmoid(z[:, LANE:])
        log_a = (-LRU_C) * r * sp[d:d + 1]
        a = jnp.exp(log_a)
        return a, jnp.sqrt(1.0 - a * a) * (i * xc)

    def scan_tile(a, u, carry, reverse):
        hs, cum = [None] * V, [None] * V
        hp = ap = None
        for v in (range(V - 1, -1, -1) if reverse else range(V)):
            av, uv = vreg(a, v), vreg(u, v)
            hp, ap = (uv, av) if hp is None else (av * hp + uv, av * ap)
            hs[v], cum[v] = hp, ap
        s = 1
        while s < SUBLANE:
            keep = (sub < SUBLANE - s) if reverse else (sub >= s)
            sh = SUBLANE - s if reverse else s
            h_sh = jnp.where(keep, pltpu.roll(hp, sh, 0), 0.0)
            a_sh = jnp.where(keep, pltpu.roll(ap, sh, 0), 1.0)
            hp = hp + ap * h_sh
            ap = ap * a_sh
            s *= 2
        state = hp + ap * carry
        if reverse:
            start, out = shift_up(state, carry), state[0:1]
        else:
            start, out = shift_down(state, carry), state[SUBLANE - 1:SUBLANE]
        h = jnp.concatenate([hs[v] + cum[v] * start for v in range(V)], axis=0)
        return h, out

    def run_segment(g_ref, x_ref, y_ref, seg_len, carry_f, carry_b):
        nt = seg_len // T
        unroll = 4 if nt % 4 == 0 else (2 if nt % 2 == 0 else 1)

        def fwd_body(i, carry):
            for j in range(unroll):
                t = i * unroll + j
                t0 = pl.multiple_of(t * T, T)
                xc = conv_tile(x_ref, t, nt, seg_len)
                xs_scr[pl.ds(t0, T), :] = xc
                a, u = gates(xc, 0)
                h, carry = scan_tile(a, u, carry, False)
                hf_scr[pl.ds(t0, T), :] = h
            return carry

        def bwd_body(i, carry):
            for j in range(unroll):
                t = nt - 1 - (i * unroll + j)
                t0 = pl.multiple_of(t * T, T)
                a, u = gates(xs_scr[pl.ds(t0, T), :], 1)
                h, carry = scan_tile(a, u, carry, True)
                g = jnp.dot(p_ref[...], g_ref[pl.ds(t0, T), :], preferred_element_type=F32)
                rec = hf_scr[pl.ds(t0, T), :] + h
                y_ref[pl.ds(t0, T), :] = (jax.nn.gelu(g) * rec).astype(y_ref.dtype)
            return carry

        steps = nt // unroll
        return (lax.fori_loop(0, steps, fwd_body, carry_f),
                lax.fori_loop(0, steps, bwd_body, carry_b))

    zero = jnp.zeros((1, LANE), F32)
    hf, hb = run_segment(gc_ref, xc_ref, yc_ref, xc_ref.shape[0], zero, zero)
    run_segment(gl_ref, xl_ref, yl_ref, xl_ref.shape[0], hf, hb)


def _time_permutation():
    v = ROW_TILE // SUBLANE
    dst = jnp.arange(ROW_TILE)
    src = (dst % SUBLANE) * v + dst // SUBLANE
    return src[:, None] == jnp.arange(ROW_TILE)[None, :]


def _lru(u, lru_w, *, batch, seq, n_ctx, g_col0, x_col0):
    width = LRU_HEADS * LANE
    cw, cb, wg, bg, lam = lru_w
    ctx_blk0 = batch * seq // n_ctx
    perm = _time_permutation()
    once = pl.Buffered(1)
    return pl.pallas_call(
        _lru_kernel,
        out_shape=(jax.ShapeDtypeStruct((batch * n_ctx, width), BF16),
                   jax.ShapeDtypeStruct((batch * seq, width), BF16)),
        grid=(batch, LRU_HEADS),
        in_specs=[pl.BlockSpec((n_ctx, LANE), lambda b, h: (ctx_blk0 + b, g_col0 + h)),
                  pl.BlockSpec((n_ctx, LANE), lambda b, h: (ctx_blk0 + b, x_col0 + h)),
                  pl.BlockSpec((seq, LANE), lambda b, h: (b, g_col0 + h)),
                  pl.BlockSpec((seq, LANE), lambda b, h: (b, x_col0 + h)),
                  pl.BlockSpec((ROW_TILE, ROW_TILE), lambda b, h: (0, 0), pipeline_mode=once),
                  pl.BlockSpec((8, LANE), lambda b, h: (0, h)),
                  pl.BlockSpec((1, LANE), lambda b, h: (0, h)),
                  pl.BlockSpec((2, 1, LANE, 2 * LANE), lambda b, h: (0, h, 0, 0)),
                  pl.BlockSpec((2, 1, 1, 2 * LANE), lambda b, h: (0, h, 0, 0)),
                  pl.BlockSpec((2, LANE), lambda b, h: (0, h))],
        out_specs=(pl.BlockSpec((n_ctx, LANE), lambda b, h: (b, h)),
                   pl.BlockSpec((seq, LANE), lambda b, h: (b, h))),
        scratch_shapes=[pltpu.VMEM((seq, LANE), F32), pltpu.VMEM((seq, LANE), F32)],
        compiler_params=_params("parallel", "parallel"),
        name="rglru",
    )(u, u, u, u, perm.astype(BF16), cw, cb, wg, bg, lam)


def _chunk_mask(rows, reverse):
    r = lax.broadcasted_iota(jnp.int32, (rows, rows), 0)
    c = lax.broadcasted_iota(jnp.int32, (rows, rows), 1)
    same = (r >> CHUNK_SHIFT) == (c >> CHUNK_SHIFT)
    tri = (c >= r) if reverse else (c <= r)
    return jnp.where(same, jnp.where(tri, 1.0, 0.0), 0.0)


def _split3(x):
    hi = x.astype(BF16)
    r1 = x - hi.astype(F32)
    mid = r1.astype(BF16)
    lo = (r1 - mid.astype(F32)).astype(BF16)
    return hi, mid, lo


def _block_scan(g_all, head_qkv, st_scr, o_ref, *, heads, dv, reverse, batched):
    rows, wall = g_all.shape
    width = wall // heads
    nchunk = rows // CHUNK
    mask = _chunk_mask(rows, reverse)
    mask_b = mask.astype(BF16)
    keep = mask > 0.5
    keep_c = keep[:CHUNK, :CHUNK]
    iref = CHUNK // 2 if reverse else CHUNK // 2 - 1
    ilast = 0 if reverse else CHUNK - 1
    order = range(nchunk - 1, -1, -1) if reverse else range(nchunk)

    def per_chunk(rows_1, width):
        return jnp.concatenate([jnp.broadcast_to(r, (CHUNK, width)) for r in rows_1], axis=0)

    pieces = jnp.concatenate(_split3(g_all), axis=1)
    sums = jnp.dot(mask_b, pieces, preferred_element_type=F32)
    b_all = sums[:, :wall] + sums[:, wall:2 * wall] + sums[:, 2 * wall:]

    def prepare(h):
        q, k, v = head_qkv(h)
        b = b_all[:, h * width:(h + 1) * width]
        b_ref = [b[c * CHUNK + iref:c * CHUNK + iref + 1] for c in range(nchunk)]
        b_last = [b[c * CHUNK + ilast:c * CHUNK + ilast + 1] for c in range(nchunk)]
        e1 = jnp.exp(b - per_chunk(b_ref, width))
        r1 = 1.0 / e1
        qe = q * e1
        ke = k * r1
        qd = (qe * per_chunk([jnp.exp(r) for r in b_ref], width)).astype(BF16)
        kd = (ke * per_chunk([jnp.exp(l - r) for l, r in zip(b_last, b_ref)], width)).astype(BF16)
        decays = [jnp.exp(l) for l in b_last]
        return qe.astype(BF16), ke.astype(BF16), qd, kd, v.astype(BF16), decays

    nxt = prepare(0)
    for h in range(heads):
        qe, ke, qd, kd, vb, decays = nxt
        if h + 1 < heads:
            nxt = prepare(h + 1)
        if batched:
            sc = lax.dot_general(qe, ke, _NT, preferred_element_type=F32)
            sc = jnp.where(keep, sc, 0.0).astype(BF16)
            o = jnp.dot(sc, vb, preferred_element_type=F32)
        st = st_scr[h]
        parts = [None] * nchunk
        for ci in order:
            rs = slice(ci * CHUNK, (ci + 1) * CHUNK)
            if batched:
                o_c = o[rs]
            else:
                sc = lax.dot_general(qe[rs], ke[rs], _NT, preferred_element_type=F32)
                sc = jnp.where(keep_c, sc, 0.0).astype(BF16)
                o_c = jnp.dot(sc, vb[rs], preferred_element_type=F32)
            parts[ci] = o_c + lax.dot_general(qd[rs], st.astype(BF16), _NT,
                                              preferred_element_type=F32)
            st = st * decays[ci] + lax.dot_general(vb[rs], kd[rs], _TN, preferred_element_type=F32)
        if o_ref is not None:
            o_ref[:, h * dv:(h + 1) * dv] = jnp.concatenate(parts, axis=0).astype(o_ref.dtype)
        st_scr[h] = st


def _gla_kernel(q_ref, k_ref, v_ref, a_ref, wa_ref, ba_ref, o_ref, st_scr):
    t = pl.program_id(1)
    d = pl.program_id(2)
    dk = q_ref.shape[1] // GLA_HEADS
    dv = v_ref.shape[1] // GLA_HEADS

    @pl.when((t == 0) & (d == 0))
    def _():
        st_scr[...] = jnp.zeros(st_scr.shape, F32)

    def head_qkv(h):
        ks = slice(h * dk, (h + 1) * dk)
        q = q_ref[:, ks].astype(F32) * (dk ** -0.5)
        return q, k_ref[:, ks].astype(F32), v_ref[:, h * dv:(h + 1) * dv]

    for dd in range(2):
        @pl.when(d == dd)
        def _(dd=dd):
            z = jnp.dot(a_ref[...], wa_ref[dd], preferred_element_type=F32) + ba_ref[dd]
            g_all = _log_sigmoid(z) * (1.0 / GLA_TAU)
            _block_scan(g_all, head_qkv, st_scr.at[dd], o_ref,
                        heads=GLA_HEADS, dv=dv, reverse=dd == 1, batched=True)


def _gla(u, wa, ba, *, batch, seq, n_ctx, cols, dv_total):
    ntok = u.shape[0]
    q0, k0, v0, a0 = cols
    dkt = wa.shape[2]
    dvt = dv_total
    nblk = seq // ROW_TILE
    ctx_blk0 = batch * nblk
    assert n_ctx == ROW_TILE

    def blk(b, t, d):
        lat = b * nblk + jnp.where(d == 0, t - 1, nblk - t)
        return jnp.where(t == 0, ctx_blk0 + b, lat)

    in_specs = [pl.BlockSpec((ROW_TILE, dkt), lambda b, t, d: (blk(b, t, d), q0 * LANE // dkt)),
                pl.BlockSpec((ROW_TILE, dkt), lambda b, t, d: (blk(b, t, d), k0 * LANE // dkt)),
                pl.BlockSpec((ROW_TILE, dvt), lambda b, t, d: (blk(b, t, d), v0 * LANE // dvt)),
                pl.BlockSpec((ROW_TILE, LANE), lambda b, t, d: (blk(b, t, d), a0)),
                pl.BlockSpec(wa.shape, lambda b, t, d: (0, 0, 0)),
                pl.BlockSpec(ba.shape, lambda b, t, d: (0, 0, 0))]
    return pl.pallas_call(
        _gla_kernel,
        out_shape=jax.ShapeDtypeStruct((2, ntok, dvt), BF16),
        grid=(batch, nblk + 1, 2),
        in_specs=in_specs,
        out_specs=pl.BlockSpec((None, ROW_TILE, dvt), lambda b, t, d: (d, blk(b, t, d), 0)),
        scratch_shapes=[pltpu.VMEM((2, GLA_HEADS, dvt // GLA_HEADS, dkt // GLA_HEADS), F32)],
        compiler_params=_params("parallel", "arbitrary", "arbitrary"),
        name="gla",
    )(u, u, u, u, wa, ba)


def _hgrn_kernel(*refs, emit_o):
    q_ref, z_ref, v_ref, lb_ref, ub_ref, s0_ref = refs[:6]
    if emit_o:
        o_ref, st_ref, st_scr = refs[6:]
    else:
        st_ref, st_scr = refs[6:]
        o_ref = None
    t = pl.program_id(1)
    d = pl.program_id(2)
    dh = q_ref.shape[1] // HGRN_HEADS

    @pl.when((t == 0) & (d == 0))
    def _():
        st_scr[...] = s0_ref[:, 0]

    lb = lb_ref[...]
    ub = ub_ref[...]

    for dd in range(2):
        @pl.when(d == dd)
        def _(dd=dd):
            z = z_ref[...].astype(F32)
            e = jnp.exp(-jnp.abs(z))
            r = 1.0 / (1.0 + e)
            er = e * r
            pos = z >= 0.0
            log_f = jnp.log(lb + ub * jnp.where(pos, r, er))
            k_all = ub * jnp.where(pos, er, r)

            def head_qkv(h):
                hs = slice(h * dh, (h + 1) * dh)
                return q_ref[:, hs].astype(F32), k_all[:, hs], v_ref[:, hs]

            _block_scan(log_f, head_qkv, st_scr.at[dd], o_ref,
                        heads=HGRN_HEADS, dv=dh, reverse=dd == 1, batched=True)

    @pl.when((t == pl.num_programs(1) - 1) & (d == 1))
    def _():
        st_ref[:, 0] = st_scr[...]


def _hgrn_segment(u, s0, lb, ub, *, batch, seg_len, emit_o):
    ntok = u.shape[0]
    width = lb.shape[1]
    nblk = seg_len // ROW_TILE

    def blk(b, t, d):
        return b * nblk + jnp.where(d == 0, t, nblk - 1 - t)

    st_spec = pl.BlockSpec((2, 1) + s0.shape[2:], lambda b, t, d: (0, b, 0, 0, 0))
    in_specs = [pl.BlockSpec((ROW_TILE, width), lambda b, t, d: (blk(b, t, d), 0)),
                pl.BlockSpec((ROW_TILE, width), lambda b, t, d: (blk(b, t, d), 1 + d)),
                pl.BlockSpec((ROW_TILE, width), lambda b, t, d: (blk(b, t, d), 3)),
                pl.BlockSpec((1, width), lambda b, t, d: (0, 0)),
                pl.BlockSpec((1, width), lambda b, t, d: (0, 0)),
                st_spec]
    st_shape = jax.ShapeDtypeStruct(s0.shape, F32)
    if emit_o:
        out_shape = (jax.ShapeDtypeStruct((2, ntok, width), F32), st_shape)
        out_specs = (pl.BlockSpec((None, ROW_TILE, width), lambda b, t, d: (d, blk(b, t, d), 0)),
                     st_spec)
    else:
        out_shape = (st_shape,)
        out_specs = (st_spec,)
    return pl.pallas_call(
        functools.partial(_hgrn_kernel, emit_o=emit_o),
        out_shape=out_shape,
        grid=(batch, nblk, 2),
        in_specs=in_specs,
        out_specs=out_specs,
        scratch_shapes=[pltpu.VMEM((2,) + s0.shape[2:], F32)],
        compiler_params=_params("parallel", "arbitrary", "arbitrary"),
        name="hgrn_segment",
    )(u, u, u, lb, ub, s0)


def _head_norm_gate(o, gate, heads):
    dh = o.shape[1] // heads
    parts = []
    for h in range(heads):
        oh = o[:, h * dh:(h + 1) * dh]
        ms = jnp.mean(oh * oh, axis=-1, keepdims=True)
        parts.append((oh * lax.rsqrt(ms + EPS) * _silu(gate[:, h * dh:(h + 1) * dh])).astype(BF16))
    return jnp.concatenate(parts, axis=1)


def _load_rows(ref, r0, nrows):
    s_n = CHUNKS_PER_ROW
    return jnp.concatenate([ref[pl.ds(r0 * s_n + s, nrows, stride=s_n), :] for s in range(s_n)],
                           axis=1)


def _store_rows(ref, r0, val):
    s_n = CHUNKS_PER_ROW
    for s in range(s_n):
        ref[pl.ds(r0 * s_n + s, val.shape[0], stride=s_n), :] = val[:, s * LANE:(s + 1) * LANE]


def _out_proj_tail(o, x_ref, tab_ref, pg_ref, ng_ref, rt_ref, xo_ref, h_ref, lg_ref):
    xn = _gated_residual(x_ref[...], _rms(o, pg_ref[...]), tab_ref, 2)
    xo_ref[...] = xn
    h2 = _modulate(_rms(xn, ng_ref[...]), tab_ref, 3, 4)
    _store_rows(h_ref, 0, h2)
    lg_ref[...] = lax.dot_general(rt_ref[...], h2, _NT, precision=lax.Precision.HIGHEST,
                                  preferred_element_type=F32)


def _out_proj_ab_kernel(yl_ref, yc_ref, pt_ref, of_ref, ob_ref, gate_ref, w_ref, x_ref, tab_ref,
                        pg_ref, ng_ref, rt_ref, xo_ref, h_ref, lg_ref, *, lat_tiles):
    o_att = of_ref[...].astype(F32) + ob_ref[...].astype(F32)
    yb = _head_norm_gate(o_att, gate_ref[...].astype(F32), GLA_HEADS)
    ya = jnp.where(pl.program_id(0) < lat_tiles, yl_ref[...], yc_ref[...])
    ya = jnp.concatenate(
        [jnp.dot(pt_ref[...], ya[s:s + ROW_TILE], preferred_element_type=F32).astype(BF16)
         for s in range(0, ya.shape[0], ROW_TILE)], axis=0)
    da = ya.shape[1]
    o = jnp.dot(ya, w_ref[:da], preferred_element_type=F32)
    o = o + jnp.dot(yb, w_ref[da:], preferred_element_type=F32)
    _out_proj_tail(o, x_ref, tab_ref, pg_ref, ng_ref, rt_ref, xo_ref, h_ref, lg_ref)


def _out_proj_c_kernel(of_ref, ob_ref, gate_ref, p_ref, w_ref, x_ref, tab_ref, pg_ref, ng_ref,
                       rt_ref, xo_ref, h_ref, lg_ref):
    tm = x_ref.shape[0]
    width = of_ref.shape[-1]
    o_att = of_ref[...].astype(F32) + ob_ref[...].astype(F32)
    y = _head_norm_gate(o_att.reshape(tm, width), gate_ref[...].astype(F32).reshape(tm, width),
                        HGRN_HEADS)
    y = jnp.dot(p_ref[...], y, preferred_element_type=F32).astype(BF16)
    o = jnp.dot(y, w_ref[...], preferred_element_type=F32)
    _out_proj_tail(o, x_ref, tab_ref, pg_ref, ng_ref, rt_ref, xo_ref, h_ref, lg_ref)


def _out_proj(kernel, lead, lead_specs, w, x, tab, post_gain, next_gain, router_t, tm, ntiles):
    d = x.shape[1]
    ntok = ntiles * tm
    nsub = tm // ROW_TILE
    ne = router_t.shape[0]
    in_specs = list(lead_specs) + [
        pl.BlockSpec(w.shape, lambda i: (0, 0), pipeline_mode=pl.Buffered(1)),
        pl.BlockSpec((tm, d), lambda i: (i, 0)),
        pl.BlockSpec((nsub, 8, d), lambda i: (i, 0, 0)),
        pl.BlockSpec((1, d), lambda i: (0, 0)),
        pl.BlockSpec((1, d), lambda i: (0, 0)),
        pl.BlockSpec((ne, d), lambda i: (0, 0))]
    return pl.pallas_call(
        kernel,
        out_shape=(jax.ShapeDtypeStruct((ntok, d), F32),
                   jax.ShapeDtypeStruct((ntok * CHUNKS_PER_ROW, LANE), F32),
                   jax.ShapeDtypeStruct((ne, ntok), F32)),
        grid=(ntiles,),
        in_specs=in_specs,
        out_specs=(pl.BlockSpec((tm, d), lambda i: (i, 0)),
                   pl.BlockSpec((tm * CHUNKS_PER_ROW, LANE), lambda i: (i, 0)),
                   pl.BlockSpec((ne, tm), lambda i: (0, i))),
        compiler_params=_params("parallel"),
        name="out_proj_norm_residual",
    )(*lead, w, x, tab, post_gain, next_gain, router_t)


def _route_kernel(lg_ref, bias_ref, ids_ref, w_ref):
    tm = lg_ref.shape[1]
    per_group = N_EXPERTS // N_GROUPS
    shape3 = (N_GROUPS, per_group, tm)
    aff = _sigmoid(lg_ref[...]).reshape(shape3)
    biased = aff + bias_ref[...].reshape(shape3)
    neg = -jnp.inf
    sub = lax.broadcasted_iota(jnp.int32, shape3, 1)
    grp = lax.broadcasted_iota(jnp.int32, shape3, 0)
    m1 = jnp.max(biased, axis=1, keepdims=True)
    i1 = jnp.min(jnp.where(biased == m1, sub, per_group), axis=1, keepdims=True)
    m2 = jnp.max(jnp.where(sub == i1, neg, biased), axis=1, keepdims=True)
    score = m1 + m2
    gidx = lax.broadcasted_iota(jnp.int32, score.shape, 0)
    keep = jnp.zeros(score.shape, F32)
    for _ in range(TOPK_GROUPS):
        m = jnp.max(score, axis=0, keepdims=True)
        im = jnp.min(jnp.where(score == m, gidx, N_GROUPS), axis=0, keepdims=True)
        sel = gidx == im
        keep = jnp.where(sel, 1.0, keep)
        score = jnp.where(sel, neg, score)
    work = jnp.where(jnp.broadcast_to(keep, shape3) > 0.0, biased, neg)
    eidx = grp * per_group + sub
    ids, gates = [], []
    for _ in range(TOP_K):
        m = jnp.max(jnp.max(work, axis=0, keepdims=True), axis=1, keepdims=True)
        cand = jnp.where(work == m, eidx, N_EXPERTS)
        im = jnp.min(jnp.min(cand, axis=0, keepdims=True), axis=1, keepdims=True)
        sel = eidx == im
        ids.append(im)
        gates.append(jnp.sum(jnp.sum(jnp.where(sel, aff, 0.0), axis=0, keepdims=True),
                             axis=1, keepdims=True))
        work = jnp.where(sel, neg, work)
    gate = jnp.concatenate(gates, axis=1)
    den = jnp.sum(gate, axis=1, keepdims=True)
    w_ref[...] = (ROUTED_SCALE * gate / den).reshape(TOP_K, tm)
    ids_ref[...] = jnp.concatenate(ids, axis=1).reshape(TOP_K, tm)


def _route(logits_t, bias, tm):
    ne, ntok = logits_t.shape
    bias_b = jnp.broadcast_to(bias.astype(F32)[:, None], (ne, tm))
    return pl.pallas_call(
        _route_kernel,
        out_shape=(jax.ShapeDtypeStruct((TOP_K, ntok), jnp.int32),
                   jax.ShapeDtypeStruct((TOP_K, ntok), F32)),
        grid=(ntok // tm,),
        in_specs=[pl.BlockSpec((ne, tm), lambda i: (0, i)),
                  pl.BlockSpec((ne, tm), lambda i: (0, 0))],
        out_specs=(pl.BlockSpec((TOP_K, tm), lambda i: (0, i)),
                   pl.BlockSpec((TOP_K, tm), lambda i: (0, i))),
        compiler_params=_params("parallel"),
        name="moe_route",
    )(logits_t, bias_b)


LIST_PAD = 1024


def _dispatch_plan(ids, w, t0, tile, ntiles):
    n = ntiles * tile
    e = ids[:, t0:t0 + n].T.reshape(ntiles, tile * TOP_K)
    g = w[:, t0:t0 + n].T.reshape(ntiles, tile * TOP_K)
    nassign = tile * TOP_K
    assert nassign & (nassign - 1) == 0 and N_EXPERTS * nassign < 2 ** 31
    keys = e * nassign + jnp.arange(nassign, dtype=jnp.int32)
    order = jnp.sort(keys, axis=1) & (nassign - 1)
    rows = (order // TOP_K) * CHUNKS_PER_ROW
    wts = jnp.take_along_axis(g, order, axis=1)
    experts = jnp.arange(N_EXPERTS, dtype=jnp.int32)
    counts = jnp.sum((e[:, :, None] == experts).astype(jnp.int32), axis=1)
    starts = jnp.concatenate([jnp.zeros((ntiles, 1), jnp.int32), jnp.cumsum(counts, axis=1)], axis=1)
    rows = jnp.pad(rows, ((0, 0), (0, LIST_PAD)))
    wts = jnp.pad(wts, ((0, 0), (0, LIST_PAD)))
    starts = jnp.pad(starts, ((0, 0), (0, LANE - N_EXPERTS - 1)))
    return rows, wts, starts


def _moe_kernel(rows_hbm, wts_hbm, starts_hbm, h_ref, wg_ref, wu_ref, wd_ref, sg_ref, su_ref,
                sd_ref, o_ref, rows_s, wts_s, starts_s, sem, gbuf, ybuf, *, epb):
    i = pl.program_id(0)
    j = pl.program_id(1)
    s_n = CHUNKS_PER_ROW
    blk = MOE_BLOCK
    tile = h_ref.shape[0] // s_n

    def swiglu(x, w_gate, w_up, w_down):
        g = jnp.dot(x, w_gate, preferred_element_type=F32)
        u = jnp.dot(x, w_up, preferred_element_type=F32)
        return jnp.dot((_silu(g) * u).astype(BF16), w_down, preferred_element_type=F32)

    @pl.when(j == 0)
    def _():
        copies = [pltpu.make_async_copy(rows_hbm.at[i], rows_s, sem.at[0]),
                  pltpu.make_async_copy(wts_hbm.at[i], wts_s, sem.at[1]),
                  pltpu.make_async_copy(starts_hbm.at[i], starts_s, sem.at[2])]
        for c in copies:
            c.start()
        for r0 in range(0, tile, ROW_TILE):
            x = _load_rows(h_ref, r0, ROW_TILE).astype(BF16)
            _store_rows(o_ref, r0, swiglu(x, sg_ref[...], su_ref[...], sd_ref[...]))
        for c in copies:
            c.wait()

    for e in range(epb):
        base = starts_s[j * epb + e]
        count = starts_s[j * epb + e + 1] - base
        w_gate = wg_ref[e].astype(BF16)
        w_up = wu_ref[e].astype(BF16)
        w_down = wd_ref[e].astype(BF16)

        def block(bi, carry, base=base, count=count, w_gate=w_gate, w_up=w_up, w_down=w_down):
            p0 = base + bi * blk
            left = count - bi * blk
            offs = [pl.multiple_of(rows_s[p0 + r], s_n) for r in range(blk)]
            for r in range(blk):
                gbuf[pl.ds(r * s_n, s_n), :] = h_ref[pl.ds(offs[r], s_n), :]
            x = _load_rows(gbuf, 0, blk).astype(BF16)
            y = swiglu(x, w_gate, w_up, w_down)
            y = jnp.where(lax.broadcasted_iota(jnp.int32, y.shape, 0) < left, y, 0.0)
            _store_rows(ybuf, 0, y)
            group = 2 * SUBLANE
            for r0 in range(0, blk, group):
                vals = [o_ref[pl.ds(offs[r], s_n), :] + wts_s[p0 + r] * ybuf[pl.ds(r * s_n, s_n), :]
                        for r in range(r0, r0 + group)]
                for r in reversed(range(r0, r0 + group)):
                    o_ref[pl.ds(offs[r], s_n), :] = vals[r - r0]
            return carry

        nblk = lax.shift_right_logical(count + (blk - 1), blk.bit_length() - 1)
        lax.fori_loop(0, nblk, block, 0)


def _moe(h, plan, layer, wg, wu, wd, sg, su, sd, tile0, ntiles, tile, epb):
    rows, wts, starts = plan
    s_n = CHUNKS_PER_ROW
    _, ne, d, ff = wg.shape
    once = pl.Buffered(1)
    hbm = pl.BlockSpec(memory_space=pl.ANY)
    return pl.pallas_call(
        functools.partial(_moe_kernel, epb=epb),
        out_shape=jax.ShapeDtypeStruct((ntiles * tile * s_n, LANE), F32),
        grid=(ntiles, ne // epb),
        in_specs=[hbm, hbm, hbm,
                  pl.BlockSpec((tile * s_n, LANE), lambda i, j: (tile0 + i, 0), pipeline_mode=once),
                  pl.BlockSpec((None, epb, d, ff), lambda i, j: (layer, j, 0, 0)),
                  pl.BlockSpec((None, epb, d, ff), lambda i, j: (layer, j, 0, 0)),
                  pl.BlockSpec((None, epb, ff, d), lambda i, j: (layer, j, 0, 0)),
                  pl.BlockSpec(sg.shape, lambda i, j: (0, 0), pipeline_mode=once),
                  pl.BlockSpec(su.shape, lambda i, j: (0, 0), pipeline_mode=once),
                  pl.BlockSpec(sd.shape, lambda i, j: (0, 0), pipeline_mode=once)],
        out_specs=pl.BlockSpec((tile * s_n, LANE), lambda i, j: (i, 0)),
        scratch_shapes=[pltpu.SMEM((rows.shape[1],), jnp.int32),
                        pltpu.SMEM((wts.shape[1],), F32),
                        pltpu.SMEM((starts.shape[1],), jnp.int32),
                        pltpu.SemaphoreType.DMA((3,)),
                        pltpu.VMEM((MOE_BLOCK * s_n, LANE), F32),
                        pltpu.VMEM((MOE_BLOCK * s_n, LANE), F32)],
        compiler_params=_params("parallel", "arbitrary", vmem=VMEM_LIMIT_MOE),
        name="moe_experts",
    )(rows, wts, starts, h, wg, wu, wd, sg, su, sd)


def _moe_finish_kernel(f_ref, x_ref, tab_ref, pg_ref, o_ref):
    f = _load_rows(f_ref, 0, x_ref.shape[0])
    o_ref[...] = _gated_residual(x_ref[...], _rms(f, pg_ref[...]), tab_ref, 5)


def _moe_finish(f, x, tab, post_gain, tm, x_tile0, tab0):
    d = x.shape[1]
    s_n = CHUNKS_PER_ROW
    nsub = tm // ROW_TILE
    ntiles = f.shape[0] // (tm * s_n)
    return pl.pallas_call(
        _moe_finish_kernel,
        out_shape=jax.ShapeDtypeStruct((ntiles * tm, d), F32),
        grid=(ntiles,),
        in_specs=[pl.BlockSpec((tm * s_n, LANE), lambda i: (i, 0)),
                  pl.BlockSpec((tm, d), lambda i: (x_tile0 + i, 0)),
                  pl.BlockSpec((nsub, 8, d), lambda i: (tab0 + i, 0, 0)),
                  pl.BlockSpec((1, d), lambda i: (0, 0))],
        out_specs=pl.BlockSpec((tm, d), lambda i: (i, 0)),
        compiler_params=_params("parallel"),
        name="moe_norm_residual",
    )(f, x, tab, post_gain)


def _grid_permutation(band):
    tm = band * GRID_W
    dst = jnp.arange(tm)
    src = (dst % band) * GRID_W + dst // band
    return (src[:, None] == jnp.arange(tm)[None, :]).astype(BF16)


def kernel(x, c, ctx, c_ctx, mod_w, mod_b, pre_gain, post_gain, ab_w_in, ab_conv_w, ab_conv_b, ab_lru_wa, ab_lru_ba, ab_lru_wi, ab_lru_bi, ab_lru_lambda, ab_gla_wa2, ab_gla_ba, ab_w_out, c_w_in, hgrn_lb_logits, c_w_out, moe_router, moe_bias, moe_w_gate, moe_w_up, moe_w_down, moe_ws_gate, moe_ws_up, moe_ws_down):
    batch, seq, d = x.shape
    n_ctx = ctx.shape[1]
    depth = mod_w.shape[0]
    band = SUBLANE
    tm_grid = band * GRID_W
    rows_img = seq // GRID_W
    assert n_ctx == ROW_TILE and seq % (2 * ROW_TILE) == 0 and d % LANE == 0
    assert depth == 2 and batch + 1 <= 8 and rows_img % band == 0 and tm_grid % ROW_TILE == 0
    n_lat = batch * seq
    n_ctx_tok = batch * n_ctx
    ntok = n_lat + n_ctx_tok

    xs = jnp.concatenate([x.reshape(n_lat, d), ctx.reshape(n_ctx_tok, d)], axis=0)

    cvec = jnp.concatenate([c, c_ctx[None, :], jnp.zeros((8 - batch - 1, d), F32)], axis=0)
    mods = _modulation(cvec, mod_w, mod_b).reshape(depth, 8, N_MOD, d)
    tile_row = jnp.concatenate([jnp.repeat(jnp.arange(batch), seq // ROW_TILE),
                                jnp.full((batch,), batch)]).astype(jnp.int32)
    tabs = jnp.pad(mods[:, tile_row], ((0, 0), (0, 0), (0, 8 - N_MOD), (0, 0)))

    def experts(layer, h2, routed, t0, ntiles, tile):
        plan = _dispatch_plan(*routed, t0, tile, ntiles)
        return _moe(h2, plan, layer, moe_w_gate, moe_w_up, moe_w_down,
                    moe_ws_gate[layer].astype(BF16), moe_ws_up[layer].astype(BF16),
                    moe_ws_down[layer].astype(BF16), t0 // tile, ntiles, tile, epb)

    tm_proj, tm_out, tm_route, tm_moe, epb = 512, 512, 512, 2048, 4
    assert n_lat % tm_moe == 0 and n_lat % n_ctx_tok == 0

    ab_cols = ab_w_in.shape[2]
    gla_dk = ab_gla_wa2.shape[3]
    lru_w = d
    gla_dv = (ab_cols - 2 * lru_w - 2 * gla_dk - 2 * GLA_RANK) // 2
    tn0 = 1792
    n0 = -(-(ab_cols) // tn0) * tn0
    w_in0 = jnp.pad(ab_w_in[0], ((0, 0), (0, n0 - ab_cols))).astype(BF16)
    u0 = _in_proj(xs, tabs[0], pre_gain[0, 0][None, :], w_in0, tm_proj, tn0, 0)

    cw = jnp.pad(ab_conv_w[0], ((0, 8 - CONV_W), (0, 0)))
    cb = ab_conv_b[0][None, :]
    wg = jnp.concatenate([ab_lru_wa[0], ab_lru_wi[0]], axis=-1).astype(BF16)
    bg = jnp.concatenate([ab_lru_ba[0].reshape(2, LRU_HEADS, 1, LANE),
                          ab_lru_bi[0].reshape(2, LRU_HEADS, 1, LANE)], axis=-1)
    ya_c, ya_l = _lru(u0, (cw, cb, wg, bg, ab_lru_lambda[0]), batch=batch, seq=seq, n_ctx=n_ctx,
                      g_col0=0, x_col0=lru_w // LANE)

    q0 = 2 * lru_w // LANE
    k0 = q0 + gla_dk // LANE
    v0 = k0 + gla_dk // LANE
    gate0 = v0 + gla_dv // LANE
    a0 = gate0 + gla_dv // LANE
    wa = jnp.stack([jnp.zeros((LANE, gla_dk), F32).at[dr * GLA_RANK:(dr + 1) * GLA_RANK]
                    .set(ab_gla_wa2[0, dr]) for dr in range(2)]).astype(BF16)
    ba = ab_gla_ba[0][:, None, :]
    o_gla = _gla(u0, wa, ba, batch=batch, seq=seq, n_ctx=n_ctx, cols=(q0, k0, v0, a0),
                 dv_total=gla_dv)

    lat_tiles = n_lat // tm_out
    ctx_tiles = n_ctx_tok // tm_out
    lead_specs = [pl.BlockSpec((tm_out, lru_w), lambda i: (jnp.minimum(i, lat_tiles - 1), 0)),
                  pl.BlockSpec((tm_out, lru_w),
                               lambda i: (jnp.clip(i - lat_tiles, 0, ctx_tiles - 1), 0)),
                  pl.BlockSpec((ROW_TILE, ROW_TILE), lambda i: (0, 0)),
                  pl.BlockSpec((None, tm_out, gla_dv), lambda i: (0, i, 0)),
                  pl.BlockSpec((None, tm_out, gla_dv), lambda i: (1, i, 0)),
                  pl.BlockSpec((tm_out, gla_dv), lambda i: (i, gate0 * LANE // gla_dv))]
    xs, h2, lg = _out_proj(functools.partial(_out_proj_ab_kernel, lat_tiles=lat_tiles),
                           (ya_l, ya_c, _time_permutation().T.astype(BF16), o_gla, o_gla, u0),
                           lead_specs,
                           ab_w_out[0].astype(BF16), xs, tabs[0], post_gain[0, 0][None, :],
                           pre_gain[0, 1][None, :], moe_router[0].T, tm_out, ntok // tm_out)
    cwt = _route(lg, moe_bias[0], tm_route)
    f_ctx = experts(0, h2, cwt, n_lat, 1, n_ctx_tok)
    f_lat = experts(0, h2, cwt, 0, n_lat // tm_moe, tm_moe)
    pg0 = post_gain[0, 1][None, :]

    hd = c_w_out.shape[1]
    tn1 = 1280
    w_in1 = c_w_in[0].astype(BF16)
    c_cols = w_in1.shape[1]
    nsec = c_cols // hd
    perm = _grid_permutation(band)
    gain1 = pre_gain[1, 0][None, :]
    u1c = _in_proj_ctx(f_ctx, xs, tabs[0], pg0, tabs[1], gain1, w_in1, tm_proj, tn1,
                       n_lat // tm_proj)
    u1l, xs = _in_proj_grid(f_lat, xs, tabs[0], pg0, tabs[1], gain1, perm, w_in1, tn1,
                            n_lat // tm_grid, band, rows_img // band)
    lb_sm = jax.nn.softmax(hgrn_lb_logits.astype(F32), axis=0)
    lb_cum = jnp.cumsum(lb_sm, axis=0)
    lb = (lb_cum[1] - lb_cum[0])[None, :]
    ub = 1.0 - lb
    dh = hd // HGRN_HEADS
    s_zero = jnp.zeros((2, batch, HGRN_HEADS, dh, dh), F32)
    u1l_flat = u1l.reshape(n_lat, c_cols)
    (s_c,) = _hgrn_segment(u1c, s_zero, lb, ub, batch=batch, seg_len=n_ctx, emit_o=False)
    o_hgrn, _ = _hgrn_segment(u1l_flat, s_c, lb, ub, batch=batch, seg_len=seq, emit_o=True)
    o_hgrn = o_hgrn.reshape(2, n_lat // rows_img, rows_img // band, band, hd)

    bpi = rows_img // band
    grid_blk = (GRID_W, None, band, hd)
    lead_specs = [pl.BlockSpec((None,) + grid_blk, lambda i: (0, i // bpi, i % bpi, 0, 0)),
                  pl.BlockSpec((None,) + grid_blk, lambda i: (1, i // bpi, i % bpi, 0, 0)),
                  pl.BlockSpec(grid_blk, lambda i: (i // bpi, i % bpi, 0, nsec - 1)),
                  pl.BlockSpec((tm_grid, tm_grid), lambda i: (0, 0))]
    xs, h2, lg = _out_proj(_out_proj_c_kernel, (o_hgrn, o_hgrn, u1l, perm.T), lead_specs,
                           c_w_out[0].astype(BF16), xs, tabs[1], post_gain[1, 0][None, :],
                           pre_gain[1, 1][None, :], moe_router[1].T, tm_grid, n_lat // tm_grid)
    f_lat = experts(1, h2, _route(lg, moe_bias[1], tm_route), 0, n_lat // tm_moe, tm_moe)
    out = _moe_finish(f_lat, xs, tabs[1], post_gain[1, 1][None, :], tm_out, 0, 0)
    return out.reshape(batch, seq, d)
```

```python
import functools

import jax
import jax.numpy as jnp
from jax import lax
from jax.experimental import pallas as pl
from jax.experimental.pallas import tpu as pltpu

F32 = jnp.float32
BF16 = jnp.bfloat16

EPS = 1e-6
N_MOD = 6
ROW_TILE = 256
CHUNK = 64
CHUNK_SHIFT = 6
GRID_W = 64
LRU_HEADS = 8
LRU_C = 8.0
CONV_W = 4
GLA_HEADS = 4
GLA_RANK = 16
GLA_TAU = 16.0
HGRN_HEADS = 8
N_EXPERTS = 64
TOP_K = 8
N_GROUPS = 8
TOPK_GROUPS = 4
ROUTED_SCALE = 2.5
LANE = 128
SUBLANE = 8
BF16_ROWS = 16
CHUNKS_PER_ROW = 8
MOE_BLOCK = 128
VMEM_LIMIT = 48 * 1024 * 1024
VMEM_LIMIT_MOE = 58 * 1024 * 1024

_NT = (((1,), (1,)), ((), ()))
_TN = (((0,), (0,)), ((), ()))


def _params(*sem, vmem=VMEM_LIMIT):
    return pltpu.CompilerParams(dimension_semantics=sem, vmem_limit_bytes=vmem)


def _rms(x, gain):
    ms = jnp.mean(x * x, axis=-1, keepdims=True)
    return x * lax.rsqrt(ms + EPS) * gain


def _sigmoid(x):
    return jax.nn.sigmoid(x)


def _silu(x):
    return x * _sigmoid(x)


def _log_sigmoid(x):
    return jnp.minimum(x, 0.0) - jnp.log1p(jnp.exp(-jnp.abs(x)))


def _softplus(x):
    return jnp.maximum(x, 0.0) + jnp.log1p(jnp.exp(-jnp.abs(x)))


def _modulate(xn, tab_ref, shift_row, scale_row):
    parts = []
    for s in range(tab_ref.shape[0]):
        t = tab_ref[s]
        rows = xn[s * ROW_TILE:(s + 1) * ROW_TILE]
        parts.append(rows * (1.0 + t[scale_row:scale_row + 1]) + t[shift_row:shift_row + 1])
    return parts[0] if len(parts) == 1 else jnp.concatenate(parts, axis=0)


def _gated_residual(x, branch, tab_ref, gate_row):
    parts = []
    for s in range(tab_ref.shape[0]):
        sl = slice(s * ROW_TILE, (s + 1) * ROW_TILE)
        parts.append(x[sl] + tab_ref[s][gate_row:gate_row + 1] * branch[sl])
    return parts[0] if len(parts) == 1 else jnp.concatenate(parts, axis=0)


def _mod_kernel(c_ref, w_ref, b_ref, o_ref):
    s = _silu(c_ref[...])
    o_ref[0] = jnp.dot(s, w_ref[0], precision=lax.Precision.HIGHEST,
                       preferred_element_type=F32) + b_ref[0]


def _modulation(cvec, mod_w, mod_b):
    depth, d, n = mod_w.shape
    tn = 1024
    return pl.pallas_call(
        _mod_kernel,
        out_shape=jax.ShapeDtypeStruct((depth, 8, n), F32),
        grid=(depth, n // tn),
        in_specs=[pl.BlockSpec((8, d), lambda l, j: (0, 0)),
                  pl.BlockSpec((1, d, tn), lambda l, j: (l, 0, j)),
                  pl.BlockSpec((1, 1, tn), lambda l, j: (l, 0, j))],
        out_specs=pl.BlockSpec((1, 8, tn), lambda l, j: (l, 0, j)),
        compiler_params=_params("parallel", "parallel"),
        name="adaln_modulation",
    )(cvec, mod_w, mod_b.reshape(depth, 1, n))


def _in_proj_kernel(x_ref, tab_ref, gain_ref, w_ref, wt_ref, o_ref, *, tn):
    h = _modulate(_rms(x_ref[...], gain_ref[...]), tab_ref, 0, 1).astype(BF16)
    n_main = w_ref.shape[1]
    for j in range(n_main // tn):
        cs = slice(j * tn, (j + 1) * tn)
        o_ref[:, cs] = jnp.dot(h, w_ref[:, cs], preferred_element_type=F32).astype(o_ref.dtype)
    o_ref[:, n_main:] = jnp.dot(h, wt_ref[...], preferred_element_type=F32).astype(o_ref.dtype)


def _in_proj(x, tab, gain, w, w_tail, tm, tn, tab0):
    d = x.shape[1]
    n = w.shape[1] + w_tail.shape[1]
    nsub = tm // ROW_TILE
    ntiles = x.shape[0] // tm
    once = pl.Buffered(1)
    return pl.pallas_call(
        functools.partial(_in_proj_kernel, tn=tn),
        out_shape=jax.ShapeDtypeStruct((ntiles * tm, n), BF16),
        grid=(ntiles,),
        in_specs=[pl.BlockSpec((tm, d), lambda i: (i, 0)),
                  pl.BlockSpec((nsub, 8, d), lambda i: (tab0 + i, 0, 0)),
                  pl.BlockSpec((1, d), lambda i: (0, 0)),
                  pl.BlockSpec(w.shape, lambda i: (0, 0), pipeline_mode=once),
                  pl.BlockSpec(w_tail.shape, lambda i: (0, 0), pipeline_mode=once)],
        out_specs=pl.BlockSpec((tm, n), lambda i: (i, 0)),
        compiler_params=_params("parallel"),
        name="norm_mod_in_proj",
    )(x, tab, gain, w, w_tail)


def _after_moe(f_ref, x_ref, ptab_ref, pg_ref):
    f = _load_rows(f_ref, 0, x_ref.shape[0])
    return _gated_residual(x_ref[...], _rms(f, pg_ref[...]), ptab_ref, 5)


def _in_proj_ctx_kernel(f_ref, x_ref, ptab_ref, pg_ref, tab_ref, gain_ref, w_ref, o_ref, *, tn):
    xn = _after_moe(f_ref, x_ref, ptab_ref, pg_ref)
    h = _modulate(_rms(xn, gain_ref[...]), tab_ref, 0, 1).astype(BF16)
    for j in range(w_ref.shape[1] // tn):
        cs = slice(j * tn, (j + 1) * tn)
        o_ref[:, cs] = jnp.dot(h, w_ref[:, cs], preferred_element_type=F32).astype(o_ref.dtype)


def _in_proj_ctx(f, x, ptab, post_gain, tab, gain, w, tm, tn, tile0):
    d = x.shape[1]
    n = w.shape[1]
    s_n = CHUNKS_PER_ROW
    nsub = tm // ROW_TILE
    ntiles = f.shape[0] // (tm * s_n)
    tabspec = pl.BlockSpec((nsub, 8, d), lambda i: (tile0 + i, 0, 0))
    vec = pl.BlockSpec((1, d), lambda i: (0, 0))
    return pl.pallas_call(
        functools.partial(_in_proj_ctx_kernel, tn=tn),
        out_shape=jax.ShapeDtypeStruct((ntiles * tm, n), BF16),
        grid=(ntiles,),
        in_specs=[pl.BlockSpec((tm * s_n, LANE), lambda i: (i, 0)),
                  pl.BlockSpec((tm, d), lambda i: (tile0 + i, 0)),
                  tabspec, vec, tabspec, vec,
                  pl.BlockSpec((d, n), lambda i: (0, 0), pipeline_mode=pl.Buffered(1))],
        out_specs=pl.BlockSpec((tm, n), lambda i: (i, 0)),
        compiler_params=_params("parallel"),
        name="moe_close_in_proj",
    )(f, x, ptab, post_gain, tab, gain, w)


def _in_proj_grid_kernel(f_ref, x_ref, ptab_ref, pg_ref, tab_ref, gain_ref, p_ref, w_ref,
                         o_ref, xo_ref, *, tn):
    xn = _after_moe(f_ref, x_ref, ptab_ref, pg_ref)
    xo_ref[...] = xn
    h = _modulate(_rms(xn, gain_ref[...]), tab_ref, 0, 1).astype(BF16)
    hp = jnp.dot(p_ref[...], h, preferred_element_type=F32).astype(BF16)
    for j in range(w_ref.shape[1] // tn):
        cs = slice(j * tn, (j + 1) * tn)
        o = jnp.dot(hp, w_ref[:, cs], preferred_element_type=F32).astype(o_ref.dtype)
        o_ref[:, :, cs] = o.reshape(o_ref.shape[:2] + (tn,))


def _in_proj_grid(f, x, ptab, post_gain, tab, gain, perm, w, tn, ntiles, band, bands_per_image):
    d = x.shape[1]
    n = w.shape[1]
    s_n = CHUNKS_PER_ROW
    tm = band * GRID_W
    nsub = tm // ROW_TILE
    images = ntiles // bands_per_image
    once = pl.Buffered(1)
    tabspec = pl.BlockSpec((nsub, 8, d), lambda i: (i, 0, 0))
    vec = pl.BlockSpec((1, d), lambda i: (0, 0))
    return pl.pallas_call(
        functools.partial(_in_proj_grid_kernel, tn=tn),
        out_shape=(jax.ShapeDtypeStruct((images * GRID_W, bands_per_image, band, n), F32),
                   jax.ShapeDtypeStruct((ntiles * tm, d), F32)),
        grid=(ntiles,),
        in_specs=[pl.BlockSpec((tm * s_n, LANE), lambda i: (i, 0)),
                  pl.BlockSpec((tm, d), lambda i: (i, 0)),
                  tabspec, vec, tabspec, vec,
                  pl.BlockSpec((tm, tm), lambda i: (0, 0), pipeline_mode=once),
                  pl.BlockSpec((d, n), lambda i: (0, 0), pipeline_mode=once)],
        out_specs=(pl.BlockSpec((GRID_W, None, band, n),
                                lambda i: (i // bands_per_image, i % bands_per_image, 0, 0)),
                   pl.BlockSpec((tm, d), lambda i: (i, 0))),
        compiler_params=_params("parallel"),
        name="moe_close_in_proj_grid",
    )(f, x, ptab, post_gain, tab, gain, perm, w)


def _lru_kernel(gc_ref, xc_ref, gl_ref, xl_ref, p_ref, cw_ref, cb_ref, wg_ref, bg_ref,
                lam_ref, yc_ref, yl_ref, xs_scr, hf_scr):
    T = ROW_TILE
    V = T // SUBLANE
    H = BF16_ROWS
    sub = lax.broadcasted_iota(jnp.int32, (SUBLANE, LANE), 0)
    cw = cw_ref[...]
    cb = cb_ref[...]
    sp = _softplus(-lam_ref[...])

    def vreg(a, v):
        return a[v * SUBLANE:(v + 1) * SUBLANE]

    def shift_down(y, first):
        return jnp.where(sub == 0, first, pltpu.roll(y, 1, 0))

    def shift_up(y, last):
        return jnp.where(sub == SUBLANE - 1, last, pltpu.roll(y, SUBLANE - 1, 0))

    def conv_tile(x_ref, t, nt, seg_len):
        t0 = pl.multiple_of(t * T, T)
        xp = jnp.dot(p_ref[...], x_ref[pl.ds(t0, T), :], preferred_element_type=F32)
        p0 = pl.multiple_of(jnp.maximum(t0 - H, 0), H)
        n0 = pl.multiple_of(jnp.minimum(t0 + T, seg_len - H), H)
        prev = x_ref[pl.ds(p0, H), :].astype(F32) * jnp.where(t > 0, 1.0, 0.0)
        nxt = x_ref[pl.ds(n0, H), :].astype(F32) * jnp.where(t < nt - 1, 1.0, 0.0)
        m1, m2, p1 = prev[H - 1:H], prev[H - 2:H - 1], nxt[0:1]
        last_m1 = shift_down(vreg(xp, V - 1), m1)
        xm1 = jnp.concatenate([last_m1, xp[:T - SUBLANE]], axis=0)
        xm2 = jnp.concatenate([shift_down(vreg(xp, V - 2), m2), last_m1, xp[:T - 2 * SUBLANE]], axis=0)
        xp1 = jnp.concatenate([xp[SUBLANE:], shift_up(vreg(xp, 0), p1)], axis=0)
        return cb + xm2 * cw[0:1] + xm1 * cw[1:2] + xp * cw[2:3] + xp1 * cw[3:4]

    def gates(xc, d):
        z = jnp.dot(xc.astype(BF16), wg_ref[d, 0], preferred_element_type=F32) + bg_ref[d, 0]
        r = _sigmoid(z[:, :LANE])
        i = _sigmoid(z[:, LANE:])
        log_a = (-LRU_C) * r * sp[d:d + 1]
        a = jnp.exp(log_a)
        return a, jnp.sqrt(1.0 - a * a) * (i * xc)

    def scan_tile(a, u, carry, reverse):
        hs, cum = [None] * V, [None] * V
        hp = ap = None
        for v in (range(V - 1, -1, -1) if reverse else range(V)):
            av, uv = vreg(a, v), vreg(u, v)
            hp, ap = (uv, av) if hp is None else (av * hp + uv, av * ap)
            hs[v], cum[v] = hp, ap
        s = 1
        while s < SUBLANE:
            keep = (sub < SUBLANE - s) if reverse else (sub >= s)
            sh = SUBLANE - s if reverse else s
            h_sh = jnp.where(keep, pltpu.roll(hp, sh, 0), 0.0)
            a_sh = jnp.where(keep, pltpu.roll(ap, sh, 0), 1.0)
            hp = hp + ap * h_sh
            ap = ap * a_sh
            s *= 2
        state = hp + ap * carry
        if reverse:
            start, out = shift_up(state, carry), state[0:1]
        else:
            start, out = shift_down(state, carry), state[SUBLANE - 1:SUBLANE]
        h = jnp.concatenate([hs[v] + cum[v] * start for v in range(V)], axis=0)
        return h, out

    def run_segment(g_ref, x_ref, y_ref, seg_len, carry_f, carry_b):
        nt = seg_len // T
        unroll = 4 if nt % 4 == 0 else (2 if nt % 2 == 0 else 1)

        def fwd_body(i, carry):
            for j in range(unroll):
                t = i * unroll + j
                t0 = pl.multiple_of(t * T, T)
                xc = conv_tile(x_ref, t, nt, seg_len)
                xs_scr[pl.ds(t0, T), :] = xc
                a, u = gates(xc, 0)
                h, carry = scan_tile(a, u, carry, False)
                hf_scr[pl.ds(t0, T), :] = h
            return carry

        def bwd_body(i, carry):
            for j in range(unroll):
                t = nt - 1 - (i * unroll + j)
                t0 = pl.multiple_of(t * T, T)
                a, u = gates(xs_scr[pl.ds(t0, T), :], 1)
                h, carry = scan_tile(a, u, carry, True)
                g = jnp.dot(p_ref[...], g_ref[pl.ds(t0, T), :], preferred_element_type=F32)
                rec = hf_scr[pl.ds(t0, T), :] + h
                y_ref[pl.ds(t0, T), :] = (jax.nn.gelu(g) * rec).astype(y_ref.dtype)
            return carry

        steps = nt // unroll
        return (lax.fori_loop(0, steps, fwd_body, carry_f),
                lax.fori_loop(0, steps, bwd_body, carry_b))

    zero = jnp.zeros((1, LANE), F32)
    hf, hb = run_segment(gc_ref, xc_ref, yc_ref, xc_ref.shape[0], zero, zero)
    run_segment(gl_ref, xl_ref, yl_ref, xl_ref.shape[0], hf, hb)


def _time_permutation():
    v = ROW_TILE // SUBLANE
    dst = jnp.arange(ROW_TILE)
    src = (dst % SUBLANE) * v + dst // SUBLANE
    return src[:, None] == jnp.arange(ROW_TILE)[None, :]


def _lru(u, lru_w, *, batch, seq, n_ctx, g_col0, x_col0):
    width = LRU_HEADS * LANE
    cw, cb, wg, bg, lam = lru_w
    ctx_blk0 = batch * seq // n_ctx
    perm = _time_permutation()
    once = pl.Buffered(1)
    return pl.pallas_call(
        _lru_kernel,
        out_shape=(jax.ShapeDtypeStruct((batch * n_ctx, width), BF16),
                   jax.ShapeDtypeStruct((batch * seq, width), BF16)),
        grid=(batch, LRU_HEADS),
        in_specs=[pl.BlockSpec((n_ctx, LANE), lambda b, h: (ctx_blk0 + b, g_col0 + h)),
                  pl.BlockSpec((n_ctx, LANE), lambda b, h: (ctx_blk0 + b, x_col0 + h)),
                  pl.BlockSpec((seq, LANE), lambda b, h: (b, g_col0 + h)),
                  pl.BlockSpec((seq, LANE), lambda b, h: (b, x_col0 + h)),
                  pl.BlockSpec((ROW_TILE, ROW_TILE), lambda b, h: (0, 0), pipeline_mode=once),
                  pl.BlockSpec((8, LANE), lambda b, h: (0, h)),
                  pl.BlockSpec((1, LANE), lambda b, h: (0, h)),
                  pl.BlockSpec((2, 1, LANE, 2 * LANE), lambda b, h: (0, h, 0, 0)),
                  pl.BlockSpec((2, 1, 1, 2 * LANE), lambda b, h: (0, h, 0, 0)),
                  pl.BlockSpec((2, LANE), lambda b, h: (0, h))],
        out_specs=(pl.BlockSpec((n_ctx, LANE), lambda b, h: (b, h)),
                   pl.BlockSpec((seq, LANE), lambda b, h: (b, h))),
        scratch_shapes=[pltpu.VMEM((seq, LANE), F32), pltpu.VMEM((seq, LANE), F32)],
        compiler_params=_params("parallel", "parallel"),
        name="rglru",
    )(u, u, u, u, perm.astype(BF16), cw, cb, wg, bg, lam)


def _chunk_mask(rows, reverse):
    r = lax.broadcasted_iota(jnp.int32, (rows, rows), 0)
    c = lax.broadcasted_iota(jnp.int32, (rows, rows), 1)
    same = (r >> CHUNK_SHIFT) == (c >> CHUNK_SHIFT)
    tri = (c >= r) if reverse else (c <= r)
    return jnp.where(same, jnp.where(tri, 1.0, 0.0), 0.0)


def _split3(x):
    hi = x.astype(BF16)
    r1 = x - hi.astype(F32)
    mid = r1.astype(BF16)
    lo = (r1 - mid.astype(F32)).astype(BF16)
    return hi, mid, lo


def _block_scan(g_all, head_qkv, st_scr, o_ref, *, heads, dv, reverse, batched):
    rows, wall = g_all.shape
    width = wall // heads
    nchunk = rows // CHUNK
    mask = _chunk_mask(rows, reverse)
    mask_b = mask.astype(BF16)
    keep = mask > 0.5
    keep_c = keep[:CHUNK, :CHUNK]
    iref = CHUNK // 2 if reverse else CHUNK // 2 - 1
    ilast = 0 if reverse else CHUNK - 1
    order = range(nchunk - 1, -1, -1) if reverse else range(nchunk)

    def per_chunk(rows_1, width):
        return jnp.concatenate([jnp.broadcast_to(r, (CHUNK, width)) for r in rows_1], axis=0)

    pieces = jnp.concatenate(_split3(g_all), axis=1)
    sums = jnp.dot(mask_b, pieces, preferred_element_type=F32)
    b_all = sums[:, :wall] + sums[:, wall:2 * wall] + sums[:, 2 * wall:]

    def prepare(h):
        q, k, v = head_qkv(h)
        b = b_all[:, h * width:(h + 1) * width]
        b_ref = [b[c * CHUNK + iref:c * CHUNK + iref + 1] for c in range(nchunk)]
        b_last = [b[c * CHUNK + ilast:c * CHUNK + ilast + 1] for c in range(nchunk)]
        e1 = jnp.exp(b - per_chunk(b_ref, width))
        r1 = 1.0 / e1
        qe = q * e1
        ke = k * r1
        qd = (qe * per_chunk([jnp.exp(r) for r in b_ref], width)).astype(BF16)
        kd = (ke * per_chunk([jnp.exp(l - r) for l, r in zip(b_last, b_ref)], width)).astype(BF16)
        decays = [jnp.exp(l) for l in b_last]
        return qe.astype(BF16), ke.astype(BF16), qd, kd, v.astype(BF16), decays

    nxt = prepare(0)
    for h in range(heads):
        qe, ke, qd, kd, vb, decays = nxt
        if h + 1 < heads:
            nxt = prepare(h + 1)
        if batched:
            sc = lax.dot_general(qe, ke, _NT, preferred_element_type=F32)
            sc = jnp.where(keep, sc, 0.0).astype(BF16)
            o = jnp.dot(sc, vb, preferred_element_type=F32)
        st = st_scr[h]
        parts = [None] * nchunk
        for ci in order:
            rs = slice(ci * CHUNK, (ci + 1) * CHUNK)
            if batched:
                o_c = o[rs]
            else:
                sc = lax.dot_general(qe[rs], ke[rs], _NT, preferred_element_type=F32)
                sc = jnp.where(keep_c, sc, 0.0).astype(BF16)
                o_c = jnp.dot(sc, vb[rs], preferred_element_type=F32)
            parts[ci] = o_c + lax.dot_general(qd[rs], st.astype(BF16), _NT,
                                              preferred_element_type=F32)
            st = st * decays[ci] + lax.dot_general(vb[rs], kd[rs], _TN, preferred_element_type=F32)
        if o_ref is not None:
            o_ref[:, h * dv:(h + 1) * dv] = jnp.concatenate(parts, axis=0).astype(o_ref.dtype)
        st_scr[h] = st


def _gla_kernel(q_ref, k_ref, v_ref, a_ref, wa_ref, ba_ref, o_ref, st_scr):
    t = pl.program_id(1)
    d = pl.program_id(2)
    dk = q_ref.shape[1] // GLA_HEADS
    dv = v_ref.shape[1] // GLA_HEADS

    @pl.when((t == 0) & (d == 0))
    def _():
        st_scr[...] = jnp.zeros(st_scr.shape, F32)

    def head_qkv(h):
        ks = slice(h * dk, (h + 1) * dk)
        q = q_ref[:, ks].astype(F32) * (dk ** -0.5)
        return q, k_ref[:, ks].astype(F32), v_ref[:, h * dv:(h + 1) * dv]

    for dd in range(2):
        @pl.when(d == dd)
        def _(dd=dd):
            z = jnp.dot(a_ref[...], wa_ref[dd], preferred_element_type=F32) + ba_ref[dd]
            g_all = _log_sigmoid(z) * (1.0 / GLA_TAU)
            _block_scan(g_all, head_qkv, st_scr.at[dd], o_ref,
                        heads=GLA_HEADS, dv=dv, reverse=dd == 1, batched=True)


def _gla(u, wa, ba, *, batch, seq, n_ctx, cols, dv_total):
    ntok = u.shape[0]
    q0, k0, v0, a0 = cols
    dkt = wa.shape[2]
    dvt = dv_total
    nblk = seq // ROW_TILE
    ctx_blk0 = batch * nblk
    assert n_ctx == ROW_TILE

    def blk(b, t, d):
        lat = b * nblk + jnp.where(d == 0, t - 1, nblk - t)
        return jnp.where(t == 0, ctx_blk0 + b, lat)

    in_specs = [pl.BlockSpec((ROW_TILE, dkt), lambda b, t, d: (blk(b, t, d), q0 * LANE // dkt)),
                pl.BlockSpec((ROW_TILE, dkt), lambda b, t, d: (blk(b, t, d), k0 * LANE // dkt)),
                pl.BlockSpec((ROW_TILE, dvt), lambda b, t, d: (blk(b, t, d), v0 * LANE // dvt)),
                pl.BlockSpec((ROW_TILE, LANE), lambda b, t, d: (blk(b, t, d), a0)),
                pl.BlockSpec(wa.shape, lambda b, t, d: (0, 0, 0)),
                pl.BlockSpec(ba.shape, lambda b, t, d: (0, 0, 0))]
    return pl.pallas_call(
        _gla_kernel,
        out_shape=jax.ShapeDtypeStruct((2, ntok, dvt), BF16),
        grid=(batch, nblk + 1, 2),
        in_specs=in_specs,
        out_specs=pl.BlockSpec((None, ROW_TILE, dvt), lambda b, t, d: (d, blk(b, t, d), 0)),
        scratch_shapes=[pltpu.VMEM((2, GLA_HEADS, dvt // GLA_HEADS, dkt // GLA_HEADS), F32)],
        compiler_params=_params("parallel", "arbitrary", "arbitrary"),
        name="gla",
    )(u, u, u, u, wa, ba)


def _hgrn_kernel(*refs, emit_o):
    q_ref, z_ref, v_ref, lb_ref, ub_ref, s0_ref = refs[:6]
    if emit_o:
        o_ref, st_ref, st_scr = refs[6:]
    else:
        st_ref, st_scr = refs[6:]
        o_ref = None
    t = pl.program_id(1)
    d = pl.program_id(2)
    dh = q_ref.shape[1] // HGRN_HEADS

    @pl.when((t == 0) & (d == 0))
    def _():
        st_scr[...] = s0_ref[:, 0]

    lb = lb_ref[...]
    ub = ub_ref[...]

    for dd in range(2):
        @pl.when(d == dd)
        def _(dd=dd):
            z = z_ref[...].astype(F32)
            e = jnp.exp(-jnp.abs(z))
            r = 1.0 / (1.0 + e)
            er = e * r
            pos = z >= 0.0
            log_f = jnp.log(lb + ub * jnp.where(pos, r, er))
            k_all = ub * jnp.where(pos, er, r)

            def head_qkv(h):
                hs = slice(h * dh, (h + 1) * dh)
                return q_ref[:, hs].astype(F32), k_all[:, hs], v_ref[:, hs]

            _block_scan(log_f, head_qkv, st_scr.at[dd], o_ref,
                        heads=HGRN_HEADS, dv=dh, reverse=dd == 1, batched=True)

    @pl.when((t == pl.num_programs(1) - 1) & (d == 1))
    def _():
        st_ref[:, 0] = st_scr[...]


def _hgrn_segment(u, s0, lb, ub, *, batch, seg_len, emit_o):
    ntok = u.shape[0]
    width = lb.shape[1]
    nblk = seg_len // ROW_TILE

    def blk(b, t, d):
        return b * nblk + jnp.where(d == 0, t, nblk - 1 - t)

    st_spec = pl.BlockSpec((2, 1) + s0.shape[2:], lambda b, t, d: (0, b, 0, 0, 0))
    in_specs = [pl.BlockSpec((ROW_TILE, width), lambda b, t, d: (blk(b, t, d), 0)),
                pl.BlockSpec((ROW_TILE, width), lambda b, t, d: (blk(b, t, d), 1 + d)),
                pl.BlockSpec((ROW_TILE, width), lambda b, t, d: (blk(b, t, d), 3)),
                pl.BlockSpec((1, width), lambda b, t, d: (0, 0)),
                pl.BlockSpec((1, width), lambda b, t, d: (0, 0)),
                st_spec]
    st_shape = jax.ShapeDtypeStruct(s0.shape, F32)
    if emit_o:
        out_shape = (jax.ShapeDtypeStruct((2, ntok, width), F32), st_shape)
        out_specs = (pl.BlockSpec((None, ROW_TILE, width), lambda b, t, d: (d, blk(b, t, d), 0)),
                     st_spec)
    else:
        out_shape = (st_shape,)
        out_specs = (st_spec,)
    return pl.pallas_call(
        functools.partial(_hgrn_kernel, emit_o=emit_o),
        out_shape=out_shape,
        grid=(batch, nblk, 2),
        in_specs=in_specs,
        out_specs=out_specs,
        scratch_shapes=[pltpu.VMEM((2,) + s0.shape[2:], F32)],
        compiler_params=_params("parallel", "arbitrary", "arbitrary"),
        name="hgrn_segment",
    )(u, u, u, lb, ub, s0)


def _head_norm_gate(o, gate, heads):
    dh = o.shape[1] // heads
    parts = []
    for h in range(heads):
        oh = o[:, h * dh:(h + 1) * dh]
        ms = jnp.mean(oh * oh, axis=-1, keepdims=True)
        parts.append((oh * lax.rsqrt(ms + EPS) * _silu(gate[:, h * dh:(h + 1) * dh])).astype(BF16))
    return jnp.concatenate(parts, axis=1)


def _load_rows(ref, r0, nrows):
    s_n = CHUNKS_PER_ROW
    return jnp.concatenate([ref[pl.ds(r0 * s_n + s, nrows, stride=s_n), :] for s in range(s_n)],
                           axis=1)


def _store_rows(ref, r0, val):
    s_n = CHUNKS_PER_ROW
    for s in range(s_n):
        ref[pl.ds(r0 * s_n + s, val.shape[0], stride=s_n), :] = val[:, s * LANE:(s + 1) * LANE]


def _out_proj_tail(o, x_ref, tab_ref, pg_ref, ng_ref, rt_ref, xo_ref, h_ref, lg_ref):
    xn = _gated_residual(x_ref[...], _rms(o, pg_ref[...]), tab_ref, 2)
    xo_ref[...] = xn
    h2 = _modulate(_rms(xn, ng_ref[...]), tab_ref, 3, 4)
    _store_rows(h_ref, 0, h2)
    lg_ref[...] = lax.dot_general(rt_ref[...], h2, _NT, precision=lax.Precision.HIGHEST,
                                  preferred_element_type=F32)


def _out_proj_ab_kernel(yl_ref, yc_ref, pt_ref, of_ref, ob_ref, gate_ref, w_ref, x_ref, tab_ref,
                        pg_ref, ng_ref, rt_ref, xo_ref, h_ref, lg_ref, *, lat_tiles):
    o_att = of_ref[...].astype(F32) + ob_ref[...].astype(F32)
    yb = _head_norm_gate(o_att, gate_ref[...].astype(F32), GLA_HEADS)
    ya = jnp.where(pl.program_id(0) < lat_tiles, yl_ref[...], yc_ref[...])
    ya = jnp.concatenate(
        [jnp.dot(pt_ref[...], ya[s:s + ROW_TILE], preferred_element_type=F32).astype(BF16)
         for s in range(0, ya.shape[0], ROW_TILE)], axis=0)
    da = ya.shape[1]
    o = jnp.dot(ya, w_ref[:da], preferred_element_type=F32)
    o = o + jnp.dot(yb, w_ref[da:], preferred_element_type=F32)
    _out_proj_tail(o, x_ref, tab_ref, pg_ref, ng_ref, rt_ref, xo_ref, h_ref, lg_ref)


def _out_proj_c_kernel(of_ref, ob_ref, gate_ref, p_ref, w_ref, x_ref, tab_ref, pg_ref, ng_ref,
                       rt_ref, xo_ref, h_ref, lg_ref):
    tm = x_ref.shape[0]
    width = of_ref.shape[-1]
    o_att = of_ref[...].astype(F32) + ob_ref[...].astype(F32)
    y = _head_norm_gate(o_att.reshape(tm, width), gate_ref[...].astype(F32).reshape(tm, width),
                        HGRN_HEADS)
    y = jnp.dot(p_ref[...], y, preferred_element_type=F32).astype(BF16)
    o = jnp.dot(y, w_ref[...], preferred_element_type=F32)
    _out_proj_tail(o, x_ref, tab_ref, pg_ref, ng_ref, rt_ref, xo_ref, h_ref, lg_ref)


def _out_proj(kernel, lead, lead_specs, w, x, tab, post_gain, next_gain, router_t, tm, ntiles):
    d = x.shape[1]
    ntok = ntiles * tm
    nsub = tm // ROW_TILE
    ne = router_t.shape[0]
    in_specs = list(lead_specs) + [
        pl.BlockSpec(w.shape, lambda i: (0, 0), pipeline_mode=pl.Buffered(1)),
        pl.BlockSpec((tm, d), lambda i: (i, 0)),
        pl.BlockSpec((nsub, 8, d), lambda i: (i, 0, 0)),
        pl.BlockSpec((1, d), lambda i: (0, 0)),
        pl.BlockSpec((1, d), lambda i: (0, 0)),
        pl.BlockSpec((ne, d), lambda i: (0, 0))]
    return pl.pallas_call(
        kernel,
        out_shape=(jax.ShapeDtypeStruct((ntok, d), F32),
                   jax.ShapeDtypeStruct((ntok * CHUNKS_PER_ROW, LANE), F32),
                   jax.ShapeDtypeStruct((ne, ntok), F32)),
        grid=(ntiles,),
        in_specs=in_specs,
        out_specs=(pl.BlockSpec((tm, d), lambda i: (i, 0)),
                   pl.BlockSpec((tm * CHUNKS_PER_ROW, LANE), lambda i: (i, 0)),
                   pl.BlockSpec((ne, tm), lambda i: (0, i))),
        compiler_params=_params("parallel"),
        name="out_proj_norm_residual",
    )(*lead, w, x, tab, post_gain, next_gain, router_t)


def _route_kernel(lg_ref, bias_ref, ids_ref, w_ref):
    tm = lg_ref.shape[1]
    per_group = N_EXPERTS // N_GROUPS
    shape3 = (N_GROUPS, per_group, tm)
    aff = _sigmoid(lg_ref[...]).reshape(shape3)
    biased = aff + bias_ref[...].reshape(shape3)
    neg = -jnp.inf
    sub = lax.broadcasted_iota(jnp.int32, shape3, 1)
    grp = lax.broadcasted_iota(jnp.int32, shape3, 0)
    m1 = jnp.max(biased, axis=1, keepdims=True)
    i1 = jnp.min(jnp.where(biased == m1, sub, per_group), axis=1, keepdims=True)
    m2 = jnp.max(jnp.where(sub == i1, neg, biased), axis=1, keepdims=True)
    score = m1 + m2
    gidx = lax.broadcasted_iota(jnp.int32, score.shape, 0)
    keep = jnp.zeros(score.shape, F32)
    for _ in range(TOPK_GROUPS):
        m = jnp.max(score, axis=0, keepdims=True)
        im = jnp.min(jnp.where(score == m, gidx, N_GROUPS), axis=0, keepdims=True)
        sel = gidx == im
        keep = jnp.where(sel, 1.0, keep)
        score = jnp.where(sel, neg, score)
    work = jnp.where(jnp.broadcast_to(keep, shape3) > 0.0, biased, neg)
    eidx = grp * per_group + sub
    ids, gates = [], []
    for _ in range(TOP_K):
        m = jnp.max(jnp.max(work, axis=0, keepdims=True), axis=1, keepdims=True)
        cand = jnp.where(work == m, eidx, N_EXPERTS)
        im = jnp.min(jnp.min(cand, axis=0, keepdims=True), axis=1, keepdims=True)
        sel = eidx == im
        ids.append(im)
        gates.append(jnp.sum(jnp.sum(jnp.where(sel, aff, 0.0), axis=0, keepdims=True),
                             axis=1, keepdims=True))
        work = jnp.where(sel, neg, work)
    gate = jnp.concatenate(gates, axis=1)
    den = jnp.sum(gate, axis=1, keepdims=True)
    w_ref[...] = (ROUTED_SCALE * gate / den).reshape(TOP_K, tm)
    ids_ref[...] = jnp.concatenate(ids, axis=1).reshape(TOP_K, tm)


def _route(logits_t, bias, tm):
    ne, ntok = logits_t.shape
    bias_b = jnp.broadcast_to(bias.astype(F32)[:, None], (ne, tm))
    return pl.pallas_call(
        _route_kernel,
        out_shape=(jax.ShapeDtypeStruct((TOP_K, ntok), jnp.int32),
                   jax.ShapeDtypeStruct((TOP_K, ntok), F32)),
        grid=(ntok // tm,),
        in_specs=[pl.BlockSpec((ne, tm), lambda i: (0, i)),
                  pl.BlockSpec((ne, tm), lambda i: (0, 0))],
        out_specs=(pl.BlockSpec((TOP_K, tm), lambda i: (0, i)),
                   pl.BlockSpec((TOP_K, tm), lambda i: (0, i))),
        compiler_params=_params("parallel"),
        name="moe_route",
    )(logits_t, bias_b)


LIST_PAD = 1024


def _dispatch_plan(ids, w, t0, tile, ntiles):
    n = ntiles * tile
    e = ids[:, t0:t0 + n].T.reshape(ntiles, tile * TOP_K)
    g = w[:, t0:t0 + n].T.reshape(ntiles, tile * TOP_K)
    nassign = tile * TOP_K
    assert nassign & (nassign - 1) == 0 and N_EXPERTS * nassign < 2 ** 31
    keys = e * nassign + jnp.arange(nassign, dtype=jnp.int32)
    order = jnp.sort(keys, axis=1) & (nassign - 1)
    rows = (order // TOP_K) * CHUNKS_PER_ROW
    wts = jnp.take_along_axis(g, order, axis=1)
    experts = jnp.arange(N_EXPERTS, dtype=jnp.int32)
    counts = jnp.sum((e[:, :, None] == experts).astype(jnp.int32), axis=1)
    starts = jnp.concatenate([jnp.zeros((ntiles, 1), jnp.int32), jnp.cumsum(counts, axis=1)], axis=1)
    rows = jnp.pad(rows, ((0, 0), (0, LIST_PAD)))
    wts = jnp.pad(wts, ((0, 0), (0, LIST_PAD)))
    starts = jnp.pad(starts, ((0, 0), (0, LANE - N_EXPERTS - 1)))
    return rows, wts, starts


def _moe_kernel(rows_hbm, wts_hbm, starts_hbm, h_ref, wg_ref, wu_ref, wd_ref, sg_ref, su_ref,
                sd_ref, o_ref, rows_s, wts_s, starts_s, sem, gbuf, ybuf, *, epb):
    i = pl.program_id(0)
    j = pl.program_id(1)
    s_n = CHUNKS_PER_ROW
    blk = MOE_BLOCK
    tile = h_ref.shape[0] // s_n

    def swiglu(x, w_gate, w_up, w_down):
        g = jnp.dot(x, w_gate, preferred_element_type=F32)
        u = jnp.dot(x, w_up, preferred_element_type=F32)
        return jnp.dot((_silu(g) * u).astype(BF16), w_down, preferred_element_type=F32)

    @pl.when(j == 0)
    def _():
        copies = [pltpu.make_async_copy(rows_hbm.at[i], rows_s, sem.at[0]),
                  pltpu.make_async_copy(wts_hbm.at[i], wts_s, sem.at[1]),
                  pltpu.make_async_copy(starts_hbm.at[i], starts_s, sem.at[2])]
        for c in copies:
            c.start()
        for r0 in range(0, tile, ROW_TILE):
            x = _load_rows(h_ref, r0, ROW_TILE).astype(BF16)
            _store_rows(o_ref, r0, swiglu(x, sg_ref[...], su_ref[...], sd_ref[...]))
        for c in copies:
            c.wait()

    for e in range(epb):
        base = starts_s[j * epb + e]
        count = starts_s[j * epb + e + 1] - base
        w_gate = wg_ref[e].astype(BF16)
        w_up = wu_ref[e].astype(BF16)
        w_down = wd_ref[e].astype(BF16)

        def block(bi, carry, base=base, count=count, w_gate=w_gate, w_up=w_up, w_down=w_down):
            p0 = base + bi * blk
            left = count - bi * blk
            offs = [pl.multiple_of(rows_s[p0 + r], s_n) for r in range(blk)]
            for r in range(blk):
                gbuf[pl.ds(r * s_n, s_n), :] = h_ref[pl.ds(offs[r], s_n), :]
            x = _load_rows(gbuf, 0, blk).astype(BF16)
            y = swiglu(x, w_gate, w_up, w_down)
            y = jnp.where(lax.broadcasted_iota(jnp.int32, y.shape, 0) < left, y, 0.0)
            _store_rows(ybuf, 0, y)
            group = 2 * SUBLANE
            for r0 in range(0, blk, group):
                vals = [o_ref[pl.ds(offs[r], s_n), :] + wts_s[p0 + r] * ybuf[pl.ds(r * s_n, s_n), :]
                        for r in range(r0, r0 + group)]
                for r in reversed(range(r0, r0 + group)):
                    o_ref[pl.ds(offs[r], s_n), :] = vals[r - r0]
            return carry

        nblk = lax.shift_right_logical(count + (blk - 1), blk.bit_length() - 1)
        lax.fori_loop(0, nblk, block, 0)


def _moe(h, plan, layer, wg, wu, wd, sg, su, sd, tile0, ntiles, tile, epb):
    rows, wts, starts = plan
    s_n = CHUNKS_PER_ROW
    _, ne, d, ff = wg.shape
    once = pl.Buffered(1)
    hbm = pl.BlockSpec(memory_space=pl.ANY)
    return pl.pallas_call(
        functools.partial(_moe_kernel, epb=epb),
        out_shape=jax.ShapeDtypeStruct((ntiles * tile * s_n, LANE), F32),
        grid=(ntiles, ne // epb),
        in_specs=[hbm, hbm, hbm,
                  pl.BlockSpec((tile * s_n, LANE), lambda i, j: (tile0 + i, 0), pipeline_mode=once),
                  pl.BlockSpec((None, epb, d, ff), lambda i, j: (layer, j, 0, 0)),
                  pl.BlockSpec((None, epb, d, ff), lambda i, j: (layer, j, 0, 0)),
                  pl.BlockSpec((None, epb, ff, d), lambda i, j: (layer, j, 0, 0)),
                  pl.BlockSpec(sg.shape, lambda i, j: (0, 0), pipeline_mode=once),
                  pl.BlockSpec(su.shape, lambda i, j: (0, 0), pipeline_mode=once),
                  pl.BlockSpec(sd.shape, lambda i, j: (0, 0), pipeline_mode=once)],
        out_specs=pl.BlockSpec((tile * s_n, LANE), lambda i, j: (i, 0)),
        scratch_shapes=[pltpu.SMEM((rows.shape[1],), jnp.int32),
                        pltpu.SMEM((wts.shape[1],), F32),
                        pltpu.SMEM((starts.shape[1],), jnp.int32),
                        pltpu.SemaphoreType.DMA((3,)),
                        pltpu.VMEM((MOE_BLOCK * s_n, LANE), F32),
                        pltpu.VMEM((MOE_BLOCK * s_n, LANE), F32)],
        compiler_params=_params("parallel", "arbitrary", vmem=VMEM_LIMIT_MOE),
        name="moe_experts",
    )(rows, wts, starts, h, wg, wu, wd, sg, su, sd)


def _moe_finish_kernel(f_ref, x_ref, tab_ref, pg_ref, o_ref):
    f = _load_rows(f_ref, 0, x_ref.shape[0])
    o_ref[...] = _gated_residual(x_ref[...], _rms(f, pg_ref[...]), tab_ref, 5)


def _moe_finish(f, x, tab, post_gain, tm, x_tile0, tab0):
    d = x.shape[1]
    s_n = CHUNKS_PER_ROW
    nsub = tm // ROW_TILE
    ntiles = f.shape[0] // (tm * s_n)
    return pl.pallas_call(
        _moe_finish_kernel,
        out_shape=jax.ShapeDtypeStruct((ntiles * tm, d), F32),
        grid=(ntiles,),
        in_specs=[pl.BlockSpec((tm * s_n, LANE), lambda i: (i, 0)),
                  pl.BlockSpec((tm, d), lambda i: (x_tile0 + i, 0)),
                  pl.BlockSpec((nsub, 8, d), lambda i: (tab0 + i, 0, 0)),
                  pl.BlockSpec((1, d), lambda i: (0, 0))],
        out_specs=pl.BlockSpec((tm, d), lambda i: (i, 0)),
        compiler_params=_params("parallel"),
        name="moe_norm_residual",
    )(f, x, tab, post_gain)


def _grid_permutation(band):
    tm = band * GRID_W
    dst = jnp.arange(tm)
    src = (dst % band) * GRID_W + dst // band
    return (src[:, None] == jnp.arange(tm)[None, :]).astype(BF16)


def kernel(x, c, ctx, c_ctx, mod_w, mod_b, pre_gain, post_gain, ab_w_in, ab_conv_w, ab_conv_b, ab_lru_wa, ab_lru_ba, ab_lru_wi, ab_lru_bi, ab_lru_lambda, ab_gla_wa2, ab_gla_ba, ab_w_out, c_w_in, hgrn_lb_logits, c_w_out, moe_router, moe_bias, moe_w_gate, moe_w_up, moe_w_down, moe_ws_gate, moe_ws_up, moe_ws_down):
    batch, seq, d = x.shape
    n_ctx = ctx.shape[1]
    depth = mod_w.shape[0]
    band = SUBLANE
    tm_grid = band * GRID_W
    rows_img = seq // GRID_W
    assert n_ctx == ROW_TILE and seq % (2 * ROW_TILE) == 0 and d % LANE == 0
    assert depth == 2 and batch + 1 <= 8 and rows_img % band == 0 and tm_grid % ROW_TILE == 0
    n_lat = batch * seq
    n_ctx_tok = batch * n_ctx
    ntok = n_lat + n_ctx_tok

    xs = jnp.concatenate([x.reshape(n_lat, d), ctx.reshape(n_ctx_tok, d)], axis=0)

    cvec = jnp.concatenate([c, c_ctx[None, :], jnp.zeros((8 - batch - 1, d), F32)], axis=0)
    mods = _modulation(cvec, mod_w, mod_b).reshape(depth, 8, N_MOD, d)
    tile_row = jnp.concatenate([jnp.repeat(jnp.arange(batch), seq // ROW_TILE),
                                jnp.full((batch,), batch)]).astype(jnp.int32)
    tabs = jnp.pad(mods[:, tile_row], ((0, 0), (0, 0), (0, 8 - N_MOD), (0, 0)))

    def experts(layer, h2, routed, t0, ntiles, tile):
        plan = _dispatch_plan(*routed, t0, tile, ntiles)
        return _moe(h2, plan, layer, moe_w_gate, moe_w_up, moe_w_down,
                    moe_ws_gate[layer].astype(BF16), moe_ws_up[layer].astype(BF16),
                    moe_ws_down[layer].astype(BF16), t0 // tile, ntiles, tile, epb)

    tm_proj, tm_out, tm_route, tm_moe, epb = 512, 512, 512, 2048, 4
    assert n_lat % tm_moe == 0 and n_lat % n_ctx_tok == 0

    ab_cols = ab_w_in.shape[2]
    gla_dk = ab_gla_wa2.shape[3]
    lru_w = d
    gla_dv = (ab_cols - 2 * lru_w - 2 * gla_dk - 2 * GLA_RANK) // 2
    n_main = ab_cols // LANE * LANE
    tn0 = n_main // 4
    w_main = ab_w_in[0][:, :n_main].astype(BF16)
    w_tail = jnp.pad(ab_w_in[0][:, n_main:], ((0, 0), (0, LANE - (ab_cols - n_main)))).astype(BF16)
    u0 = _in_proj(xs, tabs[0], pre_gain[0, 0][None, :], w_main, w_tail, tm_proj, tn0, 0)

    cw = jnp.pad(ab_conv_w[0], ((0, 8 - CONV_W), (0, 0)))
    cb = ab_conv_b[0][None, :]
    wg = jnp.concatenate([ab_lru_wa[0], ab_lru_wi[0]], axis=-1).astype(BF16)
    bg = jnp.concatenate([ab_lru_ba[0].reshape(2, LRU_HEADS, 1, LANE),
                          ab_lru_bi[0].reshape(2, LRU_HEADS, 1, LANE)], axis=-1)
    ya_c, ya_l = _lru(u0, (cw, cb, wg, bg, ab_lru_lambda[0]), batch=batch, seq=seq, n_ctx=n_ctx,
                      g_col0=0, x_col0=lru_w // LANE)

    q0 = 2 * lru_w // LANE
    k0 = q0 + gla_dk // LANE
    v0 = k0 + gla_dk // LANE
    gate0 = v0 + gla_dv // LANE
    a0 = gate0 + gla_dv // LANE
    wa = jnp.stack([jnp.zeros((LANE, gla_dk), F32).at[dr * GLA_RANK:(dr + 1) * GLA_RANK]
                    .set(ab_gla_wa2[0, dr]) for dr in range(2)]).astype(BF16)
    ba = ab_gla_ba[0][:, None, :]
    o_gla = _gla(u0, wa, ba, batch=batch, seq=seq, n_ctx=n_ctx, cols=(q0, k0, v0, a0),
                 dv_total=gla_dv)

    lat_tiles = n_lat // tm_out
    ctx_tiles = n_ctx_tok // tm_out
    lead_specs = [pl.BlockSpec((tm_out, lru_w), lambda i: (jnp.minimum(i, lat_tiles - 1), 0)),
                  pl.BlockSpec((tm_out, lru_w),
                               lambda i: (jnp.clip(i - lat_tiles, 0, ctx_tiles - 1), 0)),
                  pl.BlockSpec((ROW_TILE, ROW_TILE), lambda i: (0, 0)),
                  pl.BlockSpec((None, tm_out, gla_dv), lambda i: (0, i, 0)),
                  pl.BlockSpec((None, tm_out, gla_dv), lambda i: (1, i, 0)),
                  pl.BlockSpec((tm_out, gla_dv), lambda i: (i, gate0 * LANE // gla_dv))]
    xs, h2, lg = _out_proj(functools.partial(_out_proj_ab_kernel, lat_tiles=lat_tiles),
                           (ya_l, ya_c, _time_permutation().T.astype(BF16), o_gla, o_gla, u0),
                           lead_specs,
                           ab_w_out[0].astype(BF16), xs, tabs[0], post_gain[0, 0][None, :],
                           pre_gain[0, 1][None, :], moe_router[0].T, tm_out, ntok // tm_out)
    cwt = _route(lg, moe_bias[0], tm_route)
    f_ctx = experts(0, h2, cwt, n_lat, 1, n_ctx_tok)
    f_lat = experts(0, h2, cwt, 0, n_lat // tm_moe, tm_moe)
    pg0 = post_gain[0, 1][None, :]

    hd = c_w_out.shape[1]
    tn1 = 1280
    w_in1 = c_w_in[0].astype(BF16)
    c_cols = w_in1.shape[1]
    nsec = c_cols // hd
    perm = _grid_permutation(band)
    gain1 = pre_gain[1, 0][None, :]
    u1c = _in_proj_ctx(f_ctx, xs, tabs[0], pg0, tabs[1], gain1, w_in1, tm_proj, tn1,
                       n_lat // tm_proj)
    u1l, xs = _in_proj_grid(f_lat, xs, tabs[0], pg0, tabs[1], gain1, perm, w_in1, tn1,
                            n_lat // tm_grid, band, rows_img // band)
    lb_sm = jax.nn.softmax(hgrn_lb_logits.astype(F32), axis=0)
    lb_cum = jnp.cumsum(lb_sm, axis=0)
    lb = (lb_cum[1] - lb_cum[0])[None, :]
    ub = 1.0 - lb
    dh = hd // HGRN_HEADS
    s_zero = jnp.zeros((2, batch, HGRN_HEADS, dh, dh), F32)
    u1l_flat = u1l.reshape(n_lat, c_cols)
    (s_c,) = _hgrn_segment(u1c, s_zero, lb, ub, batch=batch, seg_len=n_ctx, emit_o=False)
    o_hgrn, _ = _hgrn_segment(u1l_flat, s_c, lb, ub, batch=batch, seg_len=seq, emit_o=True)
    o_hgrn = o_hgrn.reshape(2, n_lat // rows_img, rows_img // band, band, hd)

    bpi = rows_img // band
    grid_blk = (GRID_W, None, band, hd)
    lead_specs = [pl.BlockSpec((None,) + grid_blk, lambda i: (0, i // bpi, i % bpi, 0, 0)),
                  pl.BlockSpec((None,) + grid_blk, lambda i: (1, i // bpi, i % bpi, 0, 0)),
                  pl.BlockSpec(grid_blk, lambda i: (i // bpi, i % bpi, 0, nsec - 1)),
                  pl.BlockSpec((tm_grid, tm_grid), lambda i: (0, 0))]
    xs, h2, lg = _out_proj(_out_proj_c_kernel, (o_hgrn, o_hgrn, u1l, perm.T), lead_specs,
                           c_w_out[0].astype(BF16), xs, tabs[1], post_gain[1, 0][None, :],
                           pre_gain[1, 1][None, :], moe_router[1].T, tm_grid, n_lat // tm_grid)
    f_lat = experts(1, h2, _route(lg, moe_bias[1], tm_route), 0, n_lat // tm_moe, tm_moe)
    out = _moe_finish(f_lat, xs, tabs[1], post_gain[1, 1][None, :], tm_out, 0, 0)
    return out.reshape(batch, seq, d)
```

```python
import functools

import jax
import jax.numpy as jnp
from jax import lax
from jax.experimental import pallas as pl
from jax.experimental.pallas import tpu as pltpu

F32 = jnp.float32
BF16 = jnp.bfloat16

EPS = 1e-6
N_MOD = 6
ROW_TILE = 256
CHUNK = 64
CHUNK_SHIFT = 6
GRID_W = 64
LRU_HEADS = 8
LRU_C = 8.0
CONV_W = 4
GLA_HEADS = 4
GLA_RANK = 16
GLA_TAU = 16.0
HGRN_HEADS = 8
N_EXPERTS = 64
TOP_K = 8
N_GROUPS = 8
TOPK_GROUPS = 4
ROUTED_SCALE = 2.5
LANE = 128
SUBLANE = 8
BF16_ROWS = 16
CHUNKS_PER_ROW = 8
MOE_BLOCK = 128
VMEM_LIMIT = 48 * 1024 * 1024
VMEM_LIMIT_MOE = 58 * 1024 * 1024

_NT = (((1,), (1,)), ((), ()))
_TN = (((0,), (0,)), ((), ()))


def _params(*sem, vmem=VMEM_LIMIT):
    return pltpu.CompilerParams(dimension_semantics=sem, vmem_limit_bytes=vmem)


def _rms(x, gain):
    ms = jnp.mean(x * x, axis=-1, keepdims=True)
    return x * lax.rsqrt(ms + EPS) * gain


def _sigmoid(x):
    return jax.nn.sigmoid(x)


def _silu(x):
    return x * _sigmoid(x)


def _log_sigmoid(x):
    return jnp.minimum(x, 0.0) - jnp.log1p(jnp.exp(-jnp.abs(x)))


def _softplus(x):
    return jnp.maximum(x, 0.0) + jnp.log1p(jnp.exp(-jnp.abs(x)))


def _modulate(xn, tab_ref, shift_row, scale_row):
    parts = []
    for s in range(tab_ref.shape[0]):
        t = tab_ref[s]
        rows = xn[s * ROW_TILE:(s + 1) * ROW_TILE]
        parts.append(rows * (1.0 + t[scale_row:scale_row + 1]) + t[shift_row:shift_row + 1])
    return parts[0] if len(parts) == 1 else jnp.concatenate(parts, axis=0)


def _gated_residual(x, branch, tab_ref, gate_row):
    parts = []
    for s in range(tab_ref.shape[0]):
        sl = slice(s * ROW_TILE, (s + 1) * ROW_TILE)
        parts.append(x[sl] + tab_ref[s][gate_row:gate_row + 1] * branch[sl])
    return parts[0] if len(parts) == 1 else jnp.concatenate(parts, axis=0)


def _mod_kernel(c_ref, w_ref, b_ref, o_ref):
    s = _silu(c_ref[...])
    o_ref[0] = jnp.dot(s, w_ref[0], precision=lax.Precision.HIGHEST,
                       preferred_element_type=F32) + b_ref[0]


def _modulation(cvec, mod_w, mod_b):
    depth, d, n = mod_w.shape
    tn = 1024
    return pl.pallas_call(
        _mod_kernel,
        out_shape=jax.ShapeDtypeStruct((depth, 8, n), F32),
        grid=(depth, n // tn),
        in_specs=[pl.BlockSpec((8, d), lambda l, j: (0, 0)),
                  pl.BlockSpec((1, d, tn), lambda l, j: (l, 0, j)),
                  pl.BlockSpec((1, 1, tn), lambda l, j: (l, 0, j))],
        out_specs=pl.BlockSpec((1, 8, tn), lambda l, j: (l, 0, j)),
        compiler_params=_params("parallel", "parallel"),
        name="adaln_modulation",
    )(cvec, mod_w, mod_b.reshape(depth, 1, n))


def _in_proj_kernel(xl_ref, xc_ref, tab_ref, gain_ref, w_ref, wt_ref, o_ref, *, tn, lat_tiles):
    x = jnp.where(pl.program_id(0) < lat_tiles, xl_ref[...], xc_ref[...])
    h = _modulate(_rms(x, gain_ref[...]), tab_ref, 0, 1).astype(BF16)
    n_main = w_ref.shape[1]
    for j in range(n_main // tn):
        cs = slice(j * tn, (j + 1) * tn)
        o_ref[:, cs] = jnp.dot(h, w_ref[:, cs], preferred_element_type=F32).astype(o_ref.dtype)
    o_ref[:, n_main:] = jnp.dot(h, wt_ref[...], preferred_element_type=F32).astype(o_ref.dtype)


def _in_proj(x_lat, x_ctx, tab, gain, w, w_tail, tm, tn):
    d = x_lat.shape[1]
    n = w.shape[1] + w_tail.shape[1]
    nsub = tm // ROW_TILE
    lat_tiles = x_lat.shape[0] // tm
    ctx_tiles = x_ctx.shape[0] // tm
    ntiles = lat_tiles + ctx_tiles
    once = pl.Buffered(1)
    return pl.pallas_call(
        functools.partial(_in_proj_kernel, tn=tn, lat_tiles=lat_tiles),
        out_shape=jax.ShapeDtypeStruct((ntiles * tm, n), BF16),
        grid=(ntiles,),
        in_specs=[pl.BlockSpec((tm, d), lambda i: (jnp.minimum(i, lat_tiles - 1), 0)),
                  pl.BlockSpec((tm, d), lambda i: (jnp.clip(i - lat_tiles, 0, ctx_tiles - 1), 0)),
                  pl.BlockSpec((nsub, 8, d), lambda i: (i, 0, 0)),
                  pl.BlockSpec((1, d), lambda i: (0, 0)),
                  pl.BlockSpec(w.shape, lambda i: (0, 0), pipeline_mode=once),
                  pl.BlockSpec(w_tail.shape, lambda i: (0, 0), pipeline_mode=once)],
        out_specs=pl.BlockSpec((tm, n), lambda i: (i, 0)),
        compiler_params=_params("parallel"),
        name="norm_mod_in_proj",
    )(x_lat, x_ctx, tab, gain, w, w_tail)


def _after_moe(f_ref, x_ref, ptab_ref, pg_ref):
    f = _load_rows(f_ref, 0, x_ref.shape[0])
    return _gated_residual(x_ref[...], _rms(f, pg_ref[...]), ptab_ref, 5)


def _in_proj_ctx_kernel(f_ref, x_ref, ptab_ref, pg_ref, tab_ref, gain_ref, w_ref, o_ref, *, tn):
    xn = _after_moe(f_ref, x_ref, ptab_ref, pg_ref)
    h = _modulate(_rms(xn, gain_ref[...]), tab_ref, 0, 1).astype(BF16)
    for j in range(w_ref.shape[1] // tn):
        cs = slice(j * tn, (j + 1) * tn)
        o_ref[:, cs] = jnp.dot(h, w_ref[:, cs], preferred_element_type=F32).astype(o_ref.dtype)


def _in_proj_ctx(f, x, ptab, post_gain, tab, gain, w, tm, tn, tile0):
    d = x.shape[1]
    n = w.shape[1]
    s_n = CHUNKS_PER_ROW
    nsub = tm // ROW_TILE
    ntiles = f.shape[0] // (tm * s_n)
    tabspec = pl.BlockSpec((nsub, 8, d), lambda i: (tile0 + i, 0, 0))
    vec = pl.BlockSpec((1, d), lambda i: (0, 0))
    return pl.pallas_call(
        functools.partial(_in_proj_ctx_kernel, tn=tn),
        out_shape=jax.ShapeDtypeStruct((ntiles * tm, n), BF16),
        grid=(ntiles,),
        in_specs=[pl.BlockSpec((tm * s_n, LANE), lambda i: (i, 0)),
                  pl.BlockSpec((tm, d), lambda i: (tile0 + i, 0)),
                  tabspec, vec, tabspec, vec,
                  pl.BlockSpec((d, n), lambda i: (0, 0), pipeline_mode=pl.Buffered(1))],
        out_specs=pl.BlockSpec((tm, n), lambda i: (i, 0)),
        compiler_params=_params("parallel"),
        name="moe_close_in_proj",
    )(f, x, ptab, post_gain, tab, gain, w)


def _in_proj_grid_kernel(f_ref, x_ref, ptab_ref, pg_ref, tab_ref, gain_ref, p_ref, w_ref,
                         o_ref, xo_ref, *, tn):
    xn = _after_moe(f_ref, x_ref, ptab_ref, pg_ref)
    xo_ref[...] = xn
    h = _modulate(_rms(xn, gain_ref[...]), tab_ref, 0, 1).astype(BF16)
    hp = jnp.dot(p_ref[...], h, preferred_element_type=F32).astype(BF16)
    for j in range(w_ref.shape[1] // tn):
        cs = slice(j * tn, (j + 1) * tn)
        o = jnp.dot(hp, w_ref[:, cs], preferred_element_type=F32).astype(o_ref.dtype)
        o_ref[:, :, cs] = o.reshape(o_ref.shape[:2] + (tn,))


def _in_proj_grid(f, x, ptab, post_gain, tab, gain, perm, w, tn, ntiles, band, bands_per_image):
    d = x.shape[1]
    n = w.shape[1]
    s_n = CHUNKS_PER_ROW
    tm = band * GRID_W
    nsub = tm // ROW_TILE
    images = ntiles // bands_per_image
    once = pl.Buffered(1)
    tabspec = pl.BlockSpec((nsub, 8, d), lambda i: (i, 0, 0))
    vec = pl.BlockSpec((1, d), lambda i: (0, 0))
    return pl.pallas_call(
        functools.partial(_in_proj_grid_kernel, tn=tn),
        out_shape=(jax.ShapeDtypeStruct((images * GRID_W, bands_per_image, band, n), F32),
                   jax.ShapeDtypeStruct((ntiles * tm, d), F32)),
        grid=(ntiles,),
        in_specs=[pl.BlockSpec((tm * s_n, LANE), lambda i: (i, 0)),
                  pl.BlockSpec((tm, d), lambda i: (i, 0)),
                  tabspec, vec, tabspec, vec,
                  pl.BlockSpec((tm, tm), lambda i: (0, 0), pipeline_mode=once),
                  pl.BlockSpec((d, n), lambda i: (0, 0), pipeline_mode=once)],
        out_specs=(pl.BlockSpec((GRID_W, None, band, n),
                                lambda i: (i // bands_per_image, i % bands_per_image, 0, 0)),
                   pl.BlockSpec((tm, d), lambda i: (i, 0))),
        compiler_params=_params("parallel"),
        name="moe_close_in_proj_grid",
    )(f, x, ptab, post_gain, tab, gain, perm, w)


def _lru_kernel(gc_ref, xc_ref, gl_ref, xl_ref, p_ref, cw_ref, cb_ref, wg_ref, bg_ref,
                lam_ref, yc_ref, yl_ref, xs_scr, hf_scr):
    T = ROW_TILE
    V = T // SUBLANE
    H = BF16_ROWS
    sub = lax.broadcasted_iota(jnp.int32, (SUBLANE, LANE), 0)
    cw = cw_ref[...]
    cb = cb_ref[...]
    sp = _softplus(-lam_ref[...])

    def vreg(a, v):
        return a[v * SUBLANE:(v + 1) * SUBLANE]

    def shift_down(y, first):
        return jnp.where(sub == 0, first, pltpu.roll(y, 1, 0))

    def shift_up(y, last):
        return jnp.where(sub == SUBLANE - 1, last, pltpu.roll(y, SUBLANE - 1, 0))

    def conv_tile(x_ref, t, nt, seg_len):
        t0 = pl.multiple_of(t * T, T)
        xp = jnp.dot(p_ref[...], x_ref[pl.ds(t0, T), :], preferred_element_type=F32)
        p0 = pl.multiple_of(jnp.maximum(t0 - H, 0), H)
        n0 = pl.multiple_of(jnp.minimum(t0 + T, seg_len - H), H)
        prev = x_ref[pl.ds(p0, H), :].astype(F32) * jnp.where(t > 0, 1.0, 0.0)
        nxt = x_ref[pl.ds(n0, H), :].astype(F32) * jnp.where(t < nt - 1, 1.0, 0.0)
        m1, m2, p1 = prev[H - 1:H], prev[H - 2:H - 1], nxt[0:1]
        last_m1 = shift_down(vreg(xp, V - 1), m1)
        xm1 = jnp.concatenate([last_m1, xp[:T - SUBLANE]], axis=0)
        xm2 = jnp.concatenate([shift_down(vreg(xp, V - 2), m2), last_m1, xp[:T - 2 * SUBLANE]], axis=0)
        xp1 = jnp.concatenate([xp[SUBLANE:], shift_up(vreg(xp, 0), p1)], axis=0)
        return cb + xm2 * cw[0:1] + xm1 * cw[1:2] + xp * cw[2:3] + xp1 * cw[3:4]

    def gates(xc, d):
        z = jnp.dot(xc.astype(BF16), wg_ref[d, 0], preferred_element_type=F32) + bg_ref[d, 0]
        r = _sigmoid(z[:, :LANE])
        i = _sigmoid(z[:, LANE:])
        log_a = (-LRU_C) * r * sp[d:d + 1]
        a = jnp.exp(log_a)
        return a, jnp.sqrt(1.0 - a * a) * (i * xc)

    def scan_tile(a, u, carry, reverse):
        hs, cum = [None] * V, [None] * V
        hp = ap = None
        for v in (range(V - 1, -1, -1) if reverse else range(V)):
            av, uv = vreg(a, v), vreg(u, v)
            hp, ap = (uv, av) if hp is None else (av * hp + uv, av * ap)
            hs[v], cum[v] = hp, ap
        s = 1
        while s < SUBLANE:
            keep = (sub < SUBLANE - s) if reverse else (sub >= s)
            sh = SUBLANE - s if reverse else s
            h_sh = jnp.where(keep, pltpu.roll(hp, sh, 0), 0.0)
            a_sh = jnp.where(keep, pltpu.roll(ap, sh, 0), 1.0)
            hp = hp + ap * h_sh
            ap = ap * a_sh
            s *= 2
        state = hp + ap * carry
        if reverse:
            start, out = shift_up(state, carry), state[0:1]
        else:
            start, out = shift_down(state, carry), state[SUBLANE - 1:SUBLANE]
        h = jnp.concatenate([hs[v] + cum[v] * start for v in range(V)], axis=0)
        return h, out

    def run_segment(g_ref, x_ref, y_ref, seg_len, carry_f, carry_b):
        nt = seg_len // T
        unroll = 4 if nt % 4 == 0 else (2 if nt % 2 == 0 else 1)

        def fwd_body(i, carry):
            for j in range(unroll):
                t = i * unroll + j
                t0 = pl.multiple_of(t * T, T)
                xc = conv_tile(x_ref, t, nt, seg_len)
                xs_scr[pl.ds(t0, T), :] = xc
                a, u = gates(xc, 0)
                h, carry = scan_tile(a, u, carry, False)
                hf_scr[pl.ds(t0, T), :] = h
            return carry

        def bwd_body(i, carry):
            for j in range(unroll):
                t = nt - 1 - (i * unroll + j)
                t0 = pl.multiple_of(t * T, T)
                a, u = gates(xs_scr[pl.ds(t0, T), :], 1)
                h, carry = scan_tile(a, u, carry, True)
                g = jnp.dot(p_ref[...], g_ref[pl.ds(t0, T), :], preferred_element_type=F32)
                rec = hf_scr[pl.ds(t0, T), :] + h
                y_ref[pl.ds(t0, T), :] = (jax.nn.gelu(g) * rec).astype(y_ref.dtype)
            return carry

        steps = nt // unroll
        return (lax.fori_loop(0, steps, fwd_body, carry_f),
                lax.fori_loop(0, steps, bwd_body, carry_b))

    zero = jnp.zeros((1, LANE), F32)
    hf, hb = run_segment(gc_ref, xc_ref, yc_ref, xc_ref.shape[0], zero, zero)
    run_segment(gl_ref, xl_ref, yl_ref, xl_ref.shape[0], hf, hb)


def _time_permutation():
    v = ROW_TILE // SUBLANE
    dst = jnp.arange(ROW_TILE)
    src = (dst % SUBLANE) * v + dst // SUBLANE
    return src[:, None] == jnp.arange(ROW_TILE)[None, :]


def _lru(u, lru_w, *, batch, seq, n_ctx, g_col0, x_col0):
    width = LRU_HEADS * LANE
    cw, cb, wg, bg, lam = lru_w
    ctx_blk0 = batch * seq // n_ctx
    perm = _time_permutation()
    once = pl.Buffered(1)
    return pl.pallas_call(
        _lru_kernel,
        out_shape=(jax.ShapeDtypeStruct((batch * n_ctx, width), BF16),
                   jax.ShapeDtypeStruct((batch * seq, width), BF16)),
        grid=(batch, LRU_HEADS),
        in_specs=[pl.BlockSpec((n_ctx, LANE), lambda b, h: (ctx_blk0 + b, g_col0 + h)),
                  pl.BlockSpec((n_ctx, LANE), lambda b, h: (ctx_blk0 + b, x_col0 + h)),
                  pl.BlockSpec((seq, LANE), lambda b, h: (b, g_col0 + h)),
                  pl.BlockSpec((seq, LANE), lambda b, h: (b, x_col0 + h)),
                  pl.BlockSpec((ROW_TILE, ROW_TILE), lambda b, h: (0, 0), pipeline_mode=once),
                  pl.BlockSpec((8, LANE), lambda b, h: (0, h)),
                  pl.BlockSpec((1, LANE), lambda b, h: (0, h)),
                  pl.BlockSpec((2, 1, LANE, 2 * LANE), lambda b, h: (0, h, 0, 0)),
                  pl.BlockSpec((2, 1, 1, 2 * LANE), lambda b, h: (0, h, 0, 0)),
                  pl.BlockSpec((2, LANE), lambda b, h: (0, h))],
        out_specs=(pl.BlockSpec((n_ctx, LANE), lambda b, h: (b, h)),
                   pl.BlockSpec((seq, LANE), lambda b, h: (b, h))),
        scratch_shapes=[pltpu.VMEM((seq, LANE), F32), pltpu.VMEM((seq, LANE), F32)],
        compiler_params=_params("parallel", "parallel"),
        name="rglru",
    )(u, u, u, u, perm.astype(BF16), cw, cb, wg, bg, lam)


def _chunk_mask(rows, reverse):
    r = lax.broadcasted_iota(jnp.int32, (rows, rows), 0)
    c = lax.broadcasted_iota(jnp.int32, (rows, rows), 1)
    same = (r >> CHUNK_SHIFT) == (c >> CHUNK_SHIFT)
    tri = (c >= r) if reverse else (c <= r)
    return jnp.where(same, jnp.where(tri, 1.0, 0.0), 0.0)


def _split3(x):
    hi = x.astype(BF16)
    r1 = x - hi.astype(F32)
    mid = r1.astype(BF16)
    lo = (r1 - mid.astype(F32)).astype(BF16)
    return hi, mid, lo


def _block_scan(g_all, head_qkv, st_scr, o_ref, *, heads, dv, reverse, batched):
    rows, wall = g_all.shape
    width = wall // heads
    nchunk = rows // CHUNK
    mask = _chunk_mask(rows, reverse)
    mask_b = mask.astype(BF16)
    keep = mask > 0.5
    keep_c = keep[:CHUNK, :CHUNK]
    iref = CHUNK // 2 if reverse else CHUNK // 2 - 1
    ilast = 0 if reverse else CHUNK - 1
    order = range(nchunk - 1, -1, -1) if reverse else range(nchunk)

    def per_chunk(rows_1, width):
        return jnp.concatenate([jnp.broadcast_to(r, (CHUNK, width)) for r in rows_1], axis=0)

    pieces = jnp.concatenate(_split3(g_all), axis=1)
    sums = jnp.dot(mask_b, pieces, preferred_element_type=F32)
    b_all = sums[:, :wall] + sums[:, wall:2 * wall] + sums[:, 2 * wall:]

    def prepare(h):
        q, k, v = head_qkv(h)
        b = b_all[:, h * width:(h + 1) * width]
        b_ref = [b[c * CHUNK + iref:c * CHUNK + iref + 1] for c in range(nchunk)]
        b_last = [b[c * CHUNK + ilast:c * CHUNK + ilast + 1] for c in range(nchunk)]
        e1 = jnp.exp(b - per_chunk(b_ref, width))
        r1 = 1.0 / e1
        qe = q * e1
        ke = k * r1
        qd = (qe * per_chunk([jnp.exp(r) for r in b_ref], width)).astype(BF16)
        kd = (ke * per_chunk([jnp.exp(l - r) for l, r in zip(b_last, b_ref)], width)).astype(BF16)
        decays = [jnp.exp(l) for l in b_last]
        return qe.astype(BF16), ke.astype(BF16), qd, kd, v.astype(BF16), decays

    nxt = prepare(0)
    for h in range(heads):
        qe, ke, qd, kd, vb, decays = nxt
        if h + 1 < heads:
            nxt = prepare(h + 1)
        if batched:
            sc = lax.dot_general(qe, ke, _NT, preferred_element_type=F32)
            sc = jnp.where(keep, sc, 0.0).astype(BF16)
            o = jnp.dot(sc, vb, preferred_element_type=F32)
        st = st_scr[h]
        parts = [None] * nchunk
        for ci in order:
            rs = slice(ci * CHUNK, (ci + 1) * CHUNK)
            if batched:
                o_c = o[rs]
            else:
                sc = lax.dot_general(qe[rs], ke[rs], _NT, preferred_element_type=F32)
                sc = jnp.where(keep_c, sc, 0.0).astype(BF16)
                o_c = jnp.dot(sc, vb[rs], preferred_element_type=F32)
            parts[ci] = o_c + lax.dot_general(qd[rs], st.astype(BF16), _NT,
                                              preferred_element_type=F32)
            st = st * decays[ci] + lax.dot_general(vb[rs], kd[rs], _TN, preferred_element_type=F32)
        if o_ref is not None:
            o_ref[:, h * dv:(h + 1) * dv] = jnp.concatenate(parts, axis=0).astype(o_ref.dtype)
        st_scr[h] = st


def _gla_kernel(q_ref, k_ref, v_ref, a_ref, wa_ref, ba_ref, o_ref, st_scr):
    t = pl.program_id(1)
    d = pl.program_id(2)
    dk = q_ref.shape[1] // GLA_HEADS
    dv = v_ref.shape[1] // GLA_HEADS

    @pl.when((t == 0) & (d == 0))
    def _():
        st_scr[...] = jnp.zeros(st_scr.shape, F32)

    def head_qkv(h):
        ks = slice(h * dk, (h + 1) * dk)
        q = q_ref[:, ks].astype(F32) * (dk ** -0.5)
        return q, k_ref[:, ks].astype(F32), v_ref[:, h * dv:(h + 1) * dv]

    for dd in range(2):
        @pl.when(d == dd)
        def _(dd=dd):
            z = jnp.dot(a_ref[...], wa_ref[dd], preferred_element_type=F32) + ba_ref[dd]
            g_all = _log_sigmoid(z) * (1.0 / GLA_TAU)
            _block_scan(g_all, head_qkv, st_scr.at[dd], o_ref,
                        heads=GLA_HEADS, dv=dv, reverse=dd == 1, batched=True)


def _gla(u, wa, ba, *, batch, seq, n_ctx, cols, dv_total):
    ntok = u.shape[0]
    q0, k0, v0, a0 = cols
    dkt = wa.shape[2]
    dvt = dv_total
    nblk = seq // ROW_TILE
    ctx_blk0 = batch * nblk
    assert n_ctx == ROW_TILE

    def blk(b, t, d):
        lat = b * nblk + jnp.where(d == 0, t - 1, nblk - t)
        return jnp.where(t == 0, ctx_blk0 + b, lat)

    in_specs = [pl.BlockSpec((ROW_TILE, dkt), lambda b, t, d: (blk(b, t, d), q0 * LANE // dkt)),
                pl.BlockSpec((ROW_TILE, dkt), lambda b, t, d: (blk(b, t, d), k0 * LANE // dkt)),
                pl.BlockSpec((ROW_TILE, dvt), lambda b, t, d: (blk(b, t, d), v0 * LANE // dvt)),
                pl.BlockSpec((ROW_TILE, LANE), lambda b, t, d: (blk(b, t, d), a0)),
                pl.BlockSpec(wa.shape, lambda b, t, d: (0, 0, 0)),
                pl.BlockSpec(ba.shape, lambda b, t, d: (0, 0, 0))]
    return pl.pallas_call(
        _gla_kernel,
        out_shape=jax.ShapeDtypeStruct((2, ntok, dvt), BF16),
        grid=(batch, nblk + 1, 2),
        in_specs=in_specs,
        out_specs=pl.BlockSpec((None, ROW_TILE, dvt), lambda b, t, d: (d, blk(b, t, d), 0)),
        scratch_shapes=[pltpu.VMEM((2, GLA_HEADS, dvt // GLA_HEADS, dkt // GLA_HEADS), F32)],
        compiler_params=_params("parallel", "arbitrary", "arbitrary"),
        name="gla",
    )(u, u, u, u, wa, ba)


def _hgrn_kernel(*refs, emit_o):
    q_ref, z_ref, v_ref, lb_ref, ub_ref, s0_ref = refs[:6]
    if emit_o:
        o_ref, st_ref, st_scr = refs[6:]
    else:
        st_ref, st_scr = refs[6:]
        o_ref = None
    t = pl.program_id(1)
    d = pl.program_id(2)
    dh = q_ref.shape[1] // HGRN_HEADS

    @pl.when((t == 0) & (d == 0))
    def _():
        st_scr[...] = s0_ref[:, 0]

    lb = lb_ref[...]
    ub = ub_ref[...]

    for dd in range(2):
        @pl.when(d == dd)
        def _(dd=dd):
            z = z_ref[...].astype(F32)
            e = jnp.exp(-jnp.abs(z))
            r = 1.0 / (1.0 + e)
            er = e * r
            pos = z >= 0.0
            log_f = jnp.log(lb + ub * jnp.where(pos, r, er))
            k_all = ub * jnp.where(pos, er, r)

            def head_qkv(h):
                hs = slice(h * dh, (h + 1) * dh)
                return q_ref[:, hs].astype(F32), k_all[:, hs], v_ref[:, hs]

            _block_scan(log_f, head_qkv, st_scr.at[dd], o_ref,
                        heads=HGRN_HEADS, dv=dh, reverse=dd == 1, batched=True)

    @pl.when((t == pl.num_programs(1) - 1) & (d == 1))
    def _():
        st_ref[:, 0] = st_scr[...]


def _hgrn_segment(u, s0, lb, ub, *, batch, seg_len, emit_o):
    ntok = u.shape[0]
    width = lb.shape[1]
    nblk = seg_len // ROW_TILE

    def blk(b, t, d):
        return b * nblk + jnp.where(d == 0, t, nblk - 1 - t)

    st_spec = pl.BlockSpec((2, 1) + s0.shape[2:], lambda b, t, d: (0, b, 0, 0, 0))
    in_specs = [pl.BlockSpec((ROW_TILE, width), lambda b, t, d: (blk(b, t, d), 0)),
                pl.BlockSpec((ROW_TILE, width), lambda b, t, d: (blk(b, t, d), 1 + d)),
                pl.BlockSpec((ROW_TILE, width), lambda b, t, d: (blk(b, t, d), 3)),
                pl.BlockSpec((1, width), lambda b, t, d: (0, 0)),
                pl.BlockSpec((1, width), lambda b, t, d: (0, 0)),
                st_spec]
    st_shape = jax.ShapeDtypeStruct(s0.shape, F32)
    if emit_o:
        out_shape = (jax.ShapeDtypeStruct((2, ntok, width), F32), st_shape)
        out_specs = (pl.BlockSpec((None, ROW_TILE, width), lambda b, t, d: (d, blk(b, t, d), 0)),
                     st_spec)
    else:
        out_shape = (st_shape,)
        out_specs = (st_spec,)
    return pl.pallas_call(
        functools.partial(_hgrn_kernel, emit_o=emit_o),
        out_shape=out_shape,
        grid=(batch, nblk, 2),
        in_specs=in_specs,
        out_specs=out_specs,
        scratch_shapes=[pltpu.VMEM((2,) + s0.shape[2:], F32)],
        compiler_params=_params("parallel", "arbitrary", "arbitrary"),
        name="hgrn_segment",
    )(u, u, u, lb, ub, s0)


def _head_norm_gate(o, gate, heads):
    dh = o.shape[1] // heads
    parts = []
    for h in range(heads):
        oh = o[:, h * dh:(h + 1) * dh]
        ms = jnp.mean(oh * oh, axis=-1, keepdims=True)
        parts.append((oh * lax.rsqrt(ms + EPS) * _silu(gate[:, h * dh:(h + 1) * dh])).astype(BF16))
    return jnp.concatenate(parts, axis=1)


def _load_rows(ref, r0, nrows):
    s_n = CHUNKS_PER_ROW
    return jnp.concatenate([ref[pl.ds(r0 * s_n + s, nrows, stride=s_n), :] for s in range(s_n)],
                           axis=1)


def _store_rows(ref, r0, val):
    s_n = CHUNKS_PER_ROW
    for s in range(s_n):
        ref[pl.ds(r0 * s_n + s, val.shape[0], stride=s_n), :] = val[:, s * LANE:(s + 1) * LANE]


def _out_proj_tail(o, x, tab_ref, pg_ref, ng_ref, rt_ref, xo_ref, h_ref, lg_ref):
    xn = _gated_residual(x, _rms(o, pg_ref[...]), tab_ref, 2)
    xo_ref[...] = xn
    h2 = _modulate(_rms(xn, ng_ref[...]), tab_ref, 3, 4)
    _store_rows(h_ref, 0, h2)
    lg_ref[...] = lax.dot_general(rt_ref[...], h2, _NT, precision=lax.Precision.HIGHEST,
                                  preferred_element_type=F32)


def _out_proj_ab_kernel(yl_ref, yc_ref, pt_ref, of_ref, ob_ref, gate_ref, xc_ref, w_ref, x_ref,
                        tab_ref, pg_ref, ng_ref, rt_ref, xo_ref, h_ref, lg_ref, *, lat_tiles):
    o_att = of_ref[...].astype(F32) + ob_ref[...].astype(F32)
    yb = _head_norm_gate(o_att, gate_ref[...].astype(F32), GLA_HEADS)
    is_lat = pl.program_id(0) < lat_tiles
    ya = jnp.where(is_lat, yl_ref[...], yc_ref[...])
    x = jnp.where(is_lat, x_ref[...], xc_ref[...])
    ya = jnp.concatenate(
        [jnp.dot(pt_ref[...], ya[s:s + ROW_TILE], preferred_element_type=F32).astype(BF16)
         for s in range(0, ya.shape[0], ROW_TILE)], axis=0)
    da = ya.shape[1]
    o = jnp.dot(ya, w_ref[:da], preferred_element_type=F32)
    o = o + jnp.dot(yb, w_ref[da:], preferred_element_type=F32)
    _out_proj_tail(o, x, tab_ref, pg_ref, ng_ref, rt_ref, xo_ref, h_ref, lg_ref)


def _out_proj_c_kernel(of_ref, ob_ref, gate_ref, p_ref, w_ref, x_ref, tab_ref, pg_ref, ng_ref,
                       rt_ref, xo_ref, h_ref, lg_ref):
    tm = x_ref.shape[0]
    width = of_ref.shape[-1]
    o_att = of_ref[...].astype(F32) + ob_ref[...].astype(F32)
    y = _head_norm_gate(o_att.reshape(tm, width), gate_ref[...].astype(F32).reshape(tm, width),
                        HGRN_HEADS)
    y = jnp.dot(p_ref[...], y, preferred_element_type=F32).astype(BF16)
    o = jnp.dot(y, w_ref[...], preferred_element_type=F32)
    _out_proj_tail(o, x_ref[...], tab_ref, pg_ref, ng_ref, rt_ref, xo_ref, h_ref, lg_ref)


def _out_proj(kernel, lead, lead_specs, w, x, tab, post_gain, next_gain, router_t, tm, ntiles,
              x_tiles=None):
    d = x.shape[1]
    ntok = ntiles * tm
    nsub = tm // ROW_TILE
    ne = router_t.shape[0]
    last_x = (x_tiles or ntiles) - 1
    in_specs = list(lead_specs) + [
        pl.BlockSpec(w.shape, lambda i: (0, 0), pipeline_mode=pl.Buffered(1)),
        pl.BlockSpec((tm, d), lambda i: (jnp.minimum(i, last_x), 0)),
        pl.BlockSpec((nsub, 8, d), lambda i: (i, 0, 0)),
        pl.BlockSpec((1, d), lambda i: (0, 0)),
        pl.BlockSpec((1, d), lambda i: (0, 0)),
        pl.BlockSpec((ne, d), lambda i: (0, 0))]
    return pl.pallas_call(
        kernel,
        out_shape=(jax.ShapeDtypeStruct((ntok, d), F32),
                   jax.ShapeDtypeStruct((ntok * CHUNKS_PER_ROW, LANE), F32),
                   jax.ShapeDtypeStruct((ne, ntok), F32)),
        grid=(ntiles,),
        in_specs=in_specs,
        out_specs=(pl.BlockSpec((tm, d), lambda i: (i, 0)),
                   pl.BlockSpec((tm * CHUNKS_PER_ROW, LANE), lambda i: (i, 0)),
                   pl.BlockSpec((ne, tm), lambda i: (0, i))),
        compiler_params=_params("parallel"),
        name="out_proj_norm_residual",
    )(*lead, w, x, tab, post_gain, next_gain, router_t)


def _route_kernel(lg_ref, bias_ref, ids_ref, w_ref):
    tm = lg_ref.shape[1]
    per_group = N_EXPERTS // N_GROUPS
    shape3 = (N_GROUPS, per_group, tm)
    aff = _sigmoid(lg_ref[...]).reshape(shape3)
    biased = aff + bias_ref[...].reshape(shape3)
    neg = -jnp.inf
    sub = lax.broadcasted_iota(jnp.int32, shape3, 1)
    grp = lax.broadcasted_iota(jnp.int32, shape3, 0)
    m1 = jnp.max(biased, axis=1, keepdims=True)
    i1 = jnp.min(jnp.where(biased == m1, sub, per_group), axis=1, keepdims=True)
    m2 = jnp.max(jnp.where(sub == i1, neg, biased), axis=1, keepdims=True)
    score = m1 + m2
    gidx = lax.broadcasted_iota(jnp.int32, score.shape, 0)
    keep = jnp.zeros(score.shape, F32)
    for _ in range(TOPK_GROUPS):
        m = jnp.max(score, axis=0, keepdims=True)
        im = jnp.min(jnp.where(score == m, gidx, N_GROUPS), axis=0, keepdims=True)
        sel = gidx == im
        keep = jnp.where(sel, 1.0, keep)
        score = jnp.where(sel, neg, score)
    work = jnp.where(jnp.broadcast_to(keep, shape3) > 0.0, biased, neg)
    eidx = grp * per_group + sub
    ids, gates = [], []
    for _ in range(TOP_K):
        m = jnp.max(jnp.max(work, axis=0, keepdims=True), axis=1, keepdims=True)
        cand = jnp.where(work == m, eidx, N_EXPERTS)
        im = jnp.min(jnp.min(cand, axis=0, keepdims=True), axis=1, keepdims=True)
        sel = eidx == im
        ids.append(im)
        gates.append(jnp.sum(jnp.sum(jnp.where(sel, aff, 0.0), axis=0, keepdims=True),
                             axis=1, keepdims=True))
        work = jnp.where(sel, neg, work)
    gate = jnp.concatenate(gates, axis=1)
    den = jnp.sum(gate, axis=1, keepdims=True)
    w_ref[...] = (ROUTED_SCALE * gate / den).reshape(TOP_K, tm)
    ids_ref[...] = jnp.concatenate(ids, axis=1).reshape(TOP_K, tm)


def _route(logits_t, bias, tm):
    ne, ntok = logits_t.shape
    bias_b = jnp.broadcast_to(bias.astype(F32)[:, None], (ne, tm))
    return pl.pallas_call(
        _route_kernel,
        out_shape=(jax.ShapeDtypeStruct((TOP_K, ntok), jnp.int32),
                   jax.ShapeDtypeStruct((TOP_K, ntok), F32)),
        grid=(ntok // tm,),
        in_specs=[pl.BlockSpec((ne, tm), lambda i: (0, i)),
                  pl.BlockSpec((ne, tm), lambda i: (0, 0))],
        out_specs=(pl.BlockSpec((TOP_K, tm), lambda i: (0, i)),
                   pl.BlockSpec((TOP_K, tm), lambda i: (0, i))),
        compiler_params=_params("parallel"),
        name="moe_route",
    )(logits_t, bias_b)


LIST_PAD = 1024


def _dispatch_plan(ids, w, t0, tile, ntiles):
    n = ntiles * tile
    e = ids[:, t0:t0 + n].T.reshape(ntiles, tile * TOP_K)
    g = w[:, t0:t0 + n].T.reshape(ntiles, tile * TOP_K)
    nassign = tile * TOP_K
    assert nassign & (nassign - 1) == 0 and N_EXPERTS * nassign < 2 ** 31
    keys = e * nassign + jnp.arange(nassign, dtype=jnp.int32)
    order = jnp.sort(keys, axis=1) & (nassign - 1)
    rows = (order // TOP_K) * CHUNKS_PER_ROW
    wts = jnp.take_along_axis(g, order, axis=1)
    experts = jnp.arange(N_EXPERTS, dtype=jnp.int32)
    counts = jnp.sum((e[:, :, None] == experts).astype(jnp.int32), axis=1)
    starts = jnp.concatenate([jnp.zeros((ntiles, 1), jnp.int32), jnp.cumsum(counts, axis=1)], axis=1)
    rows = jnp.pad(rows, ((0, 0), (0, LIST_PAD)))
    wts = jnp.pad(wts, ((0, 0), (0, LIST_PAD)))
    starts = jnp.pad(starts, ((0, 0), (0, LANE - N_EXPERTS - 1)))
    return rows, wts, starts


def _moe_kernel(rows_hbm, wts_hbm, starts_hbm, h_ref, wg_ref, wu_ref, wd_ref, sg_ref, su_ref,
                sd_ref, o_ref, rows_s, wts_s, starts_s, sem, gbuf, ybuf, *, epb):
    i = pl.program_id(0)
    j = pl.program_id(1)
    s_n = CHUNKS_PER_ROW
    blk = MOE_BLOCK
    tile = h_ref.shape[0] // s_n

    def swiglu(x, w_gate, w_up, w_down):
        g = jnp.dot(x, w_gate, preferred_element_type=F32)
        u = jnp.dot(x, w_up, preferred_element_type=F32)
        return jnp.dot((_silu(g) * u).astype(BF16), w_down, preferred_element_type=F32)

    @pl.when(j == 0)
    def _():
        copies = [pltpu.make_async_copy(rows_hbm.at[i], rows_s, sem.at[0]),
                  pltpu.make_async_copy(wts_hbm.at[i], wts_s, sem.at[1]),
                  pltpu.make_async_copy(starts_hbm.at[i], starts_s, sem.at[2])]
        for c in copies:
            c.start()
        for r0 in range(0, tile, ROW_TILE):
            x = _load_rows(h_ref, r0, ROW_TILE).astype(BF16)
            _store_rows(o_ref, r0, swiglu(x, sg_ref[...], su_ref[...], sd_ref[...]))
        for c in copies:
            c.wait()

    for e in range(epb):
        base = starts_s[j * epb + e]
        count = starts_s[j * epb + e + 1] - base
        w_gate = wg_ref[e].astype(BF16)
        w_up = wu_ref[e].astype(BF16)
        w_down = wd_ref[e].astype(BF16)

        def block(bi, carry, base=base, count=count, w_gate=w_gate, w_up=w_up, w_down=w_down):
            p0 = base + bi * blk
            left = count - bi * blk
            offs = [pl.multiple_of(rows_s[p0 + r], s_n) for r in range(blk)]
            for r in range(blk):
                gbuf[pl.ds(r * s_n, s_n), :] = h_ref[pl.ds(offs[r], s_n), :]
            x = _load_rows(gbuf, 0, blk).astype(BF16)
            y = swiglu(x, w_gate, w_up, w_down)
            y = jnp.where(lax.broadcasted_iota(jnp.int32, y.shape, 0) < left, y, 0.0)
            _store_rows(ybuf, 0, y)
            group = 2 * SUBLANE
            for r0 in range(0, blk, group):
                vals = [o_ref[pl.ds(offs[r], s_n), :] + wts_s[p0 + r] * ybuf[pl.ds(r * s_n, s_n), :]
                        for r in range(r0, r0 + group)]
                for r in reversed(range(r0, r0 + group)):
                    o_ref[pl.ds(offs[r], s_n), :] = vals[r - r0]
            return carry

        nblk = lax.shift_right_logical(count + (blk - 1), blk.bit_length() - 1)
        lax.fori_loop(0, nblk, block, 0)


def _moe(h, plan, layer, wg, wu, wd, sg, su, sd, tile0, ntiles, tile, epb):
    rows, wts, starts = plan
    s_n = CHUNKS_PER_ROW
    _, ne, d, ff = wg.shape
    once = pl.Buffered(1)
    hbm = pl.BlockSpec(memory_space=pl.ANY)
    return pl.pallas_call(
        functools.partial(_moe_kernel, epb=epb),
        out_shape=jax.ShapeDtypeStruct((ntiles * tile * s_n, LANE), F32),
        grid=(ntiles, ne // epb),
        in_specs=[hbm, hbm, hbm,
                  pl.BlockSpec((tile * s_n, LANE), lambda i, j: (tile0 + i, 0), pipeline_mode=once),
                  pl.BlockSpec((None, epb, d, ff), lambda i, j: (layer, j, 0, 0)),
                  pl.BlockSpec((None, epb, d, ff), lambda i, j: (layer, j, 0, 0)),
                  pl.BlockSpec((None, epb, ff, d), lambda i, j: (layer, j, 0, 0)),
                  pl.BlockSpec(sg.shape, lambda i, j: (0, 0), pipeline_mode=once),
                  pl.BlockSpec(su.shape, lambda i, j: (0, 0), pipeline_mode=once),
                  pl.BlockSpec(sd.shape, lambda i, j: (0, 0), pipeline_mode=once)],
        out_specs=pl.BlockSpec((tile * s_n, LANE), lambda i, j: (i, 0)),
        scratch_shapes=[pltpu.SMEM((rows.shape[1],), jnp.int32),
                        pltpu.SMEM((wts.shape[1],), F32),
                        pltpu.SMEM((starts.shape[1],), jnp.int32),
                        pltpu.SemaphoreType.DMA((3,)),
                        pltpu.VMEM((MOE_BLOCK * s_n, LANE), F32),
                        pltpu.VMEM((MOE_BLOCK * s_n, LANE), F32)],
        compiler_params=_params("parallel", "arbitrary", vmem=VMEM_LIMIT_MOE),
        name="moe_experts",
    )(rows, wts, starts, h, wg, wu, wd, sg, su, sd)


def _moe_finish_kernel(f_ref, x_ref, tab_ref, pg_ref, o_ref):
    f = _load_rows(f_ref, 0, x_ref.shape[0])
    o_ref[...] = _gated_residual(x_ref[...], _rms(f, pg_ref[...]), tab_ref, 5)


def _moe_finish(f, x, tab, post_gain, tm, x_tile0, tab0):
    d = x.shape[1]
    s_n = CHUNKS_PER_ROW
    nsub = tm // ROW_TILE
    ntiles = f.shape[0] // (tm * s_n)
    return pl.pallas_call(
        _moe_finish_kernel,
        out_shape=jax.ShapeDtypeStruct((ntiles * tm, d), F32),
        grid=(ntiles,),
        in_specs=[pl.BlockSpec((tm * s_n, LANE), lambda i: (i, 0)),
                  pl.BlockSpec((tm, d), lambda i: (x_tile0 + i, 0)),
                  pl.BlockSpec((nsub, 8, d), lambda i: (tab0 + i, 0, 0)),
                  pl.BlockSpec((1, d), lambda i: (0, 0))],
        out_specs=pl.BlockSpec((tm, d), lambda i: (i, 0)),
        compiler_params=_params("parallel"),
        name="moe_norm_residual",
    )(f, x, tab, post_gain)


def _grid_permutation(band):
    tm = band * GRID_W
    dst = jnp.arange(tm)
    src = (dst % band) * GRID_W + dst // band
    return (src[:, None] == jnp.arange(tm)[None, :]).astype(BF16)


def kernel(x, c, ctx, c_ctx, mod_w, mod_b, pre_gain, post_gain, ab_w_in, ab_conv_w, ab_conv_b, ab_lru_wa, ab_lru_ba, ab_lru_wi, ab_lru_bi, ab_lru_lambda, ab_gla_wa2, ab_gla_ba, ab_w_out, c_w_in, hgrn_lb_logits, c_w_out, moe_router, moe_bias, moe_w_gate, moe_w_up, moe_w_down, moe_ws_gate, moe_ws_up, moe_ws_down):
    batch, seq, d = x.shape
    n_ctx = ctx.shape[1]
    depth = mod_w.shape[0]
    band = SUBLANE
    tm_grid = band * GRID_W
    rows_img = seq // GRID_W
    assert n_ctx == ROW_TILE and seq % (2 * ROW_TILE) == 0 and d % LANE == 0
    assert depth == 2 and batch + 1 <= 8 and rows_img % band == 0 and tm_grid % ROW_TILE == 0
    n_lat = batch * seq
    n_ctx_tok = batch * n_ctx
    ntok = n_lat + n_ctx_tok

    x_lat = x.reshape(n_lat, d)
    x_ctx = ctx.reshape(n_ctx_tok, d)

    cvec = jnp.concatenate([c, c_ctx[None, :], jnp.zeros((8 - batch - 1, d), F32)], axis=0)
    mods = _modulation(cvec, mod_w, mod_b).reshape(depth, 8, N_MOD, d)
    tile_row = jnp.concatenate([jnp.repeat(jnp.arange(batch), seq // ROW_TILE),
                                jnp.full((batch,), batch)]).astype(jnp.int32)
    tabs = jnp.pad(mods[:, tile_row], ((0, 0), (0, 0), (0, 8 - N_MOD), (0, 0)))

    def experts(layer, h2, routed, t0, ntiles, tile):
        plan = _dispatch_plan(*routed, t0, tile, ntiles)
        return _moe(h2, plan, layer, moe_w_gate, moe_w_up, moe_w_down,
                    moe_ws_gate[layer].astype(BF16), moe_ws_up[layer].astype(BF16),
                    moe_ws_down[layer].astype(BF16), t0 // tile, ntiles, tile, epb)

    tm_proj, tm_out, tm_route, tm_moe, epb = 512, 512, 512, 2048, 4
    assert n_lat % tm_moe == 0 and n_lat % n_ctx_tok == 0

    ab_cols = ab_w_in.shape[2]
    gla_dk = ab_gla_wa2.shape[3]
    lru_w = d
    gla_dv = (ab_cols - 2 * lru_w - 2 * gla_dk - 2 * GLA_RANK) // 2
    n_main = ab_cols // LANE * LANE
    tn0 = n_main // 4
    w_main = ab_w_in[0][:, :n_main].astype(BF16)
    w_tail = jnp.pad(ab_w_in[0][:, n_main:], ((0, 0), (0, LANE - (ab_cols - n_main)))).astype(BF16)
    u0 = _in_proj(x_lat, x_ctx, tabs[0], pre_gain[0, 0][None, :], w_main, w_tail, tm_proj, tn0)

    cw = jnp.pad(ab_conv_w[0], ((0, 8 - CONV_W), (0, 0)))
    cb = ab_conv_b[0][None, :]
    wg = jnp.concatenate([ab_lru_wa[0], ab_lru_wi[0]], axis=-1).astype(BF16)
    bg = jnp.concatenate([ab_lru_ba[0].reshape(2, LRU_HEADS, 1, LANE),
                          ab_lru_bi[0].reshape(2, LRU_HEADS, 1, LANE)], axis=-1)
    ya_c, ya_l = _lru(u0, (cw, cb, wg, bg, ab_lru_lambda[0]), batch=batch, seq=seq, n_ctx=n_ctx,
                      g_col0=0, x_col0=lru_w // LANE)

    q0 = 2 * lru_w // LANE
    k0 = q0 + gla_dk // LANE
    v0 = k0 + gla_dk // LANE
    gate0 = v0 + gla_dv // LANE
    a0 = gate0 + gla_dv // LANE
    wa = jnp.stack([jnp.zeros((LANE, gla_dk), F32).at[dr * GLA_RANK:(dr + 1) * GLA_RANK]
                    .set(ab_gla_wa2[0, dr]) for dr in range(2)]).astype(BF16)
    ba = ab_gla_ba[0][:, None, :]
    o_gla = _gla(u0, wa, ba, batch=batch, seq=seq, n_ctx=n_ctx, cols=(q0, k0, v0, a0),
                 dv_total=gla_dv)

    lat_tiles = n_lat // tm_out
    ctx_tiles = n_ctx_tok // tm_out
    lead_specs = [pl.BlockSpec((tm_out, lru_w), lambda i: (jnp.minimum(i, lat_tiles - 1), 0)),
                  pl.BlockSpec((tm_out, lru_w),
                               lambda i: (jnp.clip(i - lat_tiles, 0, ctx_tiles - 1), 0)),
                  pl.BlockSpec((ROW_TILE, ROW_TILE), lambda i: (0, 0)),
                  pl.BlockSpec((None, tm_out, gla_dv), lambda i: (0, i, 0)),
                  pl.BlockSpec((None, tm_out, gla_dv), lambda i: (1, i, 0)),
                  pl.BlockSpec((tm_out, gla_dv), lambda i: (i, gate0 * LANE // gla_dv)),
                  pl.BlockSpec((tm_out, d), lambda i: (jnp.clip(i - lat_tiles, 0, ctx_tiles - 1), 0))]
    xs, h2, lg = _out_proj(functools.partial(_out_proj_ab_kernel, lat_tiles=lat_tiles),
                           (ya_l, ya_c, _time_permutation().T.astype(BF16), o_gla, o_gla, u0, x_ctx),
                           lead_specs,
                           ab_w_out[0].astype(BF16), x_lat, tabs[0], post_gain[0, 0][None, :],
                           pre_gain[0, 1][None, :], moe_router[0].T, tm_out, ntok // tm_out,
                           x_tiles=lat_tiles)
    cwt = _route(lg, moe_bias[0], tm_route)
    f_ctx = experts(0, h2, cwt, n_lat, 1, n_ctx_tok)
    f_lat = experts(0, h2, cwt, 0, n_lat // tm_moe, tm_moe)
    pg0 = post_gain[0, 1][None, :]

    hd = c_w_out.shape[1]
    tn1 = 1280
    w_in1 = c_w_in[0].astype(BF16)
    c_cols = w_in1.shape[1]
    nsec = c_cols // hd
    perm = _grid_permutation(band)
    gain1 = pre_gain[1, 0][None, :]
    u1c = _in_proj_ctx(f_ctx, xs, tabs[0], pg0, tabs[1], gain1, w_in1, tm_proj, tn1,
                       n_lat // tm_proj)
    u1l, xs = _in_proj_grid(f_lat, xs, tabs[0], pg0, tabs[1], gain1, perm, w_in1, tn1,
                            n_lat // tm_grid, band, rows_img // band)
    lb_sm = jax.nn.softmax(hgrn_lb_logits.astype(F32), axis=0)
    lb_cum = jnp.cumsum(lb_sm, axis=0)
    lb = (lb_cum[1] - lb_cum[0])[None, :]
    ub = 1.0 - lb
    dh = hd // HGRN_HEADS
    s_zero = jnp.zeros((2, batch, HGRN_HEADS, dh, dh), F32)
    u1l_flat = u1l.reshape(n_lat, c_cols)
    (s_c,) = _hgrn_segment(u1c, s_zero, lb, ub, batch=batch, seg_len=n_ctx, emit_o=False)
    o_hgrn, _ = _hgrn_segment(u1l_flat, s_c, lb, ub, batch=batch, seg_len=seq, emit_o=True)
    o_hgrn = o_hgrn.reshape(2, n_lat // rows_img, rows_img // band, band, hd)

    bpi = rows_img // band
    grid_blk = (GRID_W, None, band, hd)
    lead_specs = [pl.BlockSpec((None,) + grid_blk, lambda i: (0, i // bpi, i % bpi, 0, 0)),
                  pl.BlockSpec((None,) + grid_blk, lambda i: (1, i // bpi, i % bpi, 0, 0)),
                  pl.BlockSpec(grid_blk, lambda i: (i // bpi, i % bpi, 0, nsec - 1)),
                  pl.BlockSpec((tm_grid, tm_grid), lambda i: (0, 0))]
    xs, h2, lg = _out_proj(_out_proj_c_kernel, (o_hgrn, o_hgrn, u1l, perm.T), lead_specs,
                           c_w_out[0].astype(BF16), xs, tabs[1], post_gain[1, 0][None, :],
                           pre_gain[1, 1][None, :], moe_router[1].T, tm_grid, n_lat // tm_grid)
    f_lat = experts(1, h2, _route(lg, moe_bias[1], tm_route), 0, n_lat // tm_moe, tm_moe)
    out = _moe_finish(f_lat, xs, tabs[1], post_gain[1, 1][None, :], tm_out, 0, 0)
    return out.reshape(batch, seq, d)
```

```python
import functools

import jax
import jax.numpy as jnp
from jax import lax
from jax.experimental import pallas as pl
from jax.experimental.pallas import tpu as pltpu

F32 = jnp.float32
BF16 = jnp.bfloat16

EPS = 1e-6
N_MOD = 6
ROW_TILE = 256
CHUNK = 64
CHUNK_SHIFT = 6
GRID_W = 64
LRU_HEADS = 8
LRU_C = 8.0
CONV_W = 4
GLA_HEADS = 4
GLA_RANK = 16
GLA_TAU = 16.0
HGRN_HEADS = 8
N_EXPERTS = 64
TOP_K = 8
N_GROUPS = 8
TOPK_GROUPS = 4
ROUTED_SCALE = 2.5
LANE = 128
SUBLANE = 8
BF16_ROWS = 16
CHUNKS_PER_ROW = 8
MOE_BLOCK = 128
VMEM_LIMIT = 48 * 1024 * 1024
VMEM_LIMIT_MOE = 58 * 1024 * 1024

_NT = (((1,), (1,)), ((), ()))
_TN = (((0,), (0,)), ((), ()))


def _params(*sem, vmem=VMEM_LIMIT):
    return pltpu.CompilerParams(dimension_semantics=sem, vmem_limit_bytes=vmem)


def _rms(x, gain):
    ms = jnp.mean(x * x, axis=-1, keepdims=True)
    return x * lax.rsqrt(ms + EPS) * gain


def _sigmoid(x):
    return jax.nn.sigmoid(x)


def _silu(x):
    return x * _sigmoid(x)


def _log_sigmoid(x):
    return jnp.minimum(x, 0.0) - jnp.log1p(jnp.exp(-jnp.abs(x)))


def _softplus(x):
    return jnp.maximum(x, 0.0) + jnp.log1p(jnp.exp(-jnp.abs(x)))


def _modulate(xn, tab_ref, shift_row, scale_row):
    parts = []
    for s in range(tab_ref.shape[0]):
        t = tab_ref[s]
        rows = xn[s * ROW_TILE:(s + 1) * ROW_TILE]
        parts.append(rows * (1.0 + t[scale_row:scale_row + 1]) + t[shift_row:shift_row + 1])
    return parts[0] if len(parts) == 1 else jnp.concatenate(parts, axis=0)


def _gated_residual(x, branch, tab_ref, gate_row):
    parts = []
    for s in range(tab_ref.shape[0]):
        sl = slice(s * ROW_TILE, (s + 1) * ROW_TILE)
        parts.append(x[sl] + tab_ref[s][gate_row:gate_row + 1] * branch[sl])
    return parts[0] if len(parts) == 1 else jnp.concatenate(parts, axis=0)


def _mod_kernel(c_ref, w_ref, b_ref, o_ref):
    s = _silu(c_ref[...])
    o_ref[0] = jnp.dot(s, w_ref[0], precision=lax.Precision.HIGHEST,
                       preferred_element_type=F32) + b_ref[0]


def _modulation(cvec, mod_w, mod_b):
    depth, d, n = mod_w.shape
    tn = 1024
    return pl.pallas_call(
        _mod_kernel,
        out_shape=jax.ShapeDtypeStruct((depth, 8, n), F32),
        grid=(depth, n // tn),
        in_specs=[pl.BlockSpec((8, d), lambda l, j: (0, 0)),
                  pl.BlockSpec((1, d, tn), lambda l, j: (l, 0, j)),
                  pl.BlockSpec((1, 1, tn), lambda l, j: (l, 0, j))],
        out_specs=pl.BlockSpec((1, 8, tn), lambda l, j: (l, 0, j)),
        compiler_params=_params("parallel", "parallel"),
        name="adaln_modulation",
    )(cvec, mod_w, mod_b.reshape(depth, 1, n))


def _in_proj_kernel(xl_ref, xc_ref, tab_ref, gain_ref, w_ref, wt_ref, o_ref, *, tn, lat_tiles):
    x = jnp.where(pl.program_id(0) < lat_tiles, xl_ref[...], xc_ref[...])
    h = _modulate(_rms(x, gain_ref[...]), tab_ref, 0, 1).astype(BF16)
    n_main = w_ref.shape[1]
    for j in range(n_main // tn):
        cs = slice(j * tn, (j + 1) * tn)
        o_ref[:, cs] = jnp.dot(h, w_ref[:, cs], preferred_element_type=F32).astype(o_ref.dtype)
    o_ref[:, n_main:] = jnp.dot(h, wt_ref[...], preferred_element_type=F32).astype(o_ref.dtype)


def _in_proj(x_lat, x_ctx, tab, gain, w, w_tail, tm, tn):
    d = x_lat.shape[1]
    n = w.shape[1] + w_tail.shape[1]
    nsub = tm // ROW_TILE
    lat_tiles = x_lat.shape[0] // tm
    ctx_tiles = x_ctx.shape[0] // tm
    ntiles = lat_tiles + ctx_tiles
    once = pl.Buffered(1)
    return pl.pallas_call(
        functools.partial(_in_proj_kernel, tn=tn, lat_tiles=lat_tiles),
        out_shape=jax.ShapeDtypeStruct((ntiles * tm, n), BF16),
        grid=(ntiles,),
        in_specs=[pl.BlockSpec((tm, d), lambda i: (jnp.minimum(i, lat_tiles - 1), 0)),
                  pl.BlockSpec((tm, d), lambda i: (jnp.clip(i - lat_tiles, 0, ctx_tiles - 1), 0)),
                  pl.BlockSpec((nsub, 8, d), lambda i: (i, 0, 0)),
                  pl.BlockSpec((1, d), lambda i: (0, 0)),
                  pl.BlockSpec(w.shape, lambda i: (0, 0), pipeline_mode=once),
                  pl.BlockSpec(w_tail.shape, lambda i: (0, 0), pipeline_mode=once)],
        out_specs=pl.BlockSpec((tm, n), lambda i: (i, 0)),
        compiler_params=_params("parallel"),
        name="norm_mod_in_proj",
    )(x_lat, x_ctx, tab, gain, w, w_tail)


def _after_moe(f_ref, x_ref, ptab_ref, pg_ref):
    f = _load_rows(f_ref, 0, x_ref.shape[0])
    return _gated_residual(x_ref[...], _rms(f, pg_ref[...]), ptab_ref, 5)


def _in_proj_ctx_kernel(f_ref, x_ref, ptab_ref, pg_ref, tab_ref, gain_ref, w_ref, o_ref, *, tn):
    xn = _after_moe(f_ref, x_ref, ptab_ref, pg_ref)
    h = _modulate(_rms(xn, gain_ref[...]), tab_ref, 0, 1).astype(BF16)
    for j in range(w_ref.shape[1] // tn):
        cs = slice(j * tn, (j + 1) * tn)
        o_ref[:, cs] = jnp.dot(h, w_ref[:, cs], preferred_element_type=F32).astype(o_ref.dtype)


def _in_proj_ctx(f, x, ptab, post_gain, tab, gain, w, tm, tn, tile0):
    d = x.shape[1]
    n = w.shape[1]
    s_n = CHUNKS_PER_ROW
    nsub = tm // ROW_TILE
    ntiles = f.shape[0] // (tm * s_n)
    tabspec = pl.BlockSpec((nsub, 8, d), lambda i: (tile0 + i, 0, 0))
    vec = pl.BlockSpec((1, d), lambda i: (0, 0))
    return pl.pallas_call(
        functools.partial(_in_proj_ctx_kernel, tn=tn),
        out_shape=jax.ShapeDtypeStruct((ntiles * tm, n), BF16),
        grid=(ntiles,),
        in_specs=[pl.BlockSpec((tm * s_n, LANE), lambda i: (i, 0)),
                  pl.BlockSpec((tm, d), lambda i: (tile0 + i, 0)),
                  tabspec, vec, tabspec, vec,
                  pl.BlockSpec((d, n), lambda i: (0, 0), pipeline_mode=pl.Buffered(1))],
        out_specs=pl.BlockSpec((tm, n), lambda i: (i, 0)),
        compiler_params=_params("parallel"),
        name="moe_close_in_proj",
    )(f, x, ptab, post_gain, tab, gain, w)


def _in_proj_grid_kernel(f_ref, x_ref, ptab_ref, pg_ref, tab_ref, gain_ref, p_ref, w_ref,
                         o_ref, xo_ref, *, tn):
    xn = _after_moe(f_ref, x_ref, ptab_ref, pg_ref)
    xo_ref[...] = xn
    h = _modulate(_rms(xn, gain_ref[...]), tab_ref, 0, 1).astype(BF16)
    hp = jnp.dot(p_ref[...], h, preferred_element_type=F32).astype(BF16)
    for j in range(w_ref.shape[1] // tn):
        cs = slice(j * tn, (j + 1) * tn)
        o = jnp.dot(hp, w_ref[:, cs], preferred_element_type=F32).astype(o_ref.dtype)
        o_ref[:, :, cs] = o.reshape(o_ref.shape[:2] + (tn,))


def _in_proj_grid(f, x, ptab, post_gain, tab, gain, perm, w, tn, ntiles, band, bands_per_image):
    d = x.shape[1]
    n = w.shape[1]
    s_n = CHUNKS_PER_ROW
    tm = band * GRID_W
    nsub = tm // ROW_TILE
    images = ntiles // bands_per_image
    once = pl.Buffered(1)
    tabspec = pl.BlockSpec((nsub, 8, d), lambda i: (i, 0, 0))
    vec = pl.BlockSpec((1, d), lambda i: (0, 0))
    return pl.pallas_call(
        functools.partial(_in_proj_grid_kernel, tn=tn),
        out_shape=(jax.ShapeDtypeStruct((images * GRID_W, bands_per_image, band, n), F32),
                   jax.ShapeDtypeStruct((ntiles * tm, d), F32)),
        grid=(ntiles,),
        in_specs=[pl.BlockSpec((tm * s_n, LANE), lambda i: (i, 0)),
                  pl.BlockSpec((tm, d), lambda i: (i, 0)),
                  tabspec, vec, tabspec, vec,
                  pl.BlockSpec((tm, tm), lambda i: (0, 0), pipeline_mode=once),
                  pl.BlockSpec((d, n), lambda i: (0, 0), pipeline_mode=once)],
        out_specs=(pl.BlockSpec((GRID_W, None, band, n),
                                lambda i: (i // bands_per_image, i % bands_per_image, 0, 0)),
                   pl.BlockSpec((tm, d), lambda i: (i, 0))),
        compiler_params=_params("parallel"),
        name="moe_close_in_proj_grid",
    )(f, x, ptab, post_gain, tab, gain, perm, w)


def _lru_kernel(gc_ref, xc_ref, gl_ref, xl_ref, p_ref, cw_ref, cb_ref, wg_ref, bg_ref,
                lam_ref, yc_ref, yl_ref, xs_scr, hf_scr):
    T = ROW_TILE
    V = T // SUBLANE
    H = BF16_ROWS
    sub = lax.broadcasted_iota(jnp.int32, (SUBLANE, LANE), 0)
    cw = cw_ref[...]
    cb = cb_ref[...]
    sp = _softplus(-lam_ref[...])

    def vreg(a, v):
        return a[v * SUBLANE:(v + 1) * SUBLANE]

    def shift_down(y, first):
        return jnp.where(sub == 0, first, pltpu.roll(y, 1, 0))

    def shift_up(y, last):
        return jnp.where(sub == SUBLANE - 1, last, pltpu.roll(y, SUBLANE - 1, 0))

    def conv_tile(x_ref, t, nt, seg_len):
        t0 = pl.multiple_of(t * T, T)
        xp = jnp.dot(p_ref[...], x_ref[pl.ds(t0, T), :], preferred_element_type=F32)
        p0 = pl.multiple_of(jnp.maximum(t0 - H, 0), H)
        n0 = pl.multiple_of(jnp.minimum(t0 + T, seg_len - H), H)
        prev = x_ref[pl.ds(p0, H), :].astype(F32) * jnp.where(t > 0, 1.0, 0.0)
        nxt = x_ref[pl.ds(n0, H), :].astype(F32) * jnp.where(t < nt - 1, 1.0, 0.0)
        m1, m2, p1 = prev[H - 1:H], prev[H - 2:H - 1], nxt[0:1]
        last_m1 = shift_down(vreg(xp, V - 1), m1)
        xm1 = jnp.concatenate([last_m1, xp[:T - SUBLANE]], axis=0)
        xm2 = jnp.concatenate([shift_down(vreg(xp, V - 2), m2), last_m1, xp[:T - 2 * SUBLANE]], axis=0)
        xp1 = jnp.concatenate([xp[SUBLANE:], shift_up(vreg(xp, 0), p1)], axis=0)
        return cb + xm2 * cw[0:1] + xm1 * cw[1:2] + xp * cw[2:3] + xp1 * cw[3:4]

    def gates(xc, d):
        z = jnp.dot(xc.astype(BF16), wg_ref[d, 0], preferred_element_type=F32) + bg_ref[d, 0]
        r = _sigmoid(z[:, :LANE])
        i = _sigmoid(z[:, LANE:])
        log_a = (-LRU_C) * r * sp[d:d + 1]
        a = jnp.exp(log_a)
        return a, jnp.sqrt(1.0 - a * a) * (i * xc)

    def scan_tile(a, u, carry, reverse):
        hs, cum = [None] * V, [None] * V
        hp = ap = None
        for v in (range(V - 1, -1, -1) if reverse else range(V)):
            av, uv = vreg(a, v), vreg(u, v)
            hp, ap = (uv, av) if hp is None else (av * hp + uv, av * ap)
            hs[v], cum[v] = hp, ap
        s = 1
        while s < SUBLANE:
            keep = (sub < SUBLANE - s) if reverse else (sub >= s)
            sh = SUBLANE - s if reverse else s
            h_sh = jnp.where(keep, pltpu.roll(hp, sh, 0), 0.0)
            a_sh = jnp.where(keep, pltpu.roll(ap, sh, 0), 1.0)
            hp = hp + ap * h_sh
            ap = ap * a_sh
            s *= 2
        state = hp + ap * carry
        if reverse:
            start, out = shift_up(state, carry), state[0:1]
        else:
            start, out = shift_down(state, carry), state[SUBLANE - 1:SUBLANE]
        h = jnp.concatenate([hs[v] + cum[v] * start for v in range(V)], axis=0)
        return h, out

    def run_segment(g_ref, x_ref, y_ref, seg_len, carry_f, carry_b):
        nt = seg_len // T
        unroll = 4 if nt % 4 == 0 else (2 if nt % 2 == 0 else 1)

        def fwd_body(i, carry):
            for j in range(unroll):
                t = i * unroll + j
                t0 = pl.multiple_of(t * T, T)
                xc = conv_tile(x_ref, t, nt, seg_len)
                xs_scr[pl.ds(t0, T), :] = xc
                a, u = gates(xc, 0)
                h, carry = scan_tile(a, u, carry, False)
                hf_scr[pl.ds(t0, T), :] = h
            return carry

        def bwd_body(i, carry):
            for j in range(unroll):
                t = nt - 1 - (i * unroll + j)
                t0 = pl.multiple_of(t * T, T)
                a, u = gates(xs_scr[pl.ds(t0, T), :], 1)
                h, carry = scan_tile(a, u, carry, True)
                g = jnp.dot(p_ref[...], g_ref[pl.ds(t0, T), :], preferred_element_type=F32)
                rec = hf_scr[pl.ds(t0, T), :] + h
                y_ref[pl.ds(t0, T), :] = (jax.nn.gelu(g) * rec).astype(y_ref.dtype)
            return carry

        steps = nt // unroll
        return (lax.fori_loop(0, steps, fwd_body, carry_f),
                lax.fori_loop(0, steps, bwd_body, carry_b))

    zero = jnp.zeros((1, LANE), F32)
    hf, hb = run_segment(gc_ref, xc_ref, yc_ref, xc_ref.shape[0], zero, zero)
    run_segment(gl_ref, xl_ref, yl_ref, xl_ref.shape[0], hf, hb)


def _time_permutation():
    v = ROW_TILE // SUBLANE
    dst = jnp.arange(ROW_TILE)
    src = (dst % SUBLANE) * v + dst // SUBLANE
    return src[:, None] == jnp.arange(ROW_TILE)[None, :]


def _lru(u, lru_w, *, batch, seq, n_ctx, g_col0, x_col0):
    width = LRU_HEADS * LANE
    cw, cb, wg, bg, lam = lru_w
    ctx_blk0 = batch * seq // n_ctx
    perm = _time_permutation()
    once = pl.Buffered(1)
    return pl.pallas_call(
        _lru_kernel,
        out_shape=(jax.ShapeDtypeStruct((batch * n_ctx, width), BF16),
                   jax.ShapeDtypeStruct((batch * seq, width), BF16)),
        grid=(batch, LRU_HEADS),
        in_specs=[pl.BlockSpec((n_ctx, LANE), lambda b, h: (ctx_blk0 + b, g_col0 + h)),
                  pl.BlockSpec((n_ctx, LANE), lambda b, h: (ctx_blk0 + b, x_col0 + h)),
                  pl.BlockSpec((seq, LANE), lambda b, h: (b, g_col0 + h)),
                  pl.BlockSpec((seq, LANE), lambda b, h: (b, x_col0 + h)),
                  pl.BlockSpec((ROW_TILE, ROW_TILE), lambda b, h: (0, 0), pipeline_mode=once),
                  pl.BlockSpec((8, LANE), lambda b, h: (0, h)),
                  pl.BlockSpec((1, LANE), lambda b, h: (0, h)),
                  pl.BlockSpec((2, 1, LANE, 2 * LANE), lambda b, h: (0, h, 0, 0)),
                  pl.BlockSpec((2, 1, 1, 2 * LANE), lambda b, h: (0, h, 0, 0)),
                  pl.BlockSpec((2, LANE), lambda b, h: (0, h))],
        out_specs=(pl.BlockSpec((n_ctx, LANE), lambda b, h: (b, h)),
                   pl.BlockSpec((seq, LANE), lambda b, h: (b, h))),
        scratch_shapes=[pltpu.VMEM((seq, LANE), F32), pltpu.VMEM((seq, LANE), F32)],
        compiler_params=_params("parallel", "parallel"),
        name="rglru",
    )(u, u, u, u, perm.astype(BF16), cw, cb, wg, bg, lam)


def _chunk_mask(rows, reverse):
    r = lax.broadcasted_iota(jnp.int32, (rows, rows), 0)
    c = lax.broadcasted_iota(jnp.int32, (rows, rows), 1)
    same = (r >> CHUNK_SHIFT) == (c >> CHUNK_SHIFT)
    tri = (c >= r) if reverse else (c <= r)
    return jnp.where(same, jnp.where(tri, 1.0, 0.0), 0.0)


def _split3(x):
    hi = x.astype(BF16)
    r1 = x - hi.astype(F32)
    mid = r1.astype(BF16)
    lo = (r1 - mid.astype(F32)).astype(BF16)
    return hi, mid, lo


def _block_scan(g_all, head_qkv, st_scr, o_ref, *, heads, dv, reverse, batched):
    rows, wall = g_all.shape
    width = wall // heads
    nchunk = rows // CHUNK
    mask = _chunk_mask(rows, reverse)
    mask_b = mask.astype(BF16)
    keep = mask > 0.5
    keep_c = keep[:CHUNK, :CHUNK]
    iref = CHUNK // 2 if reverse else CHUNK // 2 - 1
    ilast = 0 if reverse else CHUNK - 1
    order = range(nchunk - 1, -1, -1) if reverse else range(nchunk)

    def per_chunk(rows_1, width):
        return jnp.concatenate([jnp.broadcast_to(r, (CHUNK, width)) for r in rows_1], axis=0)

    pieces = jnp.concatenate(_split3(g_all), axis=1)
    sums = jnp.dot(mask_b, pieces, preferred_element_type=F32)
    b_all = sums[:, :wall] + sums[:, wall:2 * wall] + sums[:, 2 * wall:]

    def prepare(h):
        q, k, v = head_qkv(h)
        b = b_all[:, h * width:(h + 1) * width]
        b_ref = [b[c * CHUNK + iref:c * CHUNK + iref + 1] for c in range(nchunk)]
        b_last = [b[c * CHUNK + ilast:c * CHUNK + ilast + 1] for c in range(nchunk)]
        e1 = jnp.exp(b - per_chunk(b_ref, width))
        r1 = 1.0 / e1
        qe = q * e1
        ke = k * r1
        qd = (qe * per_chunk([jnp.exp(r) for r in b_ref], width)).astype(BF16)
        kd = (ke * per_chunk([jnp.exp(l - r) for l, r in zip(b_last, b_ref)], width)).astype(BF16)
        decays = [jnp.exp(l) for l in b_last]
        return qe.astype(BF16), ke.astype(BF16), qd, kd, v.astype(BF16), decays

    nxt = prepare(0)
    for h in range(heads):
        qe, ke, qd, kd, vb, decays = nxt
        if h + 1 < heads:
            nxt = prepare(h + 1)
        if batched:
            sc = lax.dot_general(qe, ke, _NT, preferred_element_type=F32)
            sc = jnp.where(keep, sc, 0.0).astype(BF16)
            o = jnp.dot(sc, vb, preferred_element_type=F32)
        st = st_scr[h]
        parts = [None] * nchunk
        for ci in order:
            rs = slice(ci * CHUNK, (ci + 1) * CHUNK)
            if batched:
                o_c = o[rs]
            else:
                sc = lax.dot_general(qe[rs], ke[rs], _NT, preferred_element_type=F32)
                sc = jnp.where(keep_c, sc, 0.0).astype(BF16)
                o_c = jnp.dot(sc, vb[rs], preferred_element_type=F32)
            parts[ci] = o_c + lax.dot_general(qd[rs], st.astype(BF16), _NT,
                                              preferred_element_type=F32)
            st = st * decays[ci] + lax.dot_general(vb[rs], kd[rs], _TN, preferred_element_type=F32)
        if o_ref is not None:
            o_ref[:, h * dv:(h + 1) * dv] = jnp.concatenate(parts, axis=0).astype(o_ref.dtype)
        st_scr[h] = st


def _gla_kernel(q_ref, k_ref, v_ref, a_ref, wa_ref, ba_ref, o_ref, st_scr):
    t = pl.program_id(1)
    d = pl.program_id(2)
    dk = q_ref.shape[1] // GLA_HEADS
    dv = v_ref.shape[1] // GLA_HEADS

    @pl.when((t == 0) & (d == 0))
    def _():
        st_scr[...] = jnp.zeros(st_scr.shape, F32)

    def head_qkv(h):
        ks = slice(h * dk, (h + 1) * dk)
        q = q_ref[:, ks].astype(F32) * (dk ** -0.5)
        return q, k_ref[:, ks].astype(F32), v_ref[:, h * dv:(h + 1) * dv]

    for dd in range(2):
        @pl.when(d == dd)
        def _(dd=dd):
            z = jnp.dot(a_ref[...], wa_ref[dd], preferred_element_type=F32) + ba_ref[dd]
            g_all = _log_sigmoid(z) * (1.0 / GLA_TAU)
            _block_scan(g_all, head_qkv, st_scr.at[dd], o_ref,
                        heads=GLA_HEADS, dv=dv, reverse=dd == 1, batched=True)


def _gla(u, wa, ba, *, batch, seq, n_ctx, cols, dv_total):
    ntok = u.shape[0]
    q0, k0, v0, a0 = cols
    dkt = wa.shape[2]
    dvt = dv_total
    nblk = seq // ROW_TILE
    ctx_blk0 = batch * nblk
    assert n_ctx == ROW_TILE

    def blk(b, t, d):
        lat = b * nblk + jnp.where(d == 0, t - 1, nblk - t)
        return jnp.where(t == 0, ctx_blk0 + b, lat)

    in_specs = [pl.BlockSpec((ROW_TILE, dkt), lambda b, t, d: (blk(b, t, d), q0 * LANE // dkt)),
                pl.BlockSpec((ROW_TILE, dkt), lambda b, t, d: (blk(b, t, d), k0 * LANE // dkt)),
                pl.BlockSpec((ROW_TILE, dvt), lambda b, t, d: (blk(b, t, d), v0 * LANE // dvt)),
                pl.BlockSpec((ROW_TILE, LANE), lambda b, t, d: (blk(b, t, d), a0)),
                pl.BlockSpec(wa.shape, lambda b, t, d: (0, 0, 0)),
                pl.BlockSpec(ba.shape, lambda b, t, d: (0, 0, 0))]
    return pl.pallas_call(
        _gla_kernel,
        out_shape=jax.ShapeDtypeStruct((2, ntok, dvt), BF16),
        grid=(batch, nblk + 1, 2),
        in_specs=in_specs,
        out_specs=pl.BlockSpec((None, ROW_TILE, dvt), lambda b, t, d: (d, blk(b, t, d), 0)),
        scratch_shapes=[pltpu.VMEM((2, GLA_HEADS, dvt // GLA_HEADS, dkt // GLA_HEADS), F32)],
        compiler_params=_params("parallel", "arbitrary", "arbitrary"),
        name="gla",
    )(u, u, u, u, wa, ba)


def _hgrn_kernel(*refs, emit_o):
    q_ref, z_ref, v_ref, lb_ref, ub_ref, s0_ref = refs[:6]
    if emit_o:
        o_ref, st_ref, st_scr = refs[6:]
    else:
        st_ref, st_scr = refs[6:]
        o_ref = None
    t = pl.program_id(1)
    d = pl.program_id(2)
    dh = q_ref.shape[1] // HGRN_HEADS

    @pl.when((t == 0) & (d == 0))
    def _():
        st_scr[...] = s0_ref[:, 0]

    lb = lb_ref[...]
    ub = ub_ref[...]

    for dd in range(2):
        @pl.when(d == dd)
        def _(dd=dd):
            z = z_ref[...].astype(F32)
            e = jnp.exp(-jnp.abs(z))
            r = 1.0 / (1.0 + e)
            er = e * r
            pos = z >= 0.0
            log_f = jnp.log(lb + ub * jnp.where(pos, r, er))
            k_all = ub * jnp.where(pos, er, r)

            def head_qkv(h):
                hs = slice(h * dh, (h + 1) * dh)
                return q_ref[:, hs].astype(F32), k_all[:, hs], v_ref[:, hs]

            _block_scan(log_f, head_qkv, st_scr.at[dd], o_ref,
                        heads=HGRN_HEADS, dv=dh, reverse=dd == 1, batched=True)

    @pl.when((t == pl.num_programs(1) - 1) & (d == 1))
    def _():
        st_ref[:, 0] = st_scr[...]


def _hgrn_segment(u, s0, lb, ub, *, batch, seg_len, emit_o):
    ntok = u.shape[0]
    width = lb.shape[1]
    nblk = seg_len // ROW_TILE

    def blk(b, t, d):
        return b * nblk + jnp.where(d == 0, t, nblk - 1 - t)

    st_spec = pl.BlockSpec((2, 1) + s0.shape[2:], lambda b, t, d: (0, b, 0, 0, 0))
    in_specs = [pl.BlockSpec((ROW_TILE, width), lambda b, t, d: (blk(b, t, d), 0)),
                pl.BlockSpec((ROW_TILE, width), lambda b, t, d: (blk(b, t, d), 1 + d)),
                pl.BlockSpec((ROW_TILE, width), lambda b, t, d: (blk(b, t, d), 3)),
                pl.BlockSpec((1, width), lambda b, t, d: (0, 0)),
                pl.BlockSpec((1, width), lambda b, t, d: (0, 0)),
                st_spec]
    st_shape = jax.ShapeDtypeStruct(s0.shape, F32)
    if emit_o:
        out_shape = (jax.ShapeDtypeStruct((2, ntok, width), F32), st_shape)
        out_specs = (pl.BlockSpec((None, ROW_TILE, width), lambda b, t, d: (d, blk(b, t, d), 0)),
                     st_spec)
    else:
        out_shape = (st_shape,)
        out_specs = (st_spec,)
    return pl.pallas_call(
        functools.partial(_hgrn_kernel, emit_o=emit_o),
        out_shape=out_shape,
        grid=(batch, nblk, 2),
        in_specs=in_specs,
        out_specs=out_specs,
        scratch_shapes=[pltpu.VMEM((2,) + s0.shape[2:], F32)],
        compiler_params=_params("parallel", "arbitrary", "arbitrary"),
        name="hgrn_segment",
    )(u, u, u, lb, ub, s0)


def _head_norm_gate(o, gate, heads):
    dh = o.shape[1] // heads
    parts = []
    for h in range(heads):
        oh = o[:, h * dh:(h + 1) * dh]
        ms = jnp.mean(oh * oh, axis=-1, keepdims=True)
        parts.append((oh * lax.rsqrt(ms + EPS) * _silu(gate[:, h * dh:(h + 1) * dh])).astype(BF16))
    return jnp.concatenate(parts, axis=1)


def _load_rows(ref, r0, nrows):
    s_n = CHUNKS_PER_ROW
    return jnp.concatenate([ref[pl.ds(r0 * s_n + s, nrows, stride=s_n), :] for s in range(s_n)],
                           axis=1)


def _store_rows(ref, r0, val):
    s_n = CHUNKS_PER_ROW
    for s in range(s_n):
        ref[pl.ds(r0 * s_n + s, val.shape[0], stride=s_n), :] = val[:, s * LANE:(s + 1) * LANE]


def _out_proj_tail(o, x, tab_ref, pg_ref, ng_ref, rt_ref, xo_ref, h_ref, lg_ref):
    xn = _gated_residual(x, _rms(o, pg_ref[...]), tab_ref, 2)
    xo_ref[...] = xn
    h2 = _modulate(_rms(xn, ng_ref[...]), tab_ref, 3, 4)
    _store_rows(h_ref, 0, h2)
    lg_ref[...] = lax.dot_general(rt_ref[...], h2, _NT, precision=lax.Precision.HIGHEST,
                                  preferred_element_type=F32)


def _out_proj_ab_kernel(yl_ref, yc_ref, pt_ref, of_ref, ob_ref, gate_ref, xc_ref, w_ref, x_ref,
                        tab_ref, pg_ref, ng_ref, rt_ref, xo_ref, h_ref, lg_ref, *, lat_tiles):
    o_att = of_ref[...].astype(F32) + ob_ref[...].astype(F32)
    yb = _head_norm_gate(o_att, gate_ref[...].astype(F32), GLA_HEADS)
    is_lat = pl.program_id(0) < lat_tiles
    ya = jnp.where(is_lat, yl_ref[...], yc_ref[...])
    x = jnp.where(is_lat, x_ref[...], xc_ref[...])
    ya = jnp.concatenate(
        [jnp.dot(pt_ref[...], ya[s:s + ROW_TILE], preferred_element_type=F32).astype(BF16)
         for s in range(0, ya.shape[0], ROW_TILE)], axis=0)
    da = ya.shape[1]
    o = jnp.dot(ya, w_ref[:da], preferred_element_type=F32)
    o = o + jnp.dot(yb, w_ref[da:], preferred_element_type=F32)
    _out_proj_tail(o, x, tab_ref, pg_ref, ng_ref, rt_ref, xo_ref, h_ref, lg_ref)


def _out_proj_c_kernel(of_ref, ob_ref, gate_ref, p_ref, w_ref, x_ref, tab_ref, pg_ref, ng_ref,
                       rt_ref, xo_ref, h_ref, lg_ref):
    tm = x_ref.shape[0]
    width = of_ref.shape[-1]
    o_att = of_ref[...].astype(F32) + ob_ref[...].astype(F32)
    y = _head_norm_gate(o_att.reshape(tm, width), gate_ref[...].astype(F32).reshape(tm, width),
                        HGRN_HEADS)
    y = jnp.dot(p_ref[...], y, preferred_element_type=F32).astype(BF16)
    o = jnp.dot(y, w_ref[...], preferred_element_type=F32)
    _out_proj_tail(o, x_ref[...], tab_ref, pg_ref, ng_ref, rt_ref, xo_ref, h_ref, lg_ref)


def _out_proj(kernel, lead, lead_specs, w, x, tab, post_gain, next_gain, router_t, tm, ntiles,
              x_tiles=None):
    d = x.shape[1]
    ntok = ntiles * tm
    nsub = tm // ROW_TILE
    ne = router_t.shape[0]
    last_x = (x_tiles or ntiles) - 1
    in_specs = list(lead_specs) + [
        pl.BlockSpec(w.shape, lambda i: (0, 0), pipeline_mode=pl.Buffered(1)),
        pl.BlockSpec((tm, d), lambda i: (jnp.minimum(i, last_x), 0)),
        pl.BlockSpec((nsub, 8, d), lambda i: (i, 0, 0)),
        pl.BlockSpec((1, d), lambda i: (0, 0)),
        pl.BlockSpec((1, d), lambda i: (0, 0)),
        pl.BlockSpec((ne, d), lambda i: (0, 0))]
    return pl.pallas_call(
        kernel,
        out_shape=(jax.ShapeDtypeStruct((ntok, d), F32),
                   jax.ShapeDtypeStruct((ntok * CHUNKS_PER_ROW, LANE), F32),
                   jax.ShapeDtypeStruct((ne, ntok), F32)),
        grid=(ntiles,),
        in_specs=in_specs,
        out_specs=(pl.BlockSpec((tm, d), lambda i: (i, 0)),
                   pl.BlockSpec((tm * CHUNKS_PER_ROW, LANE), lambda i: (i, 0)),
                   pl.BlockSpec((ne, tm), lambda i: (0, i))),
        compiler_params=_params("parallel"),
        name="out_proj_norm_residual",
    )(*lead, w, x, tab, post_gain, next_gain, router_t)


def _route_kernel(lg_ref, bias_ref, ids_ref, w_ref):
    tm = lg_ref.shape[1]
    per_group = N_EXPERTS // N_GROUPS
    shape3 = (N_GROUPS, per_group, tm)
    aff = _sigmoid(lg_ref[...]).reshape(shape3)
    biased = aff + bias_ref[...].reshape(shape3)
    neg = -jnp.inf
    sub = lax.broadcasted_iota(jnp.int32, shape3, 1)
    grp = lax.broadcasted_iota(jnp.int32, shape3, 0)
    m1 = jnp.max(biased, axis=1, keepdims=True)
    i1 = jnp.min(jnp.where(biased == m1, sub, per_group), axis=1, keepdims=True)
    m2 = jnp.max(jnp.where(sub == i1, neg, biased), axis=1, keepdims=True)
    score = m1 + m2
    gidx = lax.broadcasted_iota(jnp.int32, score.shape, 0)
    keep = jnp.zeros(score.shape, F32)
    for _ in range(TOPK_GROUPS):
        m = jnp.max(score, axis=0, keepdims=True)
        im = jnp.min(jnp.where(score == m, gidx, N_GROUPS), axis=0, keepdims=True)
        sel = gidx == im
        keep = jnp.where(sel, 1.0, keep)
        score = jnp.where(sel, neg, score)
    work = jnp.where(jnp.broadcast_to(keep, shape3) > 0.0, biased, neg)
    eidx = grp * per_group + sub
    ids, gates = [], []
    for _ in range(TOP_K):
        m = jnp.max(jnp.max(work, axis=0, keepdims=True), axis=1, keepdims=True)
        cand = jnp.where(work == m, eidx, N_EXPERTS)
        im = jnp.min(jnp.min(cand, axis=0, keepdims=True), axis=1, keepdims=True)
        sel = eidx == im
        ids.append(im)
        gates.append(jnp.sum(jnp.sum(jnp.where(sel, aff, 0.0), axis=0, keepdims=True),
                             axis=1, keepdims=True))
        work = jnp.where(sel, neg, work)
    gate = jnp.concatenate(gates, axis=1)
    den = jnp.sum(gate, axis=1, keepdims=True)
    w_ref[...] = (ROUTED_SCALE * gate / den).reshape(TOP_K, tm)
    ids_ref[...] = jnp.concatenate(ids, axis=1).reshape(TOP_K, tm)


def _route(logits_t, bias, tm):
    ne, ntok = logits_t.shape
    bias_b = jnp.broadcast_to(bias.astype(F32)[:, None], (ne, tm))
    return pl.pallas_call(
        _route_kernel,
        out_shape=(jax.ShapeDtypeStruct((TOP_K, ntok), jnp.int32),
                   jax.ShapeDtypeStruct((TOP_K, ntok), F32)),
        grid=(ntok // tm,),
        in_specs=[pl.BlockSpec((ne, tm), lambda i: (0, i)),
                  pl.BlockSpec((ne, tm), lambda i: (0, 0))],
        out_specs=(pl.BlockSpec((TOP_K, tm), lambda i: (0, i)),
                   pl.BlockSpec((TOP_K, tm), lambda i: (0, i))),
        compiler_params=_params("parallel"),
        name="moe_route",
    )(logits_t, bias_b)


LIST_PAD = 1024


def _dispatch_plan(ids, w, t0, tile, ntiles):
    n = ntiles * tile
    nassign = tile * TOP_K
    assert tile & (tile - 1) == 0 and N_EXPERTS * nassign < 2 ** 31

    def per_tile(a):
        return a[:, t0:t0 + n].reshape(TOP_K, ntiles, tile).transpose(1, 0, 2).reshape(ntiles, nassign)

    e = per_tile(ids)
    g = per_tile(w)
    keys = e * nassign + jnp.arange(nassign, dtype=jnp.int32)
    order = jnp.sort(keys, axis=1) & (nassign - 1)
    rows = (order & (tile - 1)) * CHUNKS_PER_ROW
    wts = jnp.take_along_axis(g, order, axis=1)
    experts = jnp.arange(N_EXPERTS, dtype=jnp.int32)
    counts = jnp.sum((e[:, :, None] == experts).astype(jnp.int32), axis=1)
    starts = jnp.concatenate([jnp.zeros((ntiles, 1), jnp.int32), jnp.cumsum(counts, axis=1)], axis=1)
    rows = jnp.pad(rows, ((0, 0), (0, LIST_PAD)))
    wts = jnp.pad(wts, ((0, 0), (0, LIST_PAD)))
    starts = jnp.pad(starts, ((0, 0), (0, LANE - N_EXPERTS - 1)))
    return rows, wts, starts


def _moe_kernel(rows_hbm, wts_hbm, starts_hbm, h_ref, wg_ref, wu_ref, wd_ref, sg_ref, su_ref,
                sd_ref, o_ref, rows_s, wts_s, starts_s, sem, gbuf, ybuf, *, epb):
    i = pl.program_id(0)
    j = pl.program_id(1)
    s_n = CHUNKS_PER_ROW
    blk = MOE_BLOCK
    tile = h_ref.shape[0] // s_n

    def swiglu(x, w_gate, w_up, w_down):
        g = jnp.dot(x, w_gate, preferred_element_type=F32)
        u = jnp.dot(x, w_up, preferred_element_type=F32)
        return jnp.dot((_silu(g) * u).astype(BF16), w_down, preferred_element_type=F32)

    @pl.when(j == 0)
    def _():
        copies = [pltpu.make_async_copy(rows_hbm.at[i], rows_s, sem.at[0]),
                  pltpu.make_async_copy(wts_hbm.at[i], wts_s, sem.at[1]),
                  pltpu.make_async_copy(starts_hbm.at[i], starts_s, sem.at[2])]
        for c in copies:
            c.start()
        for r0 in range(0, tile, ROW_TILE):
            x = _load_rows(h_ref, r0, ROW_TILE).astype(BF16)
            _store_rows(o_ref, r0, swiglu(x, sg_ref[...], su_ref[...], sd_ref[...]))
        for c in copies:
            c.wait()

    for e in range(epb):
        base = starts_s[j * epb + e]
        count = starts_s[j * epb + e + 1] - base
        w_gate = wg_ref[e].astype(BF16)
        w_up = wu_ref[e].astype(BF16)
        w_down = wd_ref[e].astype(BF16)

        def block(bi, carry, base=base, count=count, w_gate=w_gate, w_up=w_up, w_down=w_down):
            p0 = base + bi * blk
            left = count - bi * blk
            offs = [pl.multiple_of(rows_s[p0 + r], s_n) for r in range(blk)]
            for r in range(blk):
                gbuf[pl.ds(r * s_n, s_n), :] = h_ref[pl.ds(offs[r], s_n), :]
            x = _load_rows(gbuf, 0, blk).astype(BF16)
            y = swiglu(x, w_gate, w_up, w_down)
            y = jnp.where(lax.broadcasted_iota(jnp.int32, y.shape, 0) < left, y, 0.0)
            _store_rows(ybuf, 0, y)
            group = 2 * SUBLANE
            for r0 in range(0, blk, group):
                vals = [o_ref[pl.ds(offs[r], s_n), :] + wts_s[p0 + r] * ybuf[pl.ds(r * s_n, s_n), :]
                        for r in range(r0, r0 + group)]
                for r in reversed(range(r0, r0 + group)):
                    o_ref[pl.ds(offs[r], s_n), :] = vals[r - r0]
            return carry

        nblk = lax.shift_right_logical(count + (blk - 1), blk.bit_length() - 1)
        lax.fori_loop(0, nblk, block, 0)


def _moe(h, plan, layer, wg, wu, wd, sg, su, sd, tile0, ntiles, tile, epb):
    rows, wts, starts = plan
    s_n = CHUNKS_PER_ROW
    _, ne, d, ff = wg.shape
    once = pl.Buffered(1)
    hbm = pl.BlockSpec(memory_space=pl.ANY)
    return pl.pallas_call(
        functools.partial(_moe_kernel, epb=epb),
        out_shape=jax.ShapeDtypeStruct((ntiles * tile * s_n, LANE), F32),
        grid=(ntiles, ne // epb),
        in_specs=[hbm, hbm, hbm,
                  pl.BlockSpec((tile * s_n, LANE), lambda i, j: (tile0 + i, 0), pipeline_mode=once),
                  pl.BlockSpec((None, epb, d, ff), lambda i, j: (layer, j, 0, 0)),
                  pl.BlockSpec((None, epb, d, ff), lambda i, j: (layer, j, 0, 0)),
                  pl.BlockSpec((None, epb, ff, d), lambda i, j: (layer, j, 0, 0)),
                  pl.BlockSpec(sg.shape, lambda i, j: (0, 0), pipeline_mode=once),
                  pl.BlockSpec(su.shape, lambda i, j: (0, 0), pipeline_mode=once),
                  pl.BlockSpec(sd.shape, lambda i, j: (0, 0), pipeline_mode=once)],
        out_specs=pl.BlockSpec((tile * s_n, LANE), lambda i, j: (i, 0)),
        scratch_shapes=[pltpu.SMEM((rows.shape[1],), jnp.int32),
                        pltpu.SMEM((wts.shape[1],), F32),
                        pltpu.SMEM((starts.shape[1],), jnp.int32),
                        pltpu.SemaphoreType.DMA((3,)),
                        pltpu.VMEM((MOE_BLOCK * s_n, LANE), F32),
                        pltpu.VMEM((MOE_BLOCK * s_n, LANE), F32)],
        compiler_params=_params("parallel", "arbitrary", vmem=VMEM_LIMIT_MOE),
        name="moe_experts",
    )(rows, wts, starts, h, wg, wu, wd, sg, su, sd)


def _moe_finish_kernel(f_ref, x_ref, tab_ref, pg_ref, o_ref):
    f = _load_rows(f_ref, 0, x_ref.shape[0])
    o_ref[...] = _gated_residual(x_ref[...], _rms(f, pg_ref[...]), tab_ref, 5)


def _moe_finish(f, x, tab, post_gain, tm, x_tile0, tab0):
    d = x.shape[1]
    s_n = CHUNKS_PER_ROW
    nsub = tm // ROW_TILE
    ntiles = f.shape[0] // (tm * s_n)
    return pl.pallas_call(
        _moe_finish_kernel,
        out_shape=jax.ShapeDtypeStruct((ntiles * tm, d), F32),
        grid=(ntiles,),
        in_specs=[pl.BlockSpec((tm * s_n, LANE), lambda i: (i, 0)),
                  pl.BlockSpec((tm, d), lambda i: (x_tile0 + i, 0)),
                  pl.BlockSpec((nsub, 8, d), lambda i: (tab0 + i, 0, 0)),
                  pl.BlockSpec((1, d), lambda i: (0, 0))],
        out_specs=pl.BlockSpec((tm, d), lambda i: (i, 0)),
        compiler_params=_params("parallel"),
        name="moe_norm_residual",
    )(f, x, tab, post_gain)


def _grid_permutation(band):
    tm = band * GRID_W
    dst = jnp.arange(tm)
    src = (dst % band) * GRID_W + dst // band
    return (src[:, None] == jnp.arange(tm)[None, :]).astype(BF16)


def kernel(x, c, ctx, c_ctx, mod_w, mod_b, pre_gain, post_gain, ab_w_in, ab_conv_w, ab_conv_b, ab_lru_wa, ab_lru_ba, ab_lru_wi, ab_lru_bi, ab_lru_lambda, ab_gla_wa2, ab_gla_ba, ab_w_out, c_w_in, hgrn_lb_logits, c_w_out, moe_router, moe_bias, moe_w_gate, moe_w_up, moe_w_down, moe_ws_gate, moe_ws_up, moe_ws_down):
    batch, seq, d = x.shape
    n_ctx = ctx.shape[1]
    depth = mod_w.shape[0]
    band = SUBLANE
    tm_grid = band * GRID_W
    rows_img = seq // GRID_W
    assert n_ctx == ROW_TILE and seq % (2 * ROW_TILE) == 0 and d % LANE == 0
    assert depth == 2 and batch + 1 <= 8 and rows_img % band == 0 and tm_grid % ROW_TILE == 0
    n_lat = batch * seq
    n_ctx_tok = batch * n_ctx
    ntok = n_lat + n_ctx_tok

    x_lat = x.reshape(n_lat, d)
    x_ctx = ctx.reshape(n_ctx_tok, d)

    cvec = jnp.concatenate([c, c_ctx[None, :], jnp.zeros((8 - batch - 1, d), F32)], axis=0)
    mods = _modulation(cvec, mod_w, mod_b).reshape(depth, 8, N_MOD, d)
    tile_row = jnp.concatenate([jnp.repeat(jnp.arange(batch), seq // ROW_TILE),
                                jnp.full((batch,), batch)]).astype(jnp.int32)
    tabs = jnp.pad(mods[:, tile_row], ((0, 0), (0, 0), (0, 8 - N_MOD), (0, 0)))

    def experts(layer, h2, routed, t0, ntiles, tile):
        plan = _dispatch_plan(*routed, t0, tile, ntiles)
        return _moe(h2, plan, layer, moe_w_gate, moe_w_up, moe_w_down,
                    moe_ws_gate[layer].astype(BF16), moe_ws_up[layer].astype(BF16),
                    moe_ws_down[layer].astype(BF16), t0 // tile, ntiles, tile, epb)

    tm_proj, tm_out, tm_route, tm_moe, epb = 512, 512, 512, 2048, 4
    assert n_lat % tm_moe == 0 and n_lat % n_ctx_tok == 0

    ab_cols = ab_w_in.shape[2]
    gla_dk = ab_gla_wa2.shape[3]
    lru_w = d
    gla_dv = (ab_cols - 2 * lru_w - 2 * gla_dk - 2 * GLA_RANK) // 2
    n_main = ab_cols // LANE * LANE
    tn0 = n_main // 4
    w_main = ab_w_in[0][:, :n_main].astype(BF16)
    w_tail = jnp.pad(ab_w_in[0][:, n_main:], ((0, 0), (0, LANE - (ab_cols - n_main)))).astype(BF16)
    u0 = _in_proj(x_lat, x_ctx, tabs[0], pre_gain[0, 0][None, :], w_main, w_tail, tm_proj, tn0)

    cw = jnp.pad(ab_conv_w[0], ((0, 8 - CONV_W), (0, 0)))
    cb = ab_conv_b[0][None, :]
    wg = jnp.concatenate([ab_lru_wa[0], ab_lru_wi[0]], axis=-1).astype(BF16)
    bg = jnp.concatenate([ab_lru_ba[0].reshape(2, LRU_HEADS, 1, LANE),
                          ab_lru_bi[0].reshape(2, LRU_HEADS, 1, LANE)], axis=-1)
    ya_c, ya_l = _lru(u0, (cw, cb, wg, bg, ab_lru_lambda[0]), batch=batch, seq=seq, n_ctx=n_ctx,
                      g_col0=0, x_col0=lru_w // LANE)

    q0 = 2 * lru_w // LANE
    k0 = q0 + gla_dk // LANE
    v0 = k0 + gla_dk // LANE
    gate0 = v0 + gla_dv // LANE
    a0 = gate0 + gla_dv // LANE
    wa = jnp.stack([jnp.zeros((LANE, gla_dk), F32).at[dr * GLA_RANK:(dr + 1) * GLA_RANK]
                    .set(ab_gla_wa2[0, dr]) for dr in range(2)]).astype(BF16)
    ba = ab_gla_ba[0][:, None, :]
    o_gla = _gla(u0, wa, ba, batch=batch, seq=seq, n_ctx=n_ctx, cols=(q0, k0, v0, a0),
                 dv_total=gla_dv)

    lat_tiles = n_lat // tm_out
    ctx_tiles = n_ctx_tok // tm_out
    lead_specs = [pl.BlockSpec((tm_out, lru_w), lambda i: (jnp.minimum(i, lat_tiles - 1), 0)),
                  pl.BlockSpec((tm_out, lru_w),
                               lambda i: (jnp.clip(i - lat_tiles, 0, ctx_tiles - 1), 0)),
                  pl.BlockSpec((ROW_TILE, ROW_TILE), lambda i: (0, 0)),
                  pl.BlockSpec((None, tm_out, gla_dv), lambda i: (0, i, 0)),
                  pl.BlockSpec((None, tm_out, gla_dv), lambda i: (1, i, 0)),
                  pl.BlockSpec((tm_out, gla_dv), lambda i: (i, gate0 * LANE // gla_dv)),
                  pl.BlockSpec((tm_out, d), lambda i: (jnp.clip(i - lat_tiles, 0, ctx_tiles - 1), 0))]
    xs, h2, lg = _out_proj(functools.partial(_out_proj_ab_kernel, lat_tiles=lat_tiles),
                           (ya_l, ya_c, _time_permutation().T.astype(BF16), o_gla, o_gla, u0, x_ctx),
                           lead_specs,
                           ab_w_out[0].astype(BF16), x_lat, tabs[0], post_gain[0, 0][None, :],
                           pre_gain[0, 1][None, :], moe_router[0].T, tm_out, ntok // tm_out,
                           x_tiles=lat_tiles)
    cwt = _route(lg, moe_bias[0], tm_route)
    f_ctx = experts(0, h2, cwt, n_lat, 1, n_ctx_tok)
    f_lat = experts(0, h2, cwt, 0, n_lat // tm_moe, tm_moe)
    pg0 = post_gain[0, 1][None, :]

    hd = c_w_out.shape[1]
    tn1 = 1280
    w_in1 = c_w_in[0].astype(BF16)
    c_cols = w_in1.shape[1]
    nsec = c_cols // hd
    perm = _grid_permutation(band)
    gain1 = pre_gain[1, 0][None, :]
    u1c = _in_proj_ctx(f_ctx, xs, tabs[0], pg0, tabs[1], gain1, w_in1, tm_proj, tn1,
                       n_lat // tm_proj)
    u1l, xs = _in_proj_grid(f_lat, xs, tabs[0], pg0, tabs[1], gain1, perm, w_in1, tn1,
                            n_lat // tm_grid, band, rows_img // band)
    lb_sm = jax.nn.softmax(hgrn_lb_logits.astype(F32), axis=0)
    lb_cum = jnp.cumsum(lb_sm, axis=0)
    lb = (lb_cum[1] - lb_cum[0])[None, :]
    ub = 1.0 - lb
    dh = hd // HGRN_HEADS
    s_zero = jnp.zeros((2, batch, HGRN_HEADS, dh, dh), F32)
    u1l_flat = u1l.reshape(n_lat, c_cols)
    (s_c,) = _hgrn_segment(u1c, s_zero, lb, ub, batch=batch, seg_len=n_ctx, emit_o=False)
    o_hgrn, _ = _hgrn_segment(u1l_flat, s_c, lb, ub, batch=batch, seg_len=seq, emit_o=True)
    o_hgrn = o_hgrn.reshape(2, n_lat // rows_img, rows_img // band, band, hd)

    bpi = rows_img // band
    grid_blk = (GRID_W, None, band, hd)
    lead_specs = [pl.BlockSpec((None,) + grid_blk, lambda i: (0, i // bpi, i % bpi, 0, 0)),
                  pl.BlockSpec((None,) + grid_blk, lambda i: (1, i // bpi, i % bpi, 0, 0)),
                  pl.BlockSpec(grid_blk, lambda i: (i // bpi, i % bpi, 0, nsec - 1)),
                  pl.BlockSpec((tm_grid, tm_grid), lambda i: (0, 0))]
    xs, h2, lg = _out_proj(_out_proj_c_kernel, (o_hgrn, o_hgrn, u1l, perm.T), lead_specs,
                           c_w_out[0].astype(BF16), xs, tabs[1], post_gain[1, 0][None, :],
                           pre_gain[1, 1][None, :], moe_router[1].T, tm_grid, n_lat // tm_grid)
    f_lat = experts(1, h2, _route(lg, moe_bias[1], tm_route), 0, n_lat // tm_moe, tm_moe)
    out = _moe_finish(f_lat, xs, tabs[1], post_gain[1, 1][None, :], tm_out, 0, 0)
    return out.reshape(batch, seq, d)
```

```python
import functools

import jax
import jax.numpy as jnp
from jax import lax
from jax.experimental import pallas as pl
from jax.experimental.pallas import tpu as pltpu

F32 = jnp.float32
BF16 = jnp.bfloat16

EPS = 1e-6
N_MOD = 6
ROW_TILE = 256
CHUNK = 64
CHUNK_SHIFT = 6
GRID_W = 64
LRU_HEADS = 8
LRU_C = 8.0
CONV_W = 4
GLA_HEADS = 4
GLA_RANK = 16
GLA_TAU = 16.0
HGRN_HEADS = 8
N_EXPERTS = 64
TOP_K = 8
N_GROUPS = 8
TOPK_GROUPS = 4
ROUTED_SCALE = 2.5
LANE = 128
SUBLANE = 8
BF16_ROWS = 16
CHUNKS_PER_ROW = 8
MOE_BLOCK = 128
VMEM_LIMIT = 48 * 1024 * 1024
VMEM_LIMIT_MOE = 58 * 1024 * 1024

_NT = (((1,), (1,)), ((), ()))
_TN = (((0,), (0,)), ((), ()))


def _params(*sem, vmem=VMEM_LIMIT):
    return pltpu.CompilerParams(dimension_semantics=sem, vmem_limit_bytes=vmem)


def _rms(x, gain):
    ms = jnp.mean(x * x, axis=-1, keepdims=True)
    return x * lax.rsqrt(ms + EPS) * gain


def _sigmoid(x):
    return jax.nn.sigmoid(x)


def _silu(x):
    return x * _sigmoid(x)


def _log_sigmoid(x):
    return jnp.minimum(x, 0.0) - jnp.log1p(jnp.exp(-jnp.abs(x)))


def _softplus(x):
    return jnp.maximum(x, 0.0) + jnp.log1p(jnp.exp(-jnp.abs(x)))


def _modulate(xn, tab_ref, shift_row, scale_row):
    parts = []
    for s in range(tab_ref.shape[0]):
        t = tab_ref[s]
        rows = xn[s * ROW_TILE:(s + 1) * ROW_TILE]
        parts.append(rows * (1.0 + t[scale_row:scale_row + 1]) + t[shift_row:shift_row + 1])
    return parts[0] if len(parts) == 1 else jnp.concatenate(parts, axis=0)


def _gated_residual(x, branch, tab_ref, gate_row):
    parts = []
    for s in range(tab_ref.shape[0]):
        sl = slice(s * ROW_TILE, (s + 1) * ROW_TILE)
        parts.append(x[sl] + tab_ref[s][gate_row:gate_row + 1] * branch[sl])
    return parts[0] if len(parts) == 1 else jnp.concatenate(parts, axis=0)


def _mod_kernel(c_ref, w_ref, b_ref, o_ref):
    s = _silu(c_ref[...])
    o_ref[0] = jnp.dot(s, w_ref[0], precision=lax.Precision.HIGHEST,
                       preferred_element_type=F32) + b_ref[0]


def _modulation(cvec, mod_w, mod_b):
    depth, d, n = mod_w.shape
    tn = 1024
    return pl.pallas_call(
        _mod_kernel,
        out_shape=jax.ShapeDtypeStruct((depth, 8, n), F32),
        grid=(depth, n // tn),
        in_specs=[pl.BlockSpec((8, d), lambda l, j: (0, 0)),
                  pl.BlockSpec((1, d, tn), lambda l, j: (l, 0, j)),
                  pl.BlockSpec((1, 1, tn), lambda l, j: (l, 0, j))],
        out_specs=pl.BlockSpec((1, 8, tn), lambda l, j: (l, 0, j)),
        compiler_params=_params("parallel", "parallel"),
        name="adaln_modulation",
    )(cvec, mod_w, mod_b.reshape(depth, 1, n))


def _in_proj_kernel(xl_ref, xc_ref, tab_ref, gain_ref, w_ref, wt_ref, o_ref, *, tn, lat_tiles):
    x = jnp.where(pl.program_id(0) < lat_tiles, xl_ref[...], xc_ref[...])
    h = _modulate(_rms(x, gain_ref[...]), tab_ref, 0, 1).astype(BF16)
    n_main = w_ref.shape[1]
    for j in range(n_main // tn):
        cs = slice(j * tn, (j + 1) * tn)
        o_ref[:, cs] = jnp.dot(h, w_ref[:, cs], preferred_element_type=F32).astype(o_ref.dtype)
    o_ref[:, n_main:] = jnp.dot(h, wt_ref[...], preferred_element_type=F32).astype(o_ref.dtype)


def _in_proj(x_lat, x_ctx, tab, gain, w, w_tail, tm, tn):
    d = x_lat.shape[1]
    n = w.shape[1] + w_tail.shape[1]
    nsub = tm // ROW_TILE
    lat_tiles = x_lat.shape[0] // tm
    ctx_tiles = x_ctx.shape[0] // tm
    ntiles = lat_tiles + ctx_tiles
    once = pl.Buffered(1)
    return pl.pallas_call(
        functools.partial(_in_proj_kernel, tn=tn, lat_tiles=lat_tiles),
        out_shape=jax.ShapeDtypeStruct((ntiles * tm, n), BF16),
        grid=(ntiles,),
        in_specs=[pl.BlockSpec((tm, d), lambda i: (jnp.minimum(i, lat_tiles - 1), 0)),
                  pl.BlockSpec((tm, d), lambda i: (jnp.clip(i - lat_tiles, 0, ctx_tiles - 1), 0)),
                  pl.BlockSpec((nsub, 8, d), lambda i: (i, 0, 0)),
                  pl.BlockSpec((1, d), lambda i: (0, 0)),
                  pl.BlockSpec(w.shape, lambda i: (0, 0), pipeline_mode=once),
                  pl.BlockSpec(w_tail.shape, lambda i: (0, 0), pipeline_mode=once)],
        out_specs=pl.BlockSpec((tm, n), lambda i: (i, 0)),
        compiler_params=_params("parallel"),
        name="norm_mod_in_proj",
    )(x_lat, x_ctx, tab, gain, w, w_tail)


def _after_moe(f_ref, x_ref, ptab_ref, pg_ref):
    f = _load_rows(f_ref, 0, x_ref.shape[0])
    return _gated_residual(x_ref[...], _rms(f, pg_ref[...]), ptab_ref, 5)


def _in_proj_ctx_kernel(f_ref, x_ref, ptab_ref, pg_ref, tab_ref, gain_ref, w_ref, o_ref, *, tn):
    xn = _after_moe(f_ref, x_ref, ptab_ref, pg_ref)
    h = _modulate(_rms(xn, gain_ref[...]), tab_ref, 0, 1).astype(BF16)
    for j in range(w_ref.shape[1] // tn):
        cs = slice(j * tn, (j + 1) * tn)
        o_ref[:, cs] = jnp.dot(h, w_ref[:, cs], preferred_element_type=F32).astype(o_ref.dtype)


def _in_proj_ctx(f, x, ptab, post_gain, tab, gain, w, tm, tn, tile0):
    d = x.shape[1]
    n = w.shape[1]
    s_n = CHUNKS_PER_ROW
    nsub = tm // ROW_TILE
    ntiles = f.shape[0] // (tm * s_n)
    tabspec = pl.BlockSpec((nsub, 8, d), lambda i: (tile0 + i, 0, 0))
    vec = pl.BlockSpec((1, d), lambda i: (0, 0))
    return pl.pallas_call(
        functools.partial(_in_proj_ctx_kernel, tn=tn),
        out_shape=jax.ShapeDtypeStruct((ntiles * tm, n), BF16),
        grid=(ntiles,),
        in_specs=[pl.BlockSpec((tm * s_n, LANE), lambda i: (i, 0)),
                  pl.BlockSpec((tm, d), lambda i: (tile0 + i, 0)),
                  tabspec, vec, tabspec, vec,
                  pl.BlockSpec((d, n), lambda i: (0, 0), pipeline_mode=pl.Buffered(1))],
        out_specs=pl.BlockSpec((tm, n), lambda i: (i, 0)),
        compiler_params=_params("parallel"),
        name="moe_close_in_proj",
    )(f, x, ptab, post_gain, tab, gain, w)


def _in_proj_grid_kernel(f_ref, x_ref, ptab_ref, pg_ref, tab_ref, gain_ref, p_ref, w_ref,
                         o_ref, xo_ref, *, tn):
    xn = _after_moe(f_ref, x_ref, ptab_ref, pg_ref)
    xo_ref[...] = xn
    h = _modulate(_rms(xn, gain_ref[...]), tab_ref, 0, 1).astype(BF16)
    hp = jnp.dot(p_ref[...], h, preferred_element_type=F32).astype(BF16)
    for j in range(w_ref.shape[1] // tn):
        cs = slice(j * tn, (j + 1) * tn)
        o = jnp.dot(hp, w_ref[:, cs], preferred_element_type=F32).astype(o_ref.dtype)
        o_ref[:, :, cs] = o.reshape(o_ref.shape[:2] + (tn,))


def _in_proj_grid(f, x, ptab, post_gain, tab, gain, perm, w, tn, ntiles, band, bands_per_image):
    d = x.shape[1]
    n = w.shape[1]
    s_n = CHUNKS_PER_ROW
    tm = band * GRID_W
    nsub = tm // ROW_TILE
    images = ntiles // bands_per_image
    once = pl.Buffered(1)
    tabspec = pl.BlockSpec((nsub, 8, d), lambda i: (i, 0, 0))
    vec = pl.BlockSpec((1, d), lambda i: (0, 0))
    return pl.pallas_call(
        functools.partial(_in_proj_grid_kernel, tn=tn),
        out_shape=(jax.ShapeDtypeStruct((images * GRID_W, bands_per_image, band, n), F32),
                   jax.ShapeDtypeStruct((ntiles * tm, d), F32)),
        grid=(ntiles,),
        in_specs=[pl.BlockSpec((tm * s_n, LANE), lambda i: (i, 0)),
                  pl.BlockSpec((tm, d), lambda i: (i, 0)),
                  tabspec, vec, tabspec, vec,
                  pl.BlockSpec((tm, tm), lambda i: (0, 0), pipeline_mode=once),
                  pl.BlockSpec((d, n), lambda i: (0, 0), pipeline_mode=once)],
        out_specs=(pl.BlockSpec((GRID_W, None, band, n),
                                lambda i: (i // bands_per_image, i % bands_per_image, 0, 0)),
                   pl.BlockSpec((tm, d), lambda i: (i, 0))),
        compiler_params=_params("parallel"),
        name="moe_close_in_proj_grid",
    )(f, x, ptab, post_gain, tab, gain, perm, w)


def _lru_kernel(gc_ref, xc_ref, gl_ref, xl_ref, p_ref, cw_ref, cb_ref, wg_ref, bg_ref,
                lam_ref, yc_ref, yl_ref, xs_scr, hf_scr):
    T = ROW_TILE
    V = T // SUBLANE
    H = BF16_ROWS
    sub = lax.broadcasted_iota(jnp.int32, (SUBLANE, LANE), 0)
    cw = cw_ref[...]
    cb = cb_ref[...]
    sp = _softplus(-lam_ref[...])

    def vreg(a, v):
        return a[v * SUBLANE:(v + 1) * SUBLANE]

    def shift_down(y, first):
        return jnp.where(sub == 0, first, pltpu.roll(y, 1, 0))

    def shift_up(y, last):
        return jnp.where(sub == SUBLANE - 1, last, pltpu.roll(y, SUBLANE - 1, 0))

    def conv_tile(x_ref, t, nt, seg_len):
        t0 = pl.multiple_of(t * T, T)
        xp = jnp.dot(p_ref[...], x_ref[pl.ds(t0, T), :], preferred_element_type=F32)
        p0 = pl.multiple_of(jnp.maximum(t0 - H, 0), H)
        n0 = pl.multiple_of(jnp.minimum(t0 + T, seg_len - H), H)
        prev = x_ref[pl.ds(p0, H), :].astype(F32) * jnp.where(t > 0, 1.0, 0.0)
        nxt = x_ref[pl.ds(n0, H), :].astype(F32) * jnp.where(t < nt - 1, 1.0, 0.0)
        m1, m2, p1 = prev[H - 1:H], prev[H - 2:H - 1], nxt[0:1]
        last_m1 = shift_down(vreg(xp, V - 1), m1)
        xm1 = jnp.concatenate([last_m1, xp[:T - SUBLANE]], axis=0)
        xm2 = jnp.concatenate([shift_down(vreg(xp, V - 2), m2), last_m1, xp[:T - 2 * SUBLANE]], axis=0)
        xp1 = jnp.concatenate([xp[SUBLANE:], shift_up(vreg(xp, 0), p1)], axis=0)
        return cb + xm2 * cw[0:1] + xm1 * cw[1:2] + xp * cw[2:3] + xp1 * cw[3:4]

    def gates(xc, d):
        z = jnp.dot(xc.astype(BF16), wg_ref[d, 0], preferred_element_type=F32) + bg_ref[d, 0]
        r = _sigmoid(z[:, :LANE])
        i = _sigmoid(z[:, LANE:])
        log_a = (-LRU_C) * r * sp[d:d + 1]
        a = jnp.exp(log_a)
        return a, jnp.sqrt(1.0 - a * a) * (i * xc)

    def scan_tile(a, u, carry, reverse):
        hs, cum = [None] * V, [None] * V
        hp = ap = None
        for v in (range(V - 1, -1, -1) if reverse else range(V)):
            av, uv = vreg(a, v), vreg(u, v)
            hp, ap = (uv, av) if hp is None else (av * hp + uv, av * ap)
            hs[v], cum[v] = hp, ap
        s = 1
        while s < SUBLANE:
            keep = (sub < SUBLANE - s) if reverse else (sub >= s)
            sh = SUBLANE - s if reverse else s
            h_sh = jnp.where(keep, pltpu.roll(hp, sh, 0), 0.0)
            a_sh = jnp.where(keep, pltpu.roll(ap, sh, 0), 1.0)
            hp = hp + ap * h_sh
            ap = ap * a_sh
            s *= 2
        state = hp + ap * carry
        if reverse:
            start, out = shift_up(state, carry), state[0:1]
        else:
            start, out = shift_down(state, carry), state[SUBLANE - 1:SUBLANE]
        h = jnp.concatenate([hs[v] + cum[v] * start for v in range(V)], axis=0)
        return h, out

    def run_segment(g_ref, x_ref, y_ref, seg_len, carry_f, carry_b):
        nt = seg_len // T
        unroll = 4 if nt % 4 == 0 else (2 if nt % 2 == 0 else 1)

        def fwd_body(i, carry):
            for j in range(unroll):
                t = i * unroll + j
                t0 = pl.multiple_of(t * T, T)
                xc = conv_tile(x_ref, t, nt, seg_len)
                xs_scr[pl.ds(t0, T), :] = xc
                a, u = gates(xc, 0)
                h, carry = scan_tile(a, u, carry, False)
                hf_scr[pl.ds(t0, T), :] = h
            return carry

        def bwd_body(i, carry):
            for j in range(unroll):
                t = nt - 1 - (i * unroll + j)
                t0 = pl.multiple_of(t * T, T)
                a, u = gates(xs_scr[pl.ds(t0, T), :], 1)
                h, carry = scan_tile(a, u, carry, True)
                g = jnp.dot(p_ref[...], g_ref[pl.ds(t0, T), :], preferred_element_type=F32)
                rec = hf_scr[pl.ds(t0, T), :] + h
                y_ref[pl.ds(t0, T), :] = (jax.nn.gelu(g) * rec).astype(y_ref.dtype)
            return carry

        steps = nt // unroll
        return (lax.fori_loop(0, steps, fwd_body, carry_f),
                lax.fori_loop(0, steps, bwd_body, carry_b))

    zero = jnp.zeros((1, LANE), F32)
    hf, hb = run_segment(gc_ref, xc_ref, yc_ref, xc_ref.shape[0], zero, zero)
    run_segment(gl_ref, xl_ref, yl_ref, xl_ref.shape[0], hf, hb)


def _time_permutation():
    v = ROW_TILE // SUBLANE
    dst = jnp.arange(ROW_TILE)
    src = (dst % SUBLANE) * v + dst // SUBLANE
    return src[:, None] == jnp.arange(ROW_TILE)[None, :]


def _lru(u, lru_w, *, batch, seq, n_ctx, g_col0, x_col0):
    width = LRU_HEADS * LANE
    cw, cb, wg, bg, lam = lru_w
    ctx_blk0 = batch * seq // n_ctx
    perm = _time_permutation()
    once = pl.Buffered(1)
    return pl.pallas_call(
        _lru_kernel,
        out_shape=(jax.ShapeDtypeStruct((batch * n_ctx, width), BF16),
                   jax.ShapeDtypeStruct((batch * seq, width), BF16)),
        grid=(batch, LRU_HEADS),
        in_specs=[pl.BlockSpec((n_ctx, LANE), lambda b, h: (ctx_blk0 + b, g_col0 + h)),
                  pl.BlockSpec((n_ctx, LANE), lambda b, h: (ctx_blk0 + b, x_col0 + h)),
                  pl.BlockSpec((seq, LANE), lambda b, h: (b, g_col0 + h)),
                  pl.BlockSpec((seq, LANE), lambda b, h: (b, x_col0 + h)),
                  pl.BlockSpec((ROW_TILE, ROW_TILE), lambda b, h: (0, 0), pipeline_mode=once),
                  pl.BlockSpec((8, LANE), lambda b, h: (0, h)),
                  pl.BlockSpec((1, LANE), lambda b, h: (0, h)),
                  pl.BlockSpec((2, 1, LANE, 2 * LANE), lambda b, h: (0, h, 0, 0)),
                  pl.BlockSpec((2, 1, 1, 2 * LANE), lambda b, h: (0, h, 0, 0)),
                  pl.BlockSpec((2, LANE), lambda b, h: (0, h))],
        out_specs=(pl.BlockSpec((n_ctx, LANE), lambda b, h: (b, h)),
                   pl.BlockSpec((seq, LANE), lambda b, h: (b, h))),
        scratch_shapes=[pltpu.VMEM((seq, LANE), F32), pltpu.VMEM((seq, LANE), F32)],
        compiler_params=_params("parallel", "parallel"),
        name="rglru",
    )(u, u, u, u, perm.astype(BF16), cw, cb, wg, bg, lam)


def _chunk_mask(rows, reverse):
    r = lax.broadcasted_iota(jnp.int32, (rows, rows), 0)
    c = lax.broadcasted_iota(jnp.int32, (rows, rows), 1)
    same = (r >> CHUNK_SHIFT) == (c >> CHUNK_SHIFT)
    tri = (c >= r) if reverse else (c <= r)
    return jnp.where(same, jnp.where(tri, 1.0, 0.0), 0.0)


def _split3(x):
    hi = x.astype(BF16)
    r1 = x - hi.astype(F32)
    mid = r1.astype(BF16)
    lo = (r1 - mid.astype(F32)).astype(BF16)
    return hi, mid, lo


def _block_scan(g_all, head_qkv, st_scr, o_ref, *, heads, dv, reverse, batched):
    rows, wall = g_all.shape
    width = wall // heads
    nchunk = rows // CHUNK
    mask = _chunk_mask(rows, reverse)
    mask_b = mask.astype(BF16)
    keep = mask > 0.5
    keep_c = keep[:CHUNK, :CHUNK]
    iref = CHUNK // 2 if reverse else CHUNK // 2 - 1
    ilast = 0 if reverse else CHUNK - 1
    order = range(nchunk - 1, -1, -1) if reverse else range(nchunk)

    def per_chunk(rows_1, width):
        return jnp.concatenate([jnp.broadcast_to(r, (CHUNK, width)) for r in rows_1], axis=0)

    pieces = jnp.concatenate(_split3(g_all), axis=1)
    sums = jnp.dot(mask_b, pieces, preferred_element_type=F32)
    b_all = sums[:, :wall] + sums[:, wall:2 * wall] + sums[:, 2 * wall:]

    def prepare(h):
        q, k, v = head_qkv(h)
        b = b_all[:, h * width:(h + 1) * width]
        b_ref = [b[c * CHUNK + iref:c * CHUNK + iref + 1] for c in range(nchunk)]
        b_last = [b[c * CHUNK + ilast:c * CHUNK + ilast + 1] for c in range(nchunk)]
        e1 = jnp.exp(b - per_chunk(b_ref, width))
        r1 = 1.0 / e1
        qe = q * e1
        ke = k * r1
        qd = (qe * per_chunk([jnp.exp(r) for r in b_ref], width)).astype(BF16)
        kd = (ke * per_chunk([jnp.exp(l - r) for l, r in zip(b_last, b_ref)], width)).astype(BF16)
        decays = [jnp.exp(l) for l in b_last]
        return qe.astype(BF16), ke.astype(BF16), qd, kd, v.astype(BF16), decays

    nxt = prepare(0)
    for h in range(heads):
        qe, ke, qd, kd, vb, decays = nxt
        if h + 1 < heads:
            nxt = prepare(h + 1)
        if batched:
            sc = lax.dot_general(qe, ke, _NT, preferred_element_type=F32)
            sc = jnp.where(keep, sc, 0.0).astype(BF16)
            o = jnp.dot(sc, vb, preferred_element_type=F32)
        st = st_scr[h]
        parts = [None] * nchunk
        for ci in order:
            rs = slice(ci * CHUNK, (ci + 1) * CHUNK)
            if batched:
                o_c = o[rs]
            else:
                sc = lax.dot_general(qe[rs], ke[rs], _NT, preferred_element_type=F32)
                sc = jnp.where(keep_c, sc, 0.0).astype(BF16)
                o_c = jnp.dot(sc, vb[rs], preferred_element_type=F32)
            parts[ci] = o_c + lax.dot_general(qd[rs], st.astype(BF16), _NT,
                                              preferred_element_type=F32)
            st = st * decays[ci] + lax.dot_general(vb[rs], kd[rs], _TN, preferred_element_type=F32)
        if o_ref is not None:
            o_ref[:, h * dv:(h + 1) * dv] = jnp.concatenate(parts, axis=0).astype(o_ref.dtype)
        st_scr[h] = st


def _gla_kernel(q_ref, k_ref, v_ref, a_ref, wa_ref, ba_ref, o_ref, st_scr):
    t = pl.program_id(1)
    d = pl.program_id(2)
    dk = q_ref.shape[1] // GLA_HEADS
    dv = v_ref.shape[1] // GLA_HEADS

    @pl.when((t == 0) & (d == 0))
    def _():
        st_scr[...] = jnp.zeros(st_scr.shape, F32)

    def head_qkv(h):
        ks = slice(h * dk, (h + 1) * dk)
        q = q_ref[:, ks].astype(F32) * (dk ** -0.5)
        return q, k_ref[:, ks].astype(F32), v_ref[:, h * dv:(h + 1) * dv]

    for dd in range(2):
        @pl.when(d == dd)
        def _(dd=dd):
            z = jnp.dot(a_ref[...], wa_ref[dd], preferred_element_type=F32) + ba_ref[dd]
            g_all = _log_sigmoid(z) * (1.0 / GLA_TAU)
            _block_scan(g_all, head_qkv, st_scr.at[dd], o_ref,
                        heads=GLA_HEADS, dv=dv, reverse=dd == 1, batched=True)


def _gla(u, wa, ba, *, batch, seq, n_ctx, cols, dv_total):
    ntok = u.shape[0]
    q0, k0, v0, a0 = cols
    dkt = wa.shape[2]
    dvt = dv_total
    nblk = seq // ROW_TILE
    ctx_blk0 = batch * nblk
    assert n_ctx == ROW_TILE

    def blk(b, t, d):
        lat = b * nblk + jnp.where(d == 0, t - 1, nblk - t)
        return jnp.where(t == 0, ctx_blk0 + b, lat)

    in_specs = [pl.BlockSpec((ROW_TILE, dkt), lambda b, t, d: (blk(b, t, d), q0 * LANE // dkt)),
                pl.BlockSpec((ROW_TILE, dkt), lambda b, t, d: (blk(b, t, d), k0 * LANE // dkt)),
                pl.BlockSpec((ROW_TILE, dvt), lambda b, t, d: (blk(b, t, d), v0 * LANE // dvt)),
                pl.BlockSpec((ROW_TILE, LANE), lambda b, t, d: (blk(b, t, d), a0)),
                pl.BlockSpec(wa.shape, lambda b, t, d: (0, 0, 0)),
                pl.BlockSpec(ba.shape, lambda b, t, d: (0, 0, 0))]
    return pl.pallas_call(
        _gla_kernel,
        out_shape=jax.ShapeDtypeStruct((2, ntok, dvt), BF16),
        grid=(batch, nblk + 1, 2),
        in_specs=in_specs,
        out_specs=pl.BlockSpec((None, ROW_TILE, dvt), lambda b, t, d: (d, blk(b, t, d), 0)),
        scratch_shapes=[pltpu.VMEM((2, GLA_HEADS, dvt // GLA_HEADS, dkt // GLA_HEADS), F32)],
        compiler_params=_params("parallel", "arbitrary", "arbitrary"),
        name="gla",
    )(u, u, u, u, wa, ba)


def _hgrn_kernel(*refs, emit_o):
    q_ref, z_ref, v_ref, lb_ref, ub_ref, s0_ref = refs[:6]
    if emit_o:
        o_ref, st_ref, st_scr = refs[6:]
    else:
        st_ref, st_scr = refs[6:]
        o_ref = None
    t = pl.program_id(1)
    d = pl.program_id(2)
    dh = q_ref.shape[1] // HGRN_HEADS

    @pl.when((t == 0) & (d == 0))
    def _():
        st_scr[...] = s0_ref[:, 0]

    lb = lb_ref[...]
    ub = ub_ref[...]

    for dd in range(2):
        @pl.when(d == dd)
        def _(dd=dd):
            z = z_ref[...].astype(F32)
            e = jnp.exp(-jnp.abs(z))
            r = 1.0 / (1.0 + e)
            er = e * r
            pos = z >= 0.0
            log_f = jnp.log(lb + ub * jnp.where(pos, r, er))
            k_all = ub * jnp.where(pos, er, r)

            def head_qkv(h):
                hs = slice(h * dh, (h + 1) * dh)
                return q_ref[:, hs].astype(F32), k_all[:, hs], v_ref[:, hs]

            _block_scan(log_f, head_qkv, st_scr.at[dd], o_ref,
                        heads=HGRN_HEADS, dv=dh, reverse=dd == 1, batched=True)

    @pl.when((t == pl.num_programs(1) - 1) & (d == 1))
    def _():
        st_ref[:, 0] = st_scr[...]


def _hgrn_segment(u, s0, lb, ub, *, batch, seg_len, emit_o):
    ntok = u.shape[0]
    width = lb.shape[1]
    nblk = seg_len // ROW_TILE

    def blk(b, t, d):
        return b * nblk + jnp.where(d == 0, t, nblk - 1 - t)

    st_spec = pl.BlockSpec((2, 1) + s0.shape[2:], lambda b, t, d: (0, b, 0, 0, 0))
    in_specs = [pl.BlockSpec((ROW_TILE, width), lambda b, t, d: (blk(b, t, d), 0)),
                pl.BlockSpec((ROW_TILE, width), lambda b, t, d: (blk(b, t, d), 1 + d)),
                pl.BlockSpec((ROW_TILE, width), lambda b, t, d: (blk(b, t, d), 3)),
                pl.BlockSpec((1, width), lambda b, t, d: (0, 0)),
                pl.BlockSpec((1, width), lambda b, t, d: (0, 0)),
                st_spec]
    st_shape = jax.ShapeDtypeStruct(s0.shape, F32)
    if emit_o:
        out_shape = (jax.ShapeDtypeStruct((2, ntok, width), F32), st_shape)
        out_specs = (pl.BlockSpec((None, ROW_TILE, width), lambda b, t, d: (d, blk(b, t, d), 0)),
                     st_spec)
    else:
        out_shape = (st_shape,)
        out_specs = (st_spec,)
    return pl.pallas_call(
        functools.partial(_hgrn_kernel, emit_o=emit_o),
        out_shape=out_shape,
        grid=(batch, nblk, 2),
        in_specs=in_specs,
        out_specs=out_specs,
        scratch_shapes=[pltpu.VMEM((2,) + s0.shape[2:], F32)],
        compiler_params=_params("parallel", "arbitrary", "arbitrary"),
        name="hgrn_segment",
    )(u, u, u, lb, ub, s0)


def _head_norm_gate(o, gate, heads):
    dh = o.shape[1] // heads
    parts = []
    for h in range(heads):
        oh = o[:, h * dh:(h + 1) * dh]
        ms = jnp.mean(oh * oh, axis=-1, keepdims=True)
        parts.append((oh * lax.rsqrt(ms + EPS) * _silu(gate[:, h * dh:(h + 1) * dh])).astype(BF16))
    return jnp.concatenate(parts, axis=1)


def _load_rows(ref, r0, nrows):
    s_n = CHUNKS_PER_ROW
    return jnp.concatenate([ref[pl.ds(r0 * s_n + s, nrows, stride=s_n), :] for s in range(s_n)],
                           axis=1)


def _store_rows(ref, r0, val):
    s_n = CHUNKS_PER_ROW
    for s in range(s_n):
        ref[pl.ds(r0 * s_n + s, val.shape[0], stride=s_n), :] = val[:, s * LANE:(s + 1) * LANE]


def _out_proj_tail(o, x, tab_ref, pg_ref, ng_ref, rt_ref, xo_ref, h_ref, lg_ref):
    bias_ref, ids_ref, wsel_ref = lg_ref
    xn = _gated_residual(x, _rms(o, pg_ref[...]), tab_ref, 2)
    xo_ref[...] = xn
    h2 = _modulate(_rms(xn, ng_ref[...]), tab_ref, 3, 4)
    _store_rows(h_ref, 0, h2)
    logits_t = lax.dot_general(rt_ref[...], h2, _NT, precision=lax.Precision.HIGHEST,
                               preferred_element_type=F32)
    _route_kernel(logits_t, bias_ref, ids_ref, wsel_ref)


def _out_proj_ab_kernel(yl_ref, yc_ref, pt_ref, of_ref, ob_ref, gate_ref, xc_ref, w_ref, x_ref,
                        tab_ref, pg_ref, ng_ref, rt_ref, bias_ref, xo_ref, h_ref, ids_ref,
                        wsel_ref, *, lat_tiles):
    lg_ref = (bias_ref, ids_ref, wsel_ref)
    o_att = of_ref[...].astype(F32) + ob_ref[...].astype(F32)
    yb = _head_norm_gate(o_att, gate_ref[...].astype(F32), GLA_HEADS)
    is_lat = pl.program_id(0) < lat_tiles
    ya = jnp.where(is_lat, yl_ref[...], yc_ref[...])
    x = jnp.where(is_lat, x_ref[...], xc_ref[...])
    ya = jnp.concatenate(
        [jnp.dot(pt_ref[...], ya[s:s + ROW_TILE], preferred_element_type=F32).astype(BF16)
         for s in range(0, ya.shape[0], ROW_TILE)], axis=0)
    da = ya.shape[1]
    o = jnp.dot(ya, w_ref[:da], preferred_element_type=F32)
    o = o + jnp.dot(yb, w_ref[da:], preferred_element_type=F32)
    _out_proj_tail(o, x, tab_ref, pg_ref, ng_ref, rt_ref, xo_ref, h_ref, lg_ref)


def _out_proj_c_kernel(of_ref, ob_ref, gate_ref, p_ref, w_ref, x_ref, tab_ref, pg_ref, ng_ref,
                       rt_ref, bias_ref, xo_ref, h_ref, ids_ref, wsel_ref):
    lg_ref = (bias_ref, ids_ref, wsel_ref)
    tm = x_ref.shape[0]
    width = of_ref.shape[-1]
    o_att = of_ref[...].astype(F32) + ob_ref[...].astype(F32)
    y = _head_norm_gate(o_att.reshape(tm, width), gate_ref[...].astype(F32).reshape(tm, width),
                        HGRN_HEADS)
    y = jnp.dot(p_ref[...], y, preferred_element_type=F32).astype(BF16)
    o = jnp.dot(y, w_ref[...], preferred_element_type=F32)
    _out_proj_tail(o, x_ref[...], tab_ref, pg_ref, ng_ref, rt_ref, xo_ref, h_ref, lg_ref)


def _out_proj(kernel, lead, lead_specs, w, x, tab, post_gain, next_gain, router_t, bias, tm, ntiles,
              x_tiles=None):
    d = x.shape[1]
    ntok = ntiles * tm
    nsub = tm // ROW_TILE
    ne = router_t.shape[0]
    last_x = (x_tiles or ntiles) - 1
    bias_b = jnp.broadcast_to(bias.astype(F32)[:, None], (ne, tm))
    in_specs = list(lead_specs) + [
        pl.BlockSpec(w.shape, lambda i: (0, 0), pipeline_mode=pl.Buffered(1)),
        pl.BlockSpec((tm, d), lambda i: (jnp.minimum(i, last_x), 0)),
        pl.BlockSpec((nsub, 8, d), lambda i: (i, 0, 0)),
        pl.BlockSpec((1, d), lambda i: (0, 0)),
        pl.BlockSpec((1, d), lambda i: (0, 0)),
        pl.BlockSpec((ne, d), lambda i: (0, 0)),
        pl.BlockSpec((ne, tm), lambda i: (0, 0))]
    xo, h2, ids, wsel = pl.pallas_call(
        kernel,
        out_shape=(jax.ShapeDtypeStruct((ntok, d), F32),
                   jax.ShapeDtypeStruct((ntok * CHUNKS_PER_ROW, LANE), F32),
                   jax.ShapeDtypeStruct((TOP_K, ntok), jnp.int32),
                   jax.ShapeDtypeStruct((TOP_K, ntok), F32)),
        grid=(ntiles,),
        in_specs=in_specs,
        out_specs=(pl.BlockSpec((tm, d), lambda i: (i, 0)),
                   pl.BlockSpec((tm * CHUNKS_PER_ROW, LANE), lambda i: (i, 0)),
                   pl.BlockSpec((TOP_K, tm), lambda i: (0, i)),
                   pl.BlockSpec((TOP_K, tm), lambda i: (0, i))),
        compiler_params=_params("parallel"),
        name="out_proj_norm_residual_route",
    )(*lead, w, x, tab, post_gain, next_gain, router_t, bias_b)
    return xo, h2, (ids, wsel)


def _route_kernel(lg_ref, bias_ref, ids_ref, w_ref):
    tm = lg_ref.shape[1]
    per_group = N_EXPERTS // N_GROUPS
    shape3 = (N_GROUPS, per_group, tm)
    aff = _sigmoid(lg_ref[...]).reshape(shape3)
    biased = aff + bias_ref[...].reshape(shape3)
    neg = -jnp.inf
    sub = lax.broadcasted_iota(jnp.int32, shape3, 1)
    grp = lax.broadcasted_iota(jnp.int32, shape3, 0)
    m1 = jnp.max(biased, axis=1, keepdims=True)
    i1 = jnp.min(jnp.where(biased == m1, sub, per_group), axis=1, keepdims=True)
    m2 = jnp.max(jnp.where(sub == i1, neg, biased), axis=1, keepdims=True)
    score = m1 + m2
    gidx = lax.broadcasted_iota(jnp.int32, score.shape, 0)
    keep = jnp.zeros(score.shape, F32)
    for _ in range(TOPK_GROUPS):
        m = jnp.max(score, axis=0, keepdims=True)
        im = jnp.min(jnp.where(score == m, gidx, N_GROUPS), axis=0, keepdims=True)
        sel = gidx == im
        keep = jnp.where(sel, 1.0, keep)
        score = jnp.where(sel, neg, score)
    work = jnp.where(jnp.broadcast_to(keep, shape3) > 0.0, biased, neg)
    eidx = grp * per_group + sub
    ids, gates = [], []
    for _ in range(TOP_K):
        m = jnp.max(jnp.max(work, axis=0, keepdims=True), axis=1, keepdims=True)
        cand = jnp.where(work == m, eidx, N_EXPERTS)
        im = jnp.min(jnp.min(cand, axis=0, keepdims=True), axis=1, keepdims=True)
        sel = eidx == im
        ids.append(im)
        gates.append(jnp.sum(jnp.sum(jnp.where(sel, aff, 0.0), axis=0, keepdims=True),
                             axis=1, keepdims=True))
        work = jnp.where(sel, neg, work)
    gate = jnp.concatenate(gates, axis=1)
    den = jnp.sum(gate, axis=1, keepdims=True)
    w_ref[...] = (ROUTED_SCALE * gate / den).reshape(TOP_K, tm)
    ids_ref[...] = jnp.concatenate(ids, axis=1).reshape(TOP_K, tm)


LIST_PAD = 1024


def _dispatch_plan(ids, w, t0, tile, ntiles):
    n = ntiles * tile
    nassign = tile * TOP_K
    assert tile & (tile - 1) == 0 and N_EXPERTS * nassign < 2 ** 31

    def per_tile(a):
        return a[:, t0:t0 + n].reshape(TOP_K, ntiles, tile).transpose(1, 0, 2).reshape(ntiles, nassign)

    e = per_tile(ids)
    g = per_tile(w)
    keys = e * nassign + jnp.arange(nassign, dtype=jnp.int32)
    order = jnp.sort(keys, axis=1) & (nassign - 1)
    rows = (order & (tile - 1)) * CHUNKS_PER_ROW
    wts = jnp.take_along_axis(g, order, axis=1)
    experts = jnp.arange(N_EXPERTS, dtype=jnp.int32)
    counts = jnp.sum((e[:, :, None] == experts).astype(jnp.int32), axis=1)
    starts = jnp.concatenate([jnp.zeros((ntiles, 1), jnp.int32), jnp.cumsum(counts, axis=1)], axis=1)
    rows = jnp.pad(rows, ((0, 0), (0, LIST_PAD)))
    wts = jnp.pad(wts, ((0, 0), (0, LIST_PAD)))
    starts = jnp.pad(starts, ((0, 0), (0, LANE - N_EXPERTS - 1)))
    return rows, wts, starts


def _moe_kernel(rows_hbm, wts_hbm, starts_hbm, h_ref, wg_ref, wu_ref, wd_ref, sg_ref, su_ref,
                sd_ref, o_ref, rows_s, wts_s, starts_s, sem, gbuf, ybuf, *, epb):
    i = pl.program_id(0)
    j = pl.program_id(1)
    s_n = CHUNKS_PER_ROW
    blk = MOE_BLOCK
    tile = h_ref.shape[0] // s_n

    def swiglu(x, w_gate, w_up, w_down):
        g = jnp.dot(x, w_gate, preferred_element_type=F32)
        u = jnp.dot(x, w_up, preferred_element_type=F32)
        return jnp.dot((_silu(g) * u).astype(BF16), w_down, preferred_element_type=F32)

    @pl.when(j == 0)
    def _():
        copies = [pltpu.make_async_copy(rows_hbm.at[i], rows_s, sem.at[0]),
                  pltpu.make_async_copy(wts_hbm.at[i], wts_s, sem.at[1]),
                  pltpu.make_async_copy(starts_hbm.at[i], starts_s, sem.at[2])]
        for c in copies:
            c.start()
        for r0 in range(0, tile, ROW_TILE):
            x = _load_rows(h_ref, r0, ROW_TILE).astype(BF16)
            _store_rows(o_ref, r0, swiglu(x, sg_ref[...], su_ref[...], sd_ref[...]))
        for c in copies:
            c.wait()

    for e in range(epb):
        base = starts_s[j * epb + e]
        count = starts_s[j * epb + e + 1] - base
        w_gate = wg_ref[e].astype(BF16)
        w_up = wu_ref[e].astype(BF16)
        w_down = wd_ref[e].astype(BF16)

        def block(bi, carry, base=base, count=count, w_gate=w_gate, w_up=w_up, w_down=w_down):
            p0 = base + bi * blk
            left = count - bi * blk
            offs = [pl.multiple_of(rows_s[p0 + r], s_n) for r in range(blk)]
            for r in range(blk):
                gbuf[pl.ds(r * s_n, s_n), :] = h_ref[pl.ds(offs[r], s_n), :]
            x = _load_rows(gbuf, 0, blk).astype(BF16)
            y = swiglu(x, w_gate, w_up, w_down)
            y = jnp.where(lax.broadcasted_iota(jnp.int32, y.shape, 0) < left, y, 0.0)
            _store_rows(ybuf, 0, y)
            group = 2 * SUBLANE
            for r0 in range(0, blk, group):
                vals = [o_ref[pl.ds(offs[r], s_n), :] + wts_s[p0 + r] * ybuf[pl.ds(r * s_n, s_n), :]
                        for r in range(r0, r0 + group)]
                for r in reversed(range(r0, r0 + group)):
                    o_ref[pl.ds(offs[r], s_n), :] = vals[r - r0]
            return carry

        nblk = lax.shift_right_logical(count + (blk - 1), blk.bit_length() - 1)
        lax.fori_loop(0, nblk, block, 0)


def _moe(h, plan, layer, wg, wu, wd, sg, su, sd, tile0, ntiles, tile, epb):
    rows, wts, starts = plan
    s_n = CHUNKS_PER_ROW
    _, ne, d, ff = wg.shape
    once = pl.Buffered(1)
    hbm = pl.BlockSpec(memory_space=pl.ANY)
    return pl.pallas_call(
        functools.partial(_moe_kernel, epb=epb),
        out_shape=jax.ShapeDtypeStruct((ntiles * tile * s_n, LANE), F32),
        grid=(ntiles, ne // epb),
        in_specs=[hbm, hbm, hbm,
                  pl.BlockSpec((tile * s_n, LANE), lambda i, j: (tile0 + i, 0), pipeline_mode=once),
                  pl.BlockSpec((None, epb, d, ff), lambda i, j: (layer, j, 0, 0)),
                  pl.BlockSpec((None, epb, d, ff), lambda i, j: (layer, j, 0, 0)),
                  pl.BlockSpec((None, epb, ff, d), lambda i, j: (layer, j, 0, 0)),
                  pl.BlockSpec(sg.shape, lambda i, j: (0, 0), pipeline_mode=once),
                  pl.BlockSpec(su.shape, lambda i, j: (0, 0), pipeline_mode=once),
                  pl.BlockSpec(sd.shape, lambda i, j: (0, 0), pipeline_mode=once)],
        out_specs=pl.BlockSpec((tile * s_n, LANE), lambda i, j: (i, 0)),
        scratch_shapes=[pltpu.SMEM((rows.shape[1],), jnp.int32),
                        pltpu.SMEM((wts.shape[1],), F32),
                        pltpu.SMEM((starts.shape[1],), jnp.int32),
                        pltpu.SemaphoreType.DMA((3,)),
                        pltpu.VMEM((MOE_BLOCK * s_n, LANE), F32),
                        pltpu.VMEM((MOE_BLOCK * s_n, LANE), F32)],
        compiler_params=_params("parallel", "arbitrary", vmem=VMEM_LIMIT_MOE),
        name="moe_experts",
    )(rows, wts, starts, h, wg, wu, wd, sg, su, sd)


def _moe_finish_kernel(f_ref, x_ref, tab_ref, pg_ref, o_ref):
    f = _load_rows(f_ref, 0, x_ref.shape[0])
    o_ref[...] = _gated_residual(x_ref[...], _rms(f, pg_ref[...]), tab_ref, 5)


def _moe_finish(f, x, tab, post_gain, tm, x_tile0, tab0):
    d = x.shape[1]
    s_n = CHUNKS_PER_ROW
    nsub = tm // ROW_TILE
    ntiles = f.shape[0] // (tm * s_n)
    return pl.pallas_call(
        _moe_finish_kernel,
        out_shape=jax.ShapeDtypeStruct((ntiles * tm, d), F32),
        grid=(ntiles,),
        in_specs=[pl.BlockSpec((tm * s_n, LANE), lambda i: (i, 0)),
                  pl.BlockSpec((tm, d), lambda i: (x_tile0 + i, 0)),
                  pl.BlockSpec((nsub, 8, d), lambda i: (tab0 + i, 0, 0)),
                  pl.BlockSpec((1, d), lambda i: (0, 0))],
        out_specs=pl.BlockSpec((tm, d), lambda i: (i, 0)),
        compiler_params=_params("parallel"),
        name="moe_norm_residual",
    )(f, x, tab, post_gain)


def _grid_permutation(band):
    tm = band * GRID_W
    dst = jnp.arange(tm)
    src = (dst % band) * GRID_W + dst // band
    return (src[:, None] == jnp.arange(tm)[None, :]).astype(BF16)


def kernel(x, c, ctx, c_ctx, mod_w, mod_b, pre_gain, post_gain, ab_w_in, ab_conv_w, ab_conv_b, ab_lru_wa, ab_lru_ba, ab_lru_wi, ab_lru_bi, ab_lru_lambda, ab_gla_wa2, ab_gla_ba, ab_w_out, c_w_in, hgrn_lb_logits, c_w_out, moe_router, moe_bias, moe_w_gate, moe_w_up, moe_w_down, moe_ws_gate, moe_ws_up, moe_ws_down):
    batch, seq, d = x.shape
    n_ctx = ctx.shape[1]
    depth = mod_w.shape[0]
    band = SUBLANE
    tm_grid = band * GRID_W
    rows_img = seq // GRID_W
    assert n_ctx == ROW_TILE and seq % (2 * ROW_TILE) == 0 and d % LANE == 0
    assert depth == 2 and batch + 1 <= 8 and rows_img % band == 0 and tm_grid % ROW_TILE == 0
    n_lat = batch * seq
    n_ctx_tok = batch * n_ctx
    ntok = n_lat + n_ctx_tok

    x_lat = x.reshape(n_lat, d)
    x_ctx = ctx.reshape(n_ctx_tok, d)

    cvec = jnp.concatenate([c, c_ctx[None, :], jnp.zeros((8 - batch - 1, d), F32)], axis=0)
    mods = _modulation(cvec, mod_w, mod_b).reshape(depth, 8, N_MOD, d)
    tile_row = jnp.concatenate([jnp.repeat(jnp.arange(batch), seq // ROW_TILE),
                                jnp.full((batch,), batch)]).astype(jnp.int32)
    tabs = jnp.pad(mods[:, tile_row], ((0, 0), (0, 0), (0, 8 - N_MOD), (0, 0)))

    def experts(layer, h2, routed, t0, ntiles, tile):
        plan = _dispatch_plan(*routed, t0, tile, ntiles)
        return _moe(h2, plan, layer, moe_w_gate, moe_w_up, moe_w_down,
                    moe_ws_gate[layer].astype(BF16), moe_ws_up[layer].astype(BF16),
                    moe_ws_down[layer].astype(BF16), t0 // tile, ntiles, tile, epb)

    tm_proj, tm_out, tm_route, tm_moe, epb = 512, 512, 512, 2048, 4
    assert n_lat % tm_moe == 0 and n_lat % n_ctx_tok == 0

    ab_cols = ab_w_in.shape[2]
    gla_dk = ab_gla_wa2.shape[3]
    lru_w = d
    gla_dv = (ab_cols - 2 * lru_w - 2 * gla_dk - 2 * GLA_RANK) // 2
    n_main = ab_cols // LANE * LANE
    tn0 = n_main // 4
    w_main = ab_w_in[0][:, :n_main].astype(BF16)
    w_tail = jnp.pad(ab_w_in[0][:, n_main:], ((0, 0), (0, LANE - (ab_cols - n_main)))).astype(BF16)
    u0 = _in_proj(x_lat, x_ctx, tabs[0], pre_gain[0, 0][None, :], w_main, w_tail, tm_proj, tn0)

    cw = jnp.pad(ab_conv_w[0], ((0, 8 - CONV_W), (0, 0)))
    cb = ab_conv_b[0][None, :]
    wg = jnp.concatenate([ab_lru_wa[0], ab_lru_wi[0]], axis=-1).astype(BF16)
    bg = jnp.concatenate([ab_lru_ba[0].reshape(2, LRU_HEADS, 1, LANE),
                          ab_lru_bi[0].reshape(2, LRU_HEADS, 1, LANE)], axis=-1)
    ya_c, ya_l = _lru(u0, (cw, cb, wg, bg, ab_lru_lambda[0]), batch=batch, seq=seq, n_ctx=n_ctx,
                      g_col0=0, x_col0=lru_w // LANE)

    q0 = 2 * lru_w // LANE
    k0 = q0 + gla_dk // LANE
    v0 = k0 + gla_dk // LANE
    gate0 = v0 + gla_dv // LANE
    a0 = gate0 + gla_dv // LANE
    wa = jnp.stack([jnp.zeros((LANE, gla_dk), F32).at[dr * GLA_RANK:(dr + 1) * GLA_RANK]
                    .set(ab_gla_wa2[0, dr]) for dr in range(2)]).astype(BF16)
    ba = ab_gla_ba[0][:, None, :]
    o_gla = _gla(u0, wa, ba, batch=batch, seq=seq, n_ctx=n_ctx, cols=(q0, k0, v0, a0),
                 dv_total=gla_dv)

    lat_tiles = n_lat // tm_out
    ctx_tiles = n_ctx_tok // tm_out
    lead_specs = [pl.BlockSpec((tm_out, lru_w), lambda i: (jnp.minimum(i, lat_tiles - 1), 0)),
                  pl.BlockSpec((tm_out, lru_w),
                               lambda i: (jnp.clip(i - lat_tiles, 0, ctx_tiles - 1), 0)),
                  pl.BlockSpec((ROW_TILE, ROW_TILE), lambda i: (0, 0)),
                  pl.BlockSpec((None, tm_out, gla_dv), lambda i: (0, i, 0)),
                  pl.BlockSpec((None, tm_out, gla_dv), lambda i: (1, i, 0)),
                  pl.BlockSpec((tm_out, gla_dv), lambda i: (i, gate0 * LANE // gla_dv)),
                  pl.BlockSpec((tm_out, d), lambda i: (jnp.clip(i - lat_tiles, 0, ctx_tiles - 1), 0))]
    xs, h2, cwt = _out_proj(functools.partial(_out_proj_ab_kernel, lat_tiles=lat_tiles),
                            (ya_l, ya_c, _time_permutation().T.astype(BF16), o_gla, o_gla, u0, x_ctx),
                            lead_specs,
                            ab_w_out[0].astype(BF16), x_lat, tabs[0], post_gain[0, 0][None, :],
                            pre_gain[0, 1][None, :], moe_router[0].T, moe_bias[0], tm_out,
                            ntok // tm_out, x_tiles=lat_tiles)
    f_ctx = experts(0, h2, cwt, n_lat, 1, n_ctx_tok)
    f_lat = experts(0, h2, cwt, 0, n_lat // tm_moe, tm_moe)
    pg0 = post_gain[0, 1][None, :]

    hd = c_w_out.shape[1]
    tn1 = 1280
    w_in1 = c_w_in[0].astype(BF16)
    c_cols = w_in1.shape[1]
    nsec = c_cols // hd
    perm = _grid_permutation(band)
    gain1 = pre_gain[1, 0][None, :]
    u1c = _in_proj_ctx(f_ctx, xs, tabs[0], pg0, tabs[1], gain1, w_in1, tm_proj, tn1,
                       n_lat // tm_proj)
    u1l, xs = _in_proj_grid(f_lat, xs, tabs[0], pg0, tabs[1], gain1, perm, w_in1, tn1,
                            n_lat // tm_grid, band, rows_img // band)
    lb_sm = jax.nn.softmax(hgrn_lb_logits.astype(F32), axis=0)
    lb_cum = jnp.cumsum(lb_sm, axis=0)
    lb = (lb_cum[1] - lb_cum[0])[None, :]
    ub = 1.0 - lb
    dh = hd // HGRN_HEADS
    s_zero = jnp.zeros((2, batch, HGRN_HEADS, dh, dh), F32)
    u1l_flat = u1l.reshape(n_lat, c_cols)
    (s_c,) = _hgrn_segment(u1c, s_zero, lb, ub, batch=batch, seg_len=n_ctx, emit_o=False)
    o_hgrn, _ = _hgrn_segment(u1l_flat, s_c, lb, ub, batch=batch, seg_len=seq, emit_o=True)
    o_hgrn = o_hgrn.reshape(2, n_lat // rows_img, rows_img // band, band, hd)

    bpi = rows_img // band
    grid_blk = (GRID_W, None, band, hd)
    lead_specs = [pl.BlockSpec((None,) + grid_blk, lambda i: (0, i // bpi, i % bpi, 0, 0)),
                  pl.BlockSpec((None,) + grid_blk, lambda i: (1, i // bpi, i % bpi, 0, 0)),
                  pl.BlockSpec(grid_blk, lambda i: (i // bpi, i % bpi, 0, nsec - 1)),
                  pl.BlockSpec((tm_grid, tm_grid), lambda i: (0, 0))]
    xs, h2, routed = _out_proj(_out_proj_c_kernel, (o_hgrn, o_hgrn, u1l, perm.T), lead_specs,
                               c_w_out[0].astype(BF16), xs, tabs[1], post_gain[1, 0][None, :],
                               pre_gain[1, 1][None, :], moe_router[1].T, moe_bias[1], tm_grid,
                               n_lat // tm_grid)
    f_lat = experts(1, h2, routed, 0, n_lat // tm_moe, tm_moe)
    out = _moe_finish(f_lat, xs, tabs[1], post_gain[1, 1][None, :], tm_out, 0, 0)
    return out.reshape(batch, seq, d)
```
